```python
import math
import jax
import jax.numpy as jnp
from jax import lax
import numpy as np

D_MODEL = 1024
BATCH = 8
SEQ = 2048
DEPTH = 2

GRID_W = 64
CTX_LEN = 256
EPS = 1e-6
NEG_BIG = -1e30

A_HEADS = 4
A_DK = 128
A_DV = 128
A_CHUNK = 64
B_HEADS = 4
B_DK = 128
B_DV = 128
B_CHUNK = 64
B_CONV = 3
C_HEADS = 4
C_DH = 64
C_DV = 2 * C_DH
Q_BLOCK = 128
ROPE_BASE = 10000.0
ROPE_PAIRS = C_DH // 4

A_QK_W = A_HEADS * A_DK
A_V_W = A_HEADS * A_DV
B_QK_W = B_HEADS * B_DK
B_V_W = B_HEADS * B_DV
B_GATE_W = 4 * B_HEADS
C_QK_W = C_HEADS * 2 * C_DH
C_V_W = C_HEADS * C_DV
BRANCH_W = 512
PROJ_WIDTHS = (A_QK_W, A_V_W, A_V_W, A_QK_W, A_QK_W,
               B_QK_W, B_QK_W, B_V_W, B_V_W, B_GATE_W,
               C_QK_W, C_QK_W, C_V_W)

N_GROUPS = 4
EXPERTS_PER_GROUP = 8
N_EXPERTS = N_GROUPS * EXPERTS_PER_GROUP
TOP_K = 2
D_EXPERT = 512
MOE_BLOCK = 128

kernel_name = 'hybrid_hgrn2_mlstm_diffattn_hmoe_dit'


def _rms_norm(x, g):
    x32 = x.astype(jnp.float32)
    y = x32 * lax.rsqrt(jnp.mean(x32 * x32, axis=-1, keepdims=True) + EPS)
    return (y * g.astype(jnp.float32)).astype(x.dtype)


def _head_rms_norm(x, g, n_heads):
    shp = x.shape
    xh = x.reshape(shp[:-1] + (n_heads, -1)).astype(jnp.float32)
    y = xh * lax.rsqrt(jnp.mean(xh * xh, axis=-1, keepdims=True) + EPS)
    return (y.reshape(shp) * g.astype(jnp.float32)).astype(x.dtype)


def _heads(t, n_heads):
    return t.reshape(t.shape[:-1] + (n_heads, -1))


def _centred_dwconv(x, w, b):
    k = w.shape[0]
    pad = k // 2
    t = x.shape[1]
    xp = jnp.pad(x, ((0, 0), (pad, k - 1 - pad), (0, 0)))
    out = b
    for j in range(k):
        out = out + xp[:, j:j + t] * w[j]
    return out


def _rope_tables(n_tokens):
    rows = n_tokens // GRID_W
    row = jnp.repeat(jnp.arange(rows, dtype=jnp.float32), GRID_W)
    col = jnp.tile(jnp.arange(GRID_W, dtype=jnp.float32), rows)
    inv = ROPE_BASE ** (-jnp.arange(ROPE_PAIRS, dtype=jnp.float32) / ROPE_PAIRS)
    ang = jnp.stack([row[:, None] * inv, col[:, None] * inv])
    return jnp.cos(ang), jnp.sin(ang)


def _axial_rope(x, cos, sin):
    p = ROPE_PAIRS
    parts = []
    for a in range(2):
        seg = x[..., a * 2 * p:(a + 1) * 2 * p]
        x1, x2 = seg[..., :p], seg[..., p:]
        ca = cos[a][None, :, None, None, :].astype(x.dtype)
        sa = sin[a][None, :, None, None, :].astype(x.dtype)
        parts.append(x1 * ca - x2 * sa)
        parts.append(x2 * ca + x1 * sa)
    return jnp.concatenate(parts, axis=-1)


def _to_chunks(t, size):
    b, n, h = t.shape[:3]
    t = t.reshape((b, n // size, size, h) + t.shape[3:])
    return jnp.moveaxis(t, (1, 3), (0, 2))


def _from_chunks(t):
    t = jnp.moveaxis(t, (0, 2), (1, 3))
    return t.reshape((t.shape[0], t.shape[1] * t.shape[2]) + t.shape[3:])


def _hgrn2_scan(q, k, v, log_f, state0, with_out):
    xs = tuple(_to_chunks(t.astype(jnp.float32), A_CHUNK) for t in (q, k, v, log_f))
    tri = jnp.tril(jnp.ones((A_CHUNK, A_CHUNK), bool))[:, :, None]

    def step(s_prev, chunk):
        qc, kc, vc, lf = chunk
        b = jnp.cumsum(lf, axis=2)
        b_end = b[:, :, -1]
        s_new = jnp.exp(b_end)[..., None] * s_prev + jnp.einsum(
            'bhlk,bhlv->bhkv', kc * jnp.exp(b_end[:, :, None] - b), vc)
        if not with_out:
            return s_new, None
        rel = jnp.where(tri, jnp.exp(jnp.minimum(b[:, :, :, None] - b[:, :, None], 0.0)), 0.0)
        scores = jnp.einsum('bhtc,bhtsc,bhsc->bhts', qc, rel, kc)
        o = jnp.einsum('bhts,bhsv->bhtv', scores, vc) + jnp.einsum(
            'bhtk,bhkv->bhtv', qc * jnp.exp(b), s_prev)
        return s_new, o

    s_fin, o = lax.scan(step, state0, xs)
    if not with_out:
        return None, s_fin
    return _from_chunks(o).astype(v.dtype), s_fin


def _mlstm_scan(q, k, v, log_i, log_f, state0, with_out):
    xs = tuple(_to_chunks(t.astype(jnp.float32), B_CHUNK) for t in (q, k, v, log_i, log_f))
    tri = jnp.tril(jnp.ones((B_CHUNK, B_CHUNK), bool))

    def step(carry, chunk):
        c_prev, n_prev, m_prev = carry
        qc, kc, vc, li, lf = chunk
        b = jnp.cumsum(lf, axis=-1)
        b_end = b[..., -1]
        w_end = b_end[..., None] - b + li
        m_new = jnp.maximum(b_end + m_prev, jnp.max(w_end, axis=-1))
        e_end = jnp.exp(w_end - m_new[..., None])
        keep = jnp.exp(b_end + m_prev - m_new)
        c_new = keep[..., None, None] * c_prev + jnp.einsum('bhl,bhlk,bhlv->bhkv', e_end, kc, vc)
        n_new = keep[..., None] * n_prev + jnp.einsum('bhl,bhlk->bhk', e_end, kc)
        if not with_out:
            return (c_new, n_new, m_new), None
        logw = jnp.where(tri, b[..., :, None] - b[..., None, :] + li[..., None, :], NEG_BIG)
        w_state = b + m_prev[..., None]
        m_t = jnp.maximum(jnp.max(logw, axis=-1), w_state)
        s = jnp.einsum('bhtk,bhsk->bhts', qc, kc) * jnp.exp(logw - m_t[..., None])
        a_state = jnp.exp(w_state - m_t)
        num = jnp.einsum('bhts,bhsv->bhtv', s, vc) + a_state[..., None] * jnp.einsum(
            'bhtk,bhkv->bhtv', qc, c_prev)
        den = jnp.sum(s, axis=-1) + a_state * jnp.einsum('bhtk,bhk->bht', qc, n_prev)
        h = num / jnp.maximum(jnp.abs(den), jnp.exp(-m_t))[..., None]
        return (c_new, n_new, m_new), h

    state, h = lax.scan(step, state0, xs)
    if not with_out:
        return None, state
    return _from_chunks(h).astype(v.dtype), state


def _diff_attend(q, k, v, lam):
    s = jnp.einsum('bqhcd,bkhcd->bhcqk', q, k).astype(jnp.float32) * (C_DH ** -0.5)
    p = jax.nn.softmax(s, axis=-1)
    a = (p[:, :, 0] - lam * p[:, :, 1]).astype(v.dtype)
    return jnp.einsum('bhqk,bkhv->bqhv', a, v)


def _token_mixer(h_ctx, h_lat, w_in, conv_w, conv_b, gate_b, lb, a_g, b_g, c_g, lam_vec, lam_init,
                 w_branch, w_gate, b_gate, w_out, rope_cos, rope_sin, with_ctx):
    bsz, t_lat, _ = h_lat.shape
    t_ctx = h_ctx.shape[1]
    bounds = np.cumsum(PROJ_WIDTHS)[:-1].tolist()
    pc = jnp.split(h_ctx @ w_in, bounds, axis=-1)
    pl = jnp.split(h_lat @ w_in, bounds, axis=-1)
    directions = ((0, lambda t: t), (1, lambda t: t[:, ::-1]))

    def hgrn_gates(z, lb_d):
        z = _heads(z, A_HEADS).astype(jnp.float32)
        lb_h = lb_d.reshape(A_HEADS, A_DK)
        log_f = jnp.logaddexp(jnp.log(lb_h), jnp.log1p(-lb_h) + jax.nn.log_sigmoid(z))
        k = (1.0 - lb_h) * jax.nn.sigmoid(-z)
        return log_f, k

    qa_c, ia_c = _heads(pc[0], A_HEADS), _heads(pc[1], A_HEADS)
    qa_l, ia_l = _heads(pl[0], A_HEADS), _heads(pl[1], A_HEADS)
    a_lat, a_ctx = 0.0, 0.0
    for d, fl in directions:
        lf_c, k_c = hgrn_gates(pc[3 + d], lb[d])
        lf_l, k_l = hgrn_gates(pl[3 + d], lb[d])
        s0 = jnp.zeros((bsz, A_HEADS, A_DK, A_DV), jnp.float32)
        o_c, s_c = _hgrn2_scan(fl(qa_c), fl(k_c), fl(ia_c), fl(lf_c), s0, with_ctx)
        o_l, _ = _hgrn2_scan(fl(qa_l), fl(k_l), fl(ia_l), fl(lf_l), s_c, True)
        a_lat = a_lat + fl(o_l)
        if with_ctx:
            a_ctx = a_ctx + fl(o_c)
    ya_lat = _head_rms_norm(a_lat.reshape(bsz, t_lat, A_V_W), a_g, A_HEADS) * jax.nn.silu(pl[2])

    def mlstm_inputs(p):
        qk = jax.nn.silu(_centred_dwconv(jnp.concatenate([p[5], p[6]], axis=-1), conv_w, conv_b))
        q, k = jnp.split(qk, 2, axis=-1)
        q = _heads(q, B_HEADS) * (B_DK ** -0.5)
        g = p[9].astype(jnp.float32).reshape(p[9].shape[:-1] + (4, B_HEADS)) + gate_b
        return q, _heads(k, B_HEADS), _heads(p[7], B_HEADS), g

    qb_c, kb_c, vb_c, gb_c = mlstm_inputs(pc)
    qb_l, kb_l, vb_l, gb_l = mlstm_inputs(pl)
    b_lat, b_ctx = 0.0, 0.0
    for d, fl in directions:
        st0 = (jnp.zeros((bsz, B_HEADS, B_DK, B_DV), jnp.float32),
               jnp.zeros((bsz, B_HEADS, B_DK), jnp.float32),
               jnp.zeros((bsz, B_HEADS), jnp.float32))
        h_c, st_c = _mlstm_scan(fl(qb_c), fl(kb_c), fl(vb_c), fl(gb_c[..., d, :]),
                                fl(jax.nn.log_sigmoid(gb_c[..., 2 + d, :])), st0, with_ctx)
        h_l, _ = _mlstm_scan(fl(qb_l), fl(kb_l), fl(vb_l), fl(gb_l[..., d, :]),
                             fl(jax.nn.log_sigmoid(gb_l[..., 2 + d, :])), st_c, True)
        b_lat = b_lat + fl(h_l)
        if with_ctx:
            b_ctx = b_ctx + fl(h_c)
    yb_lat = _head_rms_norm(b_lat.reshape(bsz, t_lat, B_V_W), b_g, B_HEADS) * jax.nn.sigmoid(pl[8])

    lam_vec = lam_vec.astype(jnp.float32)
    lam = jnp.exp(jnp.sum(lam_vec[0] * lam_vec[1])) - jnp.exp(jnp.sum(lam_vec[2] * lam_vec[3])) + lam_init
    qc_c = pc[10].reshape(bsz, t_ctx, C_HEADS, 2, C_DH)
    kc_c = pc[11].reshape(bsz, t_ctx, C_HEADS, 2, C_DH)
    vc_c = pc[12].reshape(bsz, t_ctx, C_HEADS, C_DV)
    qc_l = _axial_rope(pl[10].reshape(bsz, t_lat, C_HEADS, 2, C_DH), rope_cos, rope_sin)
    kc_l = _axial_rope(pl[11].reshape(bsz, t_lat, C_HEADS, 2, C_DH), rope_cos, rope_sin)
    vc_l = pl[12].reshape(bsz, t_lat, C_HEADS, C_DV)
    k_all = jnp.concatenate([kc_c, kc_l], axis=1)
    v_all = jnp.concatenate([vc_c, vc_l], axis=1)
    n_blk = t_lat // Q_BLOCK
    q_blocks = jnp.moveaxis(qc_l.reshape(bsz, n_blk, Q_BLOCK, C_HEADS, 2, C_DH), 1, 0)
    o_l = lax.map(lambda qq: _diff_attend(qq, k_all, v_all, lam), q_blocks)
    o_l = jnp.moveaxis(o_l, 0, 1).reshape(bsz, t_lat, C_V_W)
    yc_lat = _head_rms_norm(o_l, c_g, C_HEADS) * (1.0 - lam_init)

    def merge(h, ya, yb, yc):
        ga, gb, gc = jnp.split(jax.nn.sigmoid(h @ w_gate + b_gate), 3, axis=-1)
        y = ga * (ya @ w_branch[0]) + gb * (yb @ w_branch[1]) + gc * (yc @ w_branch[2])
        return y @ w_out

    y_lat = merge(h_lat, ya_lat, yb_lat, yc_lat)
    if not with_ctx:
        return None, y_lat
    ya_ctx = _head_rms_norm(a_ctx.reshape(bsz, t_ctx, A_V_W), a_g, A_HEADS) * jax.nn.silu(pc[2])
    yb_ctx = _head_rms_norm(b_ctx.reshape(bsz, t_ctx, B_V_W), b_g, B_HEADS) * jax.nn.sigmoid(pc[8])
    o_c = _diff_attend(qc_c, kc_c, vc_c, lam).reshape(bsz, t_ctx, C_V_W)
    yc_ctx = _head_rms_norm(o_c, c_g, C_HEADS) * (1.0 - lam_init)
    return merge(h_ctx, ya_ctx, yb_ctx, yc_ctx), y_lat


def _hier_moe(tokens, rg_w, rg_b, re_w, re_b, w1, w3, w2):
    n_tok, d = tokens.shape
    group_logits = (tokens @ rg_w + rg_b).astype(jnp.float32)
    g_idx = jnp.argmax(group_logits, axis=-1)
    p_group = jnp.take_along_axis(jax.nn.softmax(group_logits, axis=-1), g_idx[:, None], axis=-1)
    exp_logits = (tokens @ re_w + re_b).astype(jnp.float32).reshape(n_tok, N_GROUPS, EXPERTS_PER_GROUP)
    in_group = jnp.take_along_axis(exp_logits, g_idx[:, None, None], axis=1)[:, 0]
    top_p, top_i = lax.top_k(jax.nn.softmax(in_group, axis=-1), TOP_K)
    weights = p_group * top_p / jnp.sum(top_p, axis=-1, keepdims=True)
    expert = g_idx[:, None] * EXPERTS_PER_GROUP + top_i

    n_assign = n_tok * TOP_K
    flat_e = expert.reshape(-1)
    flat_tok = jnp.repeat(jnp.arange(n_tok), TOP_K)
    flat_w = weights.reshape(-1)
    order = jnp.argsort(flat_e)
    e_s, tok_s, w_s = flat_e[order], flat_tok[order], flat_w[order]
    counts = jnp.bincount(flat_e, length=N_EXPERTS)
    start = jnp.cumsum(counts) - counts
    padded = (counts + MOE_BLOCK - 1) // MOE_BLOCK * MOE_BLOCK
    p_end = jnp.cumsum(padded)
    p_start = p_end - padded
    dest = p_start[e_s] + jnp.arange(n_assign) - start[e_s]
    n_blocks = -(-n_assign // MOE_BLOCK) + N_EXPERTS
    x_disp = jnp.zeros((n_blocks * MOE_BLOCK, d), tokens.dtype).at[dest].set(tokens[tok_s])
    block_e = jnp.minimum(jnp.searchsorted(p_end, jnp.arange(n_blocks) * MOE_BLOCK, side='right'),
                          N_EXPERTS - 1)

    def expert_block(args):
        xb, e = args
        return (jax.nn.silu(xb @ w1[e]) * (xb @ w3[e])) @ w2[e]

    y_disp = lax.map(expert_block, (x_disp.reshape(n_blocks, MOE_BLOCK, d), block_e)).reshape(-1, d)
    return jnp.zeros_like(tokens).at[tok_s].add(y_disp[dest] * w_s[:, None].astype(tokens.dtype))


def setup_inputs(seed: int = 0) -> dict:
    key = jax.random.key(seed)
    ks = jax.random.split(key, 32)
    f32 = jnp.float32
    dm = D_MODEL

    def nrm(k, shape, scale):
        return jax.random.normal(k, shape, f32) * scale

    proj_w = sum(PROJ_WIDTHS)
    gate_i = nrm(ks[12], (DEPTH, 2, B_HEADS), 0.1)
    gate_f = jnp.linspace(3.0, 6.0, B_HEADS, dtype=f32) + nrm(ks[13], (DEPTH, 2, B_HEADS), 0.1)
    return {
        'x': nrm(ks[0], (BATCH, SEQ, dm), 1.0),
        'c': nrm(ks[1], (BATCH, dm), 1.0),
        'ctx': nrm(ks[2], (BATCH, CTX_LEN, dm), 1.0),
        'c_ctx': nrm(ks[3], (dm,), 1.0),
        'ada_w': nrm(ks[4], (DEPTH, dm, 6 * dm), 0.5 * dm ** -0.5),
        'ada_b': nrm(ks[5], (DEPTH, 6 * dm), 0.02),
        'norm1_g': 1.0 + nrm(ks[6], (DEPTH, dm), 0.02),
        'norm2_g': 1.0 + nrm(ks[7], (DEPTH, dm), 0.02),
        'w_in': nrm(ks[8], (DEPTH, dm, proj_w), dm ** -0.5),
        'mlstm_conv_w': nrm(ks[9], (DEPTH, B_CONV, 2 * B_QK_W), B_CONV ** -0.5),
        'mlstm_conv_b': nrm(ks[10], (DEPTH, 2 * B_QK_W), 0.02),
        'mlstm_gate_b': jnp.concatenate([gate_i, gate_f], axis=1),
        'hgrn_lb_raw': nrm(ks[11], (DEPTH, 2, A_QK_W), 0.5),
        'hgrn_norm_g': 1.0 + nrm(ks[14], (DEPTH, A_V_W), 0.02),
        'mlstm_norm_g': 1.0 + nrm(ks[15], (DEPTH, B_V_W), 0.02),
        'diff_norm_g': 1.0 + nrm(ks[16], (DEPTH, C_V_W), 0.02),
        'diff_lambda': nrm(ks[17], (DEPTH, 4, C_DH), 0.1),
        'w_branch': nrm(ks[18], (DEPTH, 3, BRANCH_W, dm), BRANCH_W ** -0.5),
        'w_gate': nrm(ks[19], (DEPTH, dm, 3 * dm), dm ** -0.5),
        'b_gate': nrm(ks[20], (DEPTH, 3 * dm), 0.02),
        'w_out': nrm(ks[21], (DEPTH, dm, dm), dm ** -0.5),
        'router_g_w': nrm(ks[22], (DEPTH, dm, N_GROUPS), dm ** -0.5),
        'router_g_b': nrm(ks[23], (DEPTH, N_GROUPS), 0.01),
        'router_e_w': nrm(ks[24], (DEPTH, dm, N_EXPERTS), dm ** -0.5),
        'router_e_b': nrm(ks[25], (DEPTH, N_EXPERTS), 0.01),
        'moe_w1': nrm(ks[26], (DEPTH, N_EXPERTS, dm, D_EXPERT), dm ** -0.5),
        'moe_w3': nrm(ks[27], (DEPTH, N_EXPERTS, dm, D_EXPERT), dm ** -0.5),
        'moe_w2': nrm(ks[28], (DEPTH, N_EXPERTS, D_EXPERT, dm), D_EXPERT ** -0.5),
        'final_g': 1.0 + nrm(ks[29], (dm,), 0.02),
    }


def reference(x, c, ctx, c_ctx, ada_w, ada_b, norm1_g, norm2_g, w_in, mlstm_conv_w, mlstm_conv_b,
              mlstm_gate_b, hgrn_lb_raw, hgrn_norm_g, mlstm_norm_g, diff_norm_g, diff_lambda, w_branch,
              w_gate, b_gate, w_out, router_g_w, router_g_b, router_e_w, router_e_b, moe_w1, moe_w3,
              moe_w2, final_g):
    bsz, t_lat, dm = x.shape
    t_ctx = ctx.shape[1]
    rope_cos, rope_sin = _rope_tables(t_lat)
    lb_cum = jnp.cumsum(jax.nn.softmax(hgrn_lb_raw.astype(jnp.float32), axis=0), axis=0)
    lower_bounds = lb_cum - lb_cum[0]
    x_lat, x_ctx = x, ctx
    for l in range(DEPTH):
        with_ctx = l < DEPTH - 1
        lam_init = 0.8 - 0.6 * math.exp(-0.3 * l)
        mod_lat = jnp.split(jax.nn.silu(c) @ ada_w[l] + ada_b[l], 6, axis=-1)
        sh1, sc1, g1, sh2, sc2, g2 = [m[:, None, :] for m in mod_lat]
        csh1, csc1, cg1, csh2, csc2, cg2 = jnp.split(jax.nn.silu(c_ctx) @ ada_w[l] + ada_b[l], 6, axis=-1)

        h_lat = _rms_norm(x_lat, norm1_g[l]) * (1.0 + sc1) + sh1
        h_ctx = _rms_norm(x_ctx, norm1_g[l]) * (1.0 + csc1) + csh1
        y_ctx, y_lat = _token_mixer(h_ctx, h_lat, w_in[l], mlstm_conv_w[l], mlstm_conv_b[l], mlstm_gate_b[l],
                                    lower_bounds[l], hgrn_norm_g[l], mlstm_norm_g[l], diff_norm_g[l],
                                    diff_lambda[l], lam_init, w_branch[l], w_gate[l], b_gate[l], w_out[l],
                                    rope_cos, rope_sin, with_ctx)
        x_lat = x_lat + g1 * y_lat
        h_lat = _rms_norm(x_lat, norm2_g[l]) * (1.0 + sc2) + sh2
        moe_params = (router_g_w[l], router_g_b[l], router_e_w[l], router_e_b[l],
                      moe_w1[l], moe_w3[l], moe_w2[l])
        if with_ctx:
            x_ctx = x_ctx + cg1 * y_ctx
            h_ctx = _rms_norm(x_ctx, norm2_g[l]) * (1.0 + csc2) + csh2
            tokens = jnp.concatenate([h_ctx.reshape(-1, dm), h_lat.reshape(-1, dm)], axis=0)
            y = _hier_moe(tokens, *moe_params)
            x_ctx = x_ctx + cg2 * y[:bsz * t_ctx].reshape(bsz, t_ctx, dm)
            x_lat = x_lat + g2 * y[bsz * t_ctx:].reshape(bsz, t_lat, dm)
        else:
            x_lat = x_lat + g2 * _hier_moe(h_lat.reshape(-1, dm), *moe_params).reshape(bsz, t_lat, dm)
    return _rms_norm(x_lat, final_g)
```

```python
import functools
import math

import numpy as np
import jax
import jax.numpy as jnp
from jax import lax
from jax.experimental import pallas as pl
from jax.experimental.pallas import tpu as pltpu

F32 = jnp.float32
BF16 = jnp.bfloat16

EPS = 1e-6
NEG_BIG = -1e30
HEADS = 4
HEAD_W = 128
MIX_W = HEADS * HEAD_W
GRID_W = 64
ROPE_BASE = 10000.0
ROPE_PAIRS = 16
B_CONV = 3
N_GROUPS = 4
EXPERTS_PER_GROUP = 8
N_EXPERTS = N_GROUPS * EXPERTS_PER_GROUP
D_EXPERT = 512
MOE_BLOCK = 128
LANES = 128
VMEM_LIMIT = 50 * 1024 * 1024

PA_W = 5 * MIX_W
PB_W = 4 * MIX_W
PG_W = LANES
PC_W = 3 * MIX_W
PROJ_W = PA_W + PB_W + PG_W + PC_W


def _params(sem, **kw):
    return pltpu.CompilerParams(dimension_semantics=sem, vmem_limit_bytes=VMEM_LIMIT, **kw)


def _const_spec(shape):
    nd = len(shape)
    return pl.BlockSpec(shape, lambda *_: (0,) * nd, pipeline_mode=pl.Buffered(1))


def _dot(a, b):
    return jnp.dot(a, b, preferred_element_type=F32)


def _dot_nt(a, b):
    return lax.dot_general(a, b, (((1,), (1,)), ((), ())), preferred_element_type=F32)


def _dot_tn(a, b):
    return lax.dot_general(a, b, (((0,), (0,)), ((), ())), preferred_element_type=F32)


def _dot_sel(m_bf16, x):
    hi = x.astype(BF16)
    r1 = x - hi.astype(F32)
    mid = r1.astype(BF16)
    lo = (r1 - mid.astype(F32)).astype(BF16)
    return _dot(m_bf16, hi) + _dot(m_bf16, mid) + _dot(m_bf16, lo)


def _sigmoid(x):
    return 1.0 / (1.0 + jnp.exp(-x))


def _log_sigmoid(x):
    return jnp.minimum(x, 0.0) - jnp.log1p(jnp.exp(-jnp.abs(x)))


def _rms(x, g):
    return x * lax.rsqrt(jnp.mean(x * x, axis=-1, keepdims=True) + EPS) * g


def _ada_kernel(c_ref, w_ref, b_ref, o_ref):
    c = c_ref[...]
    s = c * _sigmoid(c)
    o_ref[...] = jnp.dot(s, w_ref[...], preferred_element_type=F32,
                         precision=lax.Precision.HIGHEST) + b_ref[...]


def _ada_mod(cc, ada_w, ada_b):
    depth, dm, six = ada_w.shape
    rows = cc.shape[0]
    tn = dm
    return pl.pallas_call(
        _ada_kernel,
        grid=(depth, six // tn),
        in_specs=[pl.BlockSpec((rows, dm), lambda l, n: (0, 0)),
                  pl.BlockSpec((None, dm, tn), lambda l, n: (l, 0, n)),
                  pl.BlockSpec((None, 1, tn), lambda l, n: (l, 0, n))],
        out_specs=pl.BlockSpec((None, rows, tn), lambda l, n: (l, 0, n)),
        out_shape=jax.ShapeDtypeStruct((depth, rows, six), F32),
        compiler_params=_params(("parallel", "parallel")),
        name="ada_mod",
    )(cc, ada_w, ada_b.reshape(depth, 1, six))


def _rope(x, cos, sin):
    n = x.shape[-1]
    lane = lax.broadcasted_iota(jnp.int32, (1, n), 1)
    first = (lane // ROPE_PAIRS) % 2 == 0
    partner = jnp.where(first, pltpu.roll(x, n - ROPE_PAIRS, 1), pltpu.roll(x, ROPE_PAIRS, 1))
    return x * cos + partner * sin


def _norm_proj_kernel(x_ref, mod_ref, g_ref, w_ref, cos_ref, sin_ref,
                      pa_ref, pb_ref, pg_ref, qc_ref, kc_ref, vc_ref):
    x = x_ref[...]
    mod = mod_ref[...]
    h = _rms(x, g_ref[...]) * (1.0 + mod[1:2]) + mod[0:1]
    hb = h.astype(BF16)
    pa_ref[...] = _dot(hb, w_ref[:, 0:PA_W])
    pb_ref[...] = _dot(hb, w_ref[:, PA_W:PA_W + PB_W])
    pg_ref[...] = _dot(hb, w_ref[:, PA_W + PB_W:PA_W + PB_W + PG_W])
    c0 = PA_W + PB_W + PG_W
    cos = cos_ref[...]
    sin = sin_ref[...]
    q = _dot(hb, w_ref[:, c0:c0 + MIX_W])
    qc_ref[...] = (_rope(q, cos, sin) * (64.0 ** -0.5)).astype(BF16)
    k = _dot(hb, w_ref[:, c0 + MIX_W:c0 + 2 * MIX_W])
    kc_ref[...] = _rope(k, cos, sin).astype(BF16)
    vc_ref[...] = _dot(hb, w_ref[:, c0 + 2 * MIX_W:c0 + 3 * MIX_W]).astype(BF16)


def _norm_proj(x, mod, g, w, cos, sin, tm, ctx_tiles):
    bsz, tt, dm = x.shape
    nt = tt // tm
    row = lambda b, t: (b, t, 0)
    outs = [(PA_W, F32), (PB_W, F32), (PG_W, F32), (MIX_W, BF16), (MIX_W, BF16), (MIX_W, BF16)]
    return pl.pallas_call(
        _norm_proj_kernel,
        grid=(bsz, nt),
        in_specs=[pl.BlockSpec((None, tm, dm), row),
                  pl.BlockSpec((None, None, 6, dm), lambda b, t: (b, jnp.where(t >= ctx_tiles, 1, 0), 0, 0)),
                  _const_spec((1, dm)),
                  _const_spec((dm, PROJ_W)),
                  pl.BlockSpec((tm, MIX_W), lambda b, t: (t, 0)),
                  pl.BlockSpec((tm, MIX_W), lambda b, t: (t, 0))],
        out_specs=[pl.BlockSpec((None, tm, wd), row) for wd, _ in outs],
        out_shape=[jax.ShapeDtypeStruct((bsz, tt, wd), dt) for wd, dt in outs],
        compiler_params=_params(("parallel", "parallel")),
        name="norm_proj",
    )(x, mod, g, w, cos, sin)


def _conv_kernel(x_ref, w_ref, b_ref, o_ref, *, ctx, q_blocks):
    x = x_ref[...]
    tt = x.shape[0]
    row = lax.broadcasted_iota(jnp.int32, (tt, 1), 0)
    prev = jnp.where((row == 0) | (row == ctx), 0.0, pltpu.roll(x, 1, 0))
    nxt = jnp.where((row == ctx - 1) | (row == tt - 1), 0.0, pltpu.roll(x, tt - 1, 0))
    w = w_ref[...]
    y = b_ref[...] + prev * w[0:1] + x * w[1:2] + nxt * w[2:3]
    y = y * _sigmoid(y)
    scale = jnp.where(pl.program_id(1) < q_blocks, HEAD_W ** -0.5, 1.0)
    o_ref[...] = y * scale


def _conv_silu(pb, conv_w, conv_b, ctx, cb):
    bsz, tt, _ = pb.shape
    width = 2 * MIX_W
    return pl.pallas_call(
        functools.partial(_conv_kernel, ctx=ctx, q_blocks=MIX_W // cb),
        grid=(bsz, width // cb),
        in_specs=[pl.BlockSpec((None, tt, cb), lambda b, j: (b, 0, j)),
                  pl.BlockSpec((B_CONV, cb), lambda b, j: (0, j)),
                  pl.BlockSpec((1, cb), lambda b, j: (0, j))],
        out_specs=pl.BlockSpec((None, tt, cb), lambda b, j: (b, 0, j)),
        out_shape=jax.ShapeDtypeStruct((bsz, tt, width), F32),
        compiler_params=_params(("parallel", "parallel")),
        name="mlstm_conv",
    )(pb, conv_w, conv_b.reshape(1, width))


def _chunk_consts(L):
    idx = np.arange(L)
    t, u = idx[:, None], idx[None, :]
    mats = [(u <= t), (u > t)]
    nlev = int(round(math.log2(L)))
    lvl = np.full((L, L), -1, np.int32)
    lvl[idx, idx] = nlev
    for i in range(nlev):
        h = L >> (i + 1)
        mid = (t // (2 * h)) * (2 * h) + h - 1
        mats.append(((u > mid) & (u <= t)) | ((u > t) & (u <= mid)))
        same = (t // (2 * h)) == (u // (2 * h))
        lvl[same & (t % (2 * h) >= h) & (u % (2 * h) < h)] = i
    sel_f = np.concatenate([m.astype(np.float32) for m in mats], axis=0)
    sel_b = np.concatenate([m[::-1, ::-1].astype(np.float32) for m in mats], axis=0)
    sel = jnp.asarray(np.stack([sel_f, sel_b]), BF16)
    lvls = jnp.asarray(np.stack([lvl, lvl[::-1, ::-1]]))
    return sel, lvls, nlev


def _scan_blocks(j, nc_ctx, nc):
    jb = jnp.where(j < nc_ctx, nc_ctx - 1 - j, nc - 1 + nc_ctx - j)
    return j, jb


def _hgrn_kernel(qf_ref, vf_ref, ff_ref, qb_ref, vb_ref, fb_ref, lbc_ref, sel_ref, lvl_ref,
                 of_ref, ob_ref, st_ref, *, L, nlev):
    @pl.when(pl.program_id(1) == 0)
    def _():
        st_ref[...] = jnp.zeros_like(st_ref)

    dirs = ((qf_ref, vf_ref, ff_ref, of_ref), (qb_ref, vb_ref, fb_ref, ob_ref))
    for d, (q_ref, v_ref, f_ref, o_ref) in enumerate(dirs):
        z = f_ref[...]
        lbc = lbc_ref[d]
        e = jnp.exp(-jnp.abs(z))
        lsig = jnp.minimum(z, 0.0) - jnp.log1p(e)
        a = lbc[0:1]
        bb = lbc[1:2] + lsig
        lf = jnp.maximum(a, bb) + jnp.log1p(jnp.exp(-jnp.abs(a - bb)))
        kk = lbc[2:3] * jnp.where(z > 0, e, 1.0) / (1.0 + e)
        ex = _dot_sel(sel_ref[d], lf)
        lvl = lvl_ref[d]
        last = L - 1 if d == 0 else 0
        for hd in range(HEADS):
            cs = slice(hd * HEAD_W, (hd + 1) * HEAD_W)
            q = q_ref[:, cs]
            k = kk[:, cs]
            vb = v_ref[:, cs].astype(BF16)
            b_in = ex[0:L, cs]
            b_out = ex[L:2 * L, cs]
            b_end = b_in[last:last + 1]
            scores = jnp.where(lvl == nlev, _dot_nt(q.astype(BF16), k.astype(BF16)), 0.0)
            for i in range(nlev):
                ei = jnp.exp(ex[(2 + i) * L:(3 + i) * L, cs])
                si = _dot_nt((q * ei).astype(BF16), (k * ei).astype(BF16))
                scores = scores + jnp.where(lvl == i, si, 0.0)
            st = st_ref[d, hd]
            o = _dot(scores.astype(BF16), vb) + _dot_nt((q * jnp.exp(b_in)).astype(BF16), st.astype(BF16))
            o_ref[:, cs] = o
            st_ref[d, hd] = st * jnp.exp(b_end) + _dot_tn(vb, (k * jnp.exp(b_out)).astype(BF16))


def _hgrn(pa, lbc, L, nc_ctx):
    bsz, tt, _ = pa.shape
    nc = tt // L
    sel, lvl, nlev = _chunk_consts(L)

    def spec(col, which):
        return pl.BlockSpec((None, L, MIX_W), lambda b, j: (b, _scan_blocks(j, nc_ctx, nc)[which], col))

    return pl.pallas_call(
        functools.partial(_hgrn_kernel, L=L, nlev=nlev),
        grid=(bsz, nc),
        in_specs=[spec(0, 0), spec(1, 0), spec(3, 0), spec(0, 1), spec(1, 1), spec(4, 1),
                  _const_spec(lbc.shape), _const_spec(sel.shape), _const_spec(lvl.shape)],
        out_specs=[spec(0, 0), spec(0, 1)],
        out_shape=[jax.ShapeDtypeStruct((bsz, tt, MIX_W), F32)] * 2,
        scratch_shapes=[pltpu.VMEM((2, HEADS, HEAD_W, HEAD_W), F32)],
        compiler_params=_params(("parallel", "arbitrary")),
        name="hgrn2_scan",
    )(pa, pa, pa, pa, pa, pa, lbc, sel, lvl)


def _mlstm_kernel(qf_ref, kf_ref, vf_ref, gf_ref, qb_ref, kb_ref, vb_ref, gb_ref, gbias_ref,
                  sel_ref, lvl_ref, of_ref, ob_ref, c_ref, m_ref, *, L, nlev):
    @pl.when(pl.program_id(1) == 0)
    def _():
        c_ref[...] = jnp.zeros_like(c_ref)
        m_ref[...] = jnp.zeros_like(m_ref)

    lane = lax.broadcasted_iota(jnp.int32, (1, LANES), 1)
    ones_col = jnp.where(lane == 0, 1.0, 0.0).astype(BF16)
    dirs = ((qf_ref, kf_ref, vf_ref, gf_ref, of_ref), (qb_ref, kb_ref, vb_ref, gb_ref, ob_ref))
    for d, (q_ref, k_ref, v_ref, g_ref, o_ref) in enumerate(dirs):
        g = g_ref[...] + gbias_ref[...]
        lf = jnp.where((lane >= 2 * HEADS) & (lane < 4 * HEADS), _log_sigmoid(g), 0.0)
        bcum = _dot_sel(sel_ref[d, 0:L], lf)
        causal = lvl_ref[d] >= 0
        last = L - 1 if d == 0 else 0
        for hd in range(HEADS):
            cs = slice(hd * HEAD_W, (hd + 1) * HEAD_W)
            li_lane = d * HEADS + hd
            lf_lane = 2 * HEADS + d * HEADS + hd
            li = g[:, li_lane:li_lane + 1]
            b = bcum[:, lf_lane:lf_lane + 1]
            row_term = jnp.broadcast_to(li - b, (L, LANES)).T[0:1, :]
            b_end = b[last:last + 1]
            r = d * HEADS + hd
            m_prev = m_ref[r:r + 1, 0:1]
            q = q_ref[:, cs].astype(BF16)
            k = k_ref[:, cs]
            v_aug = jnp.concatenate([v_ref[:, cs].astype(BF16), jnp.broadcast_to(ones_col, (L, LANES))], axis=1)
            c_prev = c_ref[r]
            w_end = b_end - b + li
            m_new = jnp.maximum(b_end + m_prev, jnp.max(w_end, axis=0, keepdims=True))
            e_end = jnp.exp(w_end - m_new)
            keep = jnp.exp(b_end + m_prev - m_new)
            c_ref[r] = keep * c_prev + _dot_tn((k * e_end).astype(BF16), v_aug)
            m_ref[r:r + 1, :] = jnp.broadcast_to(m_new, (1, LANES))
            logw = jnp.where(causal, b + row_term, NEG_BIG)
            w_state = b + m_prev
            m_t = jnp.maximum(jnp.max(logw, axis=-1, keepdims=True), w_state)
            s = _dot_nt(q, k.astype(BF16)) * jnp.exp(logw - m_t)
            a_state = jnp.exp(w_state - m_t)
            num = _dot(s.astype(BF16), v_aug) + a_state * _dot(q, c_prev.astype(BF16))
            den = num[:, HEAD_W:HEAD_W + 1]
            o_ref[:, cs] = num[:, 0:HEAD_W] / jnp.maximum(jnp.abs(den), jnp.exp(-m_t))


def _mlstm(qk, pb, pg, gbias, L, nc_ctx):
    bsz, tt, _ = pb.shape
    nc = tt // L
    sel, lvl, nlev = _chunk_consts(L)

    def spec(col, which, width=MIX_W):
        return pl.BlockSpec((None, L, width), lambda b, j: (b, _scan_blocks(j, nc_ctx, nc)[which], col))

    return pl.pallas_call(
        functools.partial(_mlstm_kernel, L=L, nlev=nlev),
        grid=(bsz, nc),
        in_specs=[spec(0, 0), spec(1, 0), spec(2, 0), spec(0, 0, PG_W),
                  spec(0, 1), spec(1, 1), spec(2, 1), spec(0, 1, PG_W),
                  _const_spec((1, PG_W)), _const_spec(sel.shape), _const_spec(lvl.shape)],
        out_specs=[spec(0, 0), spec(0, 1)],
        out_shape=[jax.ShapeDtypeStruct((bsz, tt, MIX_W), F32)] * 2,
        scratch_shapes=[pltpu.VMEM((2 * HEADS, HEAD_W, 2 * HEAD_W), F32),
                        pltpu.VMEM((2 * HEADS, LANES), F32)],
        compiler_params=_params(("parallel", "arbitrary")),
        name="mlstm_scan",
    )(qk, qk, pb, pg, qk, qk, pb, pg, gbias, sel, lvl)


def _attn_kernel(lam_ref, q_ref, k_ref, v_ref, o_ref, *, lam_init, ctx_tiles, t_ctx, with_ctx):
    lv = lam_ref[...]
    lam = (jnp.exp(jnp.sum(lv[0:1] * lv[1:2], axis=-1, keepdims=True))
           - jnp.exp(jnp.sum(lv[2:3] * lv[3:4], axis=-1, keepdims=True)) + lam_init)
    lane = lax.broadcasted_iota(jnp.int32, (1, HEAD_W), 1)

    def attend(n_k):
        q = q_ref[...]
        k = k_ref[0:n_k, :]
        zero = jnp.zeros_like(q)

        def probs(qm):
            s = _dot_nt(qm, k)
            p = jnp.exp(s - jnp.max(s, axis=-1, keepdims=True))
            return p / jnp.sum(p, axis=-1, keepdims=True)

        a = probs(jnp.where(lane < HEAD_W // 2, q, zero)) - lam * probs(jnp.where(lane >= HEAD_W // 2, q, zero))
        o_ref[...] = _dot(a.astype(BF16), v_ref[0:n_k, :])

    is_ctx = pl.program_id(2) < ctx_tiles

    @pl.when(is_ctx)
    def _():
        if with_ctx:
            attend(t_ctx)
        else:
            o_ref[...] = jnp.zeros_like(o_ref)

    @pl.when(jnp.logical_not(is_ctx))
    def _():
        attend(k_ref.shape[0])


def _attn(lam_vec, qc, kc, vc, lam_init, tq, t_ctx, with_ctx):
    bsz, tt, _ = qc.shape
    kv = pl.BlockSpec((None, tt, HEAD_W), lambda b, h, i: (b, 0, h))
    return pl.pallas_call(
        functools.partial(_attn_kernel, lam_init=lam_init, ctx_tiles=t_ctx // tq, t_ctx=t_ctx, with_ctx=with_ctx),
        grid=(bsz, HEADS, tt // tq),
        in_specs=[_const_spec(lam_vec.shape),
                  pl.BlockSpec((None, tq, HEAD_W), lambda b, h, i: (b, i, h)), kv, kv],
        out_specs=pl.BlockSpec((None, tq, HEAD_W), lambda b, h, i: (b, i, h)),
        out_shape=jax.ShapeDtypeStruct((bsz, tt, MIX_W), F32),
        compiler_params=_params(("parallel", "parallel", "arbitrary")),
        name="diff_attn",
    )(lam_vec, qc, kc, vc)


def _head_norm(x, g):
    parts = []
    for hd in range(HEADS):
        xs = x[:, hd * HEAD_W:(hd + 1) * HEAD_W]
        parts.append(xs * lax.rsqrt(jnp.mean(xs * xs, axis=-1, keepdims=True) + EPS))
    return jnp.concatenate(parts, axis=1) * g


def _route(logits):
    lane = lax.broadcasted_iota(jnp.int32, logits.shape, 1)
    neg = -jnp.inf
    gl = jnp.where(lane < N_GROUPS, logits, neg)
    gmax = jnp.max(gl, axis=-1, keepdims=True)
    gidx = jnp.min(jnp.where(gl == gmax, lane, LANES), axis=-1, keepdims=True)
    p_group = 1.0 / jnp.sum(jnp.exp(gl - gmax), axis=-1, keepdims=True)
    lo = N_GROUPS + EXPERTS_PER_GROUP * gidx
    in_grp = (lane >= lo) & (lane < lo + EXPERTS_PER_GROUP)
    el = jnp.where(in_grp, logits, neg)
    pe = jnp.exp(el - jnp.max(el, axis=-1, keepdims=True))
    pe = pe / jnp.sum(pe, axis=-1, keepdims=True)
    pe = jnp.where(in_grp, pe, -1.0)
    v1 = jnp.max(pe, axis=-1, keepdims=True)
    i1 = jnp.min(jnp.where(pe == v1, lane, LANES), axis=-1, keepdims=True)
    pe2 = jnp.where(lane == i1, -1.0, pe)
    v2 = jnp.max(pe2, axis=-1, keepdims=True)
    i2 = jnp.min(jnp.where(pe2 == v2, lane, LANES), axis=-1, keepdims=True)
    scale = p_group / (v1 + v2)
    e1 = (i1 - N_GROUPS).astype(F32)
    e2 = (i2 - N_GROUPS).astype(F32)
    return jnp.where(lane == 0, e1, jnp.where(lane == 1, e2, jnp.where(lane == 2, v1 * scale,
                     jnp.where(lane == 3, v2 * scale, 0.0))))


def _merge_kernel(x_ref, mod_ref, n1_ref, n2_ref, af_ref, ab_ref, ga_ref, bf_ref, bb_ref, gb_ref, oc_ref,
                  hg_ref, wg_ref, bg_ref, wbr_ref, wo_ref, rw_ref, rb_ref,
                  xo_ref, h2_ref, rt_ref, *, lam_init):
    x = x_ref[...]
    mod = mod_ref[...]
    dm = x.shape[-1]
    hb = (_rms(x, n1_ref[...]) * (1.0 + mod[1:2]) + mod[0:1]).astype(BF16)
    hg = hg_ref[...]
    ga = ga_ref[...]
    ya = _head_norm(af_ref[...] + ab_ref[...], hg[0:1]) * (ga * _sigmoid(ga))
    yb = _head_norm(bf_ref[...] + bb_ref[...], hg[1:2]) * _sigmoid(gb_ref[...])
    yc = _head_norm(oc_ref[...], hg[2:3]) * (1.0 - lam_init)
    y = jnp.zeros_like(x)
    for i, yi in enumerate((ya, yb, yc)):
        gate = _sigmoid(_dot(hb, wg_ref[:, i * dm:(i + 1) * dm]) + bg_ref[:, i * dm:(i + 1) * dm])
        y = y + gate * _dot(yi.astype(BF16), wbr_ref[i])
    xn = x + mod[2:3] * _dot(y.astype(BF16), wo_ref[...])
    xo_ref[...] = xn
    h2 = _rms(xn, n2_ref[...]) * (1.0 + mod[4:5]) + mod[3:4]
    h2_ref[...] = h2
    logits = jnp.dot(h2, rw_ref[...], preferred_element_type=F32, precision=lax.Precision.HIGHEST) + rb_ref[...]
    rt_ref[...] = _route(logits)


def _merge(x, mod, n1, n2, oaf, oab, pa, obf, obb, pb, oc, hg, wg, bg, wbr, wo, rw, rb,
           lam_init, tm, ctx_tiles, row0_tiles):
    bsz, tt, dm = x.shape
    nt = tt // tm - row0_tiles
    rows = nt * tm
    src = lambda b, t: (b, row0_tiles + t, 0)

    def col(c):
        return pl.BlockSpec((None, tm, MIX_W), lambda b, t: (b, row0_tiles + t, c))

    dst = lambda b, t: (b, t, 0)
    outs = [(dm, F32), (dm, F32), (LANES, F32)]
    return pl.pallas_call(
        functools.partial(_merge_kernel, lam_init=lam_init),
        grid=(bsz, nt),
        in_specs=[pl.BlockSpec((None, tm, dm), src),
                  pl.BlockSpec((None, None, 6, dm),
                               lambda b, t: (b, jnp.where(row0_tiles + t >= ctx_tiles, 1, 0), 0, 0)),
                  _const_spec((1, dm)), _const_spec((1, dm)),
                  col(0), col(0), col(2), col(0), col(0), col(3), col(0),
                  _const_spec((3, MIX_W)), _const_spec(wg.shape), _const_spec(bg.shape),
                  _const_spec(wbr.shape), _const_spec(wo.shape), _const_spec(rw.shape), _const_spec(rb.shape)],
        out_specs=[pl.BlockSpec((None, tm, wd), dst) for wd, _ in outs],
        out_shape=[jax.ShapeDtypeStruct((bsz, rows, wd), dt) for wd, dt in outs],
        compiler_params=_params(("parallel", "parallel")),
        name="merge_route",
    )(x, mod, n1, n2, oaf, oab, pa, obf, obb, pb, oc, hg, wg, bg, wbr, wo, rw, rb)


def _expert_kernel(be_ref, nv_ref, src_ref, dst_ref, tok_ref, w1_ref, w3_ref, w2_ref, out_ref,
                   xbuf, ybuf, sems):
    i = pl.program_id(0)
    nvalid = nv_ref[i]

    @pl.when(nvalid > 0)
    def _():
        base = i * MOE_BLOCK

        def row_in(r):
            return pltpu.make_async_copy(tok_ref.at[pl.ds(src_ref[base + r], 1)], xbuf.at[pl.ds(r, 1)], sems.at[0])

        def row_out(r):
            return pltpu.make_async_copy(ybuf.at[pl.ds(r, 1)], out_ref.at[pl.ds(dst_ref[base + r], 1)], sems.at[1])

        def start_in(r, c):
            row_in(r).start()
            return c

        def wait_in(r, c):
            row_in(r).wait()
            return c

        lax.fori_loop(0, MOE_BLOCK, start_in, 0)
        lax.fori_loop(0, MOE_BLOCK, wait_in, 0)
        xb = xbuf[...].astype(BF16)
        u = _dot(xb, w1_ref[...])
        hmid = (u * _sigmoid(u)) * _dot(xb, w3_ref[...])
        ybuf[...] = _dot(hmid.astype(BF16), w2_ref[...])

        def start_out(r, c):
            row_out(r).start()
            return c

        def wait_out(r, c):
            row_out(r).wait()
            return c

        lax.fori_loop(0, nvalid, start_out, 0)
        lax.fori_loop(0, nvalid, wait_out, 0)


def _experts(block_e, nvalid, src, dst, tokens, w1, w3, w2, n_out_rows):
    n_tok, dm = tokens.shape
    n_blocks = block_e.shape[0]
    grid_spec = pltpu.PrefetchScalarGridSpec(
        num_scalar_prefetch=4,
        grid=(n_blocks,),
        in_specs=[pl.BlockSpec(memory_space=pl.ANY),
                  pl.BlockSpec((None, dm, D_EXPERT), lambda i, be, nv, s, d: (be[i], 0, 0)),
                  pl.BlockSpec((None, dm, D_EXPERT), lambda i, be, nv, s, d: (be[i], 0, 0)),
                  pl.BlockSpec((None, D_EXPERT, dm), lambda i, be, nv, s, d: (be[i], 0, 0))],
        out_specs=pl.BlockSpec(memory_space=pl.ANY),
        scratch_shapes=[pltpu.VMEM((MOE_BLOCK, dm), F32), pltpu.VMEM((MOE_BLOCK, dm), F32),
                        pltpu.SemaphoreType.DMA((2,))],
    )
    return pl.pallas_call(
        _expert_kernel,
        grid_spec=grid_spec,
        out_shape=jax.ShapeDtypeStruct((n_out_rows, dm), F32),
        compiler_params=_params(("arbitrary",), has_side_effects=True),
        name="moe_experts",
    )(block_e, nvalid, src, dst, tokens, w1, w3, w2)


def _dispatch(route, n_tok):
    n_assign = 2 * n_tok
    flat_e = route[:, 0:2].astype(jnp.int32).reshape(-1)
    onehot = (flat_e[:, None] == jnp.arange(N_EXPERTS)[None, :]).astype(jnp.int32)
    csum = jnp.cumsum(onehot, axis=0)
    counts = csum[-1]
    rank = jnp.sum(csum * onehot, axis=1) - 1
    padded = (counts + MOE_BLOCK - 1) // MOE_BLOCK * MOE_BLOCK
    p_end = jnp.cumsum(padded)
    p_start = p_end - padded
    dest = p_start[flat_e] + rank
    n_blocks = -(-n_assign // MOE_BLOCK) + N_EXPERTS
    n_slots = n_blocks * MOE_BLOCK
    a = jnp.arange(n_assign, dtype=jnp.int32)
    src = jnp.zeros((n_slots,), jnp.int32).at[dest].set(a // 2)
    dst = jnp.zeros((n_slots,), jnp.int32).at[dest].set((a % 2) * n_tok + a // 2)
    blk_start = jnp.arange(n_blocks, dtype=jnp.int32) * MOE_BLOCK
    block_e = jnp.minimum(jnp.searchsorted(p_end, blk_start, side='right'), N_EXPERTS - 1).astype(jnp.int32)
    used_end = (p_start + counts)[block_e]
    nvalid = jnp.clip(used_end - blk_start, 0, MOE_BLOCK).astype(jnp.int32)
    nvalid = jnp.where(blk_start < p_end[-1], nvalid, 0)
    return block_e, nvalid, src, dst


def _combine_kernel(x_ref, mod_ref, rt_ref, y0_ref, y1_ref, fg_ref, o_ref, *, final):
    rt = rt_ref[...]
    y = rt[:, 2:3] * y0_ref[...] + rt[:, 3:4] * y1_ref[...]
    xn = x_ref[...] + mod_ref[5:6, :] * y
    o_ref[...] = _rms(xn, fg_ref[...]) if final else xn


def _combine(x, mod, route, ybuf, fg, final, tm, ctx_tiles, row0_tiles):
    bsz, rows, dm = x.shape
    nt = rows // tm
    n_tok = bsz * rows
    tok_blk = lambda b, t: (b * nt + t, 0)
    return pl.pallas_call(
        functools.partial(_combine_kernel, final=final),
        grid=(bsz, nt),
        in_specs=[pl.BlockSpec((None, tm, dm), lambda b, t: (b, t, 0)),
                  pl.BlockSpec((None, None, 6, dm),
                               lambda b, t: (b, jnp.where(row0_tiles + t >= ctx_tiles, 1, 0), 0, 0)),
                  pl.BlockSpec((None, tm, LANES), lambda b, t: (b, t, 0)),
                  pl.BlockSpec((tm, dm), tok_blk),
                  pl.BlockSpec((tm, dm), lambda b, t: (n_tok // tm + b * nt + t, 0)),
                  _const_spec((1, dm))],
        out_specs=pl.BlockSpec((None, tm, dm), lambda b, t: (b, t, 0)),
        out_shape=jax.ShapeDtypeStruct((bsz, rows, dm), F32),
        compiler_params=_params(("parallel", "parallel")),
        name="moe_combine",
    )(x, mod, route, ybuf, ybuf, fg)


def _rope_tables(ctx, t_lat):
    rows = t_lat // GRID_W
    row = jnp.repeat(jnp.arange(rows, dtype=F32), GRID_W)
    col = jnp.tile(jnp.arange(GRID_W, dtype=F32), rows)
    inv = ROPE_BASE ** (-jnp.arange(ROPE_PAIRS, dtype=F32) / ROPE_PAIRS)
    ang_r = row[:, None] * inv
    ang_c = col[:, None] * inv
    cos64 = jnp.concatenate([jnp.cos(ang_r), jnp.cos(ang_r), jnp.cos(ang_c), jnp.cos(ang_c)], axis=1)
    sin64 = jnp.concatenate([-jnp.sin(ang_r), jnp.sin(ang_r), -jnp.sin(ang_c), jnp.sin(ang_c)], axis=1)
    cos = jnp.tile(cos64, (1, MIX_W // 64))
    sin = jnp.tile(sin64, (1, MIX_W // 64))
    cos = jnp.concatenate([jnp.ones((ctx, MIX_W), F32), cos], axis=0)
    sin = jnp.concatenate([jnp.zeros((ctx, MIX_W), F32), sin], axis=0)
    return cos, sin


def _pack_w_in(w):
    dm = w.shape[0]
    a_end = PA_W
    b_end = a_end + PB_W
    g_end = b_end + 4 * HEADS
    pad = jnp.zeros((dm, PG_W - 4 * HEADS), w.dtype)
    return jnp.concatenate([w[:, :b_end], w[:, b_end:g_end], pad, w[:, g_end:]], axis=1).astype(BF16)


def kernel(x, c, ctx, c_ctx, ada_w, ada_b, norm1_g, norm2_g, w_in, mlstm_conv_w, mlstm_conv_b, mlstm_gate_b,
           hgrn_lb_raw, hgrn_norm_g, mlstm_norm_g, diff_norm_g, diff_lambda, w_branch, w_gate, b_gate, w_out,
           router_g_w, router_g_b, router_e_w, router_e_b, moe_w1, moe_w3, moe_w2, final_g):
    bsz, t_lat, dm = x.shape
    t_ctx = ctx.shape[1]
    depth = ada_w.shape[0]
    tt = t_ctx + t_lat
    tm = min(256, t_ctx)
    chunk = 64
    assert t_ctx % tm == 0 and t_lat % tm == 0 and t_ctx % chunk == 0 and t_lat % GRID_W == 0
    ctx_tiles = t_ctx // tm

    n_rows = -(-(bsz + 1) // 8) * 8
    cc = jnp.zeros((n_rows, dm), F32).at[:bsz].set(c).at[bsz].set(c_ctx)
    mods = _ada_mod(cc, ada_w, ada_b).reshape(depth, n_rows, 6, dm)

    cos, sin = _rope_tables(t_ctx, t_lat)
    lb_cum = jnp.cumsum(jax.nn.softmax(hgrn_lb_raw.astype(F32), axis=0), axis=0)
    lower = lb_cum - lb_cum[0]

    xc = jnp.concatenate([ctx, x], axis=1)
    out = None
    for l in range(depth):
        with_ctx = l < depth - 1
        lam_init = 0.8 - 0.6 * math.exp(-0.3 * l)
        mod = jnp.stack([jnp.broadcast_to(mods[l, bsz], (bsz, 6, dm)), mods[l, :bsz]], axis=1)
        n1 = norm1_g[l].reshape(1, dm)
        n2 = norm2_g[l].reshape(1, dm)

        pa, pb, pg, qc, kc, vc = _norm_proj(xc, mod, n1, _pack_w_in(w_in[l]), cos, sin, tm, ctx_tiles)

        lb = lower[l]
        pad = jnp.zeros((2, 5, MIX_W), F32)
        lbc = jnp.concatenate([jnp.log(lb)[:, None], jnp.log1p(-lb)[:, None], (1.0 - lb)[:, None], pad], axis=1)
        oaf, oab = _hgrn(pa, lbc, chunk, t_ctx // chunk)

        qk = _conv_silu(pb, mlstm_conv_w[l], mlstm_conv_b[l], t_ctx, 256)
        gbias = jnp.zeros((1, PG_W), F32).at[0, :4 * HEADS].set(mlstm_gate_b[l].reshape(-1))
        obf, obb = _mlstm(qk, pb, pg, gbias, chunk, t_ctx // chunk)

        lam_vec = diff_lambda[l].astype(F32)
        oc = _attn(lam_vec, qc, kc, vc, lam_init, tm, t_ctx, with_ctx)

        hg = jnp.stack([hgrn_norm_g[l], mlstm_norm_g[l], diff_norm_g[l]])
        rw = jnp.zeros((dm, LANES), F32).at[:, :N_GROUPS].set(router_g_w[l])
        rw = rw.at[:, N_GROUPS:N_GROUPS + N_EXPERTS].set(router_e_w[l])
        rb = jnp.zeros((1, LANES), F32).at[0, :N_GROUPS].set(router_g_b[l])
        rb = rb.at[0, N_GROUPS:N_GROUPS + N_EXPERTS].set(router_e_b[l])
        row0_tiles = 0 if with_ctx else ctx_tiles
        xn, h2, route = _merge(xc, mod, n1, n2, oaf, oab, pa, obf, obb, pb, oc, hg,
                               w_gate[l].astype(BF16), b_gate[l].reshape(1, 3 * dm), w_branch[l].astype(BF16),
                               w_out[l].astype(BF16), rw, rb, lam_init, tm, ctx_tiles, row0_tiles)

        rows = xn.shape[1]
        n_tok = bsz * rows
        block_e, nvalid, src, dst = _dispatch(route.reshape(n_tok, LANES), n_tok)
        ybuf = _experts(block_e, nvalid, src, dst, h2.reshape(n_tok, dm), moe_w1[l].astype(BF16),
                        moe_w3[l].astype(BF16), moe_w2[l].astype(BF16), 2 * n_tok)
        final = l == depth - 1
        xc = _combine(xn, mod, route, ybuf, final_g.reshape(1, dm), final, tm, ctx_tiles, row0_tiles)
        out = xc
    return out
```

```python
import functools
import math

import numpy as np
import jax
import jax.numpy as jnp
from jax import lax
from jax.experimental import pallas as pl
from jax.experimental.pallas import tpu as pltpu

F32 = jnp.float32
BF16 = jnp.bfloat16

EPS = 1e-6
NEG_BIG = -1e30
HEADS = 4
HEAD_W = 128
MIX_W = HEADS * HEAD_W
GRID_W = 64
ROPE_BASE = 10000.0
ROPE_PAIRS = 16
B_CONV = 3
N_GROUPS = 4
EXPERTS_PER_GROUP = 8
N_EXPERTS = N_GROUPS * EXPERTS_PER_GROUP
D_EXPERT = 512
MOE_BLOCK = 128
LANES = 128
VMEM_LIMIT = 50 * 1024 * 1024

PA_W = 5 * MIX_W
PB_W = 4 * MIX_W
PG_W = LANES
PC_W = 3 * MIX_W
PROJ_W = PA_W + PB_W + PG_W + PC_W


def _params(sem, **kw):
    return pltpu.CompilerParams(dimension_semantics=sem, vmem_limit_bytes=VMEM_LIMIT, **kw)


def _const_spec(shape):
    nd = len(shape)
    return pl.BlockSpec(shape, lambda *_: (0,) * nd, pipeline_mode=pl.Buffered(1))


def _dot(a, b):
    return jnp.dot(a, b, preferred_element_type=F32)


def _dot_nt(a, b):
    return lax.dot_general(a, b, (((1,), (1,)), ((), ())), preferred_element_type=F32)


def _dot_tn(a, b):
    return lax.dot_general(a, b, (((0,), (0,)), ((), ())), preferred_element_type=F32)


def _dot_sel(m_bf16, x):
    hi = x.astype(BF16)
    r1 = x - hi.astype(F32)
    mid = r1.astype(BF16)
    lo = (r1 - mid.astype(F32)).astype(BF16)
    return _dot(m_bf16, hi) + _dot(m_bf16, mid) + _dot(m_bf16, lo)


def _sigmoid(x):
    return 1.0 / (1.0 + jnp.exp(-x))


def _log_sigmoid(x):
    return jnp.minimum(x, 0.0) - jnp.log1p(jnp.exp(-jnp.abs(x)))


def _rms(x, g):
    return x * lax.rsqrt(jnp.mean(x * x, axis=-1, keepdims=True) + EPS) * g


def _ada_kernel(c_ref, w_ref, b_ref, o_ref):
    c = c_ref[...]
    s = c * _sigmoid(c)
    o_ref[...] = jnp.dot(s, w_ref[...], preferred_element_type=F32,
                         precision=lax.Precision.HIGHEST) + b_ref[...]


def _ada_mod(cc, ada_w, ada_b):
    depth, dm, six = ada_w.shape
    rows = cc.shape[0]
    tn = dm
    return pl.pallas_call(
        _ada_kernel,
        grid=(depth, six // tn),
        in_specs=[pl.BlockSpec((rows, dm), lambda l, n: (0, 0)),
                  pl.BlockSpec((None, dm, tn), lambda l, n: (l, 0, n)),
                  pl.BlockSpec((None, 1, tn), lambda l, n: (l, 0, n))],
        out_specs=pl.BlockSpec((None, rows, tn), lambda l, n: (l, 0, n)),
        out_shape=jax.ShapeDtypeStruct((depth, rows, six), F32),
        compiler_params=_params(("parallel", "parallel")),
        name="ada_mod",
    )(cc, ada_w, ada_b.reshape(depth, 1, six))


def _rope(x, cos, sin):
    n = x.shape[-1]
    lane = lax.broadcasted_iota(jnp.int32, (1, n), 1)
    first = (lane // ROPE_PAIRS) % 2 == 0
    partner = jnp.where(first, pltpu.roll(x, n - ROPE_PAIRS, 1), pltpu.roll(x, ROPE_PAIRS, 1))
    return x * cos + partner * sin


def _hgrn_log_forget(z, lbc):
    lsig = jnp.minimum(z, 0.0) - jnp.log(1.0 + jnp.exp(-jnp.abs(z)))
    a = lbc[0:1]
    bb = lbc[1:2] + lsig
    return jnp.maximum(a, bb) + jnp.log(1.0 + jnp.exp(-jnp.abs(a - bb)))


def _norm_proj_kernel(x_ref, mod_ref, g_ref, w_ref, cos_ref, sin_ref, lbc_ref,
                      pa_ref, pb_ref, pg_ref, qc_ref, kc_ref, vc_ref):
    x = x_ref[...]
    mod = mod_ref[...]
    h = _rms(x, g_ref[...]) * (1.0 + mod[1:2]) + mod[0:1]
    hb = h.astype(BF16)
    pa_ref[:, 0:3 * MIX_W] = _dot(hb, w_ref[:, 0:3 * MIX_W])
    for d in range(2):
        cs = slice((3 + d) * MIX_W, (4 + d) * MIX_W)
        pa_ref[:, cs] = _hgrn_log_forget(_dot(hb, w_ref[:, cs]), lbc_ref[d])
    pb_ref[...] = _dot(hb, w_ref[:, PA_W:PA_W + PB_W])
    pg_ref[...] = _dot(hb, w_ref[:, PA_W + PB_W:PA_W + PB_W + PG_W])
    c0 = PA_W + PB_W + PG_W
    cos = cos_ref[...]
    sin = sin_ref[...]
    q = _dot(hb, w_ref[:, c0:c0 + MIX_W])
    qc_ref[...] = (_rope(q, cos, sin) * (64.0 ** -0.5)).astype(BF16)
    k = _dot(hb, w_ref[:, c0 + MIX_W:c0 + 2 * MIX_W])
    kc_ref[...] = _rope(k, cos, sin).astype(BF16)
    vc_ref[...] = _dot(hb, w_ref[:, c0 + 2 * MIX_W:c0 + 3 * MIX_W]).astype(BF16)


def _norm_proj(x, mod, g, w, cos, sin, lbc, tm, ctx_tiles):
    bsz, tt, dm = x.shape
    nt = tt // tm
    row = lambda b, t: (b, t, 0)
    outs = [(PA_W, F32), (PB_W, F32), (PG_W, F32), (MIX_W, BF16), (MIX_W, BF16), (MIX_W, BF16)]
    return pl.pallas_call(
        _norm_proj_kernel,
        grid=(bsz, nt),
        in_specs=[pl.BlockSpec((None, tm, dm), row),
                  pl.BlockSpec((None, None, 6, dm), lambda b, t: (b, jnp.where(t >= ctx_tiles, 1, 0), 0, 0)),
                  _const_spec((1, dm)),
                  _const_spec((dm, PROJ_W)),
                  pl.BlockSpec((tm, MIX_W), lambda b, t: (t, 0)),
                  pl.BlockSpec((tm, MIX_W), lambda b, t: (t, 0)),
                  _const_spec(lbc.shape)],
        out_specs=[pl.BlockSpec((None, tm, wd), row) for wd, _ in outs],
        out_shape=[jax.ShapeDtypeStruct((bsz, tt, wd), dt) for wd, dt in outs],
        compiler_params=_params(("parallel", "parallel")),
        name="norm_proj",
    )(x, mod, g, w, cos, sin, lbc)


def _conv_kernel(x_ref, w_ref, b_ref, o_ref, *, ctx, q_blocks):
    x = x_ref[...]
    tt = x.shape[0]
    row = lax.broadcasted_iota(jnp.int32, (tt, 1), 0)
    prev = jnp.where((row == 0) | (row == ctx), 0.0, pltpu.roll(x, 1, 0))
    nxt = jnp.where((row == ctx - 1) | (row == tt - 1), 0.0, pltpu.roll(x, tt - 1, 0))
    w = w_ref[...]
    y = b_ref[...] + prev * w[0:1] + x * w[1:2] + nxt * w[2:3]
    y = y * _sigmoid(y)
    scale = jnp.where(pl.program_id(1) < q_blocks, HEAD_W ** -0.5, 1.0)
    o_ref[...] = y * scale


def _conv_silu(pb, conv_w, conv_b, ctx, cb):
    bsz, tt, _ = pb.shape
    width = 2 * MIX_W
    return pl.pallas_call(
        functools.partial(_conv_kernel, ctx=ctx, q_blocks=MIX_W // cb),
        grid=(bsz, width // cb),
        in_specs=[pl.BlockSpec((None, tt, cb), lambda b, j: (b, 0, j)),
                  pl.BlockSpec((B_CONV, cb), lambda b, j: (0, j)),
                  pl.BlockSpec((1, cb), lambda b, j: (0, j))],
        out_specs=pl.BlockSpec((None, tt, cb), lambda b, j: (b, 0, j)),
        out_shape=jax.ShapeDtypeStruct((bsz, tt, width), F32),
        compiler_params=_params(("parallel", "parallel")),
        name="mlstm_conv",
    )(pb, conv_w, conv_b.reshape(1, width))


def _chunk_consts(L):
    idx = np.arange(L)
    t, u = idx[:, None], idx[None, :]
    cum = (u <= t).astype(np.float32)
    nlev = int(round(math.log2(L)))
    lvl = np.full((L, L), -1, np.int32)
    lvl[idx, idx] = nlev
    for i in range(nlev):
        h = L >> (i + 1)
        same = (t // (2 * h)) == (u // (2 * h))
        lvl[same & (t % (2 * h) >= h) & (u % (2 * h) < h)] = i
    sel = jnp.asarray(np.stack([cum, cum[::-1, ::-1]]), BF16)
    lvls = jnp.asarray(np.stack([lvl, lvl[::-1, ::-1]]))
    return sel, lvls, nlev


def _midpoint_rows(b, h, d):
    L, w = b.shape
    two = 2 * h
    pos = h - 1 if d == 0 else h
    if two % 8 == 0:
        r = b.reshape(L // two, two, w)[:, pos:pos + 1, :]
        return jnp.broadcast_to(r, (L // two, two, w)).reshape(L, w)
    phase = lax.broadcasted_iota(jnp.int32, (L, 1), 0) % two
    ref = b
    for off in range(pos - two + 1, pos + 1):
        if off != 0:
            ref = jnp.where(phase == pos - off, pltpu.roll(b, (L - off) % L, 0), ref)
    return ref


def _scan_blocks(j, nc_ctx, nc):
    jb = jnp.where(j < nc_ctx, nc_ctx - 1 - j, nc - 1 + nc_ctx - j)
    return j, jb


def _hgrn_kernel(qf_ref, vf_ref, ff_ref, qb_ref, vb_ref, fb_ref, sel_ref, lvl_ref,
                 of_ref, ob_ref, st_ref, *, L, nlev):
    @pl.when(pl.program_id(1) == 0)
    def _():
        st_ref[...] = jnp.zeros_like(st_ref)

    row = lax.broadcasted_iota(jnp.int32, (L, 1), 0)
    dirs = ((qf_ref, vf_ref, ff_ref, of_ref), (qb_ref, vb_ref, fb_ref, ob_ref))
    for d, (q_ref, v_ref, f_ref, o_ref) in enumerate(dirs):
        lf = f_ref[...]
        kk = 1.0 - jnp.exp(lf)
        q_all = q_ref[...]
        b_in = _dot_sel(sel_ref[d], lf)
        lvl = lvl_ref[d]
        last = L - 1 if d == 0 else 0
        b_end = b_in[last:last + 1]
        heads = [slice(hd * HEAD_W, (hd + 1) * HEAD_W) for hd in range(HEADS)]
        qb = q_all.astype(BF16)
        kb = kk.astype(BF16)
        on_diag = lvl == nlev
        scores = [jnp.where(on_diag, _dot_nt(qb[:, cs], kb[:, cs]), 0.0) for cs in heads]
        for i in range(nlev):
            h = L >> (i + 1)
            ei = jnp.exp(-jnp.abs(b_in - _midpoint_rows(b_in, h, d)))
            is_query = (row % (2 * h) >= h) if d == 0 else (row % (2 * h) < h)
            xe = (jnp.where(is_query, q_all, kk) * ei).astype(BF16)
            at_level = lvl == i
            for hd, cs in enumerate(heads):
                scores[hd] = scores[hd] + jnp.where(at_level, _dot_nt(xe[:, cs], xe[:, cs]), 0.0)
        q_in = (q_all * jnp.exp(b_in)).astype(BF16)
        k_out = (kk * jnp.exp(b_end - b_in)).astype(BF16)
        decay = jnp.exp(b_end)
        for hd, cs in enumerate(heads):
            vb = v_ref[:, cs].astype(BF16)
            st = st_ref[d, hd]
            o_ref[:, cs] = _dot(scores[hd].astype(BF16), vb) + _dot_nt(q_in[:, cs], st.astype(BF16))
            st_ref[d, hd] = st * decay[:, cs] + _dot_tn(vb, k_out[:, cs])


def _hgrn(pa, L, nc_ctx):
    bsz, tt, _ = pa.shape
    nc = tt // L
    sel, lvl, nlev = _chunk_consts(L)

    def spec(col, which):
        return pl.BlockSpec((None, L, MIX_W), lambda b, j: (b, _scan_blocks(j, nc_ctx, nc)[which], col))

    return pl.pallas_call(
        functools.partial(_hgrn_kernel, L=L, nlev=nlev),
        grid=(bsz, nc),
        in_specs=[spec(0, 0), spec(1, 0), spec(3, 0), spec(0, 1), spec(1, 1), spec(4, 1),
                  _const_spec(sel.shape), _const_spec(lvl.shape)],
        out_specs=[spec(0, 0), spec(0, 1)],
        out_shape=[jax.ShapeDtypeStruct((bsz, tt, MIX_W), F32)] * 2,
        scratch_shapes=[pltpu.VMEM((2, HEADS, HEAD_W, HEAD_W), F32)],
        compiler_params=_params(("parallel", "arbitrary")),
        name="hgrn2_scan",
    )(pa, pa, pa, pa, pa, pa, sel, lvl)


def _mlstm_kernel(qf_ref, kf_ref, vf_ref, gf_ref, qb_ref, kb_ref, vb_ref, gb_ref, gbias_ref,
                  sel_ref, lvl_ref, of_ref, ob_ref, c_ref, m_ref, *, L, nlev):
    @pl.when(pl.program_id(1) == 0)
    def _():
        c_ref[...] = jnp.zeros_like(c_ref)
        m_ref[...] = jnp.zeros_like(m_ref)

    lane = lax.broadcasted_iota(jnp.int32, (1, LANES), 1)
    ones_col = jnp.where(lane == 0, 1.0, 0.0).astype(BF16)
    dirs = ((qf_ref, kf_ref, vf_ref, gf_ref, of_ref), (qb_ref, kb_ref, vb_ref, gb_ref, ob_ref))
    for d, (q_ref, k_ref, v_ref, g_ref, o_ref) in enumerate(dirs):
        g = g_ref[...] + gbias_ref[...]
        lf = jnp.where((lane >= 2 * HEADS) & (lane < 4 * HEADS), _log_sigmoid(g), 0.0)
        bcum = _dot_sel(sel_ref[d, 0:L], lf)
        causal = lvl_ref[d] >= 0
        row_terms = (g - pltpu.roll(bcum, LANES - 2 * HEADS, 1)).T
        last = L - 1 if d == 0 else 0
        for hd in range(HEADS):
            cs = slice(hd * HEAD_W, (hd + 1) * HEAD_W)
            li_lane = d * HEADS + hd
            lf_lane = 2 * HEADS + d * HEADS + hd
            li = g[:, li_lane:li_lane + 1]
            b = bcum[:, lf_lane:lf_lane + 1]
            row_term = row_terms[li_lane:li_lane + 1, :]
            b_end = b[last:last + 1]
            r = d * HEADS + hd
            m_prev = m_ref[r:r + 1, 0:1]
            q = q_ref[:, cs].astype(BF16)
            k = k_ref[:, cs]
            v_aug = jnp.concatenate([v_ref[:, cs].astype(BF16), jnp.broadcast_to(ones_col, (L, LANES))], axis=1)
            c_prev = c_ref[r]
            w_end = b_end - b + li
            m_new = jnp.maximum(b_end + m_prev, jnp.max(w_end, axis=0, keepdims=True))
            e_end = jnp.exp(w_end - m_new)
            keep = jnp.exp(b_end + m_prev - m_new)
            c_ref[r] = keep * c_prev + _dot_tn((k * e_end).astype(BF16), v_aug)
            m_ref[r:r + 1, :] = jnp.broadcast_to(m_new, (1, LANES))
            logw = jnp.where(causal, b + row_term, NEG_BIG)
            w_state = b + m_prev
            m_t = jnp.maximum(jnp.max(logw, axis=-1, keepdims=True), w_state)
            s = _dot_nt(q, k.astype(BF16)) * jnp.exp(logw - m_t)
            a_state = jnp.exp(w_state - m_t)
            num = _dot(s.astype(BF16), v_aug) + a_state * _dot(q, c_prev.astype(BF16))
            den = num[:, HEAD_W:HEAD_W + 1]
            o_ref[:, cs] = num[:, 0:HEAD_W] / jnp.maximum(jnp.abs(den), jnp.exp(-m_t))


def _mlstm(qk, pb, pg, gbias, L, nc_ctx):
    bsz, tt, _ = pb.shape
    nc = tt // L
    sel, lvl, nlev = _chunk_consts(L)

    def spec(col, which, width=MIX_W):
        return pl.BlockSpec((None, L, width), lambda b, j: (b, _scan_blocks(j, nc_ctx, nc)[which], col))

    return pl.pallas_call(
        functools.partial(_mlstm_kernel, L=L, nlev=nlev),
        grid=(bsz, nc),
        in_specs=[spec(0, 0), spec(1, 0), spec(2, 0), spec(0, 0, PG_W),
                  spec(0, 1), spec(1, 1), spec(2, 1), spec(0, 1, PG_W),
                  _const_spec((1, PG_W)), _const_spec(sel.shape), _const_spec(lvl.shape)],
        out_specs=[spec(0, 0), spec(0, 1)],
        out_shape=[jax.ShapeDtypeStruct((bsz, tt, MIX_W), F32)] * 2,
        scratch_shapes=[pltpu.VMEM((2 * HEADS, HEAD_W, 2 * HEAD_W), F32),
                        pltpu.VMEM((2 * HEADS, LANES), F32)],
        compiler_params=_params(("parallel", "arbitrary")),
        name="mlstm_scan",
    )(qk, qk, pb, pg, qk, qk, pb, pg, gbias, sel, lvl)


def _attn_kernel(lam_ref, q_ref, k_ref, v_ref, o_ref, *, lam_init, ctx_tiles, t_ctx, with_ctx):
    lv = lam_ref[...]
    lam = (jnp.exp(jnp.sum(lv[0:1] * lv[1:2], axis=-1, keepdims=True))
           - jnp.exp(jnp.sum(lv[2:3] * lv[3:4], axis=-1, keepdims=True)) + lam_init)
    lane = lax.broadcasted_iota(jnp.int32, (1, HEAD_W), 1)

    def attend(n_k):
        q = q_ref[...]
        k = k_ref[0:n_k, :]
        zero = jnp.zeros_like(q)

        def probs(qm):
            s = _dot_nt(qm, k)
            p = jnp.exp(s - jnp.max(s, axis=-1, keepdims=True))
            return p / jnp.sum(p, axis=-1, keepdims=True)

        a = probs(jnp.where(lane < HEAD_W // 2, q, zero)) - lam * probs(jnp.where(lane >= HEAD_W // 2, q, zero))
        o_ref[...] = _dot(a.astype(BF16), v_ref[0:n_k, :])

    is_ctx = pl.program_id(2) < ctx_tiles

    @pl.when(is_ctx)
    def _():
        if with_ctx:
            attend(t_ctx)
        else:
            o_ref[...] = jnp.zeros_like(o_ref)

    @pl.when(jnp.logical_not(is_ctx))
    def _():
        attend(k_ref.shape[0])


def _attn(lam_vec, qc, kc, vc, lam_init, tq, t_ctx, with_ctx):
    bsz, tt, _ = qc.shape
    kv = pl.BlockSpec((None, tt, HEAD_W), lambda b, h, i: (b, 0, h))
    return pl.pallas_call(
        functools.partial(_attn_kernel, lam_init=lam_init, ctx_tiles=t_ctx // tq, t_ctx=t_ctx, with_ctx=with_ctx),
        grid=(bsz, HEADS, tt // tq),
        in_specs=[_const_spec(lam_vec.shape),
                  pl.BlockSpec((None, tq, HEAD_W), lambda b, h, i: (b, i, h)), kv, kv],
        out_specs=pl.BlockSpec((None, tq, HEAD_W), lambda b, h, i: (b, i, h)),
        out_shape=jax.ShapeDtypeStruct((bsz, tt, MIX_W), F32),
        compiler_params=_params(("parallel", "parallel", "arbitrary")),
        name="diff_attn",
    )(lam_vec, qc, kc, vc)


def _head_norm(x, g):
    parts = []
    for hd in range(HEADS):
        xs = x[:, hd * HEAD_W:(hd + 1) * HEAD_W]
        parts.append(xs * lax.rsqrt(jnp.mean(xs * xs, axis=-1, keepdims=True) + EPS))
    return jnp.concatenate(parts, axis=1) * g


def _route(logits):
    lane = lax.broadcasted_iota(jnp.int32, logits.shape, 1)
    neg = -jnp.inf
    gl = jnp.where(lane < N_GROUPS, logits, neg)
    gmax = jnp.max(gl, axis=-1, keepdims=True)
    gidx = jnp.min(jnp.where(gl == gmax, lane, LANES), axis=-1, keepdims=True)
    p_group = 1.0 / jnp.sum(jnp.exp(gl - gmax), axis=-1, keepdims=True)
    lo = N_GROUPS + EXPERTS_PER_GROUP * gidx
    in_grp = (lane >= lo) & (lane < lo + EXPERTS_PER_GROUP)
    el = jnp.where(in_grp, logits, neg)
    pe = jnp.exp(el - jnp.max(el, axis=-1, keepdims=True))
    pe = pe / jnp.sum(pe, axis=-1, keepdims=True)
    pe = jnp.where(in_grp, pe, -1.0)
    v1 = jnp.max(pe, axis=-1, keepdims=True)
    i1 = jnp.min(jnp.where(pe == v1, lane, LANES), axis=-1, keepdims=True)
    pe2 = jnp.where(lane == i1, -1.0, pe)
    v2 = jnp.max(pe2, axis=-1, keepdims=True)
    i2 = jnp.min(jnp.where(pe2 == v2, lane, LANES), axis=-1, keepdims=True)
    scale = p_group / (v1 + v2)
    e1 = (i1 - N_GROUPS).astype(F32)
    e2 = (i2 - N_GROUPS).astype(F32)
    return jnp.where(lane == 0, e1, jnp.where(lane == 1, e2, jnp.where(lane == 2, v1 * scale,
                     jnp.where(lane == 3, v2 * scale, 0.0))))


def _merge_kernel(x_ref, mod_ref, n1_ref, n2_ref, af_ref, ab_ref, ga_ref, bf_ref, bb_ref, gb_ref, oc_ref,
                  hg_ref, wg_ref, bg_ref, wbr_ref, wo_ref, rw_ref, rb_ref,
                  xo_ref, h2_ref, rt_ref, *, lam_init):
    x = x_ref[...]
    mod = mod_ref[...]
    dm = x.shape[-1]
    hb = (_rms(x, n1_ref[...]) * (1.0 + mod[1:2]) + mod[0:1]).astype(BF16)
    hg = hg_ref[...]
    ga = ga_ref[...]
    ya = _head_norm(af_ref[...] + ab_ref[...], hg[0:1]) * (ga * _sigmoid(ga))
    yb = _head_norm(bf_ref[...] + bb_ref[...], hg[1:2]) * _sigmoid(gb_ref[...])
    yc = _head_norm(oc_ref[...], hg[2:3]) * (1.0 - lam_init)
    y = jnp.zeros_like(x)
    for i, yi in enumerate((ya, yb, yc)):
        gate = _sigmoid(_dot(hb, wg_ref[:, i * dm:(i + 1) * dm]) + bg_ref[:, i * dm:(i + 1) * dm])
        y = y + gate * _dot(yi.astype(BF16), wbr_ref[i])
    xn = x + mod[2:3] * _dot(y.astype(BF16), wo_ref[...])
    xo_ref[...] = xn
    h2 = _rms(xn, n2_ref[...]) * (1.0 + mod[4:5]) + mod[3:4]
    h2_ref[...] = h2
    logits = jnp.dot(h2, rw_ref[...], preferred_element_type=F32, precision=lax.Precision.HIGHEST) + rb_ref[...]
    rt_ref[...] = _route(logits)


def _merge(x, mod, n1, n2, oaf, oab, pa, obf, obb, pb, oc, hg, wg, bg, wbr, wo, rw, rb,
           lam_init, tm, ctx_tiles, row0_tiles):
    bsz, tt, dm = x.shape
    nt = tt // tm - row0_tiles
    rows = nt * tm
    src = lambda b, t: (b, row0_tiles + t, 0)

    def col(c):
        return pl.BlockSpec((None, tm, MIX_W), lambda b, t: (b, row0_tiles + t, c))

    dst = lambda b, t: (b, t, 0)
    outs = [(dm, F32), (dm, F32), (LANES, F32)]
    return pl.pallas_call(
        functools.partial(_merge_kernel, lam_init=lam_init),
        grid=(bsz, nt),
        in_specs=[pl.BlockSpec((None, tm, dm), src),
                  pl.BlockSpec((None, None, 6, dm),
                               lambda b, t: (b, jnp.where(row0_tiles + t >= ctx_tiles, 1, 0), 0, 0)),
                  _const_spec((1, dm)), _const_spec((1, dm)),
                  col(0), col(0), col(2), col(0), col(0), col(3), col(0),
                  _const_spec((3, MIX_W)), _const_spec(wg.shape), _const_spec(bg.shape),
                  _const_spec(wbr.shape), _const_spec(wo.shape), _const_spec(rw.shape), _const_spec(rb.shape)],
        out_specs=[pl.BlockSpec((None, tm, wd), dst) for wd, _ in outs],
        out_shape=[jax.ShapeDtypeStruct((bsz, rows, wd), dt) for wd, dt in outs],
        compiler_params=_params(("parallel", "parallel")),
        name="merge_route",
    )(x, mod, n1, n2, oaf, oab, pa, obf, obb, pb, oc, hg, wg, bg, wbr, wo, rw, rb)


def _expert_kernel(be_ref, nv_ref, src_ref, dst_ref, tok_ref, w1_ref, w3_ref, w2_ref, out_ref,
                   xbuf, ybuf, sems):
    i = pl.program_id(0)
    nvalid = nv_ref[i]

    @pl.when(nvalid > 0)
    def _():
        base = i * MOE_BLOCK

        def row_in(r):
            return pltpu.make_async_copy(tok_ref.at[pl.ds(src_ref[base + r], 1)], xbuf.at[pl.ds(r, 1)], sems.at[0])

        def row_out(r):
            return pltpu.make_async_copy(ybuf.at[pl.ds(r, 1)], out_ref.at[pl.ds(dst_ref[base + r], 1)], sems.at[1])

        def start_in(r, c):
            row_in(r).start()
            return c

        def wait_in(r, c):
            row_in(r).wait()
            return c

        lax.fori_loop(0, MOE_BLOCK, start_in, 0)
        lax.fori_loop(0, MOE_BLOCK, wait_in, 0)
        xb = xbuf[...].astype(BF16)
        u = _dot(xb, w1_ref[...])
        hmid = (u * _sigmoid(u)) * _dot(xb, w3_ref[...])
        ybuf[...] = _dot(hmid.astype(BF16), w2_ref[...])

        def start_out(r, c):
            row_out(r).start()
            return c

        def wait_out(r, c):
            row_out(r).wait()
            return c

        lax.fori_loop(0, nvalid, start_out, 0)
        lax.fori_loop(0, nvalid, wait_out, 0)


def _experts(block_e, nvalid, src, dst, tokens, w1, w3, w2, n_out_rows):
    n_tok, dm = tokens.shape
    n_blocks = block_e.shape[0]
    grid_spec = pltpu.PrefetchScalarGridSpec(
        num_scalar_prefetch=4,
        grid=(n_blocks,),
        in_specs=[pl.BlockSpec(memory_space=pl.ANY),
                  pl.BlockSpec((None, dm, D_EXPERT), lambda i, be, nv, s, d: (be[i], 0, 0)),
                  pl.BlockSpec((None, dm, D_EXPERT), lambda i, be, nv, s, d: (be[i], 0, 0)),
                  pl.BlockSpec((None, D_EXPERT, dm), lambda i, be, nv, s, d: (be[i], 0, 0))],
        out_specs=pl.BlockSpec(memory_space=pl.ANY),
        scratch_shapes=[pltpu.VMEM((MOE_BLOCK, dm), F32), pltpu.VMEM((MOE_BLOCK, dm), F32),
                        pltpu.SemaphoreType.DMA((2,))],
    )
    return pl.pallas_call(
        _expert_kernel,
        grid_spec=grid_spec,
        out_shape=jax.ShapeDtypeStruct((n_out_rows, dm), F32),
        compiler_params=_params(("arbitrary",), has_side_effects=True),
        name="moe_experts",
    )(block_e, nvalid, src, dst, tokens, w1, w3, w2)


def _dispatch(route, n_tok):
    n_assign = 2 * n_tok
    flat_e = route[:, 0:2].astype(jnp.int32).reshape(-1)
    onehot = (flat_e[:, None] == jnp.arange(N_EXPERTS)[None, :]).astype(jnp.int32)
    csum = jnp.cumsum(onehot, axis=0)
    counts = csum[-1]
    rank = jnp.sum(csum * onehot, axis=1) - 1
    padded = (counts + MOE_BLOCK - 1) // MOE_BLOCK * MOE_BLOCK
    p_end = jnp.cumsum(padded)
    p_start = p_end - padded
    dest = p_start[flat_e] + rank
    n_blocks = -(-n_assign // MOE_BLOCK) + N_EXPERTS
    n_slots = n_blocks * MOE_BLOCK
    a = jnp.arange(n_assign, dtype=jnp.int32)
    src = jnp.zeros((n_slots,), jnp.int32).at[dest].set(a // 2)
    dst = jnp.zeros((n_slots,), jnp.int32).at[dest].set((a % 2) * n_tok + a // 2)
    blk_start = jnp.arange(n_blocks, dtype=jnp.int32) * MOE_BLOCK
    block_e = jnp.minimum(jnp.searchsorted(p_end, blk_start, side='right'), N_EXPERTS - 1).astype(jnp.int32)
    used_end = (p_start + counts)[block_e]
    nvalid = jnp.clip(used_end - blk_start, 0, MOE_BLOCK).astype(jnp.int32)
    nvalid = jnp.where(blk_start < p_end[-1], nvalid, 0)
    return block_e, nvalid, src, dst


def _combine_kernel(x_ref, mod_ref, rt_ref, y0_ref, y1_ref, fg_ref, o_ref, *, final):
    rt = rt_ref[...]
    y = rt[:, 2:3] * y0_ref[...] + rt[:, 3:4] * y1_ref[...]
    xn = x_ref[...] + mod_ref[5:6, :] * y
    o_ref[...] = _rms(xn, fg_ref[...]) if final else xn


def _combine(x, mod, route, ybuf, fg, final, tm, ctx_tiles, row0_tiles):
    bsz, rows, dm = x.shape
    nt = rows // tm
    n_tok = bsz * rows
    tok_blk = lambda b, t: (b * nt + t, 0)
    return pl.pallas_call(
        functools.partial(_combine_kernel, final=final),
        grid=(bsz, nt),
        in_specs=[pl.BlockSpec((None, tm, dm), lambda b, t: (b, t, 0)),
                  pl.BlockSpec((None, None, 6, dm),
                               lambda b, t: (b, jnp.where(row0_tiles + t >= ctx_tiles, 1, 0), 0, 0)),
                  pl.BlockSpec((None, tm, LANES), lambda b, t: (b, t, 0)),
                  pl.BlockSpec((tm, dm), tok_blk),
                  pl.BlockSpec((tm, dm), lambda b, t: (n_tok // tm + b * nt + t, 0)),
                  _const_spec((1, dm))],
        out_specs=pl.BlockSpec((None, tm, dm), lambda b, t: (b, t, 0)),
        out_shape=jax.ShapeDtypeStruct((bsz, rows, dm), F32),
        compiler_params=_params(("parallel", "parallel")),
        name="moe_combine",
    )(x, mod, route, ybuf, ybuf, fg)


def _rope_tables(ctx, t_lat):
    rows = t_lat // GRID_W
    row = jnp.repeat(jnp.arange(rows, dtype=F32), GRID_W)
    col = jnp.tile(jnp.arange(GRID_W, dtype=F32), rows)
    inv = ROPE_BASE ** (-jnp.arange(ROPE_PAIRS, dtype=F32) / ROPE_PAIRS)
    ang_r = row[:, None] * inv
    ang_c = col[:, None] * inv
    cos64 = jnp.concatenate([jnp.cos(ang_r), jnp.cos(ang_r), jnp.cos(ang_c), jnp.cos(ang_c)], axis=1)
    sin64 = jnp.concatenate([-jnp.sin(ang_r), jnp.sin(ang_r), -jnp.sin(ang_c), jnp.sin(ang_c)], axis=1)
    cos = jnp.tile(cos64, (1, MIX_W // 64))
    sin = jnp.tile(sin64, (1, MIX_W // 64))
    cos = jnp.concatenate([jnp.ones((ctx, MIX_W), F32), cos], axis=0)
    sin = jnp.concatenate([jnp.zeros((ctx, MIX_W), F32), sin], axis=0)
    return cos, sin


def _pack_w_in(w):
    dm = w.shape[0]
    a_end = PA_W
    b_end = a_end + PB_W
    g_end = b_end + 4 * HEADS
    pad = jnp.zeros((dm, PG_W - 4 * HEADS), w.dtype)
    return jnp.concatenate([w[:, :b_end], w[:, b_end:g_end], pad, w[:, g_end:]], axis=1).astype(BF16)


def kernel(x, c, ctx, c_ctx, ada_w, ada_b, norm1_g, norm2_g, w_in, mlstm_conv_w, mlstm_conv_b, mlstm_gate_b,
           hgrn_lb_raw, hgrn_norm_g, mlstm_norm_g, diff_norm_g, diff_lambda, w_branch, w_gate, b_gate, w_out,
           router_g_w, router_g_b, router_e_w, router_e_b, moe_w1, moe_w3, moe_w2, final_g):
    bsz, t_lat, dm = x.shape
    t_ctx = ctx.shape[1]
    depth = ada_w.shape[0]
    tt = t_ctx + t_lat
    tm = min(256, t_ctx)
    hgrn_chunk = min(128, t_ctx)
    mlstm_chunk = min(256, t_ctx)
    assert t_ctx % tm == 0 and t_lat % tm == 0 and t_lat % GRID_W == 0
    assert t_ctx % hgrn_chunk == 0 and t_ctx % mlstm_chunk == 0
    ctx_tiles = t_ctx // tm

    n_rows = -(-(bsz + 1) // 8) * 8
    cc = jnp.zeros((n_rows, dm), F32).at[:bsz].set(c).at[bsz].set(c_ctx)
    mods = _ada_mod(cc, ada_w, ada_b).reshape(depth, n_rows, 6, dm)

    cos, sin = _rope_tables(t_ctx, t_lat)
    lb_cum = jnp.cumsum(jax.nn.softmax(hgrn_lb_raw.astype(F32), axis=0), axis=0)
    lower = lb_cum - lb_cum[0]

    xc = jnp.concatenate([ctx, x], axis=1)
    out = None
    for l in range(depth):
        with_ctx = l < depth - 1
        lam_init = 0.8 - 0.6 * math.exp(-0.3 * l)
        mod = jnp.stack([jnp.broadcast_to(mods[l, bsz], (bsz, 6, dm)), mods[l, :bsz]], axis=1)
        n1 = norm1_g[l].reshape(1, dm)
        n2 = norm2_g[l].reshape(1, dm)

        lb = lower[l]
        pad = jnp.zeros((2, 6, MIX_W), F32)
        lbc = jnp.concatenate([jnp.log(lb)[:, None], jnp.log1p(-lb)[:, None], pad], axis=1)
        pa, pb, pg, qc, kc, vc = _norm_proj(xc, mod, n1, _pack_w_in(w_in[l]), cos, sin, lbc, tm, ctx_tiles)

        oaf, oab = _hgrn(pa, hgrn_chunk, t_ctx // hgrn_chunk)

        qk = _conv_silu(pb, mlstm_conv_w[l], mlstm_conv_b[l], t_ctx, 256)
        gbias = jnp.zeros((1, PG_W), F32).at[0, :4 * HEADS].set(mlstm_gate_b[l].reshape(-1))
        obf, obb = _mlstm(qk, pb, pg, gbias, mlstm_chunk, t_ctx // mlstm_chunk)

        lam_vec = diff_lambda[l].astype(F32)
        oc = _attn(lam_vec, qc, kc, vc, lam_init, tm, t_ctx, with_ctx)

        hg = jnp.stack([hgrn_norm_g[l], mlstm_norm_g[l], diff_norm_g[l]])
        rw = jnp.zeros((dm, LANES), F32).at[:, :N_GROUPS].set(router_g_w[l])
        rw = rw.at[:, N_GROUPS:N_GROUPS + N_EXPERTS].set(router_e_w[l])
        rb = jnp.zeros((1, LANES), F32).at[0, :N_GROUPS].set(router_g_b[l])
        rb = rb.at[0, N_GROUPS:N_GROUPS + N_EXPERTS].set(router_e_b[l])
        row0_tiles = 0 if with_ctx else ctx_tiles
        xn, h2, route = _merge(xc, mod, n1, n2, oaf, oab, pa, obf, obb, pb, oc, hg,
                               w_gate[l].astype(BF16), b_gate[l].reshape(1, 3 * dm), w_branch[l].astype(BF16),
                               w_out[l].astype(BF16), rw, rb, lam_init, tm, ctx_tiles, row0_tiles)

        rows = xn.shape[1]
        n_tok = bsz * rows
        block_e, nvalid, src, dst = _dispatch(route.reshape(n_tok, LANES), n_tok)
        ybuf = _experts(block_e, nvalid, src, dst, h2.reshape(n_tok, dm), moe_w1[l].astype(BF16),
                        moe_w3[l].astype(BF16), moe_w2[l].astype(BF16), 2 * n_tok)
        final = l == depth - 1
        xc = _combine(xn, mod, route, ybuf, final_g.reshape(1, dm), final, tm, ctx_tiles, row0_tiles)
        out = xc
    return out
```

```python
import functools
import math

import numpy as np
import jax
import jax.numpy as jnp
from jax import lax
from jax.experimental import pallas as pl
from jax.experimental.pallas import tpu as pltpu

F32 = jnp.float32
BF16 = jnp.bfloat16

EPS = 1e-6
NEG_BIG = -1e30
HEADS = 4
HEAD_W = 128
MIX_W = HEADS * HEAD_W
GRID_W = 64
ROPE_BASE = 10000.0
ROPE_PAIRS = 16
B_CONV = 3
N_GROUPS = 4
EXPERTS_PER_GROUP = 8
N_EXPERTS = N_GROUPS * EXPERTS_PER_GROUP
D_EXPERT = 512
MOE_BLOCK = 256
LANES = 128
VMEM_LIMIT = 50 * 1024 * 1024

PA_W = 5 * MIX_W
PB_W = 4 * MIX_W
PG_W = LANES
PC_W = 3 * MIX_W
PROJ_W = PA_W + PB_W + PG_W + PC_W


def _params(sem, **kw):
    return pltpu.CompilerParams(dimension_semantics=sem, vmem_limit_bytes=VMEM_LIMIT, **kw)


def _const_spec(shape):
    nd = len(shape)
    return pl.BlockSpec(shape, lambda *_: (0,) * nd, pipeline_mode=pl.Buffered(1))


def _dot(a, b):
    return jnp.dot(a, b, preferred_element_type=F32)


def _dot_nt(a, b):
    return lax.dot_general(a, b, (((1,), (1,)), ((), ())), preferred_element_type=F32)


def _dot_tn(a, b):
    return lax.dot_general(a, b, (((0,), (0,)), ((), ())), preferred_element_type=F32)


def _dot_sel(m_bf16, x):
    hi = x.astype(BF16)
    r1 = x - hi.astype(F32)
    mid = r1.astype(BF16)
    lo = (r1 - mid.astype(F32)).astype(BF16)
    return _dot(m_bf16, hi) + _dot(m_bf16, mid) + _dot(m_bf16, lo)


def _sigmoid(x):
    return 1.0 / (1.0 + jnp.exp(-x))


def _log_sigmoid(x):
    return jnp.minimum(x, 0.0) - jnp.log1p(jnp.exp(-jnp.abs(x)))


def _rms(x, g):
    return x * lax.rsqrt(jnp.mean(x * x, axis=-1, keepdims=True) + EPS) * g


def _ada_kernel(c_ref, w_ref, b_ref, o_ref):
    c = c_ref[...]
    s = c * _sigmoid(c)
    o_ref[...] = jnp.dot(s, w_ref[...], preferred_element_type=F32,
                         precision=lax.Precision.HIGHEST) + b_ref[...]


def _ada_mod(cc, ada_w, ada_b):
    depth, dm, six = ada_w.shape
    rows = cc.shape[0]
    tn = dm
    return pl.pallas_call(
        _ada_kernel,
        grid=(depth, six // tn),
        in_specs=[pl.BlockSpec((rows, dm), lambda l, n: (0, 0)),
                  pl.BlockSpec((None, dm, tn), lambda l, n: (l, 0, n)),
                  pl.BlockSpec((None, 1, tn), lambda l, n: (l, 0, n))],
        out_specs=pl.BlockSpec((None, rows, tn), lambda l, n: (l, 0, n)),
        out_shape=jax.ShapeDtypeStruct((depth, rows, six), F32),
        compiler_params=_params(("parallel", "parallel")),
        name="ada_mod",
    )(cc, ada_w, ada_b.reshape(depth, 1, six))


def _rope(x, cos, sin):
    n = x.shape[-1]
    lane = lax.broadcasted_iota(jnp.int32, (1, n), 1)
    first = (lane // ROPE_PAIRS) % 2 == 0
    partner = jnp.where(first, pltpu.roll(x, n - ROPE_PAIRS, 1), pltpu.roll(x, ROPE_PAIRS, 1))
    return x * cos + partner * sin


def _hgrn_log_forget(z, lbc):
    lsig = jnp.minimum(z, 0.0) - jnp.log(1.0 + jnp.exp(-jnp.abs(z)))
    a = lbc[0:1]
    bb = lbc[1:2] + lsig
    return jnp.maximum(a, bb) + jnp.log(1.0 + jnp.exp(-jnp.abs(a - bb)))


def _norm_proj_kernel(x_ref, mod_ref, g_ref, w_ref, cos_ref, sin_ref, lbc_ref,
                      pa_ref, pb_ref, pg_ref, qc_ref, kc_ref, vc_ref):
    x = x_ref[...]
    mod = mod_ref[...]
    h = _rms(x, g_ref[...]) * (1.0 + mod[1:2]) + mod[0:1]
    hb = h.astype(BF16)
    pa_ref[:, 0:3 * MIX_W] = _dot(hb, w_ref[:, 0:3 * MIX_W])
    for d in range(2):
        cs = slice((3 + d) * MIX_W, (4 + d) * MIX_W)
        pa_ref[:, cs] = _hgrn_log_forget(_dot(hb, w_ref[:, cs]), lbc_ref[d])
    pb_ref[...] = _dot(hb, w_ref[:, PA_W:PA_W + PB_W])
    pg_ref[...] = _dot(hb, w_ref[:, PA_W + PB_W:PA_W + PB_W + PG_W])
    c0 = PA_W + PB_W + PG_W
    cos = cos_ref[...]
    sin = sin_ref[...]
    q = _dot(hb, w_ref[:, c0:c0 + MIX_W])
    qc_ref[...] = (_rope(q, cos, sin) * (64.0 ** -0.5)).astype(BF16)
    k = _dot(hb, w_ref[:, c0 + MIX_W:c0 + 2 * MIX_W])
    kc_ref[...] = _rope(k, cos, sin).astype(BF16)
    vc_ref[...] = _dot(hb, w_ref[:, c0 + 2 * MIX_W:c0 + 3 * MIX_W]).astype(BF16)


def _norm_proj(x, mod, g, w, cos, sin, lbc, tm, ctx_tiles):
    bsz, tt, dm = x.shape
    nt = tt // tm
    row = lambda b, t: (b, t, 0)
    outs = [(PA_W, F32), (PB_W, F32), (PG_W, F32), (MIX_W, BF16), (MIX_W, BF16), (MIX_W, BF16)]
    return pl.pallas_call(
        _norm_proj_kernel,
        grid=(bsz, nt),
        in_specs=[pl.BlockSpec((None, tm, dm), row),
                  pl.BlockSpec((None, None, 6, dm), lambda b, t: (b, jnp.where(t >= ctx_tiles, 1, 0), 0, 0)),
                  _const_spec((1, dm)),
                  _const_spec((dm, PROJ_W)),
                  pl.BlockSpec((tm, MIX_W), lambda b, t: (t, 0)),
                  pl.BlockSpec((tm, MIX_W), lambda b, t: (t, 0)),
                  _const_spec(lbc.shape)],
        out_specs=[pl.BlockSpec((None, tm, wd), row) for wd, _ in outs],
        out_shape=[jax.ShapeDtypeStruct((bsz, tt, wd), dt) for wd, dt in outs],
        compiler_params=_params(("parallel", "parallel")),
        name="norm_proj",
    )(x, mod, g, w, cos, sin, lbc)


def _conv_kernel(x_ref, w_ref, b_ref, o_ref, *, ctx, q_blocks):
    x = x_ref[...]
    tt = x.shape[0]
    row = lax.broadcasted_iota(jnp.int32, (tt, 1), 0)
    prev = jnp.where((row == 0) | (row == ctx), 0.0, pltpu.roll(x, 1, 0))
    nxt = jnp.where((row == ctx - 1) | (row == tt - 1), 0.0, pltpu.roll(x, tt - 1, 0))
    w = w_ref[...]
    y = b_ref[...] + prev * w[0:1] + x * w[1:2] + nxt * w[2:3]
    y = y * _sigmoid(y)
    scale = jnp.where(pl.program_id(1) < q_blocks, HEAD_W ** -0.5, 1.0)
    o_ref[...] = y * scale


def _conv_silu(pb, conv_w, conv_b, ctx, cb):
    bsz, tt, _ = pb.shape
    width = 2 * MIX_W
    return pl.pallas_call(
        functools.partial(_conv_kernel, ctx=ctx, q_blocks=MIX_W // cb),
        grid=(bsz, width // cb),
        in_specs=[pl.BlockSpec((None, tt, cb), lambda b, j: (b, 0, j)),
                  pl.BlockSpec((B_CONV, cb), lambda b, j: (0, j)),
                  pl.BlockSpec((1, cb), lambda b, j: (0, j))],
        out_specs=pl.BlockSpec((None, tt, cb), lambda b, j: (b, 0, j)),
        out_shape=jax.ShapeDtypeStruct((bsz, tt, width), F32),
        compiler_params=_params(("parallel", "parallel")),
        name="mlstm_conv",
    )(pb, conv_w, conv_b.reshape(1, width))


def _chunk_consts(L):
    idx = np.arange(L)
    t, u = idx[:, None], idx[None, :]
    cum = (u <= t).astype(np.float32)
    nlev = int(round(math.log2(L)))
    lvl = np.full((L, L), -1, np.int32)
    lvl[idx, idx] = nlev
    for i in range(nlev):
        h = L >> (i + 1)
        same = (t // (2 * h)) == (u // (2 * h))
        lvl[same & (t % (2 * h) >= h) & (u % (2 * h) < h)] = i
    sel = jnp.asarray(np.stack([cum, cum[::-1, ::-1]]), BF16)
    lvls = jnp.asarray(np.stack([lvl, lvl[::-1, ::-1]]))
    return sel, lvls, nlev


def _midpoint_rows(b, h, d):
    L, w = b.shape
    two = 2 * h
    pos = h - 1 if d == 0 else h
    if two % 8 == 0:
        r = b.reshape(L // two, two, w)[:, pos:pos + 1, :]
        return jnp.broadcast_to(r, (L // two, two, w)).reshape(L, w)
    phase = lax.broadcasted_iota(jnp.int32, (L, 1), 0) % two
    ref = b
    for off in range(pos - two + 1, pos + 1):
        if off != 0:
            ref = jnp.where(phase == pos - off, pltpu.roll(b, (L - off) % L, 0), ref)
    return ref


def _scan_blocks(j, nc_ctx, nc):
    jb = jnp.where(j < nc_ctx, nc_ctx - 1 - j, nc - 1 + nc_ctx - j)
    return j, jb


def _hgrn_kernel(qf_ref, vf_ref, ff_ref, qb_ref, vb_ref, fb_ref, sel_ref, lvl_ref,
                 of_ref, ob_ref, st_ref, *, L, nlev):
    @pl.when(pl.program_id(1) == 0)
    def _():
        st_ref[...] = jnp.zeros_like(st_ref)

    row = lax.broadcasted_iota(jnp.int32, (L, 1), 0)
    dirs = ((qf_ref, vf_ref, ff_ref, of_ref), (qb_ref, vb_ref, fb_ref, ob_ref))
    for d, (q_ref, v_ref, f_ref, o_ref) in enumerate(dirs):
        lf = f_ref[...]
        kk = 1.0 - jnp.exp(lf)
        q_all = q_ref[...]
        b_in = _dot_sel(sel_ref[d], lf)
        lvl = lvl_ref[d]
        last = L - 1 if d == 0 else 0
        b_end = b_in[last:last + 1]
        heads = [slice(hd * HEAD_W, (hd + 1) * HEAD_W) for hd in range(HEADS)]
        qb = q_all.astype(BF16)
        kb = kk.astype(BF16)
        on_diag = lvl == nlev
        scores = [jnp.where(on_diag, _dot_nt(qb[:, cs], kb[:, cs]), 0.0) for cs in heads]
        for i in range(nlev):
            h = L >> (i + 1)
            ei = jnp.exp(-jnp.abs(b_in - _midpoint_rows(b_in, h, d)))
            is_query = (row % (2 * h) >= h) if d == 0 else (row % (2 * h) < h)
            xe = (jnp.where(is_query, q_all, kk) * ei).astype(BF16)
            at_level = lvl == i
            for hd, cs in enumerate(heads):
                scores[hd] = scores[hd] + jnp.where(at_level, _dot_nt(xe[:, cs], xe[:, cs]), 0.0)
        q_in = (q_all * jnp.exp(b_in)).astype(BF16)
        k_out = (kk * jnp.exp(b_end - b_in)).astype(BF16)
        decay = jnp.exp(b_end)
        for hd, cs in enumerate(heads):
            vb = v_ref[:, cs].astype(BF16)
            st = st_ref[d, hd]
            o_ref[:, cs] = _dot(scores[hd].astype(BF16), vb) + _dot_nt(q_in[:, cs], st.astype(BF16))
            st_ref[d, hd] = st * decay[:, cs] + _dot_tn(vb, k_out[:, cs])


def _hgrn(pa, L, nc_ctx):
    bsz, tt, _ = pa.shape
    nc = tt // L
    sel, lvl, nlev = _chunk_consts(L)

    def spec(col, which):
        return pl.BlockSpec((None, L, MIX_W), lambda b, j: (b, _scan_blocks(j, nc_ctx, nc)[which], col))

    return pl.pallas_call(
        functools.partial(_hgrn_kernel, L=L, nlev=nlev),
        grid=(bsz, nc),
        in_specs=[spec(0, 0), spec(1, 0), spec(3, 0), spec(0, 1), spec(1, 1), spec(4, 1),
                  _const_spec(sel.shape), _const_spec(lvl.shape)],
        out_specs=[spec(0, 0), spec(0, 1)],
        out_shape=[jax.ShapeDtypeStruct((bsz, tt, MIX_W), F32)] * 2,
        scratch_shapes=[pltpu.VMEM((2, HEADS, HEAD_W, HEAD_W), F32)],
        compiler_params=_params(("parallel", "arbitrary")),
        name="hgrn2_scan",
    )(pa, pa, pa, pa, pa, pa, sel, lvl)


def _mlstm_kernel(qf_ref, kf_ref, vf_ref, gf_ref, qb_ref, kb_ref, vb_ref, gb_ref, gbias_ref,
                  sel_ref, lvl_ref, of_ref, ob_ref, c_ref, m_ref, *, L, nlev):
    @pl.when(pl.program_id(1) == 0)
    def _():
        c_ref[...] = jnp.zeros_like(c_ref)
        m_ref[...] = jnp.zeros_like(m_ref)

    lane = lax.broadcasted_iota(jnp.int32, (1, LANES), 1)
    ones_col = jnp.where(lane == 0, 1.0, 0.0).astype(BF16)
    dirs = ((qf_ref, kf_ref, vf_ref, gf_ref, of_ref), (qb_ref, kb_ref, vb_ref, gb_ref, ob_ref))
    for d, (q_ref, k_ref, v_ref, g_ref, o_ref) in enumerate(dirs):
        g = g_ref[...] + gbias_ref[...]
        lf = jnp.where((lane >= 2 * HEADS) & (lane < 4 * HEADS), _log_sigmoid(g), 0.0)
        bcum = _dot_sel(sel_ref[d, 0:L], lf)
        causal = lvl_ref[d] >= 0
        row_terms = (g - pltpu.roll(bcum, LANES - 2 * HEADS, 1)).T
        last = L - 1 if d == 0 else 0
        for hd in range(HEADS):
            cs = slice(hd * HEAD_W, (hd + 1) * HEAD_W)
            li_lane = d * HEADS + hd
            lf_lane = 2 * HEADS + d * HEADS + hd
            li = g[:, li_lane:li_lane + 1]
            b = bcum[:, lf_lane:lf_lane + 1]
            row_term = row_terms[li_lane:li_lane + 1, :]
            b_end = b[last:last + 1]
            r = d * HEADS + hd
            m_prev = m_ref[r:r + 1, 0:1]
            q = q_ref[:, cs].astype(BF16)
            k = k_ref[:, cs]
            v_aug = jnp.concatenate([v_ref[:, cs].astype(BF16), jnp.broadcast_to(ones_col, (L, LANES))], axis=1)
            c_prev = c_ref[r]
            w_end = b_end - b + li
            m_new = jnp.maximum(b_end + m_prev, jnp.max(w_end, axis=0, keepdims=True))
            e_end = jnp.exp(w_end - m_new)
            keep = jnp.exp(b_end + m_prev - m_new)
            c_ref[r] = keep * c_prev + _dot_tn((k * e_end).astype(BF16), v_aug)
            m_ref[r:r + 1, :] = jnp.broadcast_to(m_new, (1, LANES))
            logw = jnp.where(causal, b + row_term, NEG_BIG)
            w_state = b + m_prev
            m_t = jnp.maximum(jnp.max(logw, axis=-1, keepdims=True), w_state)
            s = _dot_nt(q, k.astype(BF16)) * jnp.exp(logw - m_t)
            a_state = jnp.exp(w_state - m_t)
            num = _dot(s.astype(BF16), v_aug) + a_state * _dot(q, c_prev.astype(BF16))
            den = num[:, HEAD_W:HEAD_W + 1]
            o_ref[:, cs] = num[:, 0:HEAD_W] / jnp.maximum(jnp.abs(den), jnp.exp(-m_t))


def _mlstm(qk, pb, pg, gbias, L, nc_ctx):
    bsz, tt, _ = pb.shape
    nc = tt // L
    sel, lvl, nlev = _chunk_consts(L)

    def spec(col, which, width=MIX_W):
        return pl.BlockSpec((None, L, width), lambda b, j: (b, _scan_blocks(j, nc_ctx, nc)[which], col))

    return pl.pallas_call(
        functools.partial(_mlstm_kernel, L=L, nlev=nlev),
        grid=(bsz, nc),
        in_specs=[spec(0, 0), spec(1, 0), spec(2, 0), spec(0, 0, PG_W),
                  spec(0, 1), spec(1, 1), spec(2, 1), spec(0, 1, PG_W),
                  _const_spec((1, PG_W)), _const_spec(sel.shape), _const_spec(lvl.shape)],
        out_specs=[spec(0, 0), spec(0, 1)],
        out_shape=[jax.ShapeDtypeStruct((bsz, tt, MIX_W), F32)] * 2,
        scratch_shapes=[pltpu.VMEM((2 * HEADS, HEAD_W, 2 * HEAD_W), F32),
                        pltpu.VMEM((2 * HEADS, LANES), F32)],
        compiler_params=_params(("parallel", "arbitrary")),
        name="mlstm_scan",
    )(qk, qk, pb, pg, qk, qk, pb, pg, gbias, sel, lvl)


def _attn_kernel(lam_ref, q_ref, k_ref, v_ref, o_ref, *, lam_init, ctx_tiles, t_ctx, with_ctx):
    lv = lam_ref[...]
    lam = (jnp.exp(jnp.sum(lv[0:1] * lv[1:2], axis=-1, keepdims=True))
           - jnp.exp(jnp.sum(lv[2:3] * lv[3:4], axis=-1, keepdims=True)) + lam_init)
    lane = lax.broadcasted_iota(jnp.int32, (1, HEAD_W), 1)

    def attend(n_k):
        q = q_ref[...]
        k = k_ref[0:n_k, :]
        zero = jnp.zeros_like(q)

        def probs(qm):
            s = _dot_nt(qm, k)
            p = jnp.exp(s - jnp.max(s, axis=-1, keepdims=True))
            return p / jnp.sum(p, axis=-1, keepdims=True)

        a = probs(jnp.where(lane < HEAD_W // 2, q, zero)) - lam * probs(jnp.where(lane >= HEAD_W // 2, q, zero))
        o_ref[...] = _dot(a.astype(BF16), v_ref[0:n_k, :])

    is_ctx = pl.program_id(2) < ctx_tiles

    @pl.when(is_ctx)
    def _():
        if with_ctx:
            attend(t_ctx)
        else:
            o_ref[...] = jnp.zeros_like(o_ref)

    @pl.when(jnp.logical_not(is_ctx))
    def _():
        attend(k_ref.shape[0])


def _attn(lam_vec, qc, kc, vc, lam_init, tq, t_ctx, with_ctx):
    bsz, tt, _ = qc.shape
    kv = pl.BlockSpec((None, tt, HEAD_W), lambda b, h, i: (b, 0, h))
    return pl.pallas_call(
        functools.partial(_attn_kernel, lam_init=lam_init, ctx_tiles=t_ctx // tq, t_ctx=t_ctx, with_ctx=with_ctx),
        grid=(bsz, HEADS, tt // tq),
        in_specs=[_const_spec(lam_vec.shape),
                  pl.BlockSpec((None, tq, HEAD_W), lambda b, h, i: (b, i, h)), kv, kv],
        out_specs=pl.BlockSpec((None, tq, HEAD_W), lambda b, h, i: (b, i, h)),
        out_shape=jax.ShapeDtypeStruct((bsz, tt, MIX_W), F32),
        compiler_params=_params(("parallel", "parallel", "arbitrary")),
        name="diff_attn",
    )(lam_vec, qc, kc, vc)


def _head_norm(x, g):
    parts = []
    for hd in range(HEADS):
        xs = x[:, hd * HEAD_W:(hd + 1) * HEAD_W]
        parts.append(xs * lax.rsqrt(jnp.mean(xs * xs, axis=-1, keepdims=True) + EPS))
    return jnp.concatenate(parts, axis=1) * g


def _route(logits):
    lane = lax.broadcasted_iota(jnp.int32, logits.shape, 1)
    neg = -jnp.inf
    gl = jnp.where(lane < N_GROUPS, logits, neg)
    gmax = jnp.max(gl, axis=-1, keepdims=True)
    gidx = jnp.min(jnp.where(gl == gmax, lane, LANES), axis=-1, keepdims=True)
    p_group = 1.0 / jnp.sum(jnp.exp(gl - gmax), axis=-1, keepdims=True)
    lo = N_GROUPS + EXPERTS_PER_GROUP * gidx
    in_grp = (lane >= lo) & (lane < lo + EXPERTS_PER_GROUP)
    el = jnp.where(in_grp, logits, neg)
    pe = jnp.exp(el - jnp.max(el, axis=-1, keepdims=True))
    pe = pe / jnp.sum(pe, axis=-1, keepdims=True)
    pe = jnp.where(in_grp, pe, -1.0)
    v1 = jnp.max(pe, axis=-1, keepdims=True)
    i1 = jnp.min(jnp.where(pe == v1, lane, LANES), axis=-1, keepdims=True)
    pe2 = jnp.where(lane == i1, -1.0, pe)
    v2 = jnp.max(pe2, axis=-1, keepdims=True)
    i2 = jnp.min(jnp.where(pe2 == v2, lane, LANES), axis=-1, keepdims=True)
    scale = p_group / (v1 + v2)
    e1 = (i1 - N_GROUPS).astype(F32)
    e2 = (i2 - N_GROUPS).astype(F32)
    return jnp.where(lane == 0, e1, jnp.where(lane == 1, e2, jnp.where(lane == 2, v1 * scale,
                     jnp.where(lane == 3, v2 * scale, 0.0))))


def _merge_kernel(x_ref, mod_ref, n1_ref, n2_ref, af_ref, ab_ref, ga_ref, bf_ref, bb_ref, gb_ref, oc_ref,
                  hg_ref, wg_ref, bg_ref, wbr_ref, wo_ref, rw_ref, rb_ref,
                  xo_ref, h2_ref, rt_ref, cnt_ref, *, lam_init):
    x = x_ref[...]
    mod = mod_ref[...]
    dm = x.shape[-1]
    hb = (_rms(x, n1_ref[...]) * (1.0 + mod[1:2]) + mod[0:1]).astype(BF16)
    hg = hg_ref[...]
    ga = ga_ref[...]
    ya = _head_norm(af_ref[...] + ab_ref[...], hg[0:1]) * (ga * _sigmoid(ga))
    yb = _head_norm(bf_ref[...] + bb_ref[...], hg[1:2]) * _sigmoid(gb_ref[...])
    yc = _head_norm(oc_ref[...], hg[2:3]) * (1.0 - lam_init)
    y = jnp.zeros_like(x)
    for i, yi in enumerate((ya, yb, yc)):
        gate = _sigmoid(_dot(hb, wg_ref[:, i * dm:(i + 1) * dm]) + bg_ref[:, i * dm:(i + 1) * dm])
        y = y + gate * _dot(yi.astype(BF16), wbr_ref[i])
    xn = x + mod[2:3] * _dot(y.astype(BF16), wo_ref[...])
    xo_ref[...] = xn
    h2 = _rms(xn, n2_ref[...]) * (1.0 + mod[4:5]) + mod[3:4]
    h2_ref[...] = h2
    logits = jnp.dot(h2, rw_ref[...], preferred_element_type=F32, precision=lax.Precision.HIGHEST) + rb_ref[...]
    rt = _route(logits)
    tm = x.shape[0]
    lane = lax.broadcasted_iota(jnp.int32, (tm, LANES), 1)
    oh1 = lane == rt[:, 0:1].astype(jnp.int32)
    oh2 = lane == rt[:, 1:2].astype(jnp.int32)
    oh1f = jnp.where(oh1, 1.0, 0.0)
    oh2f = jnp.where(oh2, 1.0, 0.0)
    earlier = (lax.broadcasted_iota(jnp.int32, (tm, tm), 0) > lax.broadcasted_iota(jnp.int32, (tm, tm), 1))
    earlier = jnp.where(earlier, 1.0, 0.0).astype(BF16)
    tot1 = jnp.sum(oh1f, axis=0, keepdims=True)
    tot2 = jnp.sum(oh2f, axis=0, keepdims=True)
    rank1 = jnp.sum(jnp.where(oh1, _dot(earlier, oh1f.astype(BF16)), 0.0), axis=-1, keepdims=True)
    rank2 = jnp.sum(jnp.where(oh2, _dot(earlier, oh2f.astype(BF16)) + tot1, 0.0), axis=-1, keepdims=True)
    rt_ref[...] = jnp.where(lane == 4, rank1, jnp.where(lane == 5, rank2, rt))
    cnt_ref[...] = jnp.broadcast_to(tot1 + tot2, cnt_ref.shape)


def _merge(x, mod, n1, n2, oaf, oab, pa, obf, obb, pb, oc, hg, wg, bg, wbr, wo, rw, rb,
           lam_init, tm, ctx_tiles, row0_tiles):
    bsz, tt, dm = x.shape
    nt = tt // tm - row0_tiles
    rows = nt * tm
    src = lambda b, t: (b, row0_tiles + t, 0)

    def col(c):
        return pl.BlockSpec((None, tm, MIX_W), lambda b, t: (b, row0_tiles + t, c))

    dst = lambda b, t: (b, t, 0)
    outs = [(dm, F32), (dm, F32), (LANES, F32)]
    out_specs = [pl.BlockSpec((None, tm, wd), dst) for wd, _ in outs]
    out_shape = [jax.ShapeDtypeStruct((bsz, rows, wd), dt) for wd, dt in outs]
    out_specs.append(pl.BlockSpec((None, 8, LANES), dst))
    out_shape.append(jax.ShapeDtypeStruct((bsz, nt * 8, LANES), F32))
    return pl.pallas_call(
        functools.partial(_merge_kernel, lam_init=lam_init),
        grid=(bsz, nt),
        in_specs=[pl.BlockSpec((None, tm, dm), src),
                  pl.BlockSpec((None, None, 6, dm),
                               lambda b, t: (b, jnp.where(row0_tiles + t >= ctx_tiles, 1, 0), 0, 0)),
                  _const_spec((1, dm)), _const_spec((1, dm)),
                  col(0), col(0), col(2), col(0), col(0), col(3), col(0),
                  _const_spec((3, MIX_W)), _const_spec(wg.shape), _const_spec(bg.shape),
                  _const_spec(wbr.shape), _const_spec(wo.shape), _const_spec(rw.shape), _const_spec(rb.shape)],
        out_specs=out_specs,
        out_shape=out_shape,
        compiler_params=_params(("parallel", "parallel")),
        name="merge_route",
    )(x, mod, n1, n2, oaf, oab, pa, obf, obb, pb, oc, hg, wg, bg, wbr, wo, rw, rb)


def _slot_tables(route, counts, tm):
    bsz, rows, _ = route.shape
    nt = rows // tm
    cnt = counts[:, ::8, :N_EXPERTS].astype(jnp.int32).reshape(bsz * nt, N_EXPERTS)
    tile_off = jnp.cumsum(cnt, axis=0) - cnt
    total = jnp.sum(cnt, axis=0)
    padded = (total + MOE_BLOCK - 1) // MOE_BLOCK * MOE_BLOCK
    p_end = jnp.cumsum(padded)
    base = (p_end - padded)[None, :] + tile_off
    e = route[:, :, 0:2].astype(jnp.int32).reshape(bsz * nt, tm, 2, 1)
    rank = route[:, :, 4:6].astype(jnp.int32).reshape(bsz * nt, tm, 2)
    hit = e == jnp.arange(N_EXPERTS, dtype=jnp.int32).reshape(1, 1, 1, N_EXPERTS)
    slot = jnp.sum(jnp.where(hit, base[:, None, None, :], 0), axis=-1) + rank
    n_blocks = -(-(2 * bsz * rows) // MOE_BLOCK) + N_EXPERTS
    blk_start = jnp.arange(n_blocks, dtype=jnp.int32) * MOE_BLOCK
    block_e = jnp.minimum(jnp.sum(blk_start[:, None] >= p_end[None, :], axis=1), N_EXPERTS - 1).astype(jnp.int32)
    n_used = (p_end[-1] // MOE_BLOCK).astype(jnp.int32).reshape(1)
    return slot.reshape(bsz * nt, 1, 2 * tm), block_e, n_used, n_blocks


def _row_copies(n_rows, make):
    for r in range(n_rows):
        for k in range(2):
            make(r, k).start()


def _scatter_kernel(slot_ref, h_ref, init_ref, xd_ref, sem):
    tm = h_ref.shape[0]
    _row_copies(tm, lambda r, k: pltpu.make_async_copy(
        h_ref.at[pl.ds(r, 1), :], xd_ref.at[pl.ds(slot_ref[0, 2 * r + k], 1), :], sem))
    for _ in range(2):
        pltpu.make_async_copy(h_ref, xd_ref.at[pl.ds(0, tm), :], sem).wait()


def _scatter(slot, h2, n_slots, tm):
    bsz, rows, dm = h2.shape
    nt = rows // tm
    init = jnp.zeros((n_slots, dm), F32)
    return pl.pallas_call(
        _scatter_kernel,
        grid=(bsz, nt),
        in_specs=[pl.BlockSpec((None, 1, 2 * tm), lambda b, t: (b * nt + t, 0, 0), memory_space=pltpu.SMEM),
                  pl.BlockSpec((tm, dm), lambda b, t: (b * nt + t, 0)),
                  pl.BlockSpec(memory_space=pl.ANY)],
        out_specs=pl.BlockSpec(memory_space=pl.ANY),
        out_shape=jax.ShapeDtypeStruct((n_slots, dm), F32),
        scratch_shapes=[pltpu.SemaphoreType.DMA(())],
        input_output_aliases={2: 0},
        compiler_params=_params(("arbitrary", "arbitrary"), has_side_effects=True),
        name="moe_scatter",
    )(slot, h2.reshape(bsz * rows, dm), init)


def _expert_kernel(be_ref, nu_ref, x_ref, w1_ref, w3_ref, w2_ref, y_ref, w1b, w3b, w2b):
    i = pl.program_id(0)

    @pl.when(i < nu_ref[0])
    def _():
        @pl.when((i == 0) | (be_ref[i] != be_ref[jnp.maximum(i - 1, 0)]))
        def _():
            w1b[...] = w1_ref[...].astype(BF16)
            w3b[...] = w3_ref[...].astype(BF16)
            w2b[...] = w2_ref[...].astype(BF16)

        xb = x_ref[...].astype(BF16)
        u = _dot(xb, w1b[...])
        hmid = (u * _sigmoid(u)) * _dot(xb, w3b[...])
        y_ref[...] = _dot(hmid.astype(BF16), w2b[...])

    @pl.when(i >= nu_ref[0])
    def _():
        y_ref[...] = jnp.zeros_like(y_ref)


def _experts(block_e, n_used, x_disp, w1, w3, w2):
    n_slots, dm = x_disp.shape
    n_blocks = n_slots // MOE_BLOCK
    wspec = lambda shape: pl.BlockSpec((None,) + shape, lambda i, be, nu: (be[i], 0, 0))
    grid_spec = pltpu.PrefetchScalarGridSpec(
        num_scalar_prefetch=2,
        grid=(n_blocks,),
        in_specs=[pl.BlockSpec((MOE_BLOCK, dm), lambda i, be, nu: (jnp.minimum(i, nu[0] - 1), 0)),
                  wspec((dm, D_EXPERT)), wspec((dm, D_EXPERT)), wspec((D_EXPERT, dm))],
        out_specs=pl.BlockSpec((MOE_BLOCK, dm), lambda i, be, nu: (i, 0)),
        scratch_shapes=[pltpu.VMEM((dm, D_EXPERT), BF16), pltpu.VMEM((dm, D_EXPERT), BF16),
                        pltpu.VMEM((D_EXPERT, dm), BF16)],
    )
    return pl.pallas_call(
        _expert_kernel,
        grid_spec=grid_spec,
        out_shape=jax.ShapeDtypeStruct((n_slots, dm), F32),
        compiler_params=_params(("arbitrary",)),
        name="moe_experts",
    )(block_e, n_used, x_disp, w1, w3, w2)


def _combine_kernel(slot_ref, x_ref, mod_ref, rt_ref, yd_ref, fg_ref, o_ref, ybuf, sem, *, final):
    tm = x_ref.shape[0]
    _row_copies(tm, lambda r, k: pltpu.make_async_copy(
        yd_ref.at[pl.ds(slot_ref[0, 2 * r + k], 1), :], ybuf.at[k, pl.ds(r, 1), :], sem))
    for k in range(2):
        pltpu.make_async_copy(yd_ref.at[pl.ds(0, tm), :], ybuf.at[k], sem).wait()
    rt = rt_ref[...]
    y = rt[:, 2:3] * ybuf[0] + rt[:, 3:4] * ybuf[1]
    xn = x_ref[...] + mod_ref[5:6, :] * y
    o_ref[...] = _rms(xn, fg_ref[...]) if final else xn


def _combine(slot, x, mod, route, y_disp, fg, final, tm, ctx_tiles, row0_tiles):
    bsz, rows, dm = x.shape
    nt = rows // tm
    return pl.pallas_call(
        functools.partial(_combine_kernel, final=final),
        grid=(bsz, nt),
        in_specs=[pl.BlockSpec((None, 1, 2 * tm), lambda b, t: (b * nt + t, 0, 0), memory_space=pltpu.SMEM),
                  pl.BlockSpec((None, tm, dm), lambda b, t: (b, t, 0)),
                  pl.BlockSpec((None, None, 6, dm),
                               lambda b, t: (b, jnp.where(row0_tiles + t >= ctx_tiles, 1, 0), 0, 0)),
                  pl.BlockSpec((None, tm, LANES), lambda b, t: (b, t, 0)),
                  pl.BlockSpec(memory_space=pl.ANY),
                  _const_spec((1, dm))],
        out_specs=pl.BlockSpec((None, tm, dm), lambda b, t: (b, t, 0)),
        out_shape=jax.ShapeDtypeStruct((bsz, rows, dm), F32),
        scratch_shapes=[pltpu.VMEM((2, tm, dm), F32), pltpu.SemaphoreType.DMA(())],
        compiler_params=_params(("arbitrary", "arbitrary")),
        name="moe_combine",
    )(slot, x, mod, route, y_disp, fg)


def _rope_tables(ctx, t_lat):
    rows = t_lat // GRID_W
    row = jnp.repeat(jnp.arange(rows, dtype=F32), GRID_W)
    col = jnp.tile(jnp.arange(GRID_W, dtype=F32), rows)
    inv = ROPE_BASE ** (-jnp.arange(ROPE_PAIRS, dtype=F32) / ROPE_PAIRS)
    ang_r = row[:, None] * inv
    ang_c = col[:, None] * inv
    cos64 = jnp.concatenate([jnp.cos(ang_r), jnp.cos(ang_r), jnp.cos(ang_c), jnp.cos(ang_c)], axis=1)
    sin64 = jnp.concatenate([-jnp.sin(ang_r), jnp.sin(ang_r), -jnp.sin(ang_c), jnp.sin(ang_c)], axis=1)
    cos = jnp.tile(cos64, (1, MIX_W // 64))
    sin = jnp.tile(sin64, (1, MIX_W // 64))
    cos = jnp.concatenate([jnp.ones((ctx, MIX_W), F32), cos], axis=0)
    sin = jnp.concatenate([jnp.zeros((ctx, MIX_W), F32), sin], axis=0)
    return cos, sin


def _pack_w_in(w):
    dm = w.shape[0]
    a_end = PA_W
    b_end = a_end + PB_W
    g_end = b_end + 4 * HEADS
    pad = jnp.zeros((dm, PG_W - 4 * HEADS), w.dtype)
    return jnp.concatenate([w[:, :b_end], w[:, b_end:g_end], pad, w[:, g_end:]], axis=1).astype(BF16)


def kernel(x, c, ctx, c_ctx, ada_w, ada_b, norm1_g, norm2_g, w_in, mlstm_conv_w, mlstm_conv_b, mlstm_gate_b,
           hgrn_lb_raw, hgrn_norm_g, mlstm_norm_g, diff_norm_g, diff_lambda, w_branch, w_gate, b_gate, w_out,
           router_g_w, router_g_b, router_e_w, router_e_b, moe_w1, moe_w3, moe_w2, final_g):
    bsz, t_lat, dm = x.shape
    t_ctx = ctx.shape[1]
    depth = ada_w.shape[0]
    tt = t_ctx + t_lat
    tm = min(256, t_ctx)
    hgrn_chunk = min(128, t_ctx)
    mlstm_chunk = min(256, t_ctx)
    assert t_ctx % tm == 0 and t_lat % tm == 0 and t_lat % GRID_W == 0
    assert t_ctx % hgrn_chunk == 0 and t_ctx % mlstm_chunk == 0
    ctx_tiles = t_ctx // tm

    n_rows = -(-(bsz + 1) // 8) * 8
    cc = jnp.zeros((n_rows, dm), F32).at[:bsz].set(c).at[bsz].set(c_ctx)
    mods = _ada_mod(cc, ada_w, ada_b).reshape(depth, n_rows, 6, dm)

    cos, sin = _rope_tables(t_ctx, t_lat)
    lb_cum = jnp.cumsum(jax.nn.softmax(hgrn_lb_raw.astype(F32), axis=0), axis=0)
    lower = lb_cum - lb_cum[0]

    xc = jnp.concatenate([ctx, x], axis=1)
    out = None
    for l in range(depth):
        with_ctx = l < depth - 1
        lam_init = 0.8 - 0.6 * math.exp(-0.3 * l)
        mod = jnp.stack([jnp.broadcast_to(mods[l, bsz], (bsz, 6, dm)), mods[l, :bsz]], axis=1)
        n1 = norm1_g[l].reshape(1, dm)
        n2 = norm2_g[l].reshape(1, dm)

        lb = lower[l]
        pad = jnp.zeros((2, 6, MIX_W), F32)
        lbc = jnp.concatenate([jnp.log(lb)[:, None], jnp.log1p(-lb)[:, None], pad], axis=1)
        pa, pb, pg, qc, kc, vc = _norm_proj(xc, mod, n1, _pack_w_in(w_in[l]), cos, sin, lbc, tm, ctx_tiles)

        oaf, oab = _hgrn(pa, hgrn_chunk, t_ctx // hgrn_chunk)

        qk = _conv_silu(pb, mlstm_conv_w[l], mlstm_conv_b[l], t_ctx, 256)
        gbias = jnp.zeros((1, PG_W), F32).at[0, :4 * HEADS].set(mlstm_gate_b[l].reshape(-1))
        obf, obb = _mlstm(qk, pb, pg, gbias, mlstm_chunk, t_ctx // mlstm_chunk)

        lam_vec = diff_lambda[l].astype(F32)
        oc = _attn(lam_vec, qc, kc, vc, lam_init, tm, t_ctx, with_ctx)

        hg = jnp.stack([hgrn_norm_g[l], mlstm_norm_g[l], diff_norm_g[l]])
        rw = jnp.zeros((dm, LANES), F32).at[:, :N_GROUPS].set(router_g_w[l])
        rw = rw.at[:, N_GROUPS:N_GROUPS + N_EXPERTS].set(router_e_w[l])
        rb = jnp.zeros((1, LANES), F32).at[0, :N_GROUPS].set(router_g_b[l])
        rb = rb.at[0, N_GROUPS:N_GROUPS + N_EXPERTS].set(router_e_b[l])
        row0_tiles = 0 if with_ctx else ctx_tiles
        xn, h2, route, counts = _merge(xc, mod, n1, n2, oaf, oab, pa, obf, obb, pb, oc, hg,
                                       w_gate[l].astype(BF16), b_gate[l].reshape(1, 3 * dm),
                                       w_branch[l].astype(BF16), w_out[l].astype(BF16), rw, rb,
                                       lam_init, tm, ctx_tiles, row0_tiles)

        slot, block_e, n_used, n_blocks = _slot_tables(route, counts, tm)
        x_disp = _scatter(slot, h2, n_blocks * MOE_BLOCK, tm)
        y_disp = _experts(block_e, n_used, x_disp, moe_w1[l], moe_w3[l], moe_w2[l])
        final = l == depth - 1
        xc = _combine(slot, xn, mod, route, y_disp, final_g.reshape(1, dm), final, tm, ctx_tiles, row0_tiles)
        out = xc
    return out
```

```python
import functools
import math

import numpy as np
import jax
import jax.numpy as jnp
from jax import lax
from jax.experimental import pallas as pl
from jax.experimental.pallas import tpu as pltpu

F32 = jnp.float32
BF16 = jnp.bfloat16

EPS = 1e-6
NEG_BIG = -1e30
HEADS = 4
HEAD_W = 128
MIX_W = HEADS * HEAD_W
GRID_W = 64
ROPE_BASE = 10000.0
ROPE_PAIRS = 16
B_CONV = 3
N_GROUPS = 4
EXPERTS_PER_GROUP = 8
N_EXPERTS = N_GROUPS * EXPERTS_PER_GROUP
D_EXPERT = 512
MOE_BLOCK = 256
LANES = 128
VMEM_LIMIT = 50 * 1024 * 1024

PA_W = 5 * MIX_W
PB_W = 4 * MIX_W
PG_W = LANES
PC_W = 3 * MIX_W
PROJ_W = PA_W + PB_W + PG_W + PC_W


def _params(sem, **kw):
    return pltpu.CompilerParams(dimension_semantics=sem, vmem_limit_bytes=VMEM_LIMIT, **kw)


def _const_spec(shape):
    nd = len(shape)
    return pl.BlockSpec(shape, lambda *_: (0,) * nd, pipeline_mode=pl.Buffered(1))


def _dot(a, b):
    return jnp.dot(a, b, preferred_element_type=F32)


def _dot_nt(a, b):
    return lax.dot_general(a, b, (((1,), (1,)), ((), ())), preferred_element_type=F32)


def _dot_tn(a, b):
    return lax.dot_general(a, b, (((0,), (0,)), ((), ())), preferred_element_type=F32)


def _dot_sel(m_bf16, x):
    hi = x.astype(BF16)
    r1 = x - hi.astype(F32)
    mid = r1.astype(BF16)
    lo = (r1 - mid.astype(F32)).astype(BF16)
    return _dot(m_bf16, hi) + _dot(m_bf16, mid) + _dot(m_bf16, lo)


def _sigmoid(x):
    return 1.0 / (1.0 + jnp.exp(-x))


def _log_sigmoid(x):
    return jnp.minimum(x, 0.0) - jnp.log1p(jnp.exp(-jnp.abs(x)))


def _rms(x, g):
    return x * lax.rsqrt(jnp.mean(x * x, axis=-1, keepdims=True) + EPS) * g


def _ada_kernel(c_ref, w_ref, b_ref, o_ref):
    c = c_ref[...]
    s = c * _sigmoid(c)
    o_ref[...] = jnp.dot(s, w_ref[...], preferred_element_type=F32,
                         precision=lax.Precision.HIGHEST) + b_ref[...]


def _ada_mod(cc, ada_w, ada_b):
    depth, dm, six = ada_w.shape
    rows = cc.shape[0]
    tn = dm
    return pl.pallas_call(
        _ada_kernel,
        grid=(depth, six // tn),
        in_specs=[pl.BlockSpec((rows, dm), lambda l, n: (0, 0)),
                  pl.BlockSpec((None, dm, tn), lambda l, n: (l, 0, n)),
                  pl.BlockSpec((None, 1, tn), lambda l, n: (l, 0, n))],
        out_specs=pl.BlockSpec((None, rows, tn), lambda l, n: (l, 0, n)),
        out_shape=jax.ShapeDtypeStruct((depth, rows, six), F32),
        compiler_params=_params(("parallel", "parallel")),
        name="ada_mod",
    )(cc, ada_w, ada_b.reshape(depth, 1, six))


def _rope(x, cos, sin):
    n = x.shape[-1]
    lane = lax.broadcasted_iota(jnp.int32, (1, n), 1)
    first = (lane // ROPE_PAIRS) % 2 == 0
    partner = jnp.where(first, pltpu.roll(x, n - ROPE_PAIRS, 1), pltpu.roll(x, ROPE_PAIRS, 1))
    return x * cos + partner * sin


def _hgrn_log_forget(z, lbc):
    lsig = jnp.minimum(z, 0.0) - jnp.log(1.0 + jnp.exp(-jnp.abs(z)))
    a = lbc[0:1]
    bb = lbc[1:2] + lsig
    return jnp.maximum(a, bb) + jnp.log(1.0 + jnp.exp(-jnp.abs(a - bb)))


def _norm_proj_kernel(x_ref, mod_ref, g_ref, w_ref, cos_ref, sin_ref, lbc_ref,
                      pa_ref, pb_ref, pg_ref, qc_ref, kc_ref, vc_ref):
    x = x_ref[...]
    mod = mod_ref[...]
    h = _rms(x, g_ref[...]) * (1.0 + mod[1:2]) + mod[0:1]
    hb = h.astype(BF16)
    pa_ref[:, 0:3 * MIX_W] = _dot(hb, w_ref[:, 0:3 * MIX_W])
    for d in range(2):
        cs = slice((3 + d) * MIX_W, (4 + d) * MIX_W)
        pa_ref[:, cs] = _hgrn_log_forget(_dot(hb, w_ref[:, cs]), lbc_ref[d])
    pb_ref[...] = _dot(hb, w_ref[:, PA_W:PA_W + PB_W])
    pg_ref[...] = _dot(hb, w_ref[:, PA_W + PB_W:PA_W + PB_W + PG_W])
    c0 = PA_W + PB_W + PG_W
    cos = cos_ref[...]
    sin = sin_ref[...]
    q = _dot(hb, w_ref[:, c0:c0 + MIX_W])
    qc_ref[...] = (_rope(q, cos, sin) * (64.0 ** -0.5 * math.log2(math.e))).astype(BF16)
    k = _dot(hb, w_ref[:, c0 + MIX_W:c0 + 2 * MIX_W])
    kc_ref[...] = _rope(k, cos, sin).astype(BF16)
    vc_ref[...] = _dot(hb, w_ref[:, c0 + 2 * MIX_W:c0 + 3 * MIX_W]).astype(BF16)


def _norm_proj(x, mod, g, w, cos, sin, lbc, tm, ctx_tiles):
    bsz, tt, dm = x.shape
    nt = tt // tm
    row = lambda b, t: (b, t, 0)
    outs = [(PA_W, F32), (PB_W, F32), (PG_W, F32), (MIX_W, BF16), (MIX_W, BF16), (MIX_W, BF16)]
    return pl.pallas_call(
        _norm_proj_kernel,
        grid=(bsz, nt),
        in_specs=[pl.BlockSpec((None, tm, dm), row),
                  pl.BlockSpec((None, None, 6, dm), lambda b, t: (b, jnp.where(t >= ctx_tiles, 1, 0), 0, 0)),
                  _const_spec((1, dm)),
                  _const_spec((dm, PROJ_W)),
                  pl.BlockSpec((tm, MIX_W), lambda b, t: (t, 0)),
                  pl.BlockSpec((tm, MIX_W), lambda b, t: (t, 0)),
                  _const_spec(lbc.shape)],
        out_specs=[pl.BlockSpec((None, tm, wd), row) for wd, _ in outs],
        out_shape=[jax.ShapeDtypeStruct((bsz, tt, wd), dt) for wd, dt in outs],
        compiler_params=_params(("parallel", "parallel")),
        name="norm_proj",
    )(x, mod, g, w, cos, sin, lbc)


def _conv_kernel(x_ref, w_ref, b_ref, o_ref, *, ctx, q_blocks):
    x = x_ref[...]
    tt = x.shape[0]
    row = lax.broadcasted_iota(jnp.int32, (tt, 1), 0)
    prev = jnp.where((row == 0) | (row == ctx), 0.0, pltpu.roll(x, 1, 0))
    nxt = jnp.where((row == ctx - 1) | (row == tt - 1), 0.0, pltpu.roll(x, tt - 1, 0))
    w = w_ref[...]
    y = b_ref[...] + prev * w[0:1] + x * w[1:2] + nxt * w[2:3]
    y = y * _sigmoid(y)
    scale = jnp.where(pl.program_id(1) < q_blocks, HEAD_W ** -0.5, 1.0)
    o_ref[...] = y * scale


def _conv_silu(pb, conv_w, conv_b, ctx, cb):
    bsz, tt, _ = pb.shape
    width = 2 * MIX_W
    return pl.pallas_call(
        functools.partial(_conv_kernel, ctx=ctx, q_blocks=MIX_W // cb),
        grid=(bsz, width // cb),
        in_specs=[pl.BlockSpec((None, tt, cb), lambda b, j: (b, 0, j)),
                  pl.BlockSpec((B_CONV, cb), lambda b, j: (0, j)),
                  pl.BlockSpec((1, cb), lambda b, j: (0, j))],
        out_specs=pl.BlockSpec((None, tt, cb), lambda b, j: (b, 0, j)),
        out_shape=jax.ShapeDtypeStruct((bsz, tt, width), F32),
        compiler_params=_params(("parallel", "parallel")),
        name="mlstm_conv",
    )(pb, conv_w, conv_b.reshape(1, width))


def _chunk_consts(L):
    idx = np.arange(L)
    t, u = idx[:, None], idx[None, :]
    cum = (u <= t).astype(np.float32)
    nlev = int(round(math.log2(L)))
    lvl = np.full((L, L), -1, np.int32)
    lvl[idx, idx] = nlev
    for i in range(nlev):
        h = L >> (i + 1)
        same = (t // (2 * h)) == (u // (2 * h))
        lvl[same & (t % (2 * h) >= h) & (u % (2 * h) < h)] = i
    sel = jnp.asarray(np.stack([cum, cum[::-1, ::-1]]), BF16)
    lvls = jnp.asarray(np.stack([lvl, lvl[::-1, ::-1]]))
    return sel, lvls, nlev


def _midpoint_rows(b, h, d):
    L, w = b.shape
    two = 2 * h
    pos = h - 1 if d == 0 else h
    if two % 8 == 0:
        r = b.reshape(L // two, two, w)[:, pos:pos + 1, :]
        return jnp.broadcast_to(r, (L // two, two, w)).reshape(L, w)
    phase = lax.broadcasted_iota(jnp.int32, (L, 1), 0) % two
    ref = b
    for off in range(pos - two + 1, pos + 1):
        if off != 0:
            ref = jnp.where(phase == pos - off, pltpu.roll(b, (L - off) % L, 0), ref)
    return ref


def _scan_blocks(j, nc_ctx, nc):
    jb = jnp.where(j < nc_ctx, nc_ctx - 1 - j, nc - 1 + nc_ctx - j)
    return j, jb


def _hgrn_kernel(qf_ref, vf_ref, ff_ref, qb_ref, vb_ref, fb_ref, sel_ref, lvl_ref,
                 of_ref, ob_ref, st_ref, *, L, nlev):
    @pl.when(pl.program_id(1) == 0)
    def _():
        st_ref[...] = jnp.zeros_like(st_ref)

    row = lax.broadcasted_iota(jnp.int32, (L, 1), 0)
    dirs = ((qf_ref, vf_ref, ff_ref, of_ref), (qb_ref, vb_ref, fb_ref, ob_ref))
    for d, (q_ref, v_ref, f_ref, o_ref) in enumerate(dirs):
        lf = f_ref[...]
        kk = 1.0 - jnp.exp(lf)
        q_all = q_ref[...]
        b_in = _dot_sel(sel_ref[d], lf)
        lvl = lvl_ref[d]
        last = L - 1 if d == 0 else 0
        b_end = b_in[last:last + 1]
        heads = [slice(hd * HEAD_W, (hd + 1) * HEAD_W) for hd in range(HEADS)]
        qb = q_all.astype(BF16)
        kb = kk.astype(BF16)
        on_diag = lvl == nlev
        scores = [jnp.where(on_diag, _dot_nt(qb[:, cs], kb[:, cs]), 0.0) for cs in heads]
        for i in range(nlev):
            h = L >> (i + 1)
            ei = jnp.exp(-jnp.abs(b_in - _midpoint_rows(b_in, h, d)))
            is_query = (row % (2 * h) >= h) if d == 0 else (row % (2 * h) < h)
            xe = (jnp.where(is_query, q_all, kk) * ei).astype(BF16)
            at_level = lvl == i
            for hd, cs in enumerate(heads):
                scores[hd] = scores[hd] + jnp.where(at_level, _dot_nt(xe[:, cs], xe[:, cs]), 0.0)
        q_in = (q_all * jnp.exp(b_in)).astype(BF16)
        k_out = (kk * jnp.exp(b_end - b_in)).astype(BF16)
        decay = jnp.exp(b_end)
        for hd, cs in enumerate(heads):
            vb = v_ref[:, cs].astype(BF16)
            st = st_ref[d, hd]
            o_ref[:, cs] = _dot(scores[hd].astype(BF16), vb) + _dot_nt(q_in[:, cs], st.astype(BF16))
            st_ref[d, hd] = st * decay[:, cs] + _dot_tn(vb, k_out[:, cs])


def _hgrn(pa, L, nc_ctx):
    bsz, tt, _ = pa.shape
    nc = tt // L
    sel, lvl, nlev = _chunk_consts(L)

    def spec(col, which):
        return pl.BlockSpec((None, L, MIX_W), lambda b, j: (b, _scan_blocks(j, nc_ctx, nc)[which], col))

    return pl.pallas_call(
        functools.partial(_hgrn_kernel, L=L, nlev=nlev),
        grid=(bsz, nc),
        in_specs=[spec(0, 0), spec(1, 0), spec(3, 0), spec(0, 1), spec(1, 1), spec(4, 1),
                  _const_spec(sel.shape), _const_spec(lvl.shape)],
        out_specs=[spec(0, 0), spec(0, 1)],
        out_shape=[jax.ShapeDtypeStruct((bsz, tt, MIX_W), F32)] * 2,
        scratch_shapes=[pltpu.VMEM((2, HEADS, HEAD_W, HEAD_W), F32)],
        compiler_params=_params(("parallel", "arbitrary")),
        name="hgrn2_scan",
    )(pa, pa, pa, pa, pa, pa, sel, lvl)


def _mlstm_kernel(qf_ref, kf_ref, vf_ref, gf_ref, qb_ref, kb_ref, vb_ref, gb_ref, gbias_ref,
                  sel_ref, lvl_ref, of_ref, ob_ref, c_ref, m_ref, *, L, nlev):
    @pl.when(pl.program_id(1) == 0)
    def _():
        c_ref[...] = jnp.zeros_like(c_ref)
        m_ref[...] = jnp.zeros_like(m_ref)

    lane = lax.broadcasted_iota(jnp.int32, (1, LANES), 1)
    ones_col = jnp.where(lane == 0, 1.0, 0.0).astype(BF16)
    dirs = ((qf_ref, kf_ref, vf_ref, gf_ref, of_ref), (qb_ref, kb_ref, vb_ref, gb_ref, ob_ref))
    for d, (q_ref, k_ref, v_ref, g_ref, o_ref) in enumerate(dirs):
        g = g_ref[...] + gbias_ref[...]
        lf = jnp.where((lane >= 2 * HEADS) & (lane < 4 * HEADS), _log_sigmoid(g), 0.0)
        bcum = _dot_sel(sel_ref[d, 0:L], lf)
        causal = lvl_ref[d] >= 0
        row_terms = (g - pltpu.roll(bcum, LANES - 2 * HEADS, 1)).T
        last = L - 1 if d == 0 else 0
        for hd in range(HEADS):
            cs = slice(hd * HEAD_W, (hd + 1) * HEAD_W)
            li_lane = d * HEADS + hd
            lf_lane = 2 * HEADS + d * HEADS + hd
            b = bcum[:, lf_lane:lf_lane + 1]
            row_term = row_terms[li_lane:li_lane + 1, :]
            b_end = b[last:last + 1]
            r = d * HEADS + hd
            m_prev = m_ref[r:r + 1, 0:1]
            q = q_ref[:, cs].astype(BF16)
            k = k_ref[:, cs]
            v_aug = jnp.concatenate([v_ref[:, cs].astype(BF16), jnp.broadcast_to(ones_col, (L, LANES))], axis=1)
            c_prev = c_ref[r]
            w_end = b_end + row_term
            m_new = jnp.maximum(b_end + m_prev, jnp.max(w_end, axis=-1, keepdims=True))
            e_end = jnp.exp(w_end - m_new)
            keep = jnp.exp(b_end + m_prev - m_new)
            c_ref[r] = keep * c_prev + _dot((k.T * e_end).astype(BF16), v_aug)
            m_ref[r:r + 1, :] = jnp.broadcast_to(m_new, (1, LANES))
            logw = jnp.where(causal, b + row_term, NEG_BIG)
            w_state = b + m_prev
            m_t = jnp.maximum(jnp.max(logw, axis=-1, keepdims=True), w_state)
            s = _dot_nt(q, k.astype(BF16)) * jnp.exp(logw - m_t)
            a_state = jnp.exp(w_state - m_t)
            num = _dot(s.astype(BF16), v_aug) + a_state * _dot(q, c_prev.astype(BF16))
            den = num[:, HEAD_W:HEAD_W + 1]
            o_ref[:, cs] = num[:, 0:HEAD_W] / jnp.maximum(jnp.abs(den), jnp.exp(-m_t))


def _mlstm(qk, pb, pg, gbias, L, nc_ctx):
    bsz, tt, _ = pb.shape
    nc = tt // L
    sel, lvl, nlev = _chunk_consts(L)

    def spec(col, which, width=MIX_W):
        return pl.BlockSpec((None, L, width), lambda b, j: (b, _scan_blocks(j, nc_ctx, nc)[which], col))

    return pl.pallas_call(
        functools.partial(_mlstm_kernel, L=L, nlev=nlev),
        grid=(bsz, nc),
        in_specs=[spec(0, 0), spec(1, 0), spec(2, 0), spec(0, 0, PG_W),
                  spec(0, 1), spec(1, 1), spec(2, 1), spec(0, 1, PG_W),
                  _const_spec((1, PG_W)), _const_spec(sel.shape), _const_spec(lvl.shape)],
        out_specs=[spec(0, 0), spec(0, 1)],
        out_shape=[jax.ShapeDtypeStruct((bsz, tt, MIX_W), F32)] * 2,
        scratch_shapes=[pltpu.VMEM((2 * HEADS, HEAD_W, 2 * HEAD_W), F32),
                        pltpu.VMEM((2 * HEADS, LANES), F32)],
        compiler_params=_params(("parallel", "arbitrary")),
        name="mlstm_scan",
    )(qk, qk, pb, pg, qk, qk, pb, pg, gbias, sel, lvl)


def _attn_kernel(lam_ref, q_ref, k_ref, v_ref, o_ref, *, lam_init, ctx_tiles, t_ctx, with_ctx):
    lv = lam_ref[...]
    lam = (jnp.exp(jnp.sum(lv[0:1] * lv[1:2], axis=-1, keepdims=True))
           - jnp.exp(jnp.sum(lv[2:3] * lv[3:4], axis=-1, keepdims=True)) + lam_init)
    lane = lax.broadcasted_iota(jnp.int32, (1, HEAD_W), 1)

    def attend(n_k):
        q = q_ref[...]
        k = k_ref[0:n_k, :]
        zero = jnp.zeros_like(q)

        def probs(qm):
            s = _dot_nt(qm, k)
            p = jnp.exp2(s - jnp.max(s, axis=-1, keepdims=True))
            return p / jnp.sum(p, axis=-1, keepdims=True)

        a = probs(jnp.where(lane < HEAD_W // 2, q, zero)) - lam * probs(jnp.where(lane >= HEAD_W // 2, q, zero))
        o_ref[...] = _dot(a.astype(BF16), v_ref[0:n_k, :])

    is_ctx = pl.program_id(2) < ctx_tiles

    @pl.when(is_ctx)
    def _():
        if with_ctx:
            attend(t_ctx)
        else:
            o_ref[...] = jnp.zeros_like(o_ref)

    @pl.when(jnp.logical_not(is_ctx))
    def _():
        attend(k_ref.shape[0])


def _attn(lam_vec, qc, kc, vc, lam_init, tq, t_ctx, with_ctx):
    bsz, tt, _ = qc.shape
    kv = pl.BlockSpec((None, tt, HEAD_W), lambda b, h, i: (b, 0, h))
    return pl.pallas_call(
        functools.partial(_attn_kernel, lam_init=lam_init, ctx_tiles=t_ctx // tq, t_ctx=t_ctx, with_ctx=with_ctx),
        grid=(bsz, HEADS, tt // tq),
        in_specs=[_const_spec(lam_vec.shape),
                  pl.BlockSpec((None, tq, HEAD_W), lambda b, h, i: (b, i, h)), kv, kv],
        out_specs=pl.BlockSpec((None, tq, HEAD_W), lambda b, h, i: (b, i, h)),
        out_shape=jax.ShapeDtypeStruct((bsz, tt, MIX_W), F32),
        compiler_params=_params(("parallel", "parallel", "arbitrary")),
        name="diff_attn",
    )(lam_vec, qc, kc, vc)


def _head_norm(x, g):
    parts = []
    for hd in range(HEADS):
        xs = x[:, hd * HEAD_W:(hd + 1) * HEAD_W]
        parts.append(xs * lax.rsqrt(jnp.mean(xs * xs, axis=-1, keepdims=True) + EPS))
    return jnp.concatenate(parts, axis=1) * g


def _route(logits):
    lane = lax.broadcasted_iota(jnp.int32, logits.shape, 1)
    neg = -jnp.inf
    gl = jnp.where(lane < N_GROUPS, logits, neg)
    gmax = jnp.max(gl, axis=-1, keepdims=True)
    gidx = jnp.min(jnp.where(gl == gmax, lane, LANES), axis=-1, keepdims=True)
    p_group = 1.0 / jnp.sum(jnp.exp(gl - gmax), axis=-1, keepdims=True)
    lo = N_GROUPS + EXPERTS_PER_GROUP * gidx
    in_grp = (lane >= lo) & (lane < lo + EXPERTS_PER_GROUP)
    el = jnp.where(in_grp, logits, neg)
    pe = jnp.exp(el - jnp.max(el, axis=-1, keepdims=True))
    pe = pe / jnp.sum(pe, axis=-1, keepdims=True)
    pe = jnp.where(in_grp, pe, -1.0)
    v1 = jnp.max(pe, axis=-1, keepdims=True)
    i1 = jnp.min(jnp.where(pe == v1, lane, LANES), axis=-1, keepdims=True)
    pe2 = jnp.where(lane == i1, -1.0, pe)
    v2 = jnp.max(pe2, axis=-1, keepdims=True)
    i2 = jnp.min(jnp.where(pe2 == v2, lane, LANES), axis=-1, keepdims=True)
    scale = p_group / (v1 + v2)
    e1 = (i1 - N_GROUPS).astype(F32)
    e2 = (i2 - N_GROUPS).astype(F32)
    return jnp.where(lane == 0, e1, jnp.where(lane == 1, e2, jnp.where(lane == 2, v1 * scale,
                     jnp.where(lane == 3, v2 * scale, 0.0))))


def _merge_kernel(x_ref, mod_ref, n1_ref, n2_ref, af_ref, ab_ref, ga_ref, bf_ref, bb_ref, gb_ref, oc_ref,
                  hg_ref, wg_ref, bg_ref, wbr_ref, wo_ref, rw_ref, rb_ref,
                  xo_ref, h2_ref, rt_ref, cnt_ref, *, lam_init):
    x = x_ref[...]
    mod = mod_ref[...]
    dm = x.shape[-1]
    hb = (_rms(x, n1_ref[...]) * (1.0 + mod[1:2]) + mod[0:1]).astype(BF16)
    hg = hg_ref[...]
    ga = ga_ref[...]
    ya = _head_norm(af_ref[...] + ab_ref[...], hg[0:1]) * (ga * _sigmoid(ga))
    yb = _head_norm(bf_ref[...] + bb_ref[...], hg[1:2]) * _sigmoid(gb_ref[...])
    yc = _head_norm(oc_ref[...], hg[2:3]) * (1.0 - lam_init)
    y = jnp.zeros_like(x)
    for i, yi in enumerate((ya, yb, yc)):
        gate = _sigmoid(_dot(hb, wg_ref[:, i * dm:(i + 1) * dm]) + bg_ref[:, i * dm:(i + 1) * dm])
        y = y + gate * _dot(yi.astype(BF16), wbr_ref[i])
    xn = x + mod[2:3] * _dot(y.astype(BF16), wo_ref[...])
    xo_ref[...] = xn
    h2 = _rms(xn, n2_ref[...]) * (1.0 + mod[4:5]) + mod[3:4]
    h2_ref[...] = h2
    logits = jnp.dot(h2, rw_ref[...], preferred_element_type=F32, precision=lax.Precision.HIGHEST) + rb_ref[...]
    rt = _route(logits)
    tm = x.shape[0]
    lane = lax.broadcasted_iota(jnp.int32, (tm, LANES), 1)
    oh1 = lane == rt[:, 0:1].astype(jnp.int32)
    oh2 = lane == rt[:, 1:2].astype(jnp.int32)
    oh1f = jnp.where(oh1, 1.0, 0.0)
    oh2f = jnp.where(oh2, 1.0, 0.0)
    earlier = (lax.broadcasted_iota(jnp.int32, (tm, tm), 0) > lax.broadcasted_iota(jnp.int32, (tm, tm), 1))
    earlier = jnp.where(earlier, 1.0, 0.0).astype(BF16)
    tot1 = jnp.sum(oh1f, axis=0, keepdims=True)
    tot2 = jnp.sum(oh2f, axis=0, keepdims=True)
    rank1 = jnp.sum(jnp.where(oh1, _dot(earlier, oh1f.astype(BF16)), 0.0), axis=-1, keepdims=True)
    rank2 = jnp.sum(jnp.where(oh2, _dot(earlier, oh2f.astype(BF16)) + tot1, 0.0), axis=-1, keepdims=True)
    rt_ref[...] = jnp.where(lane == 4, rank1, jnp.where(lane == 5, rank2, rt))
    cnt_ref[...] = jnp.broadcast_to(tot1 + tot2, cnt_ref.shape)


def _merge(x, mod, n1, n2, oaf, oab, pa, obf, obb, pb, oc, hg, wg, bg, wbr, wo, rw, rb,
           lam_init, tm, ctx_tiles, row0_tiles):
    bsz, tt, dm = x.shape
    nt = tt // tm - row0_tiles
    rows = nt * tm
    src = lambda b, t: (b, row0_tiles + t, 0)

    def col(c):
        return pl.BlockSpec((None, tm, MIX_W), lambda b, t: (b, row0_tiles + t, c))

    dst = lambda b, t: (b, t, 0)
    outs = [(dm, F32), (dm, F32), (LANES, F32)]
    out_specs = [pl.BlockSpec((None, tm, wd), dst) for wd, _ in outs]
    out_shape = [jax.ShapeDtypeStruct((bsz, rows, wd), dt) for wd, dt in outs]
    out_specs.append(pl.BlockSpec((None, 8, LANES), dst))
    out_shape.append(jax.ShapeDtypeStruct((bsz, nt * 8, LANES), F32))
    return pl.pallas_call(
        functools.partial(_merge_kernel, lam_init=lam_init),
        grid=(bsz, nt),
        in_specs=[pl.BlockSpec((None, tm, dm), src),
                  pl.BlockSpec((None, None, 6, dm),
                               lambda b, t: (b, jnp.where(row0_tiles + t >= ctx_tiles, 1, 0), 0, 0)),
                  _const_spec((1, dm)), _const_spec((1, dm)),
                  col(0), col(0), col(2), col(0), col(0), col(3), col(0),
                  _const_spec((3, MIX_W)), _const_spec(wg.shape), _const_spec(bg.shape),
                  _const_spec(wbr.shape), _const_spec(wo.shape), _const_spec(rw.shape), _const_spec(rb.shape)],
        out_specs=out_specs,
        out_shape=out_shape,
        compiler_params=_params(("parallel", "parallel")),
        name="merge_route",
    )(x, mod, n1, n2, oaf, oab, pa, obf, obb, pb, oc, hg, wg, bg, wbr, wo, rw, rb)


def _slot_tables(route, counts, tm):
    bsz, rows, _ = route.shape
    nt = rows // tm
    cnt = counts[:, ::8, :N_EXPERTS].astype(jnp.int32).reshape(bsz * nt, N_EXPERTS)
    tile_off = jnp.cumsum(cnt, axis=0) - cnt
    total = jnp.sum(cnt, axis=0)
    padded = (total + MOE_BLOCK - 1) // MOE_BLOCK * MOE_BLOCK
    p_end = jnp.cumsum(padded)
    base = (p_end - padded)[None, :] + tile_off
    base = jnp.pad(base.astype(F32), ((0, 0), (0, LANES - N_EXPERTS)))
    base = jnp.broadcast_to(base[:, None, :], (bsz * nt, 8, LANES)).reshape(bsz, nt * 8, LANES)
    slot = pl.pallas_call(
        _slot_kernel,
        grid=(bsz, nt),
        in_specs=[pl.BlockSpec((None, tm, LANES), lambda b, t: (b, t, 0)),
                  pl.BlockSpec((None, 8, LANES), lambda b, t: (b, t, 0))],
        out_specs=pl.BlockSpec((None, 8, tm), lambda b, t: (b * nt + t, 0, 0)),
        out_shape=jax.ShapeDtypeStruct((bsz * nt, 8, tm), jnp.int32),
        compiler_params=_params(("parallel", "parallel")),
        name="moe_slots",
    )(route, base)
    n_blocks = -(-(2 * bsz * rows) // MOE_BLOCK) + N_EXPERTS
    blk_start = jnp.arange(n_blocks, dtype=jnp.int32) * MOE_BLOCK
    block_e = jnp.minimum(jnp.sum(blk_start[:, None] >= p_end[None, :], axis=1), N_EXPERTS - 1).astype(jnp.int32)
    n_used = (p_end[-1] // MOE_BLOCK).astype(jnp.int32).reshape(1)
    return slot[:, 0:2, :].reshape(bsz * nt, 1, 2 * tm), block_e, n_used, n_blocks


def _slot_kernel(rt_ref, base_ref, o_ref):
    rt = rt_ref[...]
    lane = lax.broadcasted_iota(jnp.int32, rt.shape, 1)
    base = base_ref[0:1, :]
    s1 = jnp.sum(jnp.where(lane == rt[:, 0:1].astype(jnp.int32), base, 0.0), axis=-1, keepdims=True) + rt[:, 4:5]
    s2 = jnp.sum(jnp.where(lane == rt[:, 1:2].astype(jnp.int32), base, 0.0), axis=-1, keepdims=True) + rt[:, 5:6]
    both = jnp.where(lane == 0, s1, jnp.where(lane == 1, s2, 0.0))
    o_ref[...] = both.T[0:8, :].astype(jnp.int32)


def _row_copies(n_rows, make):
    for r in range(n_rows):
        for k in range(2):
            make(r, k).start()


def _scatter_kernel(slot_ref, h_ref, init_ref, xd_ref, sem):
    tm = h_ref.shape[0]
    _row_copies(tm, lambda r, k: pltpu.make_async_copy(
        h_ref.at[pl.ds(r, 1), :], xd_ref.at[pl.ds(slot_ref[0, k * tm + r], 1), :], sem))
    for _ in range(2):
        pltpu.make_async_copy(h_ref, xd_ref.at[pl.ds(0, tm), :], sem).wait()


def _scatter(slot, h2, n_slots, tm):
    bsz, rows, dm = h2.shape
    nt = rows // tm
    init = jnp.zeros((n_slots, dm), F32)
    return pl.pallas_call(
        _scatter_kernel,
        grid=(bsz, nt),
        in_specs=[pl.BlockSpec((None, 1, 2 * tm), lambda b, t: (b * nt + t, 0, 0), memory_space=pltpu.SMEM),
                  pl.BlockSpec((tm, dm), lambda b, t: (b * nt + t, 0)),
                  pl.BlockSpec(memory_space=pl.ANY)],
        out_specs=pl.BlockSpec(memory_space=pl.ANY),
        out_shape=jax.ShapeDtypeStruct((n_slots, dm), F32),
        scratch_shapes=[pltpu.SemaphoreType.DMA(())],
        input_output_aliases={2: 0},
        compiler_params=_params(("arbitrary", "arbitrary"), has_side_effects=True),
        name="moe_scatter",
    )(slot, h2.reshape(bsz * rows, dm), init)


def _expert_kernel(be_ref, nu_ref, x_ref, w1_ref, w3_ref, w2_ref, y_ref, w1b, w3b, w2b):
    i = pl.program_id(0)

    @pl.when(i < nu_ref[0])
    def _():
        @pl.when((i == 0) | (be_ref[i] != be_ref[jnp.maximum(i - 1, 0)]))
        def _():
            w1b[...] = w1_ref[...].astype(BF16)
            w3b[...] = w3_ref[...].astype(BF16)
            w2b[...] = w2_ref[...].astype(BF16)

        xb = x_ref[...].astype(BF16)
        u = _dot(xb, w1b[...])
        hmid = (u * _sigmoid(u)) * _dot(xb, w3b[...])
        y_ref[...] = _dot(hmid.astype(BF16), w2b[...])

    @pl.when(i >= nu_ref[0])
    def _():
        y_ref[...] = jnp.zeros_like(y_ref)


def _experts(block_e, n_used, x_disp, w1, w3, w2, layer):
    n_slots, dm = x_disp.shape
    n_blocks = n_slots // MOE_BLOCK
    wspec = lambda shape: pl.BlockSpec((None, None) + shape, lambda i, be, nu: (layer, be[i], 0, 0))
    grid_spec = pltpu.PrefetchScalarGridSpec(
        num_scalar_prefetch=2,
        grid=(n_blocks,),
        in_specs=[pl.BlockSpec((MOE_BLOCK, dm), lambda i, be, nu: (jnp.minimum(i, nu[0] - 1), 0)),
                  wspec((dm, D_EXPERT)), wspec((dm, D_EXPERT)), wspec((D_EXPERT, dm))],
        out_specs=pl.BlockSpec((MOE_BLOCK, dm), lambda i, be, nu: (i, 0)),
        scratch_shapes=[pltpu.VMEM((dm, D_EXPERT), BF16), pltpu.VMEM((dm, D_EXPERT), BF16),
                        pltpu.VMEM((D_EXPERT, dm), BF16)],
    )
    return pl.pallas_call(
        _expert_kernel,
        grid_spec=grid_spec,
        out_shape=jax.ShapeDtypeStruct((n_slots, dm), F32),
        compiler_params=_params(("arbitrary",)),
        name="moe_experts",
    )(block_e, n_used, x_disp, w1, w3, w2)


def _combine_kernel(slot_ref, x_ref, mod_ref, rt_ref, yd_ref, fg_ref, o_ref, ybuf, sem, *, final):
    tm = x_ref.shape[0]
    _row_copies(tm, lambda r, k: pltpu.make_async_copy(
        yd_ref.at[pl.ds(slot_ref[0, k * tm + r], 1), :], ybuf.at[k, pl.ds(r, 1), :], sem))
    for k in range(2):
        pltpu.make_async_copy(yd_ref.at[pl.ds(0, tm), :], ybuf.at[k], sem).wait()
    rt = rt_ref[...]
    y = rt[:, 2:3] * ybuf[0] + rt[:, 3:4] * ybuf[1]
    xn = x_ref[...] + mod_ref[5:6, :] * y
    o_ref[...] = _rms(xn, fg_ref[...]) if final else xn


def _combine(slot, x, mod, route, y_disp, fg, final, tm, ctx_tiles, row0_tiles):
    bsz, rows, dm = x.shape
    nt = rows // tm
    return pl.pallas_call(
        functools.partial(_combine_kernel, final=final),
        grid=(bsz, nt),
        in_specs=[pl.BlockSpec((None, 1, 2 * tm), lambda b, t: (b * nt + t, 0, 0), memory_space=pltpu.SMEM),
                  pl.BlockSpec((None, tm, dm), lambda b, t: (b, t, 0)),
                  pl.BlockSpec((None, None, 6, dm),
                               lambda b, t: (b, jnp.where(row0_tiles + t >= ctx_tiles, 1, 0), 0, 0)),
                  pl.BlockSpec((None, tm, LANES), lambda b, t: (b, t, 0)),
                  pl.BlockSpec(memory_space=pl.ANY),
                  _const_spec((1, dm))],
        out_specs=pl.BlockSpec((None, tm, dm), lambda b, t: (b, t, 0)),
        out_shape=jax.ShapeDtypeStruct((bsz, rows, dm), F32),
        scratch_shapes=[pltpu.VMEM((2, tm, dm), F32), pltpu.SemaphoreType.DMA(())],
        compiler_params=_params(("arbitrary", "arbitrary")),
        name="moe_combine",
    )(slot, x, mod, route, y_disp, fg)


def _rope_tables(ctx, t_lat):
    rows = t_lat // GRID_W
    row = jnp.repeat(jnp.arange(rows, dtype=F32), GRID_W)
    col = jnp.tile(jnp.arange(GRID_W, dtype=F32), rows)
    inv = ROPE_BASE ** (-jnp.arange(ROPE_PAIRS, dtype=F32) / ROPE_PAIRS)
    ang_r = row[:, None] * inv
    ang_c = col[:, None] * inv
    cos64 = jnp.concatenate([jnp.cos(ang_r), jnp.cos(ang_r), jnp.cos(ang_c), jnp.cos(ang_c)], axis=1)
    sin64 = jnp.concatenate([-jnp.sin(ang_r), jnp.sin(ang_r), -jnp.sin(ang_c), jnp.sin(ang_c)], axis=1)
    cos = jnp.tile(cos64, (1, MIX_W // 64))
    sin = jnp.tile(sin64, (1, MIX_W // 64))
    cos = jnp.concatenate([jnp.ones((ctx, MIX_W), F32), cos], axis=0)
    sin = jnp.concatenate([jnp.zeros((ctx, MIX_W), F32), sin], axis=0)
    return cos, sin


def _pack_w_in(w):
    dm = w.shape[0]
    a_end = PA_W
    b_end = a_end + PB_W
    g_end = b_end + 4 * HEADS
    pad = jnp.zeros((dm, PG_W - 4 * HEADS), w.dtype)
    return jnp.concatenate([w[:, :b_end], w[:, b_end:g_end], pad, w[:, g_end:]], axis=1).astype(BF16)


def kernel(x, c, ctx, c_ctx, ada_w, ada_b, norm1_g, norm2_g, w_in, mlstm_conv_w, mlstm_conv_b, mlstm_gate_b,
           hgrn_lb_raw, hgrn_norm_g, mlstm_norm_g, diff_norm_g, diff_lambda, w_branch, w_gate, b_gate, w_out,
           router_g_w, router_g_b, router_e_w, router_e_b, moe_w1, moe_w3, moe_w2, final_g):
    bsz, t_lat, dm = x.shape
    t_ctx = ctx.shape[1]
    depth = ada_w.shape[0]
    tt = t_ctx + t_lat
    tm = min(256, t_ctx)
    hgrn_chunk = min(128, t_ctx)
    mlstm_chunk = min(256, t_ctx)
    assert t_ctx % tm == 0 and t_lat % tm == 0 and t_lat % GRID_W == 0
    assert t_ctx % hgrn_chunk == 0 and t_ctx % mlstm_chunk == 0
    ctx_tiles = t_ctx // tm

    n_rows = -(-(bsz + 1) // 8) * 8
    cc = jnp.zeros((n_rows, dm), F32).at[:bsz].set(c).at[bsz].set(c_ctx)
    mods = _ada_mod(cc, ada_w, ada_b).reshape(depth, n_rows, 6, dm)

    cos, sin = _rope_tables(t_ctx, t_lat)
    lb_cum = jnp.cumsum(jax.nn.softmax(hgrn_lb_raw.astype(F32), axis=0), axis=0)
    lower = lb_cum - lb_cum[0]

    xc = jnp.concatenate([ctx, x], axis=1)
    out = None
    for l in range(depth):
        with_ctx = l < depth - 1
        lam_init = 0.8 - 0.6 * math.exp(-0.3 * l)
        mod = jnp.stack([jnp.broadcast_to(mods[l, bsz], (bsz, 6, dm)), mods[l, :bsz]], axis=1)
        n1 = norm1_g[l].reshape(1, dm)
        n2 = norm2_g[l].reshape(1, dm)

        lb = lower[l]
        pad = jnp.zeros((2, 6, MIX_W), F32)
        lbc = jnp.concatenate([jnp.log(lb)[:, None], jnp.log1p(-lb)[:, None], pad], axis=1)
        pa, pb, pg, qc, kc, vc = _norm_proj(xc, mod, n1, _pack_w_in(w_in[l]), cos, sin, lbc, tm, ctx_tiles)

        oaf, oab = _hgrn(pa, hgrn_chunk, t_ctx // hgrn_chunk)

        qk = _conv_silu(pb, mlstm_conv_w[l], mlstm_conv_b[l], t_ctx, 256)
        gbias = jnp.zeros((1, PG_W), F32).at[0, :4 * HEADS].set(mlstm_gate_b[l].reshape(-1))
        obf, obb = _mlstm(qk, pb, pg, gbias, mlstm_chunk, t_ctx // mlstm_chunk)

        lam_vec = diff_lambda[l].astype(F32)
        oc = _attn(lam_vec, qc, kc, vc, lam_init, tm, t_ctx, with_ctx)

        hg = jnp.stack([hgrn_norm_g[l], mlstm_norm_g[l], diff_norm_g[l]])
        rw = jnp.zeros((dm, LANES), F32).at[:, :N_GROUPS].set(router_g_w[l])
        rw = rw.at[:, N_GROUPS:N_GROUPS + N_EXPERTS].set(router_e_w[l])
        rb = jnp.zeros((1, LANES), F32).at[0, :N_GROUPS].set(router_g_b[l])
        rb = rb.at[0, N_GROUPS:N_GROUPS + N_EXPERTS].set(router_e_b[l])
        row0_tiles = 0 if with_ctx else ctx_tiles
        xn, h2, route, counts = _merge(xc, mod, n1, n2, oaf, oab, pa, obf, obb, pb, oc, hg,
                                       w_gate[l].astype(BF16), b_gate[l].reshape(1, 3 * dm),
                                       w_branch[l].astype(BF16), w_out[l].astype(BF16), rw, rb,
                                       lam_init, tm, ctx_tiles, row0_tiles)

        slot, block_e, n_used, n_blocks = _slot_tables(route, counts, tm)
        x_disp = _scatter(slot, h2, n_blocks * MOE_BLOCK, tm)
        y_disp = _experts(block_e, n_used, x_disp, moe_w1, moe_w3, moe_w2, l)
        final = l == depth - 1
        xc = _combine(slot, xn, mod, route, y_disp, final_g.reshape(1, dm), final, tm, ctx_tiles, row0_tiles)
        out = xc
    return out
```

```python
import functools
import math

import numpy as np
import jax
import jax.numpy as jnp
from jax import lax
from jax.experimental import pallas as pl
from jax.experimental.pallas import tpu as pltpu

F32 = jnp.float32
BF16 = jnp.bfloat16

EPS = 1e-6
NEG_BIG = -1e30
HEADS = 4
HEAD_W = 128
MIX_W = HEADS * HEAD_W
GRID_W = 64
ROPE_BASE = 10000.0
ROPE_PAIRS = 16
B_CONV = 3
N_GROUPS = 4
EXPERTS_PER_GROUP = 8
N_EXPERTS = N_GROUPS * EXPERTS_PER_GROUP
D_EXPERT = 512
MOE_BLOCK = 256
LANES = 128
VMEM_LIMIT = 50 * 1024 * 1024

PA_W = 5 * MIX_W
PB_W = 4 * MIX_W
PG_W = LANES
PC_W = 3 * MIX_W
PROJ_W = PA_W + PB_W + PG_W + PC_W


def _params(sem, **kw):
    return pltpu.CompilerParams(dimension_semantics=sem, vmem_limit_bytes=VMEM_LIMIT, **kw)


def _const_spec(shape):
    nd = len(shape)
    return pl.BlockSpec(shape, lambda *_: (0,) * nd, pipeline_mode=pl.Buffered(1))


def _dot(a, b):
    return jnp.dot(a, b, preferred_element_type=F32)


def _dot_nt(a, b):
    return lax.dot_general(a, b, (((1,), (1,)), ((), ())), preferred_element_type=F32)


def _dot_tn(a, b):
    return lax.dot_general(a, b, (((0,), (0,)), ((), ())), preferred_element_type=F32)


def _dot_sel(m_bf16, x):
    hi = x.astype(BF16)
    r1 = x - hi.astype(F32)
    mid = r1.astype(BF16)
    lo = (r1 - mid.astype(F32)).astype(BF16)
    return _dot(m_bf16, hi) + _dot(m_bf16, mid) + _dot(m_bf16, lo)


def _sigmoid(x):
    return 1.0 / (1.0 + jnp.exp(-x))


def _log_sigmoid(x):
    return jnp.minimum(x, 0.0) - jnp.log1p(jnp.exp(-jnp.abs(x)))


def _rms(x, g):
    return x * lax.rsqrt(jnp.mean(x * x, axis=-1, keepdims=True) + EPS) * g


def _ada_kernel(c_ref, w_ref, b_ref, o_ref):
    c = c_ref[...]
    s = c * _sigmoid(c)
    o_ref[...] = jnp.dot(s, w_ref[...], preferred_element_type=F32,
                         precision=lax.Precision.HIGHEST) + b_ref[...]


def _ada_mod(cc, ada_w, ada_b):
    depth, dm, six = ada_w.shape
    rows = cc.shape[0]
    tn = dm
    return pl.pallas_call(
        _ada_kernel,
        grid=(depth, six // tn),
        in_specs=[pl.BlockSpec((rows, dm), lambda l, n: (0, 0)),
                  pl.BlockSpec((None, dm, tn), lambda l, n: (l, 0, n)),
                  pl.BlockSpec((None, 1, tn), lambda l, n: (l, 0, n))],
        out_specs=pl.BlockSpec((None, rows, tn), lambda l, n: (l, 0, n)),
        out_shape=jax.ShapeDtypeStruct((depth, rows, six), F32),
        compiler_params=_params(("parallel", "parallel")),
        name="ada_mod",
    )(cc, ada_w, ada_b.reshape(depth, 1, six))


def _rope(x, cos, sin):
    n = x.shape[-1]
    lane = lax.broadcasted_iota(jnp.int32, (1, n), 1)
    first = (lane // ROPE_PAIRS) % 2 == 0
    partner = jnp.where(first, pltpu.roll(x, n - ROPE_PAIRS, 1), pltpu.roll(x, ROPE_PAIRS, 1))
    return x * cos + partner * sin


def _hgrn_log_forget(z, lbc):
    lsig = jnp.minimum(z, 0.0) - jnp.log(1.0 + jnp.exp(-jnp.abs(z)))
    a = lbc[0:1]
    bb = lbc[1:2] + lsig
    return jnp.maximum(a, bb) + jnp.log(1.0 + jnp.exp(-jnp.abs(a - bb)))


def _norm_proj_kernel(x_ref, mod_ref, g_ref, w_ref, cos_ref, sin_ref, lbc_ref,
                      pa_ref, pb_ref, pg_ref, qc_ref, kc_ref, vc_ref):
    x = x_ref[...]
    mod = mod_ref[...]
    h = _rms(x, g_ref[...]) * (1.0 + mod[1:2]) + mod[0:1]
    hb = h.astype(BF16)
    pa_ref[:, 0:3 * MIX_W] = _dot(hb, w_ref[:, 0:3 * MIX_W])
    for d in range(2):
        cs = slice((3 + d) * MIX_W, (4 + d) * MIX_W)
        pa_ref[:, cs] = _hgrn_log_forget(_dot(hb, w_ref[:, cs]), lbc_ref[d])
    pb_ref[...] = _dot(hb, w_ref[:, PA_W:PA_W + PB_W])
    pg_ref[...] = _dot(hb, w_ref[:, PA_W + PB_W:PA_W + PB_W + PG_W])
    c0 = PA_W + PB_W + PG_W
    cos = cos_ref[...]
    sin = sin_ref[...]
    q = _dot(hb, w_ref[:, c0:c0 + MIX_W])
    qc_ref[...] = (_rope(q, cos, sin) * (64.0 ** -0.5 * math.log2(math.e))).astype(BF16)
    k = _dot(hb, w_ref[:, c0 + MIX_W:c0 + 2 * MIX_W])
    kc_ref[...] = _rope(k, cos, sin).astype(BF16)
    vc_ref[...] = _dot(hb, w_ref[:, c0 + 2 * MIX_W:c0 + 3 * MIX_W]).astype(BF16)


def _norm_proj(x, mod, g, w, cos, sin, lbc, tm, ctx_tiles):
    bsz, tt, dm = x.shape
    nt = tt // tm
    row = lambda b, t: (b, t, 0)
    outs = [(PA_W, F32), (PB_W, F32), (PG_W, F32), (MIX_W, BF16), (MIX_W, BF16), (MIX_W, BF16)]
    return pl.pallas_call(
        _norm_proj_kernel,
        grid=(bsz, nt),
        in_specs=[pl.BlockSpec((None, tm, dm), row),
                  pl.BlockSpec((None, None, 6, dm), lambda b, t: (b, jnp.where(t >= ctx_tiles, 1, 0), 0, 0)),
                  _const_spec((1, dm)),
                  _const_spec((dm, PROJ_W)),
                  pl.BlockSpec((tm, MIX_W), lambda b, t: (t, 0)),
                  pl.BlockSpec((tm, MIX_W), lambda b, t: (t, 0)),
                  _const_spec(lbc.shape)],
        out_specs=[pl.BlockSpec((None, tm, wd), row) for wd, _ in outs],
        out_shape=[jax.ShapeDtypeStruct((bsz, tt, wd), dt) for wd, dt in outs],
        compiler_params=_params(("parallel", "parallel")),
        name="norm_proj",
    )(x, mod, g, w, cos, sin, lbc)


def _conv_kernel(x_ref, w_ref, b_ref, o_ref, *, ctx, q_blocks):
    x = x_ref[...]
    tt = x.shape[0]
    row = lax.broadcasted_iota(jnp.int32, (tt, 1), 0)
    prev = jnp.where((row == 0) | (row == ctx), 0.0, pltpu.roll(x, 1, 0))
    nxt = jnp.where((row == ctx - 1) | (row == tt - 1), 0.0, pltpu.roll(x, tt - 1, 0))
    w = w_ref[...]
    y = b_ref[...] + prev * w[0:1] + x * w[1:2] + nxt * w[2:3]
    y = y * _sigmoid(y)
    scale = jnp.where(pl.program_id(1) < q_blocks, HEAD_W ** -0.5, 1.0)
    o_ref[...] = y * scale


def _conv_silu(pb, conv_w, conv_b, ctx, cb):
    bsz, tt, _ = pb.shape
    width = 2 * MIX_W
    return pl.pallas_call(
        functools.partial(_conv_kernel, ctx=ctx, q_blocks=MIX_W // cb),
        grid=(bsz, width // cb),
        in_specs=[pl.BlockSpec((None, tt, cb), lambda b, j: (b, 0, j)),
                  pl.BlockSpec((B_CONV, cb), lambda b, j: (0, j)),
                  pl.BlockSpec((1, cb), lambda b, j: (0, j))],
        out_specs=pl.BlockSpec((None, tt, cb), lambda b, j: (b, 0, j)),
        out_shape=jax.ShapeDtypeStruct((bsz, tt, width), F32),
        compiler_params=_params(("parallel", "parallel")),
        name="mlstm_conv",
    )(pb, conv_w, conv_b.reshape(1, width))


def _chunk_consts(L):
    idx = np.arange(L)
    t, u = idx[:, None], idx[None, :]
    cum = (u <= t).astype(np.float32)
    nlev = int(round(math.log2(L)))
    lvl = np.full((L, L), -1, np.int32)
    lvl[idx, idx] = nlev
    for i in range(nlev):
        h = L >> (i + 1)
        same = (t // (2 * h)) == (u // (2 * h))
        lvl[same & (t % (2 * h) >= h) & (u % (2 * h) < h)] = i
    sel = jnp.asarray(np.stack([cum, cum[::-1, ::-1]]), BF16)
    lvls = jnp.asarray(np.stack([lvl, lvl[::-1, ::-1]]))
    return sel, lvls, nlev


def _midpoint_rows(b, h, d):
    L, w = b.shape
    two = 2 * h
    pos = h - 1 if d == 0 else h
    if two % 8 == 0:
        r = b.reshape(L // two, two, w)[:, pos:pos + 1, :]
        return jnp.broadcast_to(r, (L // two, two, w)).reshape(L, w)
    phase = lax.broadcasted_iota(jnp.int32, (L, 1), 0) % two
    ref = b
    for off in range(pos - two + 1, pos + 1):
        if off != 0:
            ref = jnp.where(phase == pos - off, pltpu.roll(b, (L - off) % L, 0), ref)
    return ref


def _scan_blocks(j, nc_ctx, nc):
    jb = jnp.where(j < nc_ctx, nc_ctx - 1 - j, nc - 1 + nc_ctx - j)
    return j, jb


def _hgrn_kernel(qf_ref, vf_ref, ff_ref, qb_ref, vb_ref, fb_ref, sel_ref, lvl_ref,
                 of_ref, ob_ref, st_ref, *, L, nlev):
    @pl.when(pl.program_id(1) == 0)
    def _():
        st_ref[...] = jnp.zeros_like(st_ref)

    row = lax.broadcasted_iota(jnp.int32, (L, 1), 0)
    dirs = ((qf_ref, vf_ref, ff_ref, of_ref), (qb_ref, vb_ref, fb_ref, ob_ref))
    for d, (q_ref, v_ref, f_ref, o_ref) in enumerate(dirs):
        lf = f_ref[...]
        kk = 1.0 - jnp.exp(lf)
        q_all = q_ref[...]
        b_in = _dot_sel(sel_ref[d], lf)
        lvl = lvl_ref[d]
        last = L - 1 if d == 0 else 0
        b_end = b_in[last:last + 1]
        heads = [slice(hd * HEAD_W, (hd + 1) * HEAD_W) for hd in range(HEADS)]
        qb = q_all.astype(BF16)
        kb = kk.astype(BF16)
        on_diag = lvl == nlev
        scores = [jnp.where(on_diag, _dot_nt(qb[:, cs], kb[:, cs]), 0.0) for cs in heads]
        for i in range(nlev):
            h = L >> (i + 1)
            ei = jnp.exp(-jnp.abs(b_in - _midpoint_rows(b_in, h, d)))
            is_query = (row % (2 * h) >= h) if d == 0 else (row % (2 * h) < h)
            xe = (jnp.where(is_query, q_all, kk) * ei).astype(BF16)
            at_level = lvl == i
            for hd, cs in enumerate(heads):
                scores[hd] = scores[hd] + jnp.where(at_level, _dot_nt(xe[:, cs], xe[:, cs]), 0.0)
        q_in = (q_all * jnp.exp(b_in)).astype(BF16)
        k_out = (kk * jnp.exp(b_end - b_in)).astype(BF16)
        decay = jnp.exp(b_end)
        for hd, cs in enumerate(heads):
            vb = v_ref[:, cs].astype(BF16)
            st = st_ref[d, hd]
            o_ref[:, cs] = _dot(scores[hd].astype(BF16), vb) + _dot_nt(q_in[:, cs], st.astype(BF16))
            st_ref[d, hd] = st * decay[:, cs] + _dot_tn(vb, k_out[:, cs])


def _hgrn(pa, L, nc_ctx):
    bsz, tt, _ = pa.shape
    nc = tt // L
    sel, lvl, nlev = _chunk_consts(L)

    def spec(col, which):
        return pl.BlockSpec((None, L, MIX_W), lambda b, j: (b, _scan_blocks(j, nc_ctx, nc)[which], col))

    return pl.pallas_call(
        functools.partial(_hgrn_kernel, L=L, nlev=nlev),
        grid=(bsz, nc),
        in_specs=[spec(0, 0), spec(1, 0), spec(3, 0), spec(0, 1), spec(1, 1), spec(4, 1),
                  _const_spec(sel.shape), _const_spec(lvl.shape)],
        out_specs=[spec(0, 0), spec(0, 1)],
        out_shape=[jax.ShapeDtypeStruct((bsz, tt, MIX_W), F32)] * 2,
        scratch_shapes=[pltpu.VMEM((2, HEADS, HEAD_W, HEAD_W), F32)],
        compiler_params=_params(("parallel", "arbitrary")),
        name="hgrn2_scan",
    )(pa, pa, pa, pa, pa, pa, sel, lvl)


def _mlstm_kernel(qf_ref, kf_ref, vf_ref, gf_ref, qb_ref, kb_ref, vb_ref, gb_ref, gbias_ref,
                  sel_ref, lvl_ref, of_ref, ob_ref, c_ref, m_ref, *, L, nlev):
    @pl.when(pl.program_id(1) == 0)
    def _():
        c_ref[...] = jnp.zeros_like(c_ref)
        m_ref[...] = jnp.zeros_like(m_ref)

    lane = lax.broadcasted_iota(jnp.int32, (1, LANES), 1)
    ones_col = jnp.where(lane == 0, 1.0, 0.0).astype(BF16)
    dirs = ((qf_ref, kf_ref, vf_ref, gf_ref, of_ref), (qb_ref, kb_ref, vb_ref, gb_ref, ob_ref))
    for d, (q_ref, k_ref, v_ref, g_ref, o_ref) in enumerate(dirs):
        g = g_ref[...] + gbias_ref[...]
        lf = jnp.where((lane >= 2 * HEADS) & (lane < 4 * HEADS), _log_sigmoid(g), 0.0)
        bcum = _dot_sel(sel_ref[d, 0:L], lf)
        causal = lvl_ref[d] >= 0
        row_terms = (g - pltpu.roll(bcum, LANES - 2 * HEADS, 1)).T
        last = L - 1 if d == 0 else 0
        for hd in range(HEADS):
            cs = slice(hd * HEAD_W, (hd + 1) * HEAD_W)
            li_lane = d * HEADS + hd
            lf_lane = 2 * HEADS + d * HEADS + hd
            b = bcum[:, lf_lane:lf_lane + 1]
            row_term = row_terms[li_lane:li_lane + 1, :]
            b_end = b[last:last + 1]
            r = d * HEADS + hd
            m_prev = m_ref[r:r + 1, 0:1]
            q = q_ref[:, cs].astype(BF16)
            k = k_ref[:, cs]
            v_aug = jnp.concatenate([v_ref[:, cs].astype(BF16), jnp.broadcast_to(ones_col, (L, LANES))], axis=1)
            c_prev = c_ref[r]
            w_end = b_end + row_term
            m_new = jnp.maximum(b_end + m_prev, jnp.max(w_end, axis=-1, keepdims=True))
            e_end = jnp.exp(w_end - m_new)
            keep = jnp.exp(b_end + m_prev - m_new)
            c_ref[r] = keep * c_prev + _dot((k.T * e_end).astype(BF16), v_aug)
            m_ref[r:r + 1, :] = jnp.broadcast_to(m_new, (1, LANES))
            logw = jnp.where(causal, b + row_term, NEG_BIG)
            w_state = b + m_prev
            m_t = jnp.maximum(jnp.max(logw, axis=-1, keepdims=True), w_state)
            s = _dot_nt(q, k.astype(BF16)) * jnp.exp(logw - m_t)
            a_state = jnp.exp(w_state - m_t)
            num = _dot(s.astype(BF16), v_aug) + a_state * _dot(q, c_prev.astype(BF16))
            den = num[:, HEAD_W:HEAD_W + 1]
            o_ref[:, cs] = num[:, 0:HEAD_W] / jnp.maximum(jnp.abs(den), jnp.exp(-m_t))


def _mlstm(qk, pb, pg, gbias, L, nc_ctx):
    bsz, tt, _ = pb.shape
    nc = tt // L
    sel, lvl, nlev = _chunk_consts(L)

    def spec(col, which, width=MIX_W):
        return pl.BlockSpec((None, L, width), lambda b, j: (b, _scan_blocks(j, nc_ctx, nc)[which], col))

    return pl.pallas_call(
        functools.partial(_mlstm_kernel, L=L, nlev=nlev),
        grid=(bsz, nc),
        in_specs=[spec(0, 0), spec(1, 0), spec(2, 0), spec(0, 0, PG_W),
                  spec(0, 1), spec(1, 1), spec(2, 1), spec(0, 1, PG_W),
                  _const_spec((1, PG_W)), _const_spec(sel.shape), _const_spec(lvl.shape)],
        out_specs=[spec(0, 0), spec(0, 1)],
        out_shape=[jax.ShapeDtypeStruct((bsz, tt, MIX_W), F32)] * 2,
        scratch_shapes=[pltpu.VMEM((2 * HEADS, HEAD_W, 2 * HEAD_W), F32),
                        pltpu.VMEM((2 * HEADS, LANES), F32)],
        compiler_params=_params(("parallel", "arbitrary")),
        name="mlstm_scan",
    )(qk, qk, pb, pg, qk, qk, pb, pg, gbias, sel, lvl)


def _attn_kernel(lam_ref, q_ref, k_ref, v_ref, o_ref, *, lam_init, ctx_tiles, t_ctx, with_ctx):
    lv = lam_ref[...]
    lam = (jnp.exp(jnp.sum(lv[0:1] * lv[1:2], axis=-1, keepdims=True))
           - jnp.exp(jnp.sum(lv[2:3] * lv[3:4], axis=-1, keepdims=True)) + lam_init)
    lane = lax.broadcasted_iota(jnp.int32, (1, HEAD_W), 1)

    def attend(n_k):
        q = q_ref[...]
        k = k_ref[0:n_k, :]
        zero = jnp.zeros_like(q)

        def probs(qm):
            s = _dot_nt(qm, k)
            p = jnp.exp2(s - jnp.max(s, axis=-1, keepdims=True))
            return p / jnp.sum(p, axis=-1, keepdims=True)

        a = probs(jnp.where(lane < HEAD_W // 2, q, zero)) - lam * probs(jnp.where(lane >= HEAD_W // 2, q, zero))
        o_ref[...] = _dot(a.astype(BF16), v_ref[0:n_k, :])

    is_ctx = pl.program_id(2) < ctx_tiles

    @pl.when(is_ctx)
    def _():
        if with_ctx:
            attend(t_ctx)
        else:
            o_ref[...] = jnp.zeros_like(o_ref)

    @pl.when(jnp.logical_not(is_ctx))
    def _():
        attend(k_ref.shape[0])


def _attn(lam_vec, qc, kc, vc, lam_init, tq, t_ctx, with_ctx):
    bsz, tt, _ = qc.shape
    kv = pl.BlockSpec((None, tt, HEAD_W), lambda b, h, i: (b, 0, h))
    return pl.pallas_call(
        functools.partial(_attn_kernel, lam_init=lam_init, ctx_tiles=t_ctx // tq, t_ctx=t_ctx, with_ctx=with_ctx),
        grid=(bsz, HEADS, tt // tq),
        in_specs=[_const_spec(lam_vec.shape),
                  pl.BlockSpec((None, tq, HEAD_W), lambda b, h, i: (b, i, h)), kv, kv],
        out_specs=pl.BlockSpec((None, tq, HEAD_W), lambda b, h, i: (b, i, h)),
        out_shape=jax.ShapeDtypeStruct((bsz, tt, MIX_W), F32),
        compiler_params=_params(("parallel", "parallel", "arbitrary")),
        name="diff_attn",
    )(lam_vec, qc, kc, vc)


def _head_norm(x, g):
    parts = []
    for hd in range(HEADS):
        xs = x[:, hd * HEAD_W:(hd + 1) * HEAD_W]
        parts.append(xs * lax.rsqrt(jnp.mean(xs * xs, axis=-1, keepdims=True) + EPS))
    return jnp.concatenate(parts, axis=1) * g


ROUTE_ROWS = 40


def _route(lg):
    n = lg.shape[1]
    row = lax.broadcasted_iota(jnp.int32, lg.shape, 0)
    neg = -jnp.inf
    big = ROUTE_ROWS
    gl = jnp.where(row < N_GROUPS, lg, neg)
    gmax = jnp.max(gl, axis=0, keepdims=True)
    gidx = jnp.min(jnp.where(gl == gmax, row, big), axis=0, keepdims=True)
    p_group = 1.0 / jnp.sum(jnp.exp(gl - gmax), axis=0, keepdims=True)
    lo = N_GROUPS + EXPERTS_PER_GROUP * gidx
    in_grp = (row >= lo) & (row < lo + EXPERTS_PER_GROUP)
    el = jnp.where(in_grp, lg, neg)
    pe = jnp.exp(el - jnp.max(el, axis=0, keepdims=True))
    pe = pe / jnp.sum(pe, axis=0, keepdims=True)
    pe = jnp.where(in_grp, pe, -1.0)
    v1 = jnp.max(pe, axis=0, keepdims=True)
    i1 = jnp.min(jnp.where(pe == v1, row, big), axis=0, keepdims=True)
    pe2 = jnp.where(row == i1, -1.0, pe)
    v2 = jnp.max(pe2, axis=0, keepdims=True)
    i2 = jnp.min(jnp.where(pe2 == v2, row, big), axis=0, keepdims=True)
    scale = p_group / (v1 + v2)
    erow = lax.broadcasted_iota(jnp.int32, (N_EXPERTS, n), 0) + N_GROUPS
    oh1 = erow == i1
    oh2 = erow == i2
    oh1f = jnp.where(oh1, 1.0, 0.0)
    oh2f = jnp.where(oh2, 1.0, 0.0)
    earlier = (lax.broadcasted_iota(jnp.int32, (n, n), 0) < lax.broadcasted_iota(jnp.int32, (n, n), 1))
    earlier = jnp.where(earlier, 1.0, 0.0).astype(BF16)
    tot1 = jnp.sum(oh1f, axis=1, keepdims=True)
    tot2 = jnp.sum(oh2f, axis=1, keepdims=True)
    rank1 = jnp.sum(jnp.where(oh1, _dot(oh1f.astype(BF16), earlier), 0.0), axis=0, keepdims=True)
    rank2 = jnp.sum(jnp.where(oh2, _dot(oh2f.astype(BF16), earlier) + tot1, 0.0), axis=0, keepdims=True)
    r8 = lax.broadcasted_iota(jnp.int32, (8, n), 0)
    rows = jnp.zeros((8, n), F32)
    for i, val in enumerate(((i1 - N_GROUPS).astype(F32), (i2 - N_GROUPS).astype(F32), v1 * scale, v2 * scale,
                             rank1, rank2)):
        rows = jnp.where(r8 == i, val, rows)
    return rows, tot1 + tot2


def _merge_kernel(x_ref, mod_ref, n1_ref, n2_ref, af_ref, ab_ref, ga_ref, bf_ref, bb_ref, gb_ref, oc_ref,
                  hg_ref, wg_ref, bg_ref, wbr_ref, wo_ref, rw_ref, rb_ref,
                  xo_ref, h2_ref, rt_ref, rtt_ref, cnt_ref, *, lam_init):
    x = x_ref[...]
    mod = mod_ref[...]
    dm = x.shape[-1]
    hb = (_rms(x, n1_ref[...]) * (1.0 + mod[1:2]) + mod[0:1]).astype(BF16)
    hg = hg_ref[...]
    ga = ga_ref[...]
    ya = _head_norm(af_ref[...] + ab_ref[...], hg[0:1]) * (ga * _sigmoid(ga))
    yb = _head_norm(bf_ref[...] + bb_ref[...], hg[1:2]) * _sigmoid(gb_ref[...])
    yc = _head_norm(oc_ref[...], hg[2:3]) * (1.0 - lam_init)
    y = jnp.zeros_like(x)
    for i, yi in enumerate((ya, yb, yc)):
        gate = _sigmoid(_dot(hb, wg_ref[:, i * dm:(i + 1) * dm]) + bg_ref[:, i * dm:(i + 1) * dm])
        y = y + gate * _dot(yi.astype(BF16), wbr_ref[i])
    xn = x + mod[2:3] * _dot(y.astype(BF16), wo_ref[...])
    xo_ref[...] = xn
    h2 = _rms(xn, n2_ref[...]) * (1.0 + mod[4:5]) + mod[3:4]
    h2_ref[...] = h2
    logits = lax.dot_general(rw_ref[...], h2, (((1,), (1,)), ((), ())), preferred_element_type=F32,
                             precision=lax.Precision.HIGHEST) + rb_ref[...]
    rows, counts = _route(logits[0:ROUTE_ROWS, :])
    rtt_ref[...] = rows
    cnt_ref[...] = jnp.broadcast_to(counts, cnt_ref.shape)
    tm = x.shape[0]
    rt_ref[...] = jnp.concatenate([rows, jnp.zeros((LANES - 8, tm), F32)], axis=0).T


def _merge(x, mod, n1, n2, oaf, oab, pa, obf, obb, pb, oc, hg, wg, bg, wbr, wo, rw, rb,
           lam_init, tm, ctx_tiles, row0_tiles):
    bsz, tt, dm = x.shape
    nt = tt // tm - row0_tiles
    rows = nt * tm
    src = lambda b, t: (b, row0_tiles + t, 0)

    def col(c):
        return pl.BlockSpec((None, tm, MIX_W), lambda b, t: (b, row0_tiles + t, c))

    dst = lambda b, t: (b, t, 0)
    outs = [(dm, F32), (dm, F32), (LANES, F32)]
    out_specs = [pl.BlockSpec((None, tm, wd), dst) for wd, _ in outs]
    out_shape = [jax.ShapeDtypeStruct((bsz, rows, wd), dt) for wd, dt in outs]
    out_specs.append(pl.BlockSpec((None, 8, tm), lambda b, t: (b * nt + t, 0, 0)))
    out_shape.append(jax.ShapeDtypeStruct((bsz * nt, 8, tm), F32))
    out_specs.append(pl.BlockSpec((None, N_EXPERTS, LANES), lambda b, t: (b * nt + t, 0, 0)))
    out_shape.append(jax.ShapeDtypeStruct((bsz * nt, N_EXPERTS, LANES), F32))
    return pl.pallas_call(
        functools.partial(_merge_kernel, lam_init=lam_init),
        grid=(bsz, nt),
        in_specs=[pl.BlockSpec((None, tm, dm), src),
                  pl.BlockSpec((None, None, 6, dm),
                               lambda b, t: (b, jnp.where(row0_tiles + t >= ctx_tiles, 1, 0), 0, 0)),
                  _const_spec((1, dm)), _const_spec((1, dm)),
                  col(0), col(0), col(2), col(0), col(0), col(3), col(0),
                  _const_spec((3, MIX_W)), _const_spec(wg.shape), _const_spec(bg.shape),
                  _const_spec(wbr.shape), _const_spec(wo.shape), _const_spec(rw.shape), _const_spec(rb.shape)],
        out_specs=out_specs,
        out_shape=out_shape,
        compiler_params=_params(("parallel", "parallel")),
        name="merge_route",
    )(x, mod, n1, n2, oaf, oab, pa, obf, obb, pb, oc, hg, wg, bg, wbr, wo, rw, rb)


def _slot_tables(route_rows, counts):
    tiles, _, tm = route_rows.shape
    cnt = counts[:, :, 0].astype(jnp.int32)
    tile_off = jnp.cumsum(cnt, axis=0) - cnt
    total = jnp.sum(cnt, axis=0)
    padded = (total + MOE_BLOCK - 1) // MOE_BLOCK * MOE_BLOCK
    p_end = jnp.cumsum(padded)
    base = (p_end - padded)[None, :] + tile_off
    e = route_rows[:, 0:2, :].astype(jnp.int32)
    rank = route_rows[:, 4:6, :].astype(jnp.int32)
    hit = e[:, :, :, None] == jnp.arange(N_EXPERTS, dtype=jnp.int32)
    slot = jnp.sum(jnp.where(hit, base[:, None, None, :], 0), axis=-1) + rank
    n_blocks = -(-(2 * tiles * tm) // MOE_BLOCK) + N_EXPERTS
    blk_start = jnp.arange(n_blocks, dtype=jnp.int32) * MOE_BLOCK
    block_e = jnp.minimum(jnp.sum(blk_start[:, None] >= p_end[None, :], axis=1), N_EXPERTS - 1).astype(jnp.int32)
    n_used = (p_end[-1] // MOE_BLOCK).astype(jnp.int32).reshape(1)
    return slot.reshape(tiles, 1, 2 * tm), block_e, n_used, n_blocks


def _row_copies(n_rows, make):
    for r in range(n_rows):
        for k in range(2):
            make(r, k).start()


def _scatter_kernel(slot_ref, h_ref, init_ref, xd_ref, sem):
    tm = h_ref.shape[0]
    _row_copies(tm, lambda r, k: pltpu.make_async_copy(
        h_ref.at[pl.ds(r, 1), :], xd_ref.at[pl.ds(slot_ref[0, k * tm + r], 1), :], sem))
    for _ in range(2):
        pltpu.make_async_copy(h_ref, xd_ref.at[pl.ds(0, tm), :], sem).wait()


def _scatter(slot, h2, n_slots, tm):
    bsz, rows, dm = h2.shape
    nt = rows // tm
    init = jnp.zeros((n_slots, dm), F32)
    return pl.pallas_call(
        _scatter_kernel,
        grid=(bsz, nt),
        in_specs=[pl.BlockSpec((None, 1, 2 * tm), lambda b, t: (b * nt + t, 0, 0), memory_space=pltpu.SMEM),
                  pl.BlockSpec((tm, dm), lambda b, t: (b * nt + t, 0)),
                  pl.BlockSpec(memory_space=pl.ANY)],
        out_specs=pl.BlockSpec(memory_space=pl.ANY),
        out_shape=jax.ShapeDtypeStruct((n_slots, dm), F32),
        scratch_shapes=[pltpu.SemaphoreType.DMA(())],
        input_output_aliases={2: 0},
        compiler_params=_params(("arbitrary", "arbitrary"), has_side_effects=True),
        name="moe_scatter",
    )(slot, h2.reshape(bsz * rows, dm), init)


def _expert_kernel(be_ref, nu_ref, x_ref, w1_ref, w3_ref, w2_ref, y_ref, w1b, w3b, w2b):
    i = pl.program_id(0)

    @pl.when(i < nu_ref[0])
    def _():
        @pl.when((i == 0) | (be_ref[i] != be_ref[jnp.maximum(i - 1, 0)]))
        def _():
            w1b[...] = w1_ref[...].astype(BF16)
            w3b[...] = w3_ref[...].astype(BF16)
            w2b[...] = w2_ref[...].astype(BF16)

        xb = x_ref[...].astype(BF16)
        u = _dot(xb, w1b[...])
        hmid = (u * _sigmoid(u)) * _dot(xb, w3b[...])
        y_ref[...] = _dot(hmid.astype(BF16), w2b[...])

    @pl.when(i >= nu_ref[0])
    def _():
        y_ref[...] = jnp.zeros_like(y_ref)


def _experts(block_e, n_used, x_disp, w1, w3, w2, layer):
    n_slots, dm = x_disp.shape
    n_blocks = n_slots // MOE_BLOCK
    wspec = lambda shape: pl.BlockSpec((None, None) + shape, lambda i, be, nu: (layer, be[i], 0, 0))
    grid_spec = pltpu.PrefetchScalarGridSpec(
        num_scalar_prefetch=2,
        grid=(n_blocks,),
        in_specs=[pl.BlockSpec((MOE_BLOCK, dm), lambda i, be, nu: (jnp.minimum(i, nu[0] - 1), 0)),
                  wspec((dm, D_EXPERT)), wspec((dm, D_EXPERT)), wspec((D_EXPERT, dm))],
        out_specs=pl.BlockSpec((MOE_BLOCK, dm), lambda i, be, nu: (i, 0)),
        scratch_shapes=[pltpu.VMEM((dm, D_EXPERT), BF16), pltpu.VMEM((dm, D_EXPERT), BF16),
                        pltpu.VMEM((D_EXPERT, dm), BF16)],
    )
    return pl.pallas_call(
        _expert_kernel,
        grid_spec=grid_spec,
        out_shape=jax.ShapeDtypeStruct((n_slots, dm), F32),
        compiler_params=_params(("arbitrary",)),
        name="moe_experts",
    )(block_e, n_used, x_disp, w1, w3, w2)


def _combine_kernel(slot_ref, x_ref, mod_ref, rt_ref, yd_ref, fg_ref, o_ref, ybuf, sem, *, final):
    tm = x_ref.shape[0]
    _row_copies(tm, lambda r, k: pltpu.make_async_copy(
        yd_ref.at[pl.ds(slot_ref[0, k * tm + r], 1), :], ybuf.at[k, pl.ds(r, 1), :], sem))
    for k in range(2):
        pltpu.make_async_copy(yd_ref.at[pl.ds(0, tm), :], ybuf.at[k], sem).wait()
    rt = rt_ref[...]
    y = rt[:, 2:3] * ybuf[0] + rt[:, 3:4] * ybuf[1]
    xn = x_ref[...] + mod_ref[5:6, :] * y
    o_ref[...] = _rms(xn, fg_ref[...]) if final else xn


def _combine(slot, x, mod, route, y_disp, fg, final, tm, ctx_tiles, row0_tiles):
    bsz, rows, dm = x.shape
    nt = rows // tm
    return pl.pallas_call(
        functools.partial(_combine_kernel, final=final),
        grid=(bsz, nt),
        in_specs=[pl.BlockSpec((None, 1, 2 * tm), lambda b, t: (b * nt + t, 0, 0), memory_space=pltpu.SMEM),
                  pl.BlockSpec((None, tm, dm), lambda b, t: (b, t, 0)),
                  pl.BlockSpec((None, None, 6, dm),
                               lambda b, t: (b, jnp.where(row0_tiles + t >= ctx_tiles, 1, 0), 0, 0)),
                  pl.BlockSpec((None, tm, LANES), lambda b, t: (b, t, 0)),
                  pl.BlockSpec(memory_space=pl.ANY),
                  _const_spec((1, dm))],
        out_specs=pl.BlockSpec((None, tm, dm), lambda b, t: (b, t, 0)),
        out_shape=jax.ShapeDtypeStruct((bsz, rows, dm), F32),
        scratch_shapes=[pltpu.VMEM((2, tm, dm), F32), pltpu.SemaphoreType.DMA(())],
        compiler_params=_params(("arbitrary", "arbitrary")),
        name="moe_combine",
    )(slot, x, mod, route, y_disp, fg)


def _rope_tables(ctx, t_lat):
    rows = t_lat // GRID_W
    row = jnp.repeat(jnp.arange(rows, dtype=F32), GRID_W)
    col = jnp.tile(jnp.arange(GRID_W, dtype=F32), rows)
    inv = ROPE_BASE ** (-jnp.arange(ROPE_PAIRS, dtype=F32) / ROPE_PAIRS)
    ang_r = row[:, None] * inv
    ang_c = col[:, None] * inv
    cos64 = jnp.concatenate([jnp.cos(ang_r), jnp.cos(ang_r), jnp.cos(ang_c), jnp.cos(ang_c)], axis=1)
    sin64 = jnp.concatenate([-jnp.sin(ang_r), jnp.sin(ang_r), -jnp.sin(ang_c), jnp.sin(ang_c)], axis=1)
    cos = jnp.tile(cos64, (1, MIX_W // 64))
    sin = jnp.tile(sin64, (1, MIX_W // 64))
    cos = jnp.concatenate([jnp.ones((ctx, MIX_W), F32), cos], axis=0)
    sin = jnp.concatenate([jnp.zeros((ctx, MIX_W), F32), sin], axis=0)
    return cos, sin


def _pack_w_in(w):
    dm = w.shape[0]
    a_end = PA_W
    b_end = a_end + PB_W
    g_end = b_end + 4 * HEADS
    pad = jnp.zeros((dm, PG_W - 4 * HEADS), w.dtype)
    return jnp.concatenate([w[:, :b_end], w[:, b_end:g_end], pad, w[:, g_end:]], axis=1).astype(BF16)


def kernel(x, c, ctx, c_ctx, ada_w, ada_b, norm1_g, norm2_g, w_in, mlstm_conv_w, mlstm_conv_b, mlstm_gate_b,
           hgrn_lb_raw, hgrn_norm_g, mlstm_norm_g, diff_norm_g, diff_lambda, w_branch, w_gate, b_gate, w_out,
           router_g_w, router_g_b, router_e_w, router_e_b, moe_w1, moe_w3, moe_w2, final_g):
    bsz, t_lat, dm = x.shape
    t_ctx = ctx.shape[1]
    depth = ada_w.shape[0]
    tt = t_ctx + t_lat
    tm = min(256, t_ctx)
    hgrn_chunk = min(128, t_ctx)
    mlstm_chunk = min(256, t_ctx)
    assert t_ctx % tm == 0 and t_lat % tm == 0 and t_lat % GRID_W == 0
    assert t_ctx % hgrn_chunk == 0 and t_ctx % mlstm_chunk == 0
    ctx_tiles = t_ctx // tm

    n_rows = -(-(bsz + 1) // 8) * 8
    cc = jnp.zeros((n_rows, dm), F32).at[:bsz].set(c).at[bsz].set(c_ctx)
    mods = _ada_mod(cc, ada_w, ada_b).reshape(depth, n_rows, 6, dm)

    cos, sin = _rope_tables(t_ctx, t_lat)
    lb_cum = jnp.cumsum(jax.nn.softmax(hgrn_lb_raw.astype(F32), axis=0), axis=0)
    lower = lb_cum - lb_cum[0]

    xc = jnp.concatenate([ctx, x], axis=1)
    out = None
    for l in range(depth):
        with_ctx = l < depth - 1
        lam_init = 0.8 - 0.6 * math.exp(-0.3 * l)
        mod = jnp.stack([jnp.broadcast_to(mods[l, bsz], (bsz, 6, dm)), mods[l, :bsz]], axis=1)
        n1 = norm1_g[l].reshape(1, dm)
        n2 = norm2_g[l].reshape(1, dm)

        lb = lower[l]
        pad = jnp.zeros((2, 6, MIX_W), F32)
        lbc = jnp.concatenate([jnp.log(lb)[:, None], jnp.log1p(-lb)[:, None], pad], axis=1)
        pa, pb, pg, qc, kc, vc = _norm_proj(xc, mod, n1, _pack_w_in(w_in[l]), cos, sin, lbc, tm, ctx_tiles)

        oaf, oab = _hgrn(pa, hgrn_chunk, t_ctx // hgrn_chunk)

        qk = _conv_silu(pb, mlstm_conv_w[l], mlstm_conv_b[l], t_ctx, 256)
        gbias = jnp.zeros((1, PG_W), F32).at[0, :4 * HEADS].set(mlstm_gate_b[l].reshape(-1))
        obf, obb = _mlstm(qk, pb, pg, gbias, mlstm_chunk, t_ctx // mlstm_chunk)

        lam_vec = diff_lambda[l].astype(F32)
        oc = _attn(lam_vec, qc, kc, vc, lam_init, tm, t_ctx, with_ctx)

        hg = jnp.stack([hgrn_norm_g[l], mlstm_norm_g[l], diff_norm_g[l]])
        rw = jnp.zeros((LANES, dm), F32).at[:N_GROUPS].set(router_g_w[l].T)
        rw = rw.at[N_GROUPS:N_GROUPS + N_EXPERTS].set(router_e_w[l].T)
        rb = jnp.zeros((LANES, 1), F32).at[:N_GROUPS, 0].set(router_g_b[l])
        rb = rb.at[N_GROUPS:N_GROUPS + N_EXPERTS, 0].set(router_e_b[l])
        row0_tiles = 0 if with_ctx else ctx_tiles
        xn, h2, route, route_rows, counts = _merge(xc, mod, n1, n2, oaf, oab, pa, obf, obb, pb, oc, hg,
                                                   w_gate[l].astype(BF16), b_gate[l].reshape(1, 3 * dm),
                                                   w_branch[l].astype(BF16), w_out[l].astype(BF16), rw, rb,
                                                   lam_init, tm, ctx_tiles, row0_tiles)

        slot, block_e, n_used, n_blocks = _slot_tables(route_rows, counts)
        x_disp = _scatter(slot, h2, n_blocks * MOE_BLOCK, tm)
        y_disp = _experts(block_e, n_used, x_disp, moe_w1, moe_w3, moe_w2, l)
        final = l == depth - 1
        xc = _combine(slot, xn, mod, route, y_disp, final_g.reshape(1, dm), final, tm, ctx_tiles, row0_tiles)
        out = xc
    return out
```

```python
import functools
import math

import numpy as np
import jax
import jax.numpy as jnp
from jax import lax
from jax.experimental import pallas as pl
from jax.experimental.pallas import tpu as pltpu

F32 = jnp.float32
BF16 = jnp.bfloat16

EPS = 1e-6
NEG_BIG = -1e30
HEADS = 4
HEAD_W = 128
MIX_W = HEADS * HEAD_W
GRID_W = 64
ROPE_BASE = 10000.0
ROPE_PAIRS = 16
B_CONV = 3
N_GROUPS = 4
EXPERTS_PER_GROUP = 8
N_EXPERTS = N_GROUPS * EXPERTS_PER_GROUP
D_EXPERT = 512
MOE_BLOCK = 256
LANES = 128
VMEM_LIMIT = 50 * 1024 * 1024

PA_W = 5 * MIX_W
PB_W = 4 * MIX_W
PG_W = LANES
PC_W = 3 * MIX_W
PROJ_W = PA_W + PB_W + PG_W + PC_W


def _params(sem, **kw):
    return pltpu.CompilerParams(dimension_semantics=sem, vmem_limit_bytes=VMEM_LIMIT, **kw)


def _const_spec(shape):
    nd = len(shape)
    return pl.BlockSpec(shape, lambda *_: (0,) * nd, pipeline_mode=pl.Buffered(1))


def _dot(a, b):
    return jnp.dot(a, b, preferred_element_type=F32)


def _dot_nt(a, b):
    return lax.dot_general(a, b, (((1,), (1,)), ((), ())), preferred_element_type=F32)


def _dot_tn(a, b):
    return lax.dot_general(a, b, (((0,), (0,)), ((), ())), preferred_element_type=F32)


def _dot_sel(m_bf16, x):
    hi = x.astype(BF16)
    r1 = x - hi.astype(F32)
    mid = r1.astype(BF16)
    lo = (r1 - mid.astype(F32)).astype(BF16)
    return _dot(m_bf16, hi) + _dot(m_bf16, mid) + _dot(m_bf16, lo)


def _sigmoid(x):
    return 1.0 / (1.0 + jnp.exp(-x))


def _log_sigmoid(x):
    return jnp.minimum(x, 0.0) - jnp.log1p(jnp.exp(-jnp.abs(x)))


def _rms(x, g):
    return x * lax.rsqrt(jnp.mean(x * x, axis=-1, keepdims=True) + EPS) * g


def _ada_kernel(c_ref, w_ref, b_ref, o_ref):
    c = c_ref[...]
    s = c * _sigmoid(c)
    o_ref[...] = jnp.dot(s, w_ref[...], preferred_element_type=F32,
                         precision=lax.Precision.HIGHEST) + b_ref[...]


def _ada_mod(cc, ada_w, ada_b):
    depth, dm, six = ada_w.shape
    rows = cc.shape[0]
    tn = dm
    return pl.pallas_call(
        _ada_kernel,
        grid=(depth, six // tn),
        in_specs=[pl.BlockSpec((rows, dm), lambda l, n: (0, 0)),
                  pl.BlockSpec((None, dm, tn), lambda l, n: (l, 0, n)),
                  pl.BlockSpec((None, 1, tn), lambda l, n: (l, 0, n))],
        out_specs=pl.BlockSpec((None, rows, tn), lambda l, n: (l, 0, n)),
        out_shape=jax.ShapeDtypeStruct((depth, rows, six), F32),
        compiler_params=_params(("parallel", "parallel")),
        name="ada_mod",
    )(cc, ada_w, ada_b.reshape(depth, 1, six))


def _rope(x, cos, sin):
    n = x.shape[-1]
    lane = lax.broadcasted_iota(jnp.int32, (1, n), 1)
    first = (lane // ROPE_PAIRS) % 2 == 0
    partner = jnp.where(first, pltpu.roll(x, n - ROPE_PAIRS, 1), pltpu.roll(x, ROPE_PAIRS, 1))
    return x * cos + partner * sin


def _hgrn_log_forget(z, lbc):
    lsig = jnp.minimum(z, 0.0) - jnp.log(1.0 + jnp.exp(-jnp.abs(z)))
    a = lbc[0:1]
    bb = lbc[1:2] + lsig
    return jnp.maximum(a, bb) + jnp.log(1.0 + jnp.exp(-jnp.abs(a - bb)))


def _norm_proj_kernel(x_ref, mod_ref, g_ref, w_ref, cos_ref, sin_ref, lbc_ref,
                      pa_ref, pb_ref, pg_ref, qc_ref, kc_ref, vc_ref):
    x = x_ref[...]
    mod = mod_ref[...]
    h = _rms(x, g_ref[...]) * (1.0 + mod[1:2]) + mod[0:1]
    hb = h.astype(BF16)
    pa_ref[:, 0:3 * MIX_W] = _dot(hb, w_ref[:, 0:3 * MIX_W])
    for d in range(2):
        cs = slice((3 + d) * MIX_W, (4 + d) * MIX_W)
        pa_ref[:, cs] = _hgrn_log_forget(_dot(hb, w_ref[:, cs]), lbc_ref[d])
    pb_ref[...] = _dot(hb, w_ref[:, PA_W:PA_W + PB_W])
    pg_ref[...] = _dot(hb, w_ref[:, PA_W + PB_W:PA_W + PB_W + PG_W])
    c0 = PA_W + PB_W + PG_W
    cos = cos_ref[...]
    sin = sin_ref[...]
    q = _dot(hb, w_ref[:, c0:c0 + MIX_W])
    qc_ref[...] = (_rope(q, cos, sin) * (64.0 ** -0.5 * math.log2(math.e))).astype(BF16)
    k = _dot(hb, w_ref[:, c0 + MIX_W:c0 + 2 * MIX_W])
    kc_ref[...] = _rope(k, cos, sin).astype(BF16)
    vc_ref[...] = _dot(hb, w_ref[:, c0 + 2 * MIX_W:c0 + 3 * MIX_W]).astype(BF16)


def _norm_proj(x, mod, g, w, cos, sin, lbc, tm, ctx_tiles):
    bsz, tt, dm = x.shape
    nt = tt // tm
    row = lambda b, t: (b, t, 0)
    outs = [(PA_W, F32), (PB_W, F32), (PG_W, F32), (MIX_W, BF16), (MIX_W, BF16), (MIX_W, BF16)]
    return pl.pallas_call(
        _norm_proj_kernel,
        grid=(bsz, nt),
        in_specs=[pl.BlockSpec((None, tm, dm), row),
                  pl.BlockSpec((None, None, 6, dm), lambda b, t: (b, jnp.where(t >= ctx_tiles, 1, 0), 0, 0)),
                  _const_spec((1, dm)),
                  _const_spec((dm, PROJ_W)),
                  pl.BlockSpec((tm, MIX_W), lambda b, t: (t, 0)),
                  pl.BlockSpec((tm, MIX_W), lambda b, t: (t, 0)),
                  _const_spec(lbc.shape)],
        out_specs=[pl.BlockSpec((None, tm, wd), row) for wd, _ in outs],
        out_shape=[jax.ShapeDtypeStruct((bsz, tt, wd), dt) for wd, dt in outs],
        compiler_params=_params(("parallel", "parallel")),
        name="norm_proj",
    )(x, mod, g, w, cos, sin, lbc)


def _conv_kernel(x_ref, w_ref, b_ref, o_ref, *, ctx, q_blocks):
    x = x_ref[...]
    tt = x.shape[0]
    row = lax.broadcasted_iota(jnp.int32, (tt, 1), 0)
    prev = jnp.where((row == 0) | (row == ctx), 0.0, pltpu.roll(x, 1, 0))
    nxt = jnp.where((row == ctx - 1) | (row == tt - 1), 0.0, pltpu.roll(x, tt - 1, 0))
    w = w_ref[...]
    y = b_ref[...] + prev * w[0:1] + x * w[1:2] + nxt * w[2:3]
    y = y * _sigmoid(y)
    scale = jnp.where(pl.program_id(1) < q_blocks, HEAD_W ** -0.5, 1.0)
    o_ref[...] = y * scale


def _conv_silu(pb, conv_w, conv_b, ctx, cb):
    bsz, tt, _ = pb.shape
    width = 2 * MIX_W
    return pl.pallas_call(
        functools.partial(_conv_kernel, ctx=ctx, q_blocks=MIX_W // cb),
        grid=(bsz, width // cb),
        in_specs=[pl.BlockSpec((None, tt, cb), lambda b, j: (b, 0, j)),
                  pl.BlockSpec((B_CONV, cb), lambda b, j: (0, j)),
                  pl.BlockSpec((1, cb), lambda b, j: (0, j))],
        out_specs=pl.BlockSpec((None, tt, cb), lambda b, j: (b, 0, j)),
        out_shape=jax.ShapeDtypeStruct((bsz, tt, width), F32),
        compiler_params=_params(("parallel", "parallel")),
        name="mlstm_conv",
    )(pb, conv_w, conv_b.reshape(1, width))


def _chunk_consts(L):
    idx = np.arange(L)
    t, u = idx[:, None], idx[None, :]
    cum = (u <= t).astype(np.float32)
    nlev = int(round(math.log2(L)))
    lvl = np.full((L, L), -1, np.int32)
    lvl[idx, idx] = nlev
    for i in range(nlev):
        h = L >> (i + 1)
        same = (t // (2 * h)) == (u // (2 * h))
        lvl[same & (t % (2 * h) >= h) & (u % (2 * h) < h)] = i
    sel = jnp.asarray(np.stack([cum, cum[::-1, ::-1]]), BF16)
    lvls = jnp.asarray(np.stack([lvl, lvl[::-1, ::-1]]))
    return sel, lvls, nlev


def _midpoint_rows(b, h, d):
    L, w = b.shape
    two = 2 * h
    pos = h - 1 if d == 0 else h
    if two % 8 == 0:
        r = b.reshape(L // two, two, w)[:, pos:pos + 1, :]
        return jnp.broadcast_to(r, (L // two, two, w)).reshape(L, w)
    phase = lax.broadcasted_iota(jnp.int32, (L, 1), 0) % two
    ref = b
    for off in range(pos - two + 1, pos + 1):
        if off != 0:
            ref = jnp.where(phase == pos - off, pltpu.roll(b, (L - off) % L, 0), ref)
    return ref


def _scan_blocks(j, nc_ctx, nc):
    jb = jnp.where(j < nc_ctx, nc_ctx - 1 - j, nc - 1 + nc_ctx - j)
    return j, jb


def _hgrn_kernel(qf_ref, vf_ref, ff_ref, qb_ref, vb_ref, fb_ref, sel_ref, lvl_ref,
                 of_ref, ob_ref, st_ref, *, L, nlev):
    @pl.when(pl.program_id(1) == 0)
    def _():
        st_ref[...] = jnp.zeros_like(st_ref)

    row = lax.broadcasted_iota(jnp.int32, (L, 1), 0)
    dirs = ((qf_ref, vf_ref, ff_ref, of_ref), (qb_ref, vb_ref, fb_ref, ob_ref))
    for d, (q_ref, v_ref, f_ref, o_ref) in enumerate(dirs):
        lf = f_ref[...]
        kk = 1.0 - jnp.exp(lf)
        q_all = q_ref[...]
        b_in = _dot_sel(sel_ref[d], lf)
        lvl = lvl_ref[d]
        last = L - 1 if d == 0 else 0
        b_end = b_in[last:last + 1]
        heads = [slice(hd * HEAD_W, (hd + 1) * HEAD_W) for hd in range(HEADS)]
        qb = q_all.astype(BF16)
        kb = kk.astype(BF16)
        on_diag = lvl == nlev
        scores = [jnp.where(on_diag, _dot_nt(qb[:, cs], kb[:, cs]), 0.0) for cs in heads]
        for i in range(nlev):
            h = L >> (i + 1)
            ei = jnp.exp(-jnp.abs(b_in - _midpoint_rows(b_in, h, d)))
            is_query = (row % (2 * h) >= h) if d == 0 else (row % (2 * h) < h)
            xe = (jnp.where(is_query, q_all, kk) * ei).astype(BF16)
            at_level = lvl == i
            for hd, cs in enumerate(heads):
                scores[hd] = scores[hd] + jnp.where(at_level, _dot_nt(xe[:, cs], xe[:, cs]), 0.0)
        q_in = (q_all * jnp.exp(b_in)).astype(BF16)
        k_out = (kk * jnp.exp(b_end - b_in)).astype(BF16)
        decay = jnp.exp(b_end)
        for hd, cs in enumerate(heads):
            vb = v_ref[:, cs].astype(BF16)
            st = st_ref[d, hd]
            o_ref[:, cs] = _dot(scores[hd].astype(BF16), vb) + _dot_nt(q_in[:, cs], st.astype(BF16))
            st_ref[d, hd] = st * decay[:, cs] + _dot_tn(vb, k_out[:, cs])


def _hgrn(pa, L, nc_ctx):
    bsz, tt, _ = pa.shape
    nc = tt // L
    sel, lvl, nlev = _chunk_consts(L)

    def spec(col, which):
        return pl.BlockSpec((None, L, MIX_W), lambda b, j: (b, _scan_blocks(j, nc_ctx, nc)[which], col))

    return pl.pallas_call(
        functools.partial(_hgrn_kernel, L=L, nlev=nlev),
        grid=(bsz, nc),
        in_specs=[spec(0, 0), spec(1, 0), spec(3, 0), spec(0, 1), spec(1, 1), spec(4, 1),
                  _const_spec(sel.shape), _const_spec(lvl.shape)],
        out_specs=[spec(0, 0), spec(0, 1)],
        out_shape=[jax.ShapeDtypeStruct((bsz, tt, MIX_W), F32)] * 2,
        scratch_shapes=[pltpu.VMEM((2, HEADS, HEAD_W, HEAD_W), F32)],
        compiler_params=_params(("parallel", "arbitrary")),
        name="hgrn2_scan",
    )(pa, pa, pa, pa, pa, pa, sel, lvl)


def _mlstm_kernel(qf_ref, kf_ref, vf_ref, gf_ref, qb_ref, kb_ref, vb_ref, gb_ref, gbias_ref,
                  sel_ref, lvl_ref, of_ref, ob_ref, c_ref, m_ref, *, L, nlev):
    @pl.when(pl.program_id(1) == 0)
    def _():
        c_ref[...] = jnp.zeros_like(c_ref)
        m_ref[...] = jnp.zeros_like(m_ref)

    lane = lax.broadcasted_iota(jnp.int32, (1, LANES), 1)
    ones_col = jnp.where(lane == 0, 1.0, 0.0).astype(BF16)
    dirs = ((qf_ref, kf_ref, vf_ref, gf_ref, of_ref), (qb_ref, kb_ref, vb_ref, gb_ref, ob_ref))
    for d, (q_ref, k_ref, v_ref, g_ref, o_ref) in enumerate(dirs):
        g = g_ref[...] + gbias_ref[...]
        lf = jnp.where((lane >= 2 * HEADS) & (lane < 4 * HEADS), _log_sigmoid(g), 0.0)
        bcum = _dot_sel(sel_ref[d, 0:L], lf)
        causal = lvl_ref[d] >= 0
        row_terms = (g - pltpu.roll(bcum, LANES - 2 * HEADS, 1)).T
        last = L - 1 if d == 0 else 0
        for hd in range(HEADS):
            cs = slice(hd * HEAD_W, (hd + 1) * HEAD_W)
            li_lane = d * HEADS + hd
            lf_lane = 2 * HEADS + d * HEADS + hd
            b = bcum[:, lf_lane:lf_lane + 1]
            row_term = row_terms[li_lane:li_lane + 1, :]
            b_end = b[last:last + 1]
            r = d * HEADS + hd
            m_prev = m_ref[r:r + 1, 0:1]
            q = q_ref[:, cs].astype(BF16)
            k = k_ref[:, cs]
            v_aug = jnp.concatenate([v_ref[:, cs].astype(BF16), jnp.broadcast_to(ones_col, (L, LANES))], axis=1)
            c_prev = c_ref[r]
            w_end = b_end + row_term
            m_new = jnp.maximum(b_end + m_prev, jnp.max(w_end, axis=-1, keepdims=True))
            e_end = jnp.exp(w_end - m_new)
            keep = jnp.exp(b_end + m_prev - m_new)
            c_ref[r] = keep * c_prev + _dot((k.T * e_end).astype(BF16), v_aug)
            m_ref[r:r + 1, :] = jnp.broadcast_to(m_new, (1, LANES))
            logw = jnp.where(causal, b + row_term, NEG_BIG)
            w_state = b + m_prev
            m_t = jnp.maximum(jnp.max(logw, axis=-1, keepdims=True), w_state)
            s = _dot_nt(q, k.astype(BF16)) * jnp.exp(logw - m_t)
            a_state = jnp.exp(w_state - m_t)
            num = _dot(s.astype(BF16), v_aug) + a_state * _dot(q, c_prev.astype(BF16))
            den = num[:, HEAD_W:HEAD_W + 1]
            o_ref[:, cs] = num[:, 0:HEAD_W] / jnp.maximum(jnp.abs(den), jnp.exp(-m_t))


def _mlstm(qk, pb, pg, gbias, L, nc_ctx):
    bsz, tt, _ = pb.shape
    nc = tt // L
    sel, lvl, nlev = _chunk_consts(L)

    def spec(col, which, width=MIX_W):
        return pl.BlockSpec((None, L, width), lambda b, j: (b, _scan_blocks(j, nc_ctx, nc)[which], col))

    return pl.pallas_call(
        functools.partial(_mlstm_kernel, L=L, nlev=nlev),
        grid=(bsz, nc),
        in_specs=[spec(0, 0), spec(1, 0), spec(2, 0), spec(0, 0, PG_W),
                  spec(0, 1), spec(1, 1), spec(2, 1), spec(0, 1, PG_W),
                  _const_spec((1, PG_W)), _const_spec(sel.shape), _const_spec(lvl.shape)],
        out_specs=[spec(0, 0), spec(0, 1)],
        out_shape=[jax.ShapeDtypeStruct((bsz, tt, MIX_W), F32)] * 2,
        scratch_shapes=[pltpu.VMEM((2 * HEADS, HEAD_W, 2 * HEAD_W), F32),
                        pltpu.VMEM((2 * HEADS, LANES), F32)],
        compiler_params=_params(("parallel", "arbitrary")),
        name="mlstm_scan",
    )(qk, qk, pb, pg, qk, qk, pb, pg, gbias, sel, lvl)


def _diff_lambda(lam_ref, lam_init):
    lv = lam_ref[...]
    return (jnp.exp(jnp.sum(lv[0:1] * lv[1:2], axis=-1, keepdims=True))
            - jnp.exp(jnp.sum(lv[2:3] * lv[3:4], axis=-1, keepdims=True)) + lam_init)


def _map_scores(q, k):
    lane = lax.broadcasted_iota(jnp.int32, (1, HEAD_W), 1)
    zero = jnp.zeros_like(q)
    return (_dot_nt(jnp.where(lane < HEAD_W // 2, q, zero), k),
            _dot_nt(jnp.where(lane >= HEAD_W // 2, q, zero), k))


def _row_max(s):
    return jnp.max(s, axis=-1, keepdims=True)


def _diff_softmax_v(s1, m1, s2, m2, lam, v):
    def probs(s, m):
        p = jnp.exp2(s - m)
        return p / jnp.sum(p, axis=-1, keepdims=True)

    return _dot((probs(s1, m1) - lam * probs(s2, m2)).astype(BF16), v)


def _attn_ctx_kernel(lam_ref, q_ref, k_ref, v_ref, o_ref, *, lam_init):
    s1, s2 = _map_scores(q_ref[...], k_ref[...])
    o_ref[...] = _diff_softmax_v(s1, _row_max(s1), s2, _row_max(s2), _diff_lambda(lam_ref, lam_init), v_ref[...])


def _attn_ctx(lam_vec, qc, kc, vc, lam_init, tq, t_ctx):
    bsz = qc.shape[0]
    kv = pl.BlockSpec((None, t_ctx, HEAD_W), lambda b, h, i: (b, 0, h))
    return pl.pallas_call(
        functools.partial(_attn_ctx_kernel, lam_init=lam_init),
        grid=(bsz, HEADS, t_ctx // tq),
        in_specs=[_const_spec(lam_vec.shape),
                  pl.BlockSpec((None, tq, HEAD_W), lambda b, h, i: (b, i, h)), kv, kv],
        out_specs=pl.BlockSpec((None, tq, HEAD_W), lambda b, h, i: (b, i, h)),
        out_shape=jax.ShapeDtypeStruct((bsz, t_ctx, MIX_W), F32),
        compiler_params=_params(("parallel", "parallel", "arbitrary")),
        name="diff_attn_ctx",
    )(lam_vec, qc, kc, vc)


def _attn_lat_kernel(lam_ref, q_ref, k_ref, v_ref, o_ref, s_even, m_even, s_odd, m_odd, *, lam_init, n_tiles):
    j = pl.program_id(2)
    lam = _diff_lambda(lam_ref, lam_init)
    even, odd = (s_even, m_even), (s_odd, m_odd)

    def scores_into(bufs):
        s_buf, m_buf = bufs
        for i, s in enumerate(_map_scores(q_ref[...], k_ref[...])):
            s_buf[i] = s
            m_buf[i] = jnp.broadcast_to(_row_max(s), m_buf.shape[1:])

    def finish_from(bufs):
        s_buf, m_buf = bufs
        o_ref[...] = _diff_softmax_v(s_buf[0], m_buf[0][:, 0:1], s_buf[1], m_buf[1][:, 0:1], lam, v_ref[...])

    @pl.when(j == 0)
    def _():
        scores_into(even)

    middle = (j > 0) & (j < n_tiles)

    @pl.when(middle & (j % 2 == 1))
    def _():
        scores_into(odd)
        finish_from(even)

    @pl.when(middle & (j % 2 == 0))
    def _():
        scores_into(even)
        finish_from(odd)

    @pl.when(j == n_tiles)
    def _():
        finish_from(odd if n_tiles % 2 == 0 else even)


def _attn_lat(lam_vec, qc, kc, vc, lam_init, tq, t_ctx):
    bsz, tt, _ = qc.shape
    n_tiles = (tt - t_ctx) // tq
    q0 = t_ctx // tq
    kv = pl.BlockSpec((None, tt, HEAD_W), lambda b, h, j: (b, 0, h))
    return pl.pallas_call(
        functools.partial(_attn_lat_kernel, lam_init=lam_init, n_tiles=n_tiles),
        grid=(bsz, HEADS, n_tiles + 1),
        in_specs=[_const_spec(lam_vec.shape),
                  pl.BlockSpec((None, tq, HEAD_W), lambda b, h, j: (b, q0 + jnp.minimum(j, n_tiles - 1), h)),
                  kv, kv],
        out_specs=pl.BlockSpec((None, tq, HEAD_W), lambda b, h, j: (b, jnp.maximum(j - 1, 0), h)),
        out_shape=jax.ShapeDtypeStruct((bsz, tt - t_ctx, MIX_W), F32),
        scratch_shapes=[pltpu.VMEM((2, tq, tt), F32), pltpu.VMEM((2, tq, LANES), F32),
                        pltpu.VMEM((2, tq, tt), F32), pltpu.VMEM((2, tq, LANES), F32)],
        compiler_params=_params(("parallel", "parallel", "arbitrary")),
        name="diff_attn_lat",
    )(lam_vec, qc, kc, vc)


def _head_norm(x, g):
    parts = []
    for hd in range(HEADS):
        xs = x[:, hd * HEAD_W:(hd + 1) * HEAD_W]
        parts.append(xs * lax.rsqrt(jnp.mean(xs * xs, axis=-1, keepdims=True) + EPS))
    return jnp.concatenate(parts, axis=1) * g


ROUTE_ROWS = 40


def _route(lg):
    n = lg.shape[1]
    row = lax.broadcasted_iota(jnp.int32, lg.shape, 0)
    neg = -jnp.inf
    big = ROUTE_ROWS
    gl = jnp.where(row < N_GROUPS, lg, neg)
    gmax = jnp.max(gl, axis=0, keepdims=True)
    gidx = jnp.min(jnp.where(gl == gmax, row, big), axis=0, keepdims=True)
    p_group = 1.0 / jnp.sum(jnp.exp(gl - gmax), axis=0, keepdims=True)
    lo = N_GROUPS + EXPERTS_PER_GROUP * gidx
    in_grp = (row >= lo) & (row < lo + EXPERTS_PER_GROUP)
    el = jnp.where(in_grp, lg, neg)
    pe = jnp.exp(el - jnp.max(el, axis=0, keepdims=True))
    pe = pe / jnp.sum(pe, axis=0, keepdims=True)
    pe = jnp.where(in_grp, pe, -1.0)
    v1 = jnp.max(pe, axis=0, keepdims=True)
    i1 = jnp.min(jnp.where(pe == v1, row, big), axis=0, keepdims=True)
    pe2 = jnp.where(row == i1, -1.0, pe)
    v2 = jnp.max(pe2, axis=0, keepdims=True)
    i2 = jnp.min(jnp.where(pe2 == v2, row, big), axis=0, keepdims=True)
    scale = p_group / (v1 + v2)
    erow = lax.broadcasted_iota(jnp.int32, (N_EXPERTS, n), 0) + N_GROUPS
    oh1 = erow == i1
    oh2 = erow == i2
    oh1f = jnp.where(oh1, 1.0, 0.0)
    oh2f = jnp.where(oh2, 1.0, 0.0)
    earlier = (lax.broadcasted_iota(jnp.int32, (n, n), 0) < lax.broadcasted_iota(jnp.int32, (n, n), 1))
    earlier = jnp.where(earlier, 1.0, 0.0).astype(BF16)
    tot1 = jnp.sum(oh1f, axis=1, keepdims=True)
    tot2 = jnp.sum(oh2f, axis=1, keepdims=True)
    rank1 = jnp.sum(jnp.where(oh1, _dot(oh1f.astype(BF16), earlier), 0.0), axis=0, keepdims=True)
    rank2 = jnp.sum(jnp.where(oh2, _dot(oh2f.astype(BF16), earlier) + tot1, 0.0), axis=0, keepdims=True)
    r8 = lax.broadcasted_iota(jnp.int32, (8, n), 0)
    rows = jnp.zeros((8, n), F32)
    for i, val in enumerate(((i1 - N_GROUPS).astype(F32), (i2 - N_GROUPS).astype(F32), v1 * scale, v2 * scale,
                             rank1, rank2)):
        rows = jnp.where(r8 == i, val, rows)
    return rows, tot1 + tot2


def _merge_kernel(x_ref, mod_ref, n1_ref, n2_ref, af_ref, ab_ref, ga_ref, bf_ref, bb_ref, gb_ref,
                  occ_ref, ocl_ref, hg_ref, wg_ref, bg_ref, wbr_ref, wo_ref, rw_ref, rb_ref,
                  xo_ref, h2_ref, rt_ref, rtt_ref, cnt_ref, *, lam_init, ctx_tiles, row0_tiles):
    x = x_ref[...]
    mod = mod_ref[...]
    dm = x.shape[-1]
    hb = (_rms(x, n1_ref[...]) * (1.0 + mod[1:2]) + mod[0:1]).astype(BF16)
    hg = hg_ref[...]
    ga = ga_ref[...]
    ya = _head_norm(af_ref[...] + ab_ref[...], hg[0:1]) * (ga * _sigmoid(ga))
    yb = _head_norm(bf_ref[...] + bb_ref[...], hg[1:2]) * _sigmoid(gb_ref[...])
    oc = jnp.where(row0_tiles + pl.program_id(1) < ctx_tiles, occ_ref[...], ocl_ref[...])
    yc = _head_norm(oc, hg[2:3]) * (1.0 - lam_init)
    y = jnp.zeros_like(x)
    for i, yi in enumerate((ya, yb, yc)):
        gate = _sigmoid(_dot(hb, wg_ref[:, i * dm:(i + 1) * dm]) + bg_ref[:, i * dm:(i + 1) * dm])
        y = y + gate * _dot(yi.astype(BF16), wbr_ref[i])
    xn = x + mod[2:3] * _dot(y.astype(BF16), wo_ref[...])
    xo_ref[...] = xn
    h2 = _rms(xn, n2_ref[...]) * (1.0 + mod[4:5]) + mod[3:4]
    h2_ref[...] = h2
    logits = lax.dot_general(rw_ref[...], h2, (((1,), (1,)), ((), ())), preferred_element_type=F32,
                             precision=lax.Precision.HIGHEST) + rb_ref[...]
    rows, counts = _route(logits[0:ROUTE_ROWS, :])
    rtt_ref[...] = rows
    cnt_ref[...] = jnp.broadcast_to(counts, cnt_ref.shape)
    tm = x.shape[0]
    rt_ref[...] = jnp.concatenate([rows, jnp.zeros((LANES - 8, tm), F32)], axis=0).T


def _merge(x, mod, n1, n2, oaf, oab, pa, obf, obb, pb, oc_ctx, oc_lat, hg, wg, bg, wbr, wo, rw, rb,
           lam_init, tm, ctx_tiles, row0_tiles):
    bsz, tt, dm = x.shape
    nt = tt // tm - row0_tiles
    rows = nt * tm
    src = lambda b, t: (b, row0_tiles + t, 0)

    def col(c):
        return pl.BlockSpec((None, tm, MIX_W), lambda b, t: (b, row0_tiles + t, c))

    dst = lambda b, t: (b, t, 0)
    outs = [(dm, F32), (dm, F32), (LANES, F32)]
    out_specs = [pl.BlockSpec((None, tm, wd), dst) for wd, _ in outs]
    out_shape = [jax.ShapeDtypeStruct((bsz, rows, wd), dt) for wd, dt in outs]
    out_specs.append(pl.BlockSpec((None, 8, tm), lambda b, t: (b * nt + t, 0, 0)))
    out_shape.append(jax.ShapeDtypeStruct((bsz * nt, 8, tm), F32))
    out_specs.append(pl.BlockSpec((None, N_EXPERTS, LANES), lambda b, t: (b * nt + t, 0, 0)))
    out_shape.append(jax.ShapeDtypeStruct((bsz * nt, N_EXPERTS, LANES), F32))
    return pl.pallas_call(
        functools.partial(_merge_kernel, lam_init=lam_init, ctx_tiles=ctx_tiles, row0_tiles=row0_tiles),
        grid=(bsz, nt),
        in_specs=[pl.BlockSpec((None, tm, dm), src),
                  pl.BlockSpec((None, None, 6, dm),
                               lambda b, t: (b, jnp.where(row0_tiles + t >= ctx_tiles, 1, 0), 0, 0)),
                  _const_spec((1, dm)), _const_spec((1, dm)),
                  col(0), col(0), col(2), col(0), col(0), col(3),
                  pl.BlockSpec((None, tm, MIX_W),
                               lambda b, t: (b, jnp.minimum(row0_tiles + t, max(ctx_tiles - 1, 0)), 0)),
                  pl.BlockSpec((None, tm, MIX_W),
                               lambda b, t: (b, jnp.maximum(row0_tiles + t - ctx_tiles, 0), 0)),
                  _const_spec((3, MIX_W)), _const_spec(wg.shape), _const_spec(bg.shape),
                  _const_spec(wbr.shape), _const_spec(wo.shape), _const_spec(rw.shape), _const_spec(rb.shape)],
        out_specs=out_specs,
        out_shape=out_shape,
        compiler_params=_params(("parallel", "parallel")),
        name="merge_route",
    )(x, mod, n1, n2, oaf, oab, pa, obf, obb, pb, oc_ctx, oc_lat, hg, wg, bg, wbr, wo, rw, rb)


def _slot_tables(route_rows, counts):
    tiles, _, tm = route_rows.shape
    cnt = counts[:, :, 0].astype(jnp.int32)
    tile_off = jnp.cumsum(cnt, axis=0) - cnt
    total = jnp.sum(cnt, axis=0)
    padded = (total + MOE_BLOCK - 1) // MOE_BLOCK * MOE_BLOCK
    p_end = jnp.cumsum(padded)
    base = (p_end - padded)[None, :] + tile_off
    e = route_rows[:, 0:2, :].astype(jnp.int32)
    rank = route_rows[:, 4:6, :].astype(jnp.int32)
    hit = e[:, :, :, None] == jnp.arange(N_EXPERTS, dtype=jnp.int32)
    slot = jnp.sum(jnp.where(hit, base[:, None, None, :], 0), axis=-1) + rank
    n_blocks = -(-(2 * tiles * tm) // MOE_BLOCK) + N_EXPERTS
    blk_start = jnp.arange(n_blocks, dtype=jnp.int32) * MOE_BLOCK
    block_e = jnp.minimum(jnp.sum(blk_start[:, None] >= p_end[None, :], axis=1), N_EXPERTS - 1).astype(jnp.int32)
    n_used = (p_end[-1] // MOE_BLOCK).astype(jnp.int32).reshape(1)
    filled = jnp.clip(((p_end - padded) + total)[block_e] - blk_start, 0, MOE_BLOCK)
    filled = jnp.where(blk_start < p_end[-1], filled, 0)
    partial = (filled < MOE_BLOCK).astype(jnp.int32)
    return slot.reshape(tiles, 1, 2 * tm), block_e, n_used, partial


def _row_copies(n_rows, make):
    for r in range(n_rows):
        for k in range(2):
            make(r, k).start()


def _scatter_kernel(partial_ref, slot_ref, h_ref, xd_ref, zeros, sem, zsem):
    tm = h_ref.shape[0]
    n_blocks = partial_ref.shape[0]

    @pl.when((pl.program_id(0) == 0) & (pl.program_id(1) == 0))
    def _():
        zeros[...] = jnp.zeros_like(zeros)

        def zero_block(blk):
            return pltpu.make_async_copy(zeros, xd_ref.at[pl.ds(blk * MOE_BLOCK, MOE_BLOCK), :], zsem)

        def start(blk, c):
            @pl.when(partial_ref[blk] != 0)
            def _():
                zero_block(blk).start()
            return c

        def wait(blk, c):
            @pl.when(partial_ref[blk] != 0)
            def _():
                zero_block(blk).wait()
            return c

        lax.fori_loop(0, n_blocks, start, 0)
        lax.fori_loop(0, n_blocks, wait, 0)

    _row_copies(tm, lambda r, k: pltpu.make_async_copy(
        h_ref.at[pl.ds(r, 1), :], xd_ref.at[pl.ds(slot_ref[0, k * tm + r], 1), :], sem))
    for _ in range(2):
        pltpu.make_async_copy(h_ref, xd_ref.at[pl.ds(0, tm), :], sem).wait()


def _scatter(partial, slot, h2, n_slots, tm):
    bsz, rows, dm = h2.shape
    nt = rows // tm
    grid_spec = pltpu.PrefetchScalarGridSpec(
        num_scalar_prefetch=1,
        grid=(bsz, nt),
        in_specs=[pl.BlockSpec((None, 1, 2 * tm), lambda b, t, p: (b * nt + t, 0, 0), memory_space=pltpu.SMEM),
                  pl.BlockSpec((tm, dm), lambda b, t, p: (b * nt + t, 0))],
        out_specs=pl.BlockSpec(memory_space=pl.ANY),
        scratch_shapes=[pltpu.VMEM((MOE_BLOCK, dm), F32), pltpu.SemaphoreType.DMA(()),
                        pltpu.SemaphoreType.DMA(())],
    )
    return pl.pallas_call(
        _scatter_kernel,
        grid_spec=grid_spec,
        out_shape=jax.ShapeDtypeStruct((n_slots, dm), F32),
        compiler_params=_params(("arbitrary", "arbitrary"), has_side_effects=True),
        name="moe_scatter",
    )(partial, slot, h2.reshape(bsz * rows, dm))


def _expert_kernel(be_ref, nu_ref, x_ref, w1_ref, w3_ref, w2_ref, y_ref, w1b, w3b, w2b):
    i = pl.program_id(0)

    @pl.when(i < nu_ref[0])
    def _():
        @pl.when((i == 0) | (be_ref[i] != be_ref[jnp.maximum(i - 1, 0)]))
        def _():
            w1b[...] = w1_ref[...].astype(BF16)
            w3b[...] = w3_ref[...].astype(BF16)
            w2b[...] = w2_ref[...].astype(BF16)

        xb = x_ref[...].astype(BF16)
        u = _dot(xb, w1b[...])
        hmid = (u * _sigmoid(u)) * _dot(xb, w3b[...])
        y_ref[...] = _dot(hmid.astype(BF16), w2b[...])

    @pl.when(i >= nu_ref[0])
    def _():
        y_ref[...] = jnp.zeros_like(y_ref)


def _experts(block_e, n_used, x_disp, w1, w3, w2, layer):
    n_slots, dm = x_disp.shape
    n_blocks = n_slots // MOE_BLOCK
    wspec = lambda shape: pl.BlockSpec((None, None) + shape, lambda i, be, nu: (layer, be[i], 0, 0))
    grid_spec = pltpu.PrefetchScalarGridSpec(
        num_scalar_prefetch=2,
        grid=(n_blocks,),
        in_specs=[pl.BlockSpec((MOE_BLOCK, dm), lambda i, be, nu: (jnp.minimum(i, nu[0] - 1), 0)),
                  wspec((dm, D_EXPERT)), wspec((dm, D_EXPERT)), wspec((D_EXPERT, dm))],
        out_specs=pl.BlockSpec((MOE_BLOCK, dm), lambda i, be, nu: (i, 0)),
        scratch_shapes=[pltpu.VMEM((dm, D_EXPERT), BF16), pltpu.VMEM((dm, D_EXPERT), BF16),
                        pltpu.VMEM((D_EXPERT, dm), BF16)],
    )
    return pl.pallas_call(
        _expert_kernel,
        grid_spec=grid_spec,
        out_shape=jax.ShapeDtypeStruct((n_slots, dm), F32),
        compiler_params=_params(("arbitrary",)),
        name="moe_experts",
    )(block_e, n_used, x_disp, w1, w3, w2)


def _combine_kernel(slot_ref, x_ref, mod_ref, rt_ref, yd_ref, fg_ref, o_ref, ybuf, sem, *, final):
    tm = x_ref.shape[0]
    _row_copies(tm, lambda r, k: pltpu.make_async_copy(
        yd_ref.at[pl.ds(slot_ref[0, k * tm + r], 1), :], ybuf.at[k, pl.ds(r, 1), :], sem))
    for k in range(2):
        pltpu.make_async_copy(yd_ref.at[pl.ds(0, tm), :], ybuf.at[k], sem).wait()
    rt = rt_ref[...]
    y = rt[:, 2:3] * ybuf[0] + rt[:, 3:4] * ybuf[1]
    xn = x_ref[...] + mod_ref[5:6, :] * y
    o_ref[...] = _rms(xn, fg_ref[...]) if final else xn


def _combine(slot, x, mod, route, y_disp, fg, final, tm, ctx_tiles, row0_tiles):
    bsz, rows, dm = x.shape
    nt = rows // tm
    return pl.pallas_call(
        functools.partial(_combine_kernel, final=final),
        grid=(bsz, nt),
        in_specs=[pl.BlockSpec((None, 1, 2 * tm), lambda b, t: (b * nt + t, 0, 0), memory_space=pltpu.SMEM),
                  pl.BlockSpec((None, tm, dm), lambda b, t: (b, t, 0)),
                  pl.BlockSpec((None, None, 6, dm),
                               lambda b, t: (b, jnp.where(row0_tiles + t >= ctx_tiles, 1, 0), 0, 0)),
                  pl.BlockSpec((None, tm, LANES), lambda b, t: (b, t, 0)),
                  pl.BlockSpec(memory_space=pl.ANY),
                  _const_spec((1, dm))],
        out_specs=pl.BlockSpec((None, tm, dm), lambda b, t: (b, t, 0)),
        out_shape=jax.ShapeDtypeStruct((bsz, rows, dm), F32),
        scratch_shapes=[pltpu.VMEM((2, tm, dm), F32), pltpu.SemaphoreType.DMA(())],
        compiler_params=_params(("arbitrary", "arbitrary")),
        name="moe_combine",
    )(slot, x, mod, route, y_disp, fg)


def _rope_tables(ctx, t_lat):
    rows = t_lat // GRID_W
    row = jnp.repeat(jnp.arange(rows, dtype=F32), GRID_W)
    col = jnp.tile(jnp.arange(GRID_W, dtype=F32), rows)
    inv = ROPE_BASE ** (-jnp.arange(ROPE_PAIRS, dtype=F32) / ROPE_PAIRS)
    ang_r = row[:, None] * inv
    ang_c = col[:, None] * inv
    cos64 = jnp.concatenate([jnp.cos(ang_r), jnp.cos(ang_r), jnp.cos(ang_c), jnp.cos(ang_c)], axis=1)
    sin64 = jnp.concatenate([-jnp.sin(ang_r), jnp.sin(ang_r), -jnp.sin(ang_c), jnp.sin(ang_c)], axis=1)
    cos = jnp.tile(cos64, (1, MIX_W // 64))
    sin = jnp.tile(sin64, (1, MIX_W // 64))
    cos = jnp.concatenate([jnp.ones((ctx, MIX_W), F32), cos], axis=0)
    sin = jnp.concatenate([jnp.zeros((ctx, MIX_W), F32), sin], axis=0)
    return cos, sin


def _pack_w_in(w):
    dm = w.shape[0]
    a_end = PA_W
    b_end = a_end + PB_W
    g_end = b_end + 4 * HEADS
    pad = jnp.zeros((dm, PG_W - 4 * HEADS), w.dtype)
    return jnp.concatenate([w[:, :b_end], w[:, b_end:g_end], pad, w[:, g_end:]], axis=1).astype(BF16)


def kernel(x, c, ctx, c_ctx, ada_w, ada_b, norm1_g, norm2_g, w_in, mlstm_conv_w, mlstm_conv_b, mlstm_gate_b,
           hgrn_lb_raw, hgrn_norm_g, mlstm_norm_g, diff_norm_g, diff_lambda, w_branch, w_gate, b_gate, w_out,
           router_g_w, router_g_b, router_e_w, router_e_b, moe_w1, moe_w3, moe_w2, final_g):
    bsz, t_lat, dm = x.shape
    t_ctx = ctx.shape[1]
    depth = ada_w.shape[0]
    tt = t_ctx + t_lat
    tm = min(256, t_ctx)
    hgrn_chunk = min(128, t_ctx)
    mlstm_chunk = min(256, t_ctx)
    assert t_ctx % tm == 0 and t_lat % tm == 0 and t_lat % GRID_W == 0
    assert t_ctx % hgrn_chunk == 0 and t_ctx % mlstm_chunk == 0
    ctx_tiles = t_ctx // tm

    n_rows = -(-(bsz + 1) // 8) * 8
    cc = jnp.zeros((n_rows, dm), F32).at[:bsz].set(c).at[bsz].set(c_ctx)
    mods = _ada_mod(cc, ada_w, ada_b).reshape(depth, n_rows, 6, dm)

    cos, sin = _rope_tables(t_ctx, t_lat)
    lb_cum = jnp.cumsum(jax.nn.softmax(hgrn_lb_raw.astype(F32), axis=0), axis=0)
    lower = lb_cum - lb_cum[0]

    xc = jnp.concatenate([ctx, x], axis=1)
    out = None
    for l in range(depth):
        with_ctx = l < depth - 1
        lam_init = 0.8 - 0.6 * math.exp(-0.3 * l)
        mod = jnp.stack([jnp.broadcast_to(mods[l, bsz], (bsz, 6, dm)), mods[l, :bsz]], axis=1)
        n1 = norm1_g[l].reshape(1, dm)
        n2 = norm2_g[l].reshape(1, dm)

        lb = lower[l]
        pad = jnp.zeros((2, 6, MIX_W), F32)
        lbc = jnp.concatenate([jnp.log(lb)[:, None], jnp.log1p(-lb)[:, None], pad], axis=1)
        pa, pb, pg, qc, kc, vc = _norm_proj(xc, mod, n1, _pack_w_in(w_in[l]), cos, sin, lbc, tm, ctx_tiles)

        oaf, oab = _hgrn(pa, hgrn_chunk, t_ctx // hgrn_chunk)

        qk = _conv_silu(pb, mlstm_conv_w[l], mlstm_conv_b[l], t_ctx, 256)
        gbias = jnp.zeros((1, PG_W), F32).at[0, :4 * HEADS].set(mlstm_gate_b[l].reshape(-1))
        obf, obb = _mlstm(qk, pb, pg, gbias, mlstm_chunk, t_ctx // mlstm_chunk)

        lam_vec = diff_lambda[l].astype(F32)
        oc_lat = _attn_lat(lam_vec, qc, kc, vc, lam_init, tm, t_ctx)
        oc_ctx = _attn_ctx(lam_vec, qc, kc, vc, lam_init, tm, t_ctx) if with_ctx else oc_lat

        hg = jnp.stack([hgrn_norm_g[l], mlstm_norm_g[l], diff_norm_g[l]])
        rw = jnp.zeros((LANES, dm), F32).at[:N_GROUPS].set(router_g_w[l].T)
        rw = rw.at[N_GROUPS:N_GROUPS + N_EXPERTS].set(router_e_w[l].T)
        rb = jnp.zeros((LANES, 1), F32).at[:N_GROUPS, 0].set(router_g_b[l])
        rb = rb.at[N_GROUPS:N_GROUPS + N_EXPERTS, 0].set(router_e_b[l])
        row0_tiles = 0 if with_ctx else ctx_tiles
        xn, h2, route, route_rows, counts = _merge(xc, mod, n1, n2, oaf, oab, pa, obf, obb, pb, oc_ctx, oc_lat, hg,
                                                   w_gate[l].astype(BF16), b_gate[l].reshape(1, 3 * dm),
                                                   w_branch[l].astype(BF16), w_out[l].astype(BF16), rw, rb,
                                                   lam_init, tm, ctx_tiles, row0_tiles)

        slot, block_e, n_used, partial = _slot_tables(route_rows, counts)
        x_disp = _scatter(partial, slot, h2, partial.shape[0] * MOE_BLOCK, tm)
        y_disp = _experts(block_e, n_used, x_disp, moe_w1, moe_w3, moe_w2, l)
        final = l == depth - 1
        xc = _combine(slot, xn, mod, route, y_disp, final_g.reshape(1, dm), final, tm, ctx_tiles, row0_tiles)
        out = xc
    return out
```

```python
import functools
import math

import numpy as np
import jax
import jax.numpy as jnp
from jax import lax
from jax.experimental import pallas as pl
from jax.experimental.pallas import tpu as pltpu

F32 = jnp.float32
BF16 = jnp.bfloat16

EPS = 1e-6
NEG_BIG = -1e30
HEADS = 4
HEAD_W = 128
MIX_W = HEADS * HEAD_W
GRID_W = 64
ROPE_BASE = 10000.0
ROPE_PAIRS = 16
B_CONV = 3
N_GROUPS = 4
EXPERTS_PER_GROUP = 8
N_EXPERTS = N_GROUPS * EXPERTS_PER_GROUP
D_EXPERT = 512
MOE_BLOCK = 256
LANES = 128
VMEM_LIMIT = 50 * 1024 * 1024

PA_W = 5 * MIX_W
PB_W = 4 * MIX_W
PG_W = LANES
PC_W = 3 * MIX_W
PROJ_W = PA_W + PB_W + PG_W + PC_W


def _params(sem, **kw):
    return pltpu.CompilerParams(dimension_semantics=sem, vmem_limit_bytes=VMEM_LIMIT, **kw)


def _const_spec(shape):
    nd = len(shape)
    return pl.BlockSpec(shape, lambda *_: (0,) * nd, pipeline_mode=pl.Buffered(1))


def _dot(a, b):
    return jnp.dot(a, b, preferred_element_type=F32)


def _dot_nt(a, b):
    return lax.dot_general(a, b, (((1,), (1,)), ((), ())), preferred_element_type=F32)


def _dot_tn(a, b):
    return lax.dot_general(a, b, (((0,), (0,)), ((), ())), preferred_element_type=F32)


def _dot_sel(m_bf16, x):
    hi = x.astype(BF16)
    r1 = x - hi.astype(F32)
    mid = r1.astype(BF16)
    lo = (r1 - mid.astype(F32)).astype(BF16)
    return _dot(m_bf16, hi) + _dot(m_bf16, mid) + _dot(m_bf16, lo)


def _sigmoid(x):
    return 1.0 / (1.0 + jnp.exp(-x))


def _log_sigmoid(x):
    return jnp.minimum(x, 0.0) - jnp.log1p(jnp.exp(-jnp.abs(x)))


def _rms(x, g):
    return x * lax.rsqrt(jnp.mean(x * x, axis=-1, keepdims=True) + EPS) * g


def _ada_kernel(c_ref, w_ref, b_ref, o_ref):
    c = c_ref[...]
    s = c * _sigmoid(c)
    o_ref[...] = jnp.dot(s, w_ref[...], preferred_element_type=F32,
                         precision=lax.Precision.HIGHEST) + b_ref[...]


def _ada_mod(cc, ada_w, ada_b):
    depth, dm, six = ada_w.shape
    rows = cc.shape[0]
    tn = dm
    return pl.pallas_call(
        _ada_kernel,
        grid=(depth, six // tn),
        in_specs=[pl.BlockSpec((rows, dm), lambda l, n: (0, 0)),
                  pl.BlockSpec((None, dm, tn), lambda l, n: (l, 0, n)),
                  pl.BlockSpec((None, 1, tn), lambda l, n: (l, 0, n))],
        out_specs=pl.BlockSpec((None, rows, tn), lambda l, n: (l, 0, n)),
        out_shape=jax.ShapeDtypeStruct((depth, rows, six), F32),
        compiler_params=_params(("parallel", "parallel")),
        name="ada_mod",
    )(cc, ada_w, ada_b.reshape(depth, 1, six))


def _rope(x, cos, sin):
    n = x.shape[-1]
    lane = lax.broadcasted_iota(jnp.int32, (1, n), 1)
    first = (lane // ROPE_PAIRS) % 2 == 0
    partner = jnp.where(first, pltpu.roll(x, n - ROPE_PAIRS, 1), pltpu.roll(x, ROPE_PAIRS, 1))
    return x * cos + partner * sin


def _hgrn_log_forget(z, lbc):
    lsig = jnp.minimum(z, 0.0) - jnp.log(1.0 + jnp.exp(-jnp.abs(z)))
    a = lbc[0:1]
    bb = lbc[1:2] + lsig
    return jnp.maximum(a, bb) + jnp.log(1.0 + jnp.exp(-jnp.abs(a - bb)))


def _with_gathered_rows(slot_ref, slot_next_ref, yd_ref, ybuf, sems, body):
    i = pl.program_id(0)
    n = pl.num_programs(0)
    tm = ybuf.shape[2]

    def start(table_ref, b):
        _row_copies(tm, lambda r, k: pltpu.make_async_copy(
            yd_ref.at[pl.ds(table_ref[0, k * tm + r], 1), :], ybuf.at[b, k, pl.ds(r, 1), :], sems.at[b]))

    def wait(b):
        for k in range(2):
            pltpu.make_async_copy(yd_ref.at[pl.ds(0, tm), :], ybuf.at[b, k], sems.at[b]).wait()

    @pl.when(i == 0)
    def _():
        start(slot_ref, 0)

    for cur in range(2):
        @pl.when(i % 2 == cur)
        def _():
            wait(cur)
            start(slot_next_ref, 1 - cur)
            body(ybuf[cur, 0], ybuf[cur, 1])

            @pl.when(i == n - 1)
            def _():
                wait(1 - cur)


def _moe_residual(x_ref, mod_ref, rt_ref, y0, y1):
    rt = rt_ref[...]
    return x_ref[...] + mod_ref[5:6, :] * (rt[:, 2:3] * y0 + rt[:, 3:4] * y1)


def _norm_proj_fused_kernel(slot_ref, slot_next_ref, xp_ref, modp_ref, rt_ref, yd_ref,
                            mod_ref, g_ref, w_ref, cos_ref, sin_ref, lbc_ref,
                            xo_ref, pa_ref, pb_ref, pg_ref, qc_ref, kc_ref, vc_ref, ybuf, sems):
    def body(y0, y1):
        x = _moe_residual(xp_ref, modp_ref, rt_ref, y0, y1)
        xo_ref[...] = x
        _project(x, mod_ref, g_ref, w_ref, cos_ref, sin_ref, lbc_ref,
                 pa_ref, pb_ref, pg_ref, qc_ref, kc_ref, vc_ref)

    _with_gathered_rows(slot_ref, slot_next_ref, yd_ref, ybuf, sems, body)


def _norm_proj_kernel(x_ref, mod_ref, g_ref, w_ref, cos_ref, sin_ref, lbc_ref,
                      pa_ref, pb_ref, pg_ref, qc_ref, kc_ref, vc_ref):
    _project(x_ref[...], mod_ref, g_ref, w_ref, cos_ref, sin_ref, lbc_ref,
             pa_ref, pb_ref, pg_ref, qc_ref, kc_ref, vc_ref)


def _project(x, mod_ref, g_ref, w_ref, cos_ref, sin_ref, lbc_ref, pa_ref, pb_ref, pg_ref, qc_ref, kc_ref, vc_ref):
    mod = mod_ref[...]
    h = _rms(x, g_ref[...]) * (1.0 + mod[1:2]) + mod[0:1]
    hb = h.astype(BF16)
    pa_ref[:, 0:3 * MIX_W] = _dot(hb, w_ref[:, 0:3 * MIX_W])
    for d in range(2):
        cs = slice((3 + d) * MIX_W, (4 + d) * MIX_W)
        pa_ref[:, cs] = _hgrn_log_forget(_dot(hb, w_ref[:, cs]), lbc_ref[d])
    pb_ref[...] = _dot(hb, w_ref[:, PA_W:PA_W + PB_W])
    pg_ref[...] = _dot(hb, w_ref[:, PA_W + PB_W:PA_W + PB_W + PG_W])
    c0 = PA_W + PB_W + PG_W
    cos = cos_ref[...]
    sin = sin_ref[...]
    q = _dot(hb, w_ref[:, c0:c0 + MIX_W])
    qc_ref[...] = (_rope(q, cos, sin) * (64.0 ** -0.5 * math.log2(math.e))).astype(BF16)
    k = _dot(hb, w_ref[:, c0 + MIX_W:c0 + 2 * MIX_W])
    kc_ref[...] = _rope(k, cos, sin).astype(BF16)
    vc_ref[...] = _dot(hb, w_ref[:, c0 + 2 * MIX_W:c0 + 3 * MIX_W]).astype(BF16)


def _norm_proj_fused(moe, mod, g, w, cos, sin, lbc, tm, ctx_tiles):
    slot, xp, modp, route, y_disp = moe
    bsz, tt, dm = xp.shape
    nt = tt // tm
    n = bsz * nt
    row = lambda i: (i // nt, i % nt, 0)
    mod_row = lambda i: (i // nt, jnp.where(i % nt >= ctx_tiles, 1, 0), 0, 0)
    tab = lambda i: (i % nt, 0)
    outs = [(dm, F32), (PA_W, F32), (PB_W, F32), (PG_W, F32), (MIX_W, BF16), (MIX_W, BF16), (MIX_W, BF16)]
    return pl.pallas_call(
        _norm_proj_fused_kernel,
        grid=(n,),
        in_specs=[pl.BlockSpec((None, 1, 2 * tm), lambda i: (i, 0, 0), memory_space=pltpu.SMEM),
                  pl.BlockSpec((None, 1, 2 * tm), lambda i: (jnp.minimum(i + 1, n - 1), 0, 0),
                               memory_space=pltpu.SMEM),
                  pl.BlockSpec((None, tm, dm), row),
                  pl.BlockSpec((None, None, 6, dm), mod_row),
                  pl.BlockSpec((None, tm, LANES), row),
                  pl.BlockSpec(memory_space=pl.ANY),
                  pl.BlockSpec((None, None, 6, dm), mod_row),
                  _const_spec((1, dm)),
                  _const_spec((dm, PROJ_W)),
                  pl.BlockSpec((tm, MIX_W), tab),
                  pl.BlockSpec((tm, MIX_W), tab),
                  _const_spec(lbc.shape)],
        out_specs=[pl.BlockSpec((None, tm, wd), row) for wd, _ in outs],
        out_shape=[jax.ShapeDtypeStruct((bsz, tt, wd), dt) for wd, dt in outs],
        scratch_shapes=[pltpu.VMEM((2, 2, tm, dm), F32), pltpu.SemaphoreType.DMA((2,))],
        compiler_params=_params(("arbitrary",)),
        name="combine_norm_proj",
    )(slot, slot, xp, modp, route, y_disp, mod, g, w, cos, sin, lbc)


def _norm_proj(x, mod, g, w, cos, sin, lbc, tm, ctx_tiles):
    bsz, tt, dm = x.shape
    nt = tt // tm
    row = lambda b, t: (b, t, 0)
    outs = [(PA_W, F32), (PB_W, F32), (PG_W, F32), (MIX_W, BF16), (MIX_W, BF16), (MIX_W, BF16)]
    return pl.pallas_call(
        _norm_proj_kernel,
        grid=(bsz, nt),
        in_specs=[pl.BlockSpec((None, tm, dm), row),
                  pl.BlockSpec((None, None, 6, dm), lambda b, t: (b, jnp.where(t >= ctx_tiles, 1, 0), 0, 0)),
                  _const_spec((1, dm)),
                  _const_spec((dm, PROJ_W)),
                  pl.BlockSpec((tm, MIX_W), lambda b, t: (t, 0)),
                  pl.BlockSpec((tm, MIX_W), lambda b, t: (t, 0)),
                  _const_spec(lbc.shape)],
        out_specs=[pl.BlockSpec((None, tm, wd), row) for wd, _ in outs],
        out_shape=[jax.ShapeDtypeStruct((bsz, tt, wd), dt) for wd, dt in outs],
        compiler_params=_params(("parallel", "parallel")),
        name="norm_proj",
    )(x, mod, g, w, cos, sin, lbc)


def _conv_kernel(x_ref, w_ref, b_ref, o_ref, *, ctx, q_blocks):
    x = x_ref[...]
    tt = x.shape[0]
    row = lax.broadcasted_iota(jnp.int32, (tt, 1), 0)
    prev = jnp.where((row == 0) | (row == ctx), 0.0, pltpu.roll(x, 1, 0))
    nxt = jnp.where((row == ctx - 1) | (row == tt - 1), 0.0, pltpu.roll(x, tt - 1, 0))
    w = w_ref[...]
    y = b_ref[...] + prev * w[0:1] + x * w[1:2] + nxt * w[2:3]
    y = y * _sigmoid(y)
    scale = jnp.where(pl.program_id(1) < q_blocks, HEAD_W ** -0.5, 1.0)
    o_ref[...] = y * scale


def _conv_silu(pb, conv_w, conv_b, ctx, cb):
    bsz, tt, _ = pb.shape
    width = 2 * MIX_W
    return pl.pallas_call(
        functools.partial(_conv_kernel, ctx=ctx, q_blocks=MIX_W // cb),
        grid=(bsz, width // cb),
        in_specs=[pl.BlockSpec((None, tt, cb), lambda b, j: (b, 0, j)),
                  pl.BlockSpec((B_CONV, cb), lambda b, j: (0, j)),
                  pl.BlockSpec((1, cb), lambda b, j: (0, j))],
        out_specs=pl.BlockSpec((None, tt, cb), lambda b, j: (b, 0, j)),
        out_shape=jax.ShapeDtypeStruct((bsz, tt, width), F32),
        compiler_params=_params(("parallel", "parallel")),
        name="mlstm_conv",
    )(pb, conv_w, conv_b.reshape(1, width))


def _chunk_consts(L):
    idx = np.arange(L)
    t, u = idx[:, None], idx[None, :]
    cum = (u <= t).astype(np.float32)
    nlev = int(round(math.log2(L)))
    lvl = np.full((L, L), -1, np.int32)
    lvl[idx, idx] = nlev
    for i in range(nlev):
        h = L >> (i + 1)
        same = (t // (2 * h)) == (u // (2 * h))
        lvl[same & (t % (2 * h) >= h) & (u % (2 * h) < h)] = i
    sel = jnp.asarray(np.stack([cum, cum[::-1, ::-1]]), BF16)
    lvls = jnp.asarray(np.stack([lvl, lvl[::-1, ::-1]]))
    return sel, lvls, nlev


def _midpoint_rows(b, h, d):
    L, w = b.shape
    two = 2 * h
    pos = h - 1 if d == 0 else h
    if two % 8 == 0:
        r = b.reshape(L // two, two, w)[:, pos:pos + 1, :]
        return jnp.broadcast_to(r, (L // two, two, w)).reshape(L, w)
    phase = lax.broadcasted_iota(jnp.int32, (L, 1), 0) % two
    ref = b
    for off in range(pos - two + 1, pos + 1):
        if off != 0:
            ref = jnp.where(phase == pos - off, pltpu.roll(b, (L - off) % L, 0), ref)
    return ref


def _scan_blocks(j, nc_ctx, nc):
    jb = jnp.where(j < nc_ctx, nc_ctx - 1 - j, nc - 1 + nc_ctx - j)
    return j, jb


def _hgrn_kernel(qf_ref, vf_ref, ff_ref, qb_ref, vb_ref, fb_ref, sel_ref, lvl_ref,
                 of_ref, ob_ref, st_ref, *, L, nlev):
    @pl.when(pl.program_id(1) == 0)
    def _():
        st_ref[...] = jnp.zeros_like(st_ref)

    row = lax.broadcasted_iota(jnp.int32, (L, 1), 0)
    dirs = ((qf_ref, vf_ref, ff_ref, of_ref), (qb_ref, vb_ref, fb_ref, ob_ref))
    for d, (q_ref, v_ref, f_ref, o_ref) in enumerate(dirs):
        lf = f_ref[...]
        kk = 1.0 - jnp.exp(lf)
        q_all = q_ref[...]
        b_in = _dot_sel(sel_ref[d], lf)
        lvl = lvl_ref[d]
        last = L - 1 if d == 0 else 0
        b_end = b_in[last:last + 1]
        heads = [slice(hd * HEAD_W, (hd + 1) * HEAD_W) for hd in range(HEADS)]
        qb = q_all.astype(BF16)
        kb = kk.astype(BF16)
        on_diag = lvl == nlev
        scores = [jnp.where(on_diag, _dot_nt(qb[:, cs], kb[:, cs]), 0.0) for cs in heads]
        for i in range(nlev):
            h = L >> (i + 1)
            ei = jnp.exp(-jnp.abs(b_in - _midpoint_rows(b_in, h, d)))
            is_query = (row % (2 * h) >= h) if d == 0 else (row % (2 * h) < h)
            xe = (jnp.where(is_query, q_all, kk) * ei).astype(BF16)
            at_level = lvl == i
            for hd, cs in enumerate(heads):
                scores[hd] = scores[hd] + jnp.where(at_level, _dot_nt(xe[:, cs], xe[:, cs]), 0.0)
        q_in = (q_all * jnp.exp(b_in)).astype(BF16)
        k_out = (kk * jnp.exp(b_end - b_in)).astype(BF16)
        decay = jnp.exp(b_end)
        for hd, cs in enumerate(heads):
            vb = v_ref[:, cs].astype(BF16)
            st = st_ref[d, hd]
            o_ref[:, cs] = _dot(scores[hd].astype(BF16), vb) + _dot_nt(q_in[:, cs], st.astype(BF16))
            st_ref[d, hd] = st * decay[:, cs] + _dot_tn(vb, k_out[:, cs])


def _hgrn(pa, L, nc_ctx):
    bsz, tt, _ = pa.shape
    nc = tt // L
    sel, lvl, nlev = _chunk_consts(L)

    def spec(col, which):
        return pl.BlockSpec((None, L, MIX_W), lambda b, j: (b, _scan_blocks(j, nc_ctx, nc)[which], col))

    return pl.pallas_call(
        functools.partial(_hgrn_kernel, L=L, nlev=nlev),
        grid=(bsz, nc),
        in_specs=[spec(0, 0), spec(1, 0), spec(3, 0), spec(0, 1), spec(1, 1), spec(4, 1),
                  _const_spec(sel.shape), _const_spec(lvl.shape)],
        out_specs=[spec(0, 0), spec(0, 1)],
        out_shape=[jax.ShapeDtypeStruct((bsz, tt, MIX_W), F32)] * 2,
        scratch_shapes=[pltpu.VMEM((2, HEADS, HEAD_W, HEAD_W), F32)],
        compiler_params=_params(("parallel", "arbitrary")),
        name="hgrn2_scan",
    )(pa, pa, pa, pa, pa, pa, sel, lvl)


def _mlstm_kernel(qf_ref, kf_ref, vf_ref, gf_ref, qb_ref, kb_ref, vb_ref, gb_ref, gbias_ref,
                  sel_ref, lvl_ref, of_ref, ob_ref, c_ref, m_ref, *, L, nlev):
    @pl.when(pl.program_id(1) == 0)
    def _():
        c_ref[...] = jnp.zeros_like(c_ref)
        m_ref[...] = jnp.zeros_like(m_ref)

    lane = lax.broadcasted_iota(jnp.int32, (1, LANES), 1)
    ones_col = jnp.where(lane == 0, 1.0, 0.0).astype(BF16)
    dirs = ((qf_ref, kf_ref, vf_ref, gf_ref, of_ref), (qb_ref, kb_ref, vb_ref, gb_ref, ob_ref))
    for d, (q_ref, k_ref, v_ref, g_ref, o_ref) in enumerate(dirs):
        g = g_ref[...] + gbias_ref[...]
        lf = jnp.where((lane >= 2 * HEADS) & (lane < 4 * HEADS), _log_sigmoid(g), 0.0)
        bcum = _dot_sel(sel_ref[d, 0:L], lf)
        causal = lvl_ref[d] >= 0
        row_terms = (g - pltpu.roll(bcum, LANES - 2 * HEADS, 1)).T
        last = L - 1 if d == 0 else 0
        for hd in range(HEADS):
            cs = slice(hd * HEAD_W, (hd + 1) * HEAD_W)
            li_lane = d * HEADS + hd
            lf_lane = 2 * HEADS + d * HEADS + hd
            b = bcum[:, lf_lane:lf_lane + 1]
            row_term = row_terms[li_lane:li_lane + 1, :]
            b_end = b[last:last + 1]
            r = d * HEADS + hd
            m_prev = m_ref[r:r + 1, 0:1]
            q = q_ref[:, cs].astype(BF16)
            k = k_ref[:, cs]
            v_aug = jnp.concatenate([v_ref[:, cs].astype(BF16), jnp.broadcast_to(ones_col, (L, LANES))], axis=1)
            c_prev = c_ref[r]
            w_end = b_end + row_term
            m_new = jnp.maximum(b_end + m_prev, jnp.max(w_end, axis=-1, keepdims=True))
            e_end = jnp.exp(w_end - m_new)
            keep = jnp.exp(b_end + m_prev - m_new)
            c_ref[r] = keep * c_prev + _dot((k.T * e_end).astype(BF16), v_aug)
            m_ref[r:r + 1, :] = jnp.broadcast_to(m_new, (1, LANES))
            logw = jnp.where(causal, b + row_term, NEG_BIG)
            w_state = b + m_prev
            m_t = jnp.maximum(jnp.max(logw, axis=-1, keepdims=True), w_state)
            s = _dot_nt(q, k.astype(BF16)) * jnp.exp(logw - m_t)
            a_state = jnp.exp(w_state - m_t)
            num = _dot(s.astype(BF16), v_aug) + a_state * _dot(q, c_prev.astype(BF16))
            den = num[:, HEAD_W:HEAD_W + 1]
            o_ref[:, cs] = num[:, 0:HEAD_W] / jnp.maximum(jnp.abs(den), jnp.exp(-m_t))


def _mlstm(qk, pb, pg, gbias, L, nc_ctx):
    bsz, tt, _ = pb.shape
    nc = tt // L
    sel, lvl, nlev = _chunk_consts(L)

    def spec(col, which, width=MIX_W):
        return pl.BlockSpec((None, L, width), lambda b, j: (b, _scan_blocks(j, nc_ctx, nc)[which], col))

    return pl.pallas_call(
        functools.partial(_mlstm_kernel, L=L, nlev=nlev),
        grid=(bsz, nc),
        in_specs=[spec(0, 0), spec(1, 0), spec(2, 0), spec(0, 0, PG_W),
                  spec(0, 1), spec(1, 1), spec(2, 1), spec(0, 1, PG_W),
                  _const_spec((1, PG_W)), _const_spec(sel.shape), _const_spec(lvl.shape)],
        out_specs=[spec(0, 0), spec(0, 1)],
        out_shape=[jax.ShapeDtypeStruct((bsz, tt, MIX_W), F32)] * 2,
        scratch_shapes=[pltpu.VMEM((2 * HEADS, HEAD_W, 2 * HEAD_W), F32),
                        pltpu.VMEM((2 * HEADS, LANES), F32)],
        compiler_params=_params(("parallel", "arbitrary")),
        name="mlstm_scan",
    )(qk, qk, pb, pg, qk, qk, pb, pg, gbias, sel, lvl)


def _diff_lambda(lam_ref, lam_init):
    lv = lam_ref[...]
    return (jnp.exp(jnp.sum(lv[0:1] * lv[1:2], axis=-1, keepdims=True))
            - jnp.exp(jnp.sum(lv[2:3] * lv[3:4], axis=-1, keepdims=True)) + lam_init)


def _map_scores(q, k):
    lane = lax.broadcasted_iota(jnp.int32, (1, HEAD_W), 1)
    zero = jnp.zeros_like(q)
    return (_dot_nt(jnp.where(lane < HEAD_W // 2, q, zero), k),
            _dot_nt(jnp.where(lane >= HEAD_W // 2, q, zero), k))


def _row_max(s):
    return jnp.max(s, axis=-1, keepdims=True)


def _diff_softmax_v(s1, m1, s2, m2, lam, v):
    def probs(s, m):
        p = jnp.exp2(s - m)
        return p / jnp.sum(p, axis=-1, keepdims=True)

    return _dot((probs(s1, m1) - lam * probs(s2, m2)).astype(BF16), v)


def _attn_ctx_kernel(lam_ref, q_ref, k_ref, v_ref, o_ref, *, lam_init):
    s1, s2 = _map_scores(q_ref[...], k_ref[...])
    o_ref[...] = _diff_softmax_v(s1, _row_max(s1), s2, _row_max(s2), _diff_lambda(lam_ref, lam_init), v_ref[...])


def _attn_ctx(lam_vec, qc, kc, vc, lam_init, tq, t_ctx):
    bsz = qc.shape[0]
    kv = pl.BlockSpec((None, t_ctx, HEAD_W), lambda b, h, i: (b, 0, h))
    return pl.pallas_call(
        functools.partial(_attn_ctx_kernel, lam_init=lam_init),
        grid=(bsz, HEADS, t_ctx // tq),
        in_specs=[_const_spec(lam_vec.shape),
                  pl.BlockSpec((None, tq, HEAD_W), lambda b, h, i: (b, i, h)), kv, kv],
        out_specs=pl.BlockSpec((None, tq, HEAD_W), lambda b, h, i: (b, i, h)),
        out_shape=jax.ShapeDtypeStruct((bsz, t_ctx, MIX_W), F32),
        compiler_params=_params(("parallel", "parallel", "arbitrary")),
        name="diff_attn_ctx",
    )(lam_vec, qc, kc, vc)


def _attn_lat_kernel(lam_ref, q_ref, k_ref, v_ref, o_ref, s_even, m_even, s_odd, m_odd, *, lam_init, n_tiles):
    j = pl.program_id(2)
    lam = _diff_lambda(lam_ref, lam_init)
    even, odd = (s_even, m_even), (s_odd, m_odd)

    def scores_into(bufs):
        s_buf, m_buf = bufs
        for i, s in enumerate(_map_scores(q_ref[...], k_ref[...])):
            s_buf[i] = s
            m_buf[i] = jnp.broadcast_to(_row_max(s), m_buf.shape[1:])

    def finish_from(bufs):
        s_buf, m_buf = bufs
        o_ref[...] = _diff_softmax_v(s_buf[0], m_buf[0][:, 0:1], s_buf[1], m_buf[1][:, 0:1], lam, v_ref[...])

    @pl.when(j == 0)
    def _():
        scores_into(even)

    middle = (j > 0) & (j < n_tiles)

    @pl.when(middle & (j % 2 == 1))
    def _():
        scores_into(odd)
        finish_from(even)

    @pl.when(middle & (j % 2 == 0))
    def _():
        scores_into(even)
        finish_from(odd)

    @pl.when(j == n_tiles)
    def _():
        finish_from(odd if n_tiles % 2 == 0 else even)


def _attn_lat(lam_vec, qc, kc, vc, lam_init, tq, t_ctx):
    bsz, tt, _ = qc.shape
    n_tiles = (tt - t_ctx) // tq
    q0 = t_ctx // tq
    kv = pl.BlockSpec((None, tt, HEAD_W), lambda b, h, j: (b, 0, h))
    return pl.pallas_call(
        functools.partial(_attn_lat_kernel, lam_init=lam_init, n_tiles=n_tiles),
        grid=(bsz, HEADS, n_tiles + 1),
        in_specs=[_const_spec(lam_vec.shape),
                  pl.BlockSpec((None, tq, HEAD_W), lambda b, h, j: (b, q0 + jnp.minimum(j, n_tiles - 1), h)),
                  kv, kv],
        out_specs=pl.BlockSpec((None, tq, HEAD_W), lambda b, h, j: (b, jnp.maximum(j - 1, 0), h)),
        out_shape=jax.ShapeDtypeStruct((bsz, tt - t_ctx, MIX_W), F32),
        scratch_shapes=[pltpu.VMEM((2, tq, tt), F32), pltpu.VMEM((2, tq, LANES), F32),
                        pltpu.VMEM((2, tq, tt), F32), pltpu.VMEM((2, tq, LANES), F32)],
        compiler_params=_params(("parallel", "parallel", "arbitrary")),
        name="diff_attn_lat",
    )(lam_vec, qc, kc, vc)


def _head_norm(x, g):
    parts = []
    for hd in range(HEADS):
        xs = x[:, hd * HEAD_W:(hd + 1) * HEAD_W]
        parts.append(xs * lax.rsqrt(jnp.mean(xs * xs, axis=-1, keepdims=True) + EPS))
    return jnp.concatenate(parts, axis=1) * g


ROUTE_ROWS = 40


def _route(lg):
    n = lg.shape[1]
    row = lax.broadcasted_iota(jnp.int32, lg.shape, 0)
    neg = -jnp.inf
    big = ROUTE_ROWS
    gl = jnp.where(row < N_GROUPS, lg, neg)
    gmax = jnp.max(gl, axis=0, keepdims=True)
    gidx = jnp.min(jnp.where(gl == gmax, row, big), axis=0, keepdims=True)
    p_group = 1.0 / jnp.sum(jnp.exp(gl - gmax), axis=0, keepdims=True)
    lo = N_GROUPS + EXPERTS_PER_GROUP * gidx
    in_grp = (row >= lo) & (row < lo + EXPERTS_PER_GROUP)
    el = jnp.where(in_grp, lg, neg)
    pe = jnp.exp(el - jnp.max(el, axis=0, keepdims=True))
    pe = pe / jnp.sum(pe, axis=0, keepdims=True)
    pe = jnp.where(in_grp, pe, -1.0)
    v1 = jnp.max(pe, axis=0, keepdims=True)
    i1 = jnp.min(jnp.where(pe == v1, row, big), axis=0, keepdims=True)
    pe2 = jnp.where(row == i1, -1.0, pe)
    v2 = jnp.max(pe2, axis=0, keepdims=True)
    i2 = jnp.min(jnp.where(pe2 == v2, row, big), axis=0, keepdims=True)
    scale = p_group / (v1 + v2)
    erow = lax.broadcasted_iota(jnp.int32, (N_EXPERTS, n), 0) + N_GROUPS
    oh1 = erow == i1
    oh2 = erow == i2
    oh1f = jnp.where(oh1, 1.0, 0.0)
    oh2f = jnp.where(oh2, 1.0, 0.0)
    earlier = (lax.broadcasted_iota(jnp.int32, (n, n), 0) < lax.broadcasted_iota(jnp.int32, (n, n), 1))
    earlier = jnp.where(earlier, 1.0, 0.0).astype(BF16)
    tot1 = jnp.sum(oh1f, axis=1, keepdims=True)
    tot2 = jnp.sum(oh2f, axis=1, keepdims=True)
    rank1 = jnp.sum(jnp.where(oh1, _dot(oh1f.astype(BF16), earlier), 0.0), axis=0, keepdims=True)
    rank2 = jnp.sum(jnp.where(oh2, _dot(oh2f.astype(BF16), earlier) + tot1, 0.0), axis=0, keepdims=True)
    r8 = lax.broadcasted_iota(jnp.int32, (8, n), 0)
    rows = jnp.zeros((8, n), F32)
    for i, val in enumerate(((i1 - N_GROUPS).astype(F32), (i2 - N_GROUPS).astype(F32), v1 * scale, v2 * scale,
                             rank1, rank2)):
        rows = jnp.where(r8 == i, val, rows)
    return rows, tot1 + tot2


def _merge_kernel(x_ref, mod_ref, n1_ref, n2_ref, af_ref, ab_ref, ga_ref, bf_ref, bb_ref, gb_ref,
                  occ_ref, ocl_ref, hg_ref, wg_ref, bg_ref, wbr_ref, wo_ref, rw_ref, rb_ref,
                  xo_ref, h2_ref, rt_ref, rtt_ref, cnt_ref, *, lam_init, ctx_tiles, row0_tiles):
    x = x_ref[...]
    mod = mod_ref[...]
    dm = x.shape[-1]
    hb = (_rms(x, n1_ref[...]) * (1.0 + mod[1:2]) + mod[0:1]).astype(BF16)
    hg = hg_ref[...]
    ga = ga_ref[...]
    ya = _head_norm(af_ref[...] + ab_ref[...], hg[0:1]) * (ga * _sigmoid(ga))
    yb = _head_norm(bf_ref[...] + bb_ref[...], hg[1:2]) * _sigmoid(gb_ref[...])
    oc = jnp.where(row0_tiles + pl.program_id(1) < ctx_tiles, occ_ref[...], ocl_ref[...])
    yc = _head_norm(oc, hg[2:3]) * (1.0 - lam_init)
    y = jnp.zeros_like(x)
    for i, yi in enumerate((ya, yb, yc)):
        gate = _sigmoid(_dot(hb, wg_ref[:, i * dm:(i + 1) * dm]) + bg_ref[:, i * dm:(i + 1) * dm])
        y = y + gate * _dot(yi.astype(BF16), wbr_ref[i])
    xn = x + mod[2:3] * _dot(y.astype(BF16), wo_ref[...])
    xo_ref[...] = xn
    h2 = _rms(xn, n2_ref[...]) * (1.0 + mod[4:5]) + mod[3:4]
    h2_ref[...] = h2
    logits = lax.dot_general(rw_ref[...], h2, (((1,), (1,)), ((), ())), preferred_element_type=F32,
                             precision=lax.Precision.HIGHEST) + rb_ref[...]
    rows, counts = _route(logits[0:ROUTE_ROWS, :])
    rtt_ref[...] = rows
    cnt_ref[...] = jnp.broadcast_to(counts, cnt_ref.shape)
    tm = x.shape[0]
    rt_ref[...] = jnp.concatenate([rows, jnp.zeros((LANES - 8, tm), F32)], axis=0).T


def _merge(x, mod, n1, n2, oaf, oab, pa, obf, obb, pb, oc_ctx, oc_lat, hg, wg, bg, wbr, wo, rw, rb,
           lam_init, tm, ctx_tiles, row0_tiles):
    bsz, tt, dm = x.shape
    nt = tt // tm - row0_tiles
    rows = nt * tm
    src = lambda b, t: (b, row0_tiles + t, 0)

    def col(c):
        return pl.BlockSpec((None, tm, MIX_W), lambda b, t: (b, row0_tiles + t, c))

    dst = lambda b, t: (b, t, 0)
    outs = [(dm, F32), (dm, F32), (LANES, F32)]
    out_specs = [pl.BlockSpec((None, tm, wd), dst) for wd, _ in outs]
    out_shape = [jax.ShapeDtypeStruct((bsz, rows, wd), dt) for wd, dt in outs]
    out_specs.append(pl.BlockSpec((None, 8, tm), lambda b, t: (b * nt + t, 0, 0)))
    out_shape.append(jax.ShapeDtypeStruct((bsz * nt, 8, tm), F32))
    out_specs.append(pl.BlockSpec((None, N_EXPERTS, LANES), lambda b, t: (b * nt + t, 0, 0)))
    out_shape.append(jax.ShapeDtypeStruct((bsz * nt, N_EXPERTS, LANES), F32))
    return pl.pallas_call(
        functools.partial(_merge_kernel, lam_init=lam_init, ctx_tiles=ctx_tiles, row0_tiles=row0_tiles),
        grid=(bsz, nt),
        in_specs=[pl.BlockSpec((None, tm, dm), src),
                  pl.BlockSpec((None, None, 6, dm),
                               lambda b, t: (b, jnp.where(row0_tiles + t >= ctx_tiles, 1, 0), 0, 0)),
                  _const_spec((1, dm)), _const_spec((1, dm)),
                  col(0), col(0), col(2), col(0), col(0), col(3),
                  pl.BlockSpec((None, tm, MIX_W),
                               lambda b, t: (b, jnp.minimum(row0_tiles + t, max(ctx_tiles - 1, 0)), 0)),
                  pl.BlockSpec((None, tm, MIX_W),
                               lambda b, t: (b, jnp.maximum(row0_tiles + t - ctx_tiles, 0), 0)),
                  _const_spec((3, MIX_W)), _const_spec(wg.shape), _const_spec(bg.shape),
                  _const_spec(wbr.shape), _const_spec(wo.shape), _const_spec(rw.shape), _const_spec(rb.shape)],
        out_specs=out_specs,
        out_shape=out_shape,
        compiler_params=_params(("parallel", "parallel")),
        name="merge_route",
    )(x, mod, n1, n2, oaf, oab, pa, obf, obb, pb, oc_ctx, oc_lat, hg, wg, bg, wbr, wo, rw, rb)


def _slot_tables(route_rows, counts):
    tiles, _, tm = route_rows.shape
    cnt = counts[:, :, 0].astype(jnp.int32)
    tile_off = jnp.cumsum(cnt, axis=0) - cnt
    total = jnp.sum(cnt, axis=0)
    padded = (total + MOE_BLOCK - 1) // MOE_BLOCK * MOE_BLOCK
    p_end = jnp.cumsum(padded)
    base = (p_end - padded)[None, :] + tile_off
    e = route_rows[:, 0:2, :].astype(jnp.int32)
    rank = route_rows[:, 4:6, :].astype(jnp.int32)
    hit = e[:, :, :, None] == jnp.arange(N_EXPERTS, dtype=jnp.int32)
    slot = jnp.sum(jnp.where(hit, base[:, None, None, :], 0), axis=-1) + rank
    n_blocks = -(-(2 * tiles * tm) // MOE_BLOCK) + N_EXPERTS
    blk_start = jnp.arange(n_blocks, dtype=jnp.int32) * MOE_BLOCK
    block_e = jnp.minimum(jnp.sum(blk_start[:, None] >= p_end[None, :], axis=1), N_EXPERTS - 1).astype(jnp.int32)
    n_used = (p_end[-1] // MOE_BLOCK).astype(jnp.int32).reshape(1)
    filled = jnp.clip(((p_end - padded) + total)[block_e] - blk_start, 0, MOE_BLOCK)
    filled = jnp.where(blk_start < p_end[-1], filled, 0)
    partial = (filled < MOE_BLOCK).astype(jnp.int32)
    return slot.reshape(tiles, 1, 2 * tm), block_e, n_used, partial


def _row_copies(n_rows, make):
    for r in range(n_rows):
        for k in range(2):
            make(r, k).start()


def _scatter_kernel(partial_ref, slot_ref, h_ref, xd_ref, zeros, sem, zsem):
    tm = h_ref.shape[0]
    n_blocks = partial_ref.shape[0]

    @pl.when((pl.program_id(0) == 0) & (pl.program_id(1) == 0))
    def _():
        zeros[...] = jnp.zeros_like(zeros)

        def zero_block(blk):
            return pltpu.make_async_copy(zeros, xd_ref.at[pl.ds(blk * MOE_BLOCK, MOE_BLOCK), :], zsem)

        def start(blk, c):
            @pl.when(partial_ref[blk] != 0)
            def _():
                zero_block(blk).start()
            return c

        def wait(blk, c):
            @pl.when(partial_ref[blk] != 0)
            def _():
                zero_block(blk).wait()
            return c

        lax.fori_loop(0, n_blocks, start, 0)
        lax.fori_loop(0, n_blocks, wait, 0)

    _row_copies(tm, lambda r, k: pltpu.make_async_copy(
        h_ref.at[pl.ds(r, 1), :], xd_ref.at[pl.ds(slot_ref[0, k * tm + r], 1), :], sem))
    for _ in range(2):
        pltpu.make_async_copy(h_ref, xd_ref.at[pl.ds(0, tm), :], sem).wait()


def _scatter(partial, slot, h2, n_slots, tm):
    bsz, rows, dm = h2.shape
    nt = rows // tm
    grid_spec = pltpu.PrefetchScalarGridSpec(
        num_scalar_prefetch=1,
        grid=(bsz, nt),
        in_specs=[pl.BlockSpec((None, 1, 2 * tm), lambda b, t, p: (b * nt + t, 0, 0), memory_space=pltpu.SMEM),
                  pl.BlockSpec((tm, dm), lambda b, t, p: (b * nt + t, 0))],
        out_specs=pl.BlockSpec(memory_space=pl.ANY),
        scratch_shapes=[pltpu.VMEM((MOE_BLOCK, dm), F32), pltpu.SemaphoreType.DMA(()),
                        pltpu.SemaphoreType.DMA(())],
    )
    return pl.pallas_call(
        _scatter_kernel,
        grid_spec=grid_spec,
        out_shape=jax.ShapeDtypeStruct((n_slots, dm), F32),
        compiler_params=_params(("arbitrary", "arbitrary"), has_side_effects=True),
        name="moe_scatter",
    )(partial, slot, h2.reshape(bsz * rows, dm))


def _expert_kernel(be_ref, nu_ref, x_ref, w1_ref, w3_ref, w2_ref, y_ref, w1b, w3b, w2b):
    i = pl.program_id(0)

    @pl.when(i < nu_ref[0])
    def _():
        @pl.when((i == 0) | (be_ref[i] != be_ref[jnp.maximum(i - 1, 0)]))
        def _():
            w1b[...] = w1_ref[...].astype(BF16)
            w3b[...] = w3_ref[...].astype(BF16)
            w2b[...] = w2_ref[...].astype(BF16)

        xb = x_ref[...].astype(BF16)
        u = _dot(xb, w1b[...])
        hmid = (u * _sigmoid(u)) * _dot(xb, w3b[...])
        y_ref[...] = _dot(hmid.astype(BF16), w2b[...])

    @pl.when(i >= nu_ref[0])
    def _():
        y_ref[...] = jnp.zeros_like(y_ref)


def _experts(block_e, n_used, x_disp, w1, w3, w2, layer):
    n_slots, dm = x_disp.shape
    n_blocks = n_slots // MOE_BLOCK
    wspec = lambda shape: pl.BlockSpec((None, None) + shape, lambda i, be, nu: (layer, be[i], 0, 0))
    grid_spec = pltpu.PrefetchScalarGridSpec(
        num_scalar_prefetch=2,
        grid=(n_blocks,),
        in_specs=[pl.BlockSpec((MOE_BLOCK, dm), lambda i, be, nu: (jnp.minimum(i, nu[0] - 1), 0)),
                  wspec((dm, D_EXPERT)), wspec((dm, D_EXPERT)), wspec((D_EXPERT, dm))],
        out_specs=pl.BlockSpec((MOE_BLOCK, dm), lambda i, be, nu: (i, 0)),
        scratch_shapes=[pltpu.VMEM((dm, D_EXPERT), BF16), pltpu.VMEM((dm, D_EXPERT), BF16),
                        pltpu.VMEM((D_EXPERT, dm), BF16)],
    )
    return pl.pallas_call(
        _expert_kernel,
        grid_spec=grid_spec,
        out_shape=jax.ShapeDtypeStruct((n_slots, dm), F32),
        compiler_params=_params(("arbitrary",)),
        name="moe_experts",
    )(block_e, n_used, x_disp, w1, w3, w2)


def _combine_kernel(slot_ref, slot_next_ref, x_ref, mod_ref, rt_ref, yd_ref, fg_ref, o_ref, ybuf, sems):
    def body(y0, y1):
        o_ref[...] = _rms(_moe_residual(x_ref, mod_ref, rt_ref, y0, y1), fg_ref[...])

    _with_gathered_rows(slot_ref, slot_next_ref, yd_ref, ybuf, sems, body)


def _combine_final(slot, x, mod, route, y_disp, fg, tm, ctx_tiles, row0_tiles):
    bsz, rows, dm = x.shape
    nt = rows // tm
    n = bsz * nt
    row = lambda i: (i // nt, i % nt, 0)
    return pl.pallas_call(
        _combine_kernel,
        grid=(n,),
        in_specs=[pl.BlockSpec((None, 1, 2 * tm), lambda i: (i, 0, 0), memory_space=pltpu.SMEM),
                  pl.BlockSpec((None, 1, 2 * tm), lambda i: (jnp.minimum(i + 1, n - 1), 0, 0),
                               memory_space=pltpu.SMEM),
                  pl.BlockSpec((None, tm, dm), row),
                  pl.BlockSpec((None, None, 6, dm),
                               lambda i: (i // nt, jnp.where(row0_tiles + i % nt >= ctx_tiles, 1, 0), 0, 0)),
                  pl.BlockSpec((None, tm, LANES), row),
                  pl.BlockSpec(memory_space=pl.ANY),
                  _const_spec((1, dm))],
        out_specs=pl.BlockSpec((None, tm, dm), row),
        out_shape=jax.ShapeDtypeStruct((bsz, rows, dm), F32),
        scratch_shapes=[pltpu.VMEM((2, 2, tm, dm), F32), pltpu.SemaphoreType.DMA((2,))],
        compiler_params=_params(("arbitrary",)),
        name="moe_combine",
    )(slot, slot, x, mod, route, y_disp, fg)


def _rope_tables(ctx, t_lat):
    rows = t_lat // GRID_W
    row = jnp.repeat(jnp.arange(rows, dtype=F32), GRID_W)
    col = jnp.tile(jnp.arange(GRID_W, dtype=F32), rows)
    inv = ROPE_BASE ** (-jnp.arange(ROPE_PAIRS, dtype=F32) / ROPE_PAIRS)
    ang_r = row[:, None] * inv
    ang_c = col[:, None] * inv
    cos64 = jnp.concatenate([jnp.cos(ang_r), jnp.cos(ang_r), jnp.cos(ang_c), jnp.cos(ang_c)], axis=1)
    sin64 = jnp.concatenate([-jnp.sin(ang_r), jnp.sin(ang_r), -jnp.sin(ang_c), jnp.sin(ang_c)], axis=1)
    cos = jnp.tile(cos64, (1, MIX_W // 64))
    sin = jnp.tile(sin64, (1, MIX_W // 64))
    cos = jnp.concatenate([jnp.ones((ctx, MIX_W), F32), cos], axis=0)
    sin = jnp.concatenate([jnp.zeros((ctx, MIX_W), F32), sin], axis=0)
    return cos, sin


def _pack_w_in(w):
    dm = w.shape[0]
    a_end = PA_W
    b_end = a_end + PB_W
    g_end = b_end + 4 * HEADS
    pad = jnp.zeros((dm, PG_W - 4 * HEADS), w.dtype)
    return jnp.concatenate([w[:, :b_end], w[:, b_end:g_end], pad, w[:, g_end:]], axis=1).astype(BF16)


def kernel(x, c, ctx, c_ctx, ada_w, ada_b, norm1_g, norm2_g, w_in, mlstm_conv_w, mlstm_conv_b, mlstm_gate_b,
           hgrn_lb_raw, hgrn_norm_g, mlstm_norm_g, diff_norm_g, diff_lambda, w_branch, w_gate, b_gate, w_out,
           router_g_w, router_g_b, router_e_w, router_e_b, moe_w1, moe_w3, moe_w2, final_g):
    bsz, t_lat, dm = x.shape
    t_ctx = ctx.shape[1]
    depth = ada_w.shape[0]
    tt = t_ctx + t_lat
    tm = min(256, t_ctx)
    hgrn_chunk = min(128, t_ctx)
    mlstm_chunk = min(256, t_ctx)
    assert t_ctx % tm == 0 and t_lat % tm == 0 and t_lat % GRID_W == 0
    assert t_ctx % hgrn_chunk == 0 and t_ctx % mlstm_chunk == 0
    ctx_tiles = t_ctx // tm

    n_rows = -(-(bsz + 1) // 8) * 8
    cc = jnp.zeros((n_rows, dm), F32).at[:bsz].set(c).at[bsz].set(c_ctx)
    mods = _ada_mod(cc, ada_w, ada_b).reshape(depth, n_rows, 6, dm)

    cos, sin = _rope_tables(t_ctx, t_lat)
    lb_cum = jnp.cumsum(jax.nn.softmax(hgrn_lb_raw.astype(F32), axis=0), axis=0)
    lower = lb_cum - lb_cum[0]

    xc = jnp.concatenate([ctx, x], axis=1)
    moe = None
    for l in range(depth):
        with_ctx = l < depth - 1
        lam_init = 0.8 - 0.6 * math.exp(-0.3 * l)
        mod = jnp.stack([jnp.broadcast_to(mods[l, bsz], (bsz, 6, dm)), mods[l, :bsz]], axis=1)
        n1 = norm1_g[l].reshape(1, dm)
        n2 = norm2_g[l].reshape(1, dm)

        lb = lower[l]
        pad = jnp.zeros((2, 6, MIX_W), F32)
        lbc = jnp.concatenate([jnp.log(lb)[:, None], jnp.log1p(-lb)[:, None], pad], axis=1)
        if moe is None:
            pa, pb, pg, qc, kc, vc = _norm_proj(xc, mod, n1, _pack_w_in(w_in[l]), cos, sin, lbc, tm, ctx_tiles)
        else:
            xc, pa, pb, pg, qc, kc, vc = _norm_proj_fused(moe, mod, n1, _pack_w_in(w_in[l]), cos, sin, lbc,
                                                          tm, ctx_tiles)

        oaf, oab = _hgrn(pa, hgrn_chunk, t_ctx // hgrn_chunk)

        qk = _conv_silu(pb, mlstm_conv_w[l], mlstm_conv_b[l], t_ctx, 256)
        gbias = jnp.zeros((1, PG_W), F32).at[0, :4 * HEADS].set(mlstm_gate_b[l].reshape(-1))
        obf, obb = _mlstm(qk, pb, pg, gbias, mlstm_chunk, t_ctx // mlstm_chunk)

        lam_vec = diff_lambda[l].astype(F32)
        oc_lat = _attn_lat(lam_vec, qc, kc, vc, lam_init, tm, t_ctx)
        oc_ctx = _attn_ctx(lam_vec, qc, kc, vc, lam_init, tm, t_ctx) if with_ctx else oc_lat

        hg = jnp.stack([hgrn_norm_g[l], mlstm_norm_g[l], diff_norm_g[l]])
        rw = jnp.zeros((LANES, dm), F32).at[:N_GROUPS].set(router_g_w[l].T)
        rw = rw.at[N_GROUPS:N_GROUPS + N_EXPERTS].set(router_e_w[l].T)
        rb = jnp.zeros((LANES, 1), F32).at[:N_GROUPS, 0].set(router_g_b[l])
        rb = rb.at[N_GROUPS:N_GROUPS + N_EXPERTS, 0].set(router_e_b[l])
        row0_tiles = 0 if with_ctx else ctx_tiles
        xn, h2, route, route_rows, counts = _merge(xc, mod, n1, n2, oaf, oab, pa, obf, obb, pb, oc_ctx, oc_lat, hg,
                                                   w_gate[l].astype(BF16), b_gate[l].reshape(1, 3 * dm),
                                                   w_branch[l].astype(BF16), w_out[l].astype(BF16), rw, rb,
                                                   lam_init, tm, ctx_tiles, row0_tiles)

        slot, block_e, n_used, partial = _slot_tables(route_rows, counts)
        x_disp = _scatter(partial, slot, h2, partial.shape[0] * MOE_BLOCK, tm)
        y_disp = _experts(block_e, n_used, x_disp, moe_w1, moe_w3, moe_w2, l)
        moe = (slot, xn, mod, route, y_disp)
    slot, xn, mod, route, y_disp = moe
    return _combine_final(slot, xn, mod, route, y_disp, final_g.reshape(1, dm), tm, ctx_tiles, ctx_tiles)
```

```python
import functools
import math

import numpy as np
import jax
import jax.numpy as jnp
from jax import lax
from jax.experimental import pallas as pl
from jax.experimental.pallas import tpu as pltpu

F32 = jnp.float32
BF16 = jnp.bfloat16

EPS = 1e-6
NEG_BIG = -1e30
HEADS = 4
HEAD_W = 128
MIX_W = HEADS * HEAD_W
GRID_W = 64
ROPE_BASE = 10000.0
ROPE_PAIRS = 16
B_CONV = 3
N_GROUPS = 4
EXPERTS_PER_GROUP = 8
N_EXPERTS = N_GROUPS * EXPERTS_PER_GROUP
D_EXPERT = 512
MOE_BLOCK = 256
LANES = 128
VMEM_LIMIT = 50 * 1024 * 1024

PA_W = 5 * MIX_W
PB_W = 4 * MIX_W
PG_W = LANES
PC_W = 3 * MIX_W
PROJ_W = PA_W + PB_W + PG_W + PC_W


def _params(sem, **kw):
    return pltpu.CompilerParams(dimension_semantics=sem, vmem_limit_bytes=VMEM_LIMIT, **kw)


def _const_spec(shape):
    nd = len(shape)
    return pl.BlockSpec(shape, lambda *_: (0,) * nd, pipeline_mode=pl.Buffered(1))


def _dot(a, b):
    return jnp.dot(a, b, preferred_element_type=F32)


def _dot_nt(a, b):
    return lax.dot_general(a, b, (((1,), (1,)), ((), ())), preferred_element_type=F32)


def _dot_tn(a, b):
    return lax.dot_general(a, b, (((0,), (0,)), ((), ())), preferred_element_type=F32)


def _dot_sel(m_bf16, x):
    hi = x.astype(BF16)
    r1 = x - hi.astype(F32)
    mid = r1.astype(BF16)
    lo = (r1 - mid.astype(F32)).astype(BF16)
    return _dot(m_bf16, hi) + _dot(m_bf16, mid) + _dot(m_bf16, lo)


def _sigmoid(x):
    return 1.0 / (1.0 + jnp.exp(-x))


def _log_sigmoid(x):
    return jnp.minimum(x, 0.0) - jnp.log1p(jnp.exp(-jnp.abs(x)))


def _rms(x, g):
    return x * lax.rsqrt(jnp.mean(x * x, axis=-1, keepdims=True) + EPS) * g


def _ada_kernel(c_ref, w_ref, b_ref, o_ref):
    c = c_ref[...]
    s = c * _sigmoid(c)
    o_ref[...] = jnp.dot(s, w_ref[...], preferred_element_type=F32,
                         precision=lax.Precision.HIGHEST) + b_ref[...]


def _ada_mod(cc, ada_w, ada_b):
    depth, dm, six = ada_w.shape
    rows = cc.shape[0]
    tn = dm
    return pl.pallas_call(
        _ada_kernel,
        grid=(depth, six // tn),
        in_specs=[pl.BlockSpec((rows, dm), lambda l, n: (0, 0)),
                  pl.BlockSpec((None, dm, tn), lambda l, n: (l, 0, n)),
                  pl.BlockSpec((None, 1, tn), lambda l, n: (l, 0, n))],
        out_specs=pl.BlockSpec((None, rows, tn), lambda l, n: (l, 0, n)),
        out_shape=jax.ShapeDtypeStruct((depth, rows, six), F32),
        compiler_params=_params(("parallel", "parallel")),
        name="ada_mod",
    )(cc, ada_w, ada_b.reshape(depth, 1, six))


def _rope(x, cos, sin):
    n = x.shape[-1]
    lane = lax.broadcasted_iota(jnp.int32, (1, n), 1)
    first = (lane // ROPE_PAIRS) % 2 == 0
    partner = jnp.where(first, pltpu.roll(x, n - ROPE_PAIRS, 1), pltpu.roll(x, ROPE_PAIRS, 1))
    return x * cos + partner * sin


def _hgrn_log_forget(z, lbc):
    lsig = jnp.minimum(z, 0.0) - jnp.log(1.0 + jnp.exp(-jnp.abs(z)))
    a = lbc[0:1]
    bb = lbc[1:2] + lsig
    return jnp.maximum(a, bb) + jnp.log(1.0 + jnp.exp(-jnp.abs(a - bb)))


def _with_gathered_rows(slot_ref, slot_next_ref, yd_ref, ybuf, sems, body):
    i = pl.program_id(0)
    n = pl.num_programs(0)
    tm = ybuf.shape[2]

    def start(table_ref, b):
        _row_copies(tm, lambda r, k: pltpu.make_async_copy(
            yd_ref.at[pl.ds(table_ref[0, k * tm + r], 1), :], ybuf.at[b, k, pl.ds(r, 1), :], sems.at[b]))

    def wait(b):
        for k in range(2):
            pltpu.make_async_copy(yd_ref.at[pl.ds(0, tm), :], ybuf.at[b, k], sems.at[b]).wait()

    @pl.when(i == 0)
    def _():
        start(slot_ref, 0)

    for cur in range(2):
        @pl.when(i % 2 == cur)
        def _():
            wait(cur)
            start(slot_next_ref, 1 - cur)
            body(ybuf[cur, 0], ybuf[cur, 1])

            @pl.when(i == n - 1)
            def _():
                wait(1 - cur)


def _moe_residual(x_ref, mod_ref, rt_ref, y0, y1):
    rt = rt_ref[...]
    return x_ref[...] + mod_ref[5:6, :] * (rt[:, 2:3] * y0 + rt[:, 3:4] * y1)


def _norm_proj_fused_kernel(slot_ref, slot_next_ref, xp_ref, modp_ref, rt_ref, yd_ref,
                            mod_ref, g_ref, w_ref, cos_ref, sin_ref, lbc_ref,
                            xo_ref, pa_ref, pb_ref, pg_ref, qc_ref, kc_ref, vc_ref, ybuf, sems):
    def body(y0, y1):
        x = _moe_residual(xp_ref, modp_ref, rt_ref, y0, y1)
        xo_ref[...] = x
        _project(x, mod_ref, g_ref, w_ref, cos_ref, sin_ref, lbc_ref,
                 pa_ref, pb_ref, pg_ref, qc_ref, kc_ref, vc_ref)

    _with_gathered_rows(slot_ref, slot_next_ref, yd_ref, ybuf, sems, body)


def _norm_proj_kernel(x_ref, mod_ref, g_ref, w_ref, cos_ref, sin_ref, lbc_ref,
                      pa_ref, pb_ref, pg_ref, qc_ref, kc_ref, vc_ref):
    _project(x_ref[...], mod_ref, g_ref, w_ref, cos_ref, sin_ref, lbc_ref,
             pa_ref, pb_ref, pg_ref, qc_ref, kc_ref, vc_ref)


def _project(x, mod_ref, g_ref, w_ref, cos_ref, sin_ref, lbc_ref, pa_ref, pb_ref, pg_ref, qc_ref, kc_ref, vc_ref):
    mod = mod_ref[...]
    h = _rms(x, g_ref[...]) * (1.0 + mod[1:2]) + mod[0:1]
    hb = h.astype(BF16)
    pa_ref[:, 0:3 * MIX_W] = _dot(hb, w_ref[:, 0:3 * MIX_W])
    for d in range(2):
        cs = slice((3 + d) * MIX_W, (4 + d) * MIX_W)
        pa_ref[:, cs] = _hgrn_log_forget(_dot(hb, w_ref[:, cs]), lbc_ref[d])
    pb_ref[...] = _dot(hb, w_ref[:, PA_W:PA_W + PB_W])
    pg_ref[...] = _dot(hb, w_ref[:, PA_W + PB_W:PA_W + PB_W + PG_W])
    c0 = PA_W + PB_W + PG_W
    cos = cos_ref[...]
    sin = sin_ref[...]
    q = _dot(hb, w_ref[:, c0:c0 + MIX_W])
    qc_ref[...] = (_rope(q, cos, sin) * (64.0 ** -0.5 * math.log2(math.e))).astype(BF16)
    k = _dot(hb, w_ref[:, c0 + MIX_W:c0 + 2 * MIX_W])
    kc_ref[...] = _rope(k, cos, sin).astype(BF16)
    vc_ref[...] = _dot(hb, w_ref[:, c0 + 2 * MIX_W:c0 + 3 * MIX_W]).astype(BF16)


def _norm_proj_fused(moe, mod, g, w, cos, sin, lbc, tm, ctx_tiles):
    slot, xp, modp, route, y_disp = moe
    bsz, tt, dm = xp.shape
    nt = tt // tm
    n = bsz * nt
    row = lambda i: (i // nt, i % nt, 0)
    mod_row = lambda i: (i // nt, jnp.where(i % nt >= ctx_tiles, 1, 0), 0, 0)
    tab = lambda i: (i % nt, 0)
    outs = [(dm, F32), (PA_W, F32), (PB_W, F32), (PG_W, F32), (MIX_W, BF16), (MIX_W, BF16), (MIX_W, BF16)]
    return pl.pallas_call(
        _norm_proj_fused_kernel,
        grid=(n,),
        in_specs=[pl.BlockSpec((None, 1, 2 * tm), lambda i: (i, 0, 0), memory_space=pltpu.SMEM),
                  pl.BlockSpec((None, 1, 2 * tm), lambda i: (jnp.minimum(i + 1, n - 1), 0, 0),
                               memory_space=pltpu.SMEM),
                  pl.BlockSpec((None, tm, dm), row),
                  pl.BlockSpec((None, None, 6, dm), mod_row),
                  pl.BlockSpec((None, tm, LANES), row),
                  pl.BlockSpec(memory_space=pl.ANY),
                  pl.BlockSpec((None, None, 6, dm), mod_row),
                  _const_spec((1, dm)),
                  _const_spec((dm, PROJ_W)),
                  pl.BlockSpec((tm, MIX_W), tab),
                  pl.BlockSpec((tm, MIX_W), tab),
                  _const_spec(lbc.shape)],
        out_specs=[pl.BlockSpec((None, tm, wd), row) for wd, _ in outs],
        out_shape=[jax.ShapeDtypeStruct((bsz, tt, wd), dt) for wd, dt in outs],
        scratch_shapes=[pltpu.VMEM((2, 2, tm, dm), F32), pltpu.SemaphoreType.DMA((2,))],
        compiler_params=_params(("arbitrary",)),
        name="combine_norm_proj",
    )(slot, slot, xp, modp, route, y_disp, mod, g, w, cos, sin, lbc)


def _norm_proj(x, mod, g, w, cos, sin, lbc, tm, ctx_tiles):
    bsz, tt, dm = x.shape
    nt = tt // tm
    row = lambda b, t: (b, t, 0)
    outs = [(PA_W, F32), (PB_W, F32), (PG_W, F32), (MIX_W, BF16), (MIX_W, BF16), (MIX_W, BF16)]
    return pl.pallas_call(
        _norm_proj_kernel,
        grid=(bsz, nt),
        in_specs=[pl.BlockSpec((None, tm, dm), row),
                  pl.BlockSpec((None, None, 6, dm), lambda b, t: (b, jnp.where(t >= ctx_tiles, 1, 0), 0, 0)),
                  _const_spec((1, dm)),
                  _const_spec((dm, PROJ_W)),
                  pl.BlockSpec((tm, MIX_W), lambda b, t: (t, 0)),
                  pl.BlockSpec((tm, MIX_W), lambda b, t: (t, 0)),
                  _const_spec(lbc.shape)],
        out_specs=[pl.BlockSpec((None, tm, wd), row) for wd, _ in outs],
        out_shape=[jax.ShapeDtypeStruct((bsz, tt, wd), dt) for wd, dt in outs],
        compiler_params=_params(("parallel", "parallel")),
        name="norm_proj",
    )(x, mod, g, w, cos, sin, lbc)


def _conv_kernel(x_ref, w_ref, b_ref, o_ref, *, ctx, q_blocks):
    x = x_ref[...]
    tt = x.shape[0]
    row = lax.broadcasted_iota(jnp.int32, (tt, 1), 0)
    prev = jnp.where((row == 0) | (row == ctx), 0.0, pltpu.roll(x, 1, 0))
    nxt = jnp.where((row == ctx - 1) | (row == tt - 1), 0.0, pltpu.roll(x, tt - 1, 0))
    w = w_ref[...]
    y = b_ref[...] + prev * w[0:1] + x * w[1:2] + nxt * w[2:3]
    y = y * _sigmoid(y)
    scale = jnp.where(pl.program_id(1) < q_blocks, HEAD_W ** -0.5, 1.0)
    o_ref[...] = y * scale


def _conv_silu(pb, conv_w, conv_b, ctx, cb):
    bsz, tt, _ = pb.shape
    width = 2 * MIX_W
    return pl.pallas_call(
        functools.partial(_conv_kernel, ctx=ctx, q_blocks=MIX_W // cb),
        grid=(bsz, width // cb),
        in_specs=[pl.BlockSpec((None, tt, cb), lambda b, j: (b, 0, j)),
                  pl.BlockSpec((B_CONV, cb), lambda b, j: (0, j)),
                  pl.BlockSpec((1, cb), lambda b, j: (0, j))],
        out_specs=pl.BlockSpec((None, tt, cb), lambda b, j: (b, 0, j)),
        out_shape=jax.ShapeDtypeStruct((bsz, tt, width), F32),
        compiler_params=_params(("parallel", "parallel")),
        name="mlstm_conv",
    )(pb, conv_w, conv_b.reshape(1, width))


def _chunk_consts(L):
    idx = np.arange(L)
    t, u = idx[:, None], idx[None, :]
    mats = [(u <= t)]
    for h in range(1, N_MATMUL_LEVELS + 1):
        mid = (t // (2 * h)) * (2 * h) + h - 1
        mats.append(((u > mid) & (u <= t)) | ((u > t) & (u <= mid)))
    cum = np.concatenate([m.astype(np.float32) for m in mats], axis=0)
    cum_b = np.concatenate([m[::-1, ::-1].astype(np.float32) for m in mats], axis=0)
    nlev = int(round(math.log2(L)))
    lvl = np.full((L, L), -1, np.int32)
    lvl[idx, idx] = nlev
    for i in range(nlev):
        h = L >> (i + 1)
        same = (t // (2 * h)) == (u // (2 * h))
        lvl[same & (t % (2 * h) >= h) & (u % (2 * h) < h)] = i
    sel = jnp.asarray(np.stack([cum, cum_b]), BF16)
    lvls = jnp.asarray(np.stack([lvl, lvl[::-1, ::-1]]))
    return sel, lvls, nlev


N_MATMUL_LEVELS = 0


def _neg_abs(x):
    bits = pltpu.bitcast(x, jnp.uint32) | jnp.uint32(0x80000000)
    return pltpu.bitcast(bits, F32)


def _midpoint_rows(b, h, d):
    L, w = b.shape
    two = 2 * h
    pos = h - 1 if d == 0 else h
    if two % 8 == 0:
        r = b.reshape(L // two, two, w)[:, pos:pos + 1, :]
        return jnp.broadcast_to(r, (L // two, two, w)).reshape(L, w)
    phase = lax.broadcasted_iota(jnp.int32, (L, 1), 0) % two
    ref = b
    for off in range(pos - two + 1, pos + 1):
        if off != 0:
            ref = jnp.where(phase == pos - off, pltpu.roll(b, (L - off) % L, 0), ref)
    return ref


def _scan_blocks(j, nc_ctx, nc):
    jb = jnp.where(j < nc_ctx, nc_ctx - 1 - j, nc - 1 + nc_ctx - j)
    return j, jb


def _hgrn_kernel(qf_ref, vf_ref, ff_ref, qb_ref, vb_ref, fb_ref, sel_ref, lvl_ref,
                 of_ref, ob_ref, st_ref, *, L, nlev):
    @pl.when(pl.program_id(1) == 0)
    def _():
        st_ref[...] = jnp.zeros_like(st_ref)

    row = lax.broadcasted_iota(jnp.int32, (L, 1), 0)
    dirs = ((qf_ref, vf_ref, ff_ref, of_ref), (qb_ref, vb_ref, fb_ref, ob_ref))
    for d, (q_ref, v_ref, f_ref, o_ref) in enumerate(dirs):
        lf = f_ref[...]
        kk = 1.0 - jnp.exp(lf)
        q_all = q_ref[...]
        ex = _dot_sel(sel_ref[d], lf) * math.log2(math.e)
        b_in = ex[0:L]
        lvl = lvl_ref[d]
        last = L - 1 if d == 0 else 0
        b_end = b_in[last:last + 1]
        heads = [slice(hd * HEAD_W, (hd + 1) * HEAD_W) for hd in range(HEADS)]
        qb = q_all.astype(BF16)
        kb = kk.astype(BF16)
        on_diag = lvl == nlev
        scores = [jnp.where(on_diag, _dot_nt(qb[:, cs], kb[:, cs]), 0.0) for cs in heads]
        for i in range(nlev):
            h = L >> (i + 1)
            if h <= N_MATMUL_LEVELS:
                expo = ex[h * L:(h + 1) * L]
            else:
                expo = _neg_abs(b_in - _midpoint_rows(b_in, h, d))
            ei = jnp.exp2(expo)
            is_query = (row % (2 * h) >= h) if d == 0 else (row % (2 * h) < h)
            xe = (jnp.where(is_query, q_all, kk) * ei).astype(BF16)
            at_level = lvl == i
            for hd, cs in enumerate(heads):
                scores[hd] = scores[hd] + jnp.where(at_level, _dot_nt(xe[:, cs], xe[:, cs]), 0.0)
        q_in = (q_all * jnp.exp2(b_in)).astype(BF16)
        k_out = (kk * jnp.exp2(b_end - b_in)).astype(BF16)
        decay = jnp.exp2(b_end)
        for hd, cs in enumerate(heads):
            vb = v_ref[:, cs].astype(BF16)
            st = st_ref[d, hd]
            o_ref[:, cs] = _dot(scores[hd].astype(BF16), vb) + _dot_nt(q_in[:, cs], st.astype(BF16))
            st_ref[d, hd] = st * decay[:, cs] + _dot_tn(vb, k_out[:, cs])


def _hgrn(pa, L, nc_ctx):
    bsz, tt, _ = pa.shape
    nc = tt // L
    sel, lvl, nlev = _chunk_consts(L)

    def spec(col, which):
        return pl.BlockSpec((None, L, MIX_W), lambda b, j: (b, _scan_blocks(j, nc_ctx, nc)[which], col))

    return pl.pallas_call(
        functools.partial(_hgrn_kernel, L=L, nlev=nlev),
        grid=(bsz, nc),
        in_specs=[spec(0, 0), spec(1, 0), spec(3, 0), spec(0, 1), spec(1, 1), spec(4, 1),
                  _const_spec(sel.shape), _const_spec(lvl.shape)],
        out_specs=[spec(0, 0), spec(0, 1)],
        out_shape=[jax.ShapeDtypeStruct((bsz, tt, MIX_W), F32)] * 2,
        scratch_shapes=[pltpu.VMEM((2, HEADS, HEAD_W, HEAD_W), F32)],
        compiler_params=_params(("parallel", "arbitrary")),
        name="hgrn2_scan",
    )(pa, pa, pa, pa, pa, pa, sel, lvl)


def _mlstm_kernel(qf_ref, kf_ref, vf_ref, gf_ref, qb_ref, kb_ref, vb_ref, gb_ref, gbias_ref,
                  sel_ref, lvl_ref, of_ref, ob_ref, c_ref, m_ref, *, L, nlev):
    @pl.when(pl.program_id(1) == 0)
    def _():
        c_ref[...] = jnp.zeros_like(c_ref)
        m_ref[...] = jnp.zeros_like(m_ref)

    lane = lax.broadcasted_iota(jnp.int32, (1, LANES), 1)
    ones_col = jnp.where(lane == 0, 1.0, 0.0).astype(BF16)
    dirs = ((qf_ref, kf_ref, vf_ref, gf_ref, of_ref), (qb_ref, kb_ref, vb_ref, gb_ref, ob_ref))
    for d, (q_ref, k_ref, v_ref, g_ref, o_ref) in enumerate(dirs):
        g = g_ref[...] + gbias_ref[...]
        lf = jnp.where((lane >= 2 * HEADS) & (lane < 4 * HEADS), _log_sigmoid(g), 0.0)
        bcum = _dot_sel(sel_ref[d, 0:L], lf)
        causal = lvl_ref[d] >= 0
        row_terms = (g - pltpu.roll(bcum, LANES - 2 * HEADS, 1)).T
        last = L - 1 if d == 0 else 0
        for hd in range(HEADS):
            cs = slice(hd * HEAD_W, (hd + 1) * HEAD_W)
            li_lane = d * HEADS + hd
            lf_lane = 2 * HEADS + d * HEADS + hd
            b = bcum[:, lf_lane:lf_lane + 1]
            row_term = row_terms[li_lane:li_lane + 1, :]
            b_end = b[last:last + 1]
            r = d * HEADS + hd
            m_prev = m_ref[r:r + 1, 0:1]
            q = q_ref[:, cs].astype(BF16)
            k = k_ref[:, cs]
            v_aug = jnp.concatenate([v_ref[:, cs].astype(BF16), jnp.broadcast_to(ones_col, (L, LANES))], axis=1)
            c_prev = c_ref[r]
            w_end = b_end + row_term
            m_new = jnp.maximum(b_end + m_prev, jnp.max(w_end, axis=-1, keepdims=True))
            e_end = jnp.exp(w_end - m_new)
            keep = jnp.exp(b_end + m_prev - m_new)
            c_ref[r] = keep * c_prev + _dot((k.T * e_end).astype(BF16), v_aug)
            m_ref[r:r + 1, :] = jnp.broadcast_to(m_new, (1, LANES))
            logw = jnp.where(causal, b + row_term, NEG_BIG)
            w_state = b + m_prev
            m_t = jnp.maximum(jnp.max(logw, axis=-1, keepdims=True), w_state)
            s = _dot_nt(q, k.astype(BF16)) * jnp.exp(logw - m_t)
            a_state = jnp.exp(w_state - m_t)
            num = _dot(s.astype(BF16), v_aug) + a_state * _dot(q, c_prev.astype(BF16))
            den = num[:, HEAD_W:HEAD_W + 1]
            o_ref[:, cs] = num[:, 0:HEAD_W] / jnp.maximum(jnp.abs(den), jnp.exp(-m_t))


def _mlstm(qk, pb, pg, gbias, L, nc_ctx):
    bsz, tt, _ = pb.shape
    nc = tt // L
    sel, lvl, nlev = _chunk_consts(L)

    def spec(col, which, width=MIX_W):
        return pl.BlockSpec((None, L, width), lambda b, j: (b, _scan_blocks(j, nc_ctx, nc)[which], col))

    return pl.pallas_call(
        functools.partial(_mlstm_kernel, L=L, nlev=nlev),
        grid=(bsz, nc),
        in_specs=[spec(0, 0), spec(1, 0), spec(2, 0), spec(0, 0, PG_W),
                  spec(0, 1), spec(1, 1), spec(2, 1), spec(0, 1, PG_W),
                  _const_spec((1, PG_W)), _const_spec(sel.shape), _const_spec(lvl.shape)],
        out_specs=[spec(0, 0), spec(0, 1)],
        out_shape=[jax.ShapeDtypeStruct((bsz, tt, MIX_W), F32)] * 2,
        scratch_shapes=[pltpu.VMEM((2 * HEADS, HEAD_W, 2 * HEAD_W), F32),
                        pltpu.VMEM((2 * HEADS, LANES), F32)],
        compiler_params=_params(("parallel", "arbitrary")),
        name="mlstm_scan",
    )(qk, qk, pb, pg, qk, qk, pb, pg, gbias, sel, lvl)


def _diff_lambda(lam_ref, lam_init):
    lv = lam_ref[...]
    return (jnp.exp(jnp.sum(lv[0:1] * lv[1:2], axis=-1, keepdims=True))
            - jnp.exp(jnp.sum(lv[2:3] * lv[3:4], axis=-1, keepdims=True)) + lam_init)


def _map_scores(q, k):
    lane = lax.broadcasted_iota(jnp.int32, (1, HEAD_W), 1)
    zero = jnp.zeros_like(q)
    return (_dot_nt(jnp.where(lane < HEAD_W // 2, q, zero), k),
            _dot_nt(jnp.where(lane >= HEAD_W // 2, q, zero), k))


def _row_max(s):
    return jnp.max(s, axis=-1, keepdims=True)


def _diff_softmax_v(s1, m1, s2, m2, lam, v):
    def probs(s, m):
        p = jnp.exp2(s - m)
        return p / jnp.sum(p, axis=-1, keepdims=True)

    return _dot((probs(s1, m1) - lam * probs(s2, m2)).astype(BF16), v)


def _attn_ctx_kernel(lam_ref, q_ref, k_ref, v_ref, o_ref, *, lam_init):
    s1, s2 = _map_scores(q_ref[...], k_ref[...])
    o_ref[...] = _diff_softmax_v(s1, _row_max(s1), s2, _row_max(s2), _diff_lambda(lam_ref, lam_init), v_ref[...])


def _attn_ctx(lam_vec, qc, kc, vc, lam_init, tq, t_ctx):
    bsz = qc.shape[0]
    kv = pl.BlockSpec((None, t_ctx, HEAD_W), lambda b, h, i: (b, 0, h))
    return pl.pallas_call(
        functools.partial(_attn_ctx_kernel, lam_init=lam_init),
        grid=(bsz, HEADS, t_ctx // tq),
        in_specs=[_const_spec(lam_vec.shape),
                  pl.BlockSpec((None, tq, HEAD_W), lambda b, h, i: (b, i, h)), kv, kv],
        out_specs=pl.BlockSpec((None, tq, HEAD_W), lambda b, h, i: (b, i, h)),
        out_shape=jax.ShapeDtypeStruct((bsz, t_ctx, MIX_W), F32),
        compiler_params=_params(("parallel", "parallel", "arbitrary")),
        name="diff_attn_ctx",
    )(lam_vec, qc, kc, vc)


def _attn_lat_kernel(lam_ref, q_ref, k_ref, v_ref, o_ref, s_even, m_even, s_odd, m_odd, *, lam_init, n_tiles):
    j = pl.program_id(2)
    lam = _diff_lambda(lam_ref, lam_init)
    even, odd = (s_even, m_even), (s_odd, m_odd)

    def scores_into(bufs):
        s_buf, m_buf = bufs
        for i, s in enumerate(_map_scores(q_ref[...], k_ref[...])):
            s_buf[i] = s
            m_buf[i] = jnp.broadcast_to(_row_max(s), m_buf.shape[1:])

    def finish_from(bufs):
        s_buf, m_buf = bufs
        o_ref[...] = _diff_softmax_v(s_buf[0], m_buf[0][:, 0:1], s_buf[1], m_buf[1][:, 0:1], lam, v_ref[...])

    @pl.when(j == 0)
    def _():
        scores_into(even)

    middle = (j > 0) & (j < n_tiles)

    @pl.when(middle & (j % 2 == 1))
    def _():
        scores_into(odd)
        finish_from(even)

    @pl.when(middle & (j % 2 == 0))
    def _():
        scores_into(even)
        finish_from(odd)

    @pl.when(j == n_tiles)
    def _():
        finish_from(odd if n_tiles % 2 == 0 else even)


def _attn_lat(lam_vec, qc, kc, vc, lam_init, tq, t_ctx):
    bsz, tt, _ = qc.shape
    n_tiles = (tt - t_ctx) // tq
    q0 = t_ctx // tq
    kv = pl.BlockSpec((None, tt, HEAD_W), lambda b, h, j: (b, 0, h))
    return pl.pallas_call(
        functools.partial(_attn_lat_kernel, lam_init=lam_init, n_tiles=n_tiles),
        grid=(bsz, HEADS, n_tiles + 1),
        in_specs=[_const_spec(lam_vec.shape),
                  pl.BlockSpec((None, tq, HEAD_W), lambda b, h, j: (b, q0 + jnp.minimum(j, n_tiles - 1), h)),
                  kv, kv],
        out_specs=pl.BlockSpec((None, tq, HEAD_W), lambda b, h, j: (b, jnp.maximum(j - 1, 0), h)),
        out_shape=jax.ShapeDtypeStruct((bsz, tt - t_ctx, MIX_W), F32),
        scratch_shapes=[pltpu.VMEM((2, tq, tt), F32), pltpu.VMEM((2, tq, LANES), F32),
                        pltpu.VMEM((2, tq, tt), F32), pltpu.VMEM((2, tq, LANES), F32)],
        compiler_params=_params(("parallel", "parallel", "arbitrary")),
        name="diff_attn_lat",
    )(lam_vec, qc, kc, vc)


def _head_norm(x, g):
    parts = []
    for hd in range(HEADS):
        xs = x[:, hd * HEAD_W:(hd + 1) * HEAD_W]
        parts.append(xs * lax.rsqrt(jnp.mean(xs * xs, axis=-1, keepdims=True) + EPS))
    return jnp.concatenate(parts, axis=1) * g


ROUTE_ROWS = 40


def _route(lg):
    n = lg.shape[1]
    row = lax.broadcasted_iota(jnp.int32, lg.shape, 0)
    neg = -jnp.inf
    big = ROUTE_ROWS
    gl = jnp.where(row < N_GROUPS, lg, neg)
    gmax = jnp.max(gl, axis=0, keepdims=True)
    gidx = jnp.min(jnp.where(gl == gmax, row, big), axis=0, keepdims=True)
    p_group = 1.0 / jnp.sum(jnp.exp(gl - gmax), axis=0, keepdims=True)
    lo = N_GROUPS + EXPERTS_PER_GROUP * gidx
    in_grp = (row >= lo) & (row < lo + EXPERTS_PER_GROUP)
    el = jnp.where(in_grp, lg, neg)
    pe = jnp.exp(el - jnp.max(el, axis=0, keepdims=True))
    pe = pe / jnp.sum(pe, axis=0, keepdims=True)
    pe = jnp.where(in_grp, pe, -1.0)
    v1 = jnp.max(pe, axis=0, keepdims=True)
    i1 = jnp.min(jnp.where(pe == v1, row, big), axis=0, keepdims=True)
    pe2 = jnp.where(row == i1, -1.0, pe)
    v2 = jnp.max(pe2, axis=0, keepdims=True)
    i2 = jnp.min(jnp.where(pe2 == v2, row, big), axis=0, keepdims=True)
    scale = p_group / (v1 + v2)
    erow = lax.broadcasted_iota(jnp.int32, (N_EXPERTS, n), 0) + N_GROUPS
    oh1 = erow == i1
    oh2 = erow == i2
    oh1f = jnp.where(oh1, 1.0, 0.0)
    oh2f = jnp.where(oh2, 1.0, 0.0)
    earlier = (lax.broadcasted_iota(jnp.int32, (n, n), 0) < lax.broadcasted_iota(jnp.int32, (n, n), 1))
    earlier = jnp.where(earlier, 1.0, 0.0).astype(BF16)
    tot1 = jnp.sum(oh1f, axis=1, keepdims=True)
    tot2 = jnp.sum(oh2f, axis=1, keepdims=True)
    rank1 = jnp.sum(jnp.where(oh1, _dot(oh1f.astype(BF16), earlier), 0.0), axis=0, keepdims=True)
    rank2 = jnp.sum(jnp.where(oh2, _dot(oh2f.astype(BF16), earlier) + tot1, 0.0), axis=0, keepdims=True)
    r8 = lax.broadcasted_iota(jnp.int32, (8, n), 0)
    rows = jnp.zeros((8, n), F32)
    for i, val in enumerate(((i1 - N_GROUPS).astype(F32), (i2 - N_GROUPS).astype(F32), v1 * scale, v2 * scale,
                             rank1, rank2)):
        rows = jnp.where(r8 == i, val, rows)
    return rows, tot1 + tot2


def _merge_kernel(x_ref, mod_ref, n1_ref, n2_ref, af_ref, ab_ref, ga_ref, bf_ref, bb_ref, gb_ref,
                  occ_ref, ocl_ref, hg_ref, wg_ref, bg_ref, wbr_ref, wo_ref, rw_ref, rb_ref,
                  xo_ref, h2_ref, rt_ref, rtt_ref, cnt_ref, *, lam_init, ctx_tiles, row0_tiles):
    x = x_ref[...]
    mod = mod_ref[...]
    dm = x.shape[-1]
    hb = (_rms(x, n1_ref[...]) * (1.0 + mod[1:2]) + mod[0:1]).astype(BF16)
    hg = hg_ref[...]
    ga = ga_ref[...]
    ya = _head_norm(af_ref[...] + ab_ref[...], hg[0:1]) * (ga * _sigmoid(ga))
    yb = _head_norm(bf_ref[...] + bb_ref[...], hg[1:2]) * _sigmoid(gb_ref[...])
    oc = jnp.where(row0_tiles + pl.program_id(1) < ctx_tiles, occ_ref[...], ocl_ref[...])
    yc = _head_norm(oc, hg[2:3]) * (1.0 - lam_init)
    y = jnp.zeros_like(x)
    for i, yi in enumerate((ya, yb, yc)):
        gate = _sigmoid(_dot(hb, wg_ref[:, i * dm:(i + 1) * dm]) + bg_ref[:, i * dm:(i + 1) * dm])
        y = y + gate * _dot(yi.astype(BF16), wbr_ref[i])
    xn = x + mod[2:3] * _dot(y.astype(BF16), wo_ref[...])
    xo_ref[...] = xn
    h2 = _rms(xn, n2_ref[...]) * (1.0 + mod[4:5]) + mod[3:4]
    h2_ref[...] = h2
    h_hi = h2.astype(BF16)
    h_mid = (h2 - h_hi.astype(F32)).astype(BF16)
    logits = (_dot_nt(rw_ref[0], h_hi) + _dot_nt(rw_ref[0], h_mid) + _dot_nt(rw_ref[1], h_hi)) + rb_ref[...]
    rows, counts = _route(logits[0:ROUTE_ROWS, :])
    rtt_ref[...] = rows
    cnt_ref[...] = jnp.broadcast_to(counts, cnt_ref.shape)
    tm = x.shape[0]
    rt_ref[...] = jnp.concatenate([rows, jnp.zeros((LANES - 8, tm), F32)], axis=0).T


def _merge(x, mod, n1, n2, oaf, oab, pa, obf, obb, pb, oc_ctx, oc_lat, hg, wg, bg, wbr, wo, rw, rb,
           lam_init, tm, ctx_tiles, row0_tiles):
    bsz, tt, dm = x.shape
    nt = tt // tm - row0_tiles
    rows = nt * tm
    src = lambda b, t: (b, row0_tiles + t, 0)

    def col(c):
        return pl.BlockSpec((None, tm, MIX_W), lambda b, t: (b, row0_tiles + t, c))

    dst = lambda b, t: (b, t, 0)
    outs = [(dm, F32), (dm, F32), (LANES, F32)]
    out_specs = [pl.BlockSpec((None, tm, wd), dst) for wd, _ in outs]
    out_shape = [jax.ShapeDtypeStruct((bsz, rows, wd), dt) for wd, dt in outs]
    out_specs.append(pl.BlockSpec((None, 8, tm), lambda b, t: (b * nt + t, 0, 0)))
    out_shape.append(jax.ShapeDtypeStruct((bsz * nt, 8, tm), F32))
    out_specs.append(pl.BlockSpec((None, N_EXPERTS, LANES), lambda b, t: (b * nt + t, 0, 0)))
    out_shape.append(jax.ShapeDtypeStruct((bsz * nt, N_EXPERTS, LANES), F32))
    return pl.pallas_call(
        functools.partial(_merge_kernel, lam_init=lam_init, ctx_tiles=ctx_tiles, row0_tiles=row0_tiles),
        grid=(bsz, nt),
        in_specs=[pl.BlockSpec((None, tm, dm), src),
                  pl.BlockSpec((None, None, 6, dm),
                               lambda b, t: (b, jnp.where(row0_tiles + t >= ctx_tiles, 1, 0), 0, 0)),
                  _const_spec((1, dm)), _const_spec((1, dm)),
                  col(0), col(0), col(2), col(0), col(0), col(3),
                  pl.BlockSpec((None, tm, MIX_W),
                               lambda b, t: (b, jnp.minimum(row0_tiles + t, max(ctx_tiles - 1, 0)), 0)),
                  pl.BlockSpec((None, tm, MIX_W),
                               lambda b, t: (b, jnp.maximum(row0_tiles + t - ctx_tiles, 0), 0)),
                  _const_spec((3, MIX_W)), _const_spec(wg.shape), _const_spec(bg.shape),
                  _const_spec(wbr.shape), _const_spec(wo.shape), _const_spec(rw.shape), _const_spec(rb.shape)],
        out_specs=out_specs,
        out_shape=out_shape,
        compiler_params=_params(("parallel", "parallel")),
        name="merge_route",
    )(x, mod, n1, n2, oaf, oab, pa, obf, obb, pb, oc_ctx, oc_lat, hg, wg, bg, wbr, wo, rw, rb)


def _slot_tables(route_rows, counts):
    tiles, _, tm = route_rows.shape
    cnt = counts[:, :, 0].astype(jnp.int32)
    tile_off = jnp.cumsum(cnt, axis=0) - cnt
    total = jnp.sum(cnt, axis=0)
    padded = (total + MOE_BLOCK - 1) // MOE_BLOCK * MOE_BLOCK
    p_end = jnp.cumsum(padded)
    base = (p_end - padded)[None, :] + tile_off
    e = route_rows[:, 0:2, :].astype(jnp.int32)
    rank = route_rows[:, 4:6, :].astype(jnp.int32)
    hit = e[:, :, :, None] == jnp.arange(N_EXPERTS, dtype=jnp.int32)
    slot = jnp.sum(jnp.where(hit, base[:, None, None, :], 0), axis=-1) + rank
    n_blocks = -(-(2 * tiles * tm) // MOE_BLOCK) + N_EXPERTS
    blk_start = jnp.arange(n_blocks, dtype=jnp.int32) * MOE_BLOCK
    block_e = jnp.minimum(jnp.sum(blk_start[:, None] >= p_end[None, :], axis=1), N_EXPERTS - 1).astype(jnp.int32)
    n_used = (p_end[-1] // MOE_BLOCK).astype(jnp.int32).reshape(1)
    filled = jnp.clip(((p_end - padded) + total)[block_e] - blk_start, 0, MOE_BLOCK)
    filled = jnp.where(blk_start < p_end[-1], filled, 0)
    partial = (filled < MOE_BLOCK).astype(jnp.int32)
    return slot.reshape(tiles, 1, 2 * tm), block_e, n_used, partial


def _row_copies(n_rows, make):
    for r in range(n_rows):
        for k in range(2):
            make(r, k).start()


def _scatter_kernel(partial_ref, slot_ref, h_ref, xd_ref, zeros, sem, zsem):
    tm = h_ref.shape[0]
    n_blocks = partial_ref.shape[0]

    @pl.when((pl.program_id(0) == 0) & (pl.program_id(1) == 0))
    def _():
        zeros[...] = jnp.zeros_like(zeros)

        def zero_block(blk):
            return pltpu.make_async_copy(zeros, xd_ref.at[pl.ds(blk * MOE_BLOCK, MOE_BLOCK), :], zsem)

        def start(blk, c):
            @pl.when(partial_ref[blk] != 0)
            def _():
                zero_block(blk).start()
            return c

        def wait(blk, c):
            @pl.when(partial_ref[blk] != 0)
            def _():
                zero_block(blk).wait()
            return c

        lax.fori_loop(0, n_blocks, start, 0)
        lax.fori_loop(0, n_blocks, wait, 0)

    _row_copies(tm, lambda r, k: pltpu.make_async_copy(
        h_ref.at[pl.ds(r, 1), :], xd_ref.at[pl.ds(slot_ref[0, k * tm + r], 1), :], sem))
    for _ in range(2):
        pltpu.make_async_copy(h_ref, xd_ref.at[pl.ds(0, tm), :], sem).wait()


def _scatter(partial, slot, h2, n_slots, tm):
    bsz, rows, dm = h2.shape
    nt = rows // tm
    grid_spec = pltpu.PrefetchScalarGridSpec(
        num_scalar_prefetch=1,
        grid=(bsz, nt),
        in_specs=[pl.BlockSpec((None, 1, 2 * tm), lambda b, t, p: (b * nt + t, 0, 0), memory_space=pltpu.SMEM),
                  pl.BlockSpec((tm, dm), lambda b, t, p: (b * nt + t, 0))],
        out_specs=pl.BlockSpec(memory_space=pl.ANY),
        scratch_shapes=[pltpu.VMEM((MOE_BLOCK, dm), F32), pltpu.SemaphoreType.DMA(()),
                        pltpu.SemaphoreType.DMA(())],
    )
    return pl.pallas_call(
        _scatter_kernel,
        grid_spec=grid_spec,
        out_shape=jax.ShapeDtypeStruct((n_slots, dm), F32),
        compiler_params=_params(("arbitrary", "arbitrary"), has_side_effects=True),
        name="moe_scatter",
    )(partial, slot, h2.reshape(bsz * rows, dm))


def _expert_kernel(be_ref, nu_ref, x_ref, w1_ref, w3_ref, w2_ref, y_ref, w1b, w3b, w2b):
    i = pl.program_id(0)

    @pl.when(i < nu_ref[0])
    def _():
        @pl.when((i == 0) | (be_ref[i] != be_ref[jnp.maximum(i - 1, 0)]))
        def _():
            w1b[...] = w1_ref[...].astype(BF16)
            w3b[...] = w3_ref[...].astype(BF16)
            w2b[...] = w2_ref[...].astype(BF16)

        xb = x_ref[...].astype(BF16)
        u = _dot(xb, w1b[...])
        hmid = (u * _sigmoid(u)) * _dot(xb, w3b[...])
        y_ref[...] = _dot(hmid.astype(BF16), w2b[...])

    @pl.when(i >= nu_ref[0])
    def _():
        y_ref[...] = jnp.zeros_like(y_ref)


def _experts(block_e, n_used, x_disp, w1, w3, w2, layer):
    n_slots, dm = x_disp.shape
    n_blocks = n_slots // MOE_BLOCK
    wspec = lambda shape: pl.BlockSpec((None, None) + shape, lambda i, be, nu: (layer, be[i], 0, 0))
    grid_spec = pltpu.PrefetchScalarGridSpec(
        num_scalar_prefetch=2,
        grid=(n_blocks,),
        in_specs=[pl.BlockSpec((MOE_BLOCK, dm), lambda i, be, nu: (jnp.minimum(i, nu[0] - 1), 0)),
                  wspec((dm, D_EXPERT)), wspec((dm, D_EXPERT)), wspec((D_EXPERT, dm))],
        out_specs=pl.BlockSpec((MOE_BLOCK, dm), lambda i, be, nu: (i, 0)),
        scratch_shapes=[pltpu.VMEM((dm, D_EXPERT), BF16), pltpu.VMEM((dm, D_EXPERT), BF16),
                        pltpu.VMEM((D_EXPERT, dm), BF16)],
    )
    return pl.pallas_call(
        _expert_kernel,
        grid_spec=grid_spec,
        out_shape=jax.ShapeDtypeStruct((n_slots, dm), F32),
        compiler_params=_params(("arbitrary",)),
        name="moe_experts",
    )(block_e, n_used, x_disp, w1, w3, w2)


def _combine_kernel(slot_ref, slot_next_ref, x_ref, mod_ref, rt_ref, yd_ref, fg_ref, o_ref, ybuf, sems):
    def body(y0, y1):
        o_ref[...] = _rms(_moe_residual(x_ref, mod_ref, rt_ref, y0, y1), fg_ref[...])

    _with_gathered_rows(slot_ref, slot_next_ref, yd_ref, ybuf, sems, body)


def _combine_final(slot, x, mod, route, y_disp, fg, tm, ctx_tiles, row0_tiles):
    bsz, rows, dm = x.shape
    nt = rows // tm
    n = bsz * nt
    row = lambda i: (i // nt, i % nt, 0)
    return pl.pallas_call(
        _combine_kernel,
        grid=(n,),
        in_specs=[pl.BlockSpec((None, 1, 2 * tm), lambda i: (i, 0, 0), memory_space=pltpu.SMEM),
                  pl.BlockSpec((None, 1, 2 * tm), lambda i: (jnp.minimum(i + 1, n - 1), 0, 0),
                               memory_space=pltpu.SMEM),
                  pl.BlockSpec((None, tm, dm), row),
                  pl.BlockSpec((None, None, 6, dm),
                               lambda i: (i // nt, jnp.where(row0_tiles + i % nt >= ctx_tiles, 1, 0), 0, 0)),
                  pl.BlockSpec((None, tm, LANES), row),
                  pl.BlockSpec(memory_space=pl.ANY),
                  _const_spec((1, dm))],
        out_specs=pl.BlockSpec((None, tm, dm), row),
        out_shape=jax.ShapeDtypeStruct((bsz, rows, dm), F32),
        scratch_shapes=[pltpu.VMEM((2, 2, tm, dm), F32), pltpu.SemaphoreType.DMA((2,))],
        compiler_params=_params(("arbitrary",)),
        name="moe_combine",
    )(slot, slot, x, mod, route, y_disp, fg)


def _rope_tables(ctx, t_lat):
    rows = t_lat // GRID_W
    row = jnp.repeat(jnp.arange(rows, dtype=F32), GRID_W)
    col = jnp.tile(jnp.arange(GRID_W, dtype=F32), rows)
    inv = ROPE_BASE ** (-jnp.arange(ROPE_PAIRS, dtype=F32) / ROPE_PAIRS)
    ang_r = row[:, None] * inv
    ang_c = col[:, None] * inv
    cos64 = jnp.concatenate([jnp.cos(ang_r), jnp.cos(ang_r), jnp.cos(ang_c), jnp.cos(ang_c)], axis=1)
    sin64 = jnp.concatenate([-jnp.sin(ang_r), jnp.sin(ang_r), -jnp.sin(ang_c), jnp.sin(ang_c)], axis=1)
    cos = jnp.tile(cos64, (1, MIX_W // 64))
    sin = jnp.tile(sin64, (1, MIX_W // 64))
    cos = jnp.concatenate([jnp.ones((ctx, MIX_W), F32), cos], axis=0)
    sin = jnp.concatenate([jnp.zeros((ctx, MIX_W), F32), sin], axis=0)
    return cos, sin


def _pack_w_in(w):
    dm = w.shape[0]
    a_end = PA_W
    b_end = a_end + PB_W
    g_end = b_end + 4 * HEADS
    pad = jnp.zeros((dm, PG_W - 4 * HEADS), w.dtype)
    return jnp.concatenate([w[:, :b_end], w[:, b_end:g_end], pad, w[:, g_end:]], axis=1).astype(BF16)


def kernel(x, c, ctx, c_ctx, ada_w, ada_b, norm1_g, norm2_g, w_in, mlstm_conv_w, mlstm_conv_b, mlstm_gate_b,
           hgrn_lb_raw, hgrn_norm_g, mlstm_norm_g, diff_norm_g, diff_lambda, w_branch, w_gate, b_gate, w_out,
           router_g_w, router_g_b, router_e_w, router_e_b, moe_w1, moe_w3, moe_w2, final_g):
    bsz, t_lat, dm = x.shape
    t_ctx = ctx.shape[1]
    depth = ada_w.shape[0]
    tt = t_ctx + t_lat
    tm = min(256, t_ctx)
    hgrn_chunk = min(128, t_ctx)
    mlstm_chunk = min(256, t_ctx)
    assert t_ctx % tm == 0 and t_lat % tm == 0 and t_lat % GRID_W == 0
    assert t_ctx % hgrn_chunk == 0 and t_ctx % mlstm_chunk == 0
    ctx_tiles = t_ctx // tm

    n_rows = -(-(bsz + 1) // 8) * 8
    cc = jnp.zeros((n_rows, dm), F32).at[:bsz].set(c).at[bsz].set(c_ctx)
    mods = _ada_mod(cc, ada_w, ada_b).reshape(depth, n_rows, 6, dm)

    cos, sin = _rope_tables(t_ctx, t_lat)
    lb_cum = jnp.cumsum(jax.nn.softmax(hgrn_lb_raw.astype(F32), axis=0), axis=0)
    lower = lb_cum - lb_cum[0]

    xc = jnp.concatenate([ctx, x], axis=1)
    moe = None
    for l in range(depth):
        with_ctx = l < depth - 1
        lam_init = 0.8 - 0.6 * math.exp(-0.3 * l)
        mod = jnp.stack([jnp.broadcast_to(mods[l, bsz], (bsz, 6, dm)), mods[l, :bsz]], axis=1)
        n1 = norm1_g[l].reshape(1, dm)
        n2 = norm2_g[l].reshape(1, dm)

        lb = lower[l]
        pad = jnp.zeros((2, 6, MIX_W), F32)
        lbc = jnp.concatenate([jnp.log(lb)[:, None], jnp.log1p(-lb)[:, None], pad], axis=1)
        if moe is None:
            pa, pb, pg, qc, kc, vc = _norm_proj(xc, mod, n1, _pack_w_in(w_in[l]), cos, sin, lbc, tm, ctx_tiles)
        else:
            xc, pa, pb, pg, qc, kc, vc = _norm_proj_fused(moe, mod, n1, _pack_w_in(w_in[l]), cos, sin, lbc,
                                                          tm, ctx_tiles)

        oaf, oab = _hgrn(pa, hgrn_chunk, t_ctx // hgrn_chunk)

        qk = _conv_silu(pb, mlstm_conv_w[l], mlstm_conv_b[l], t_ctx, 256)
        gbias = jnp.zeros((1, PG_W), F32).at[0, :4 * HEADS].set(mlstm_gate_b[l].reshape(-1))
        obf, obb = _mlstm(qk, pb, pg, gbias, mlstm_chunk, t_ctx // mlstm_chunk)

        lam_vec = diff_lambda[l].astype(F32)
        oc_lat = _attn_lat(lam_vec, qc, kc, vc, lam_init, tm, t_ctx)
        oc_ctx = _attn_ctx(lam_vec, qc, kc, vc, lam_init, tm, t_ctx) if with_ctx else oc_lat

        hg = jnp.stack([hgrn_norm_g[l], mlstm_norm_g[l], diff_norm_g[l]])
        rw = jnp.zeros((LANES, dm), F32).at[:N_GROUPS].set(router_g_w[l].T)
        rw = rw.at[N_GROUPS:N_GROUPS + N_EXPERTS].set(router_e_w[l].T)
        rw_hi = rw.astype(BF16)
        rw = jnp.stack([rw_hi, (rw - rw_hi.astype(F32)).astype(BF16)])
        rb = jnp.zeros((LANES, 1), F32).at[:N_GROUPS, 0].set(router_g_b[l])
        rb = rb.at[N_GROUPS:N_GROUPS + N_EXPERTS, 0].set(router_e_b[l])
        row0_tiles = 0 if with_ctx else ctx_tiles
        xn, h2, route, route_rows, counts = _merge(xc, mod, n1, n2, oaf, oab, pa, obf, obb, pb, oc_ctx, oc_lat, hg,
                                                   w_gate[l].astype(BF16), b_gate[l].reshape(1, 3 * dm),
                                                   w_branch[l].astype(BF16), w_out[l].astype(BF16), rw, rb,
                                                   lam_init, tm, ctx_tiles, row0_tiles)

        slot, block_e, n_used, partial = _slot_tables(route_rows, counts)
        x_disp = _scatter(partial, slot, h2, partial.shape[0] * MOE_BLOCK, tm)
        y_disp = _experts(block_e, n_used, x_disp, moe_w1, moe_w3, moe_w2, l)
        moe = (slot, xn, mod, route, y_disp)
    slot, xn, mod, route, y_disp = moe
    return _combine_final(slot, xn, mod, route, y_disp, final_g.reshape(1, dm), tm, ctx_tiles, ctx_tiles)
```

```python
import functools
import math

import numpy as np
import jax
import jax.numpy as jnp
from jax import lax
from jax.experimental import pallas as pl
from jax.experimental.pallas import tpu as pltpu

F32 = jnp.float32
BF16 = jnp.bfloat16

EPS = 1e-6
NEG_BIG = -1e30
HEADS = 4
HEAD_W = 128
MIX_W = HEADS * HEAD_W
GRID_W = 64
ROPE_BASE = 10000.0
ROPE_PAIRS = 16
B_CONV = 3
N_GROUPS = 4
EXPERTS_PER_GROUP = 8
N_EXPERTS = N_GROUPS * EXPERTS_PER_GROUP
D_EXPERT = 512
MOE_BLOCK = 256
LANES = 128
VMEM_LIMIT = 50 * 1024 * 1024

PA_W = 5 * MIX_W
PB_W = 4 * MIX_W
PG_W = LANES
PC_W = 3 * MIX_W
PROJ_W = PA_W + PB_W + PG_W + PC_W


def _params(sem, **kw):
    return pltpu.CompilerParams(dimension_semantics=sem, vmem_limit_bytes=VMEM_LIMIT, **kw)


def _const_spec(shape):
    nd = len(shape)
    return pl.BlockSpec(shape, lambda *_: (0,) * nd, pipeline_mode=pl.Buffered(1))


def _dot(a, b):
    return jnp.dot(a, b, preferred_element_type=F32)


def _dot_nt(a, b):
    return lax.dot_general(a, b, (((1,), (1,)), ((), ())), preferred_element_type=F32)


def _dot_tn(a, b):
    return lax.dot_general(a, b, (((0,), (0,)), ((), ())), preferred_element_type=F32)


def _dot_sel(m_bf16, x):
    hi = x.astype(BF16)
    r1 = x - hi.astype(F32)
    mid = r1.astype(BF16)
    lo = (r1 - mid.astype(F32)).astype(BF16)
    return _dot(m_bf16, hi) + _dot(m_bf16, mid) + _dot(m_bf16, lo)


def _sigmoid(x):
    return 1.0 / (1.0 + jnp.exp(-x))


def _log_sigmoid(x):
    return jnp.minimum(x, 0.0) - jnp.log1p(jnp.exp(-jnp.abs(x)))


def _rms(x, g):
    return x * lax.rsqrt(jnp.mean(x * x, axis=-1, keepdims=True) + EPS) * g


def _ada_kernel(c_ref, w_ref, b_ref, o_ref):
    c = c_ref[...]
    s = c * _sigmoid(c)
    o_ref[...] = jnp.dot(s, w_ref[...], preferred_element_type=F32,
                         precision=lax.Precision.HIGHEST) + b_ref[...]


def _ada_mod(cc, ada_w, ada_b):
    depth, dm, six = ada_w.shape
    rows = cc.shape[0]
    tn = dm
    return pl.pallas_call(
        _ada_kernel,
        grid=(depth, six // tn),
        in_specs=[pl.BlockSpec((rows, dm), lambda l, n: (0, 0)),
                  pl.BlockSpec((None, dm, tn), lambda l, n: (l, 0, n)),
                  pl.BlockSpec((None, 1, tn), lambda l, n: (l, 0, n))],
        out_specs=pl.BlockSpec((None, rows, tn), lambda l, n: (l, 0, n)),
        out_shape=jax.ShapeDtypeStruct((depth, rows, six), F32),
        compiler_params=_params(("parallel", "parallel")),
        name="ada_mod",
    )(cc, ada_w, ada_b.reshape(depth, 1, six))


def _rope(x, cos, sin):
    n = x.shape[-1]
    lane = lax.broadcasted_iota(jnp.int32, (1, n), 1)
    first = (lane // ROPE_PAIRS) % 2 == 0
    partner = jnp.where(first, pltpu.roll(x, n - ROPE_PAIRS, 1), pltpu.roll(x, ROPE_PAIRS, 1))
    return x * cos + partner * sin


def _hgrn_log_forget(z, lbc):
    lsig = jnp.minimum(z, 0.0) - jnp.log(1.0 + jnp.exp(-jnp.abs(z)))
    a = lbc[0:1]
    bb = lbc[1:2] + lsig
    return jnp.maximum(a, bb) + jnp.log(1.0 + jnp.exp(-jnp.abs(a - bb)))


def _with_gathered_rows(slot_ref, slot_next_ref, yd_ref, ybuf, sems, body):
    i = pl.program_id(0)
    n = pl.num_programs(0)
    tm = ybuf.shape[2]

    def start(table_ref, b):
        _row_copies(tm, lambda r, k: pltpu.make_async_copy(
            yd_ref.at[pl.ds(table_ref[0, k * tm + r], 1), :], ybuf.at[b, k, pl.ds(r, 1), :], sems.at[b]))

    def wait(b):
        for k in range(2):
            pltpu.make_async_copy(yd_ref.at[pl.ds(0, tm), :], ybuf.at[b, k], sems.at[b]).wait()

    @pl.when(i == 0)
    def _():
        start(slot_ref, 0)

    for cur in range(2):
        @pl.when(i % 2 == cur)
        def _():
            wait(cur)
            start(slot_next_ref, 1 - cur)
            body(ybuf[cur, 0], ybuf[cur, 1])

            @pl.when(i == n - 1)
            def _():
                wait(1 - cur)


def _moe_residual(x_ref, mod_ref, rt_ref, y0, y1):
    rt = rt_ref[...]
    return x_ref[...] + mod_ref[5:6, :] * (rt[:, 2:3] * y0 + rt[:, 3:4] * y1)


def _norm_proj_fused_kernel(slot_ref, slot_next_ref, xp_ref, modp_ref, rt_ref, yd_ref,
                            mod_ref, g_ref, w_ref, cos_ref, sin_ref, lbc_ref,
                            xo_ref, pa_ref, pb_ref, pg_ref, qc_ref, kc_ref, vc_ref, ybuf, sems):
    def body(y0, y1):
        x = _moe_residual(xp_ref, modp_ref, rt_ref, y0, y1)
        xo_ref[...] = x
        _project(x, mod_ref, g_ref, w_ref, cos_ref, sin_ref, lbc_ref,
                 pa_ref, pb_ref, pg_ref, qc_ref, kc_ref, vc_ref)

    _with_gathered_rows(slot_ref, slot_next_ref, yd_ref, ybuf, sems, body)


def _norm_proj_kernel(x_ref, mod_ref, g_ref, w_ref, cos_ref, sin_ref, lbc_ref,
                      pa_ref, pb_ref, pg_ref, qc_ref, kc_ref, vc_ref):
    _project(x_ref[...], mod_ref, g_ref, w_ref, cos_ref, sin_ref, lbc_ref,
             pa_ref, pb_ref, pg_ref, qc_ref, kc_ref, vc_ref)


def _project(x, mod_ref, g_ref, w_ref, cos_ref, sin_ref, lbc_ref, pa_ref, pb_ref, pg_ref, qc_ref, kc_ref, vc_ref):
    mod = mod_ref[...]
    h = _rms(x, g_ref[...]) * (1.0 + mod[1:2]) + mod[0:1]
    hb = h.astype(BF16)
    pa_ref[:, 0:3 * MIX_W] = _dot(hb, w_ref[:, 0:3 * MIX_W])
    for d in range(2):
        cs = slice((3 + d) * MIX_W, (4 + d) * MIX_W)
        pa_ref[:, cs] = _hgrn_log_forget(_dot(hb, w_ref[:, cs]), lbc_ref[d])
    pb_ref[...] = _dot(hb, w_ref[:, PA_W:PA_W + PB_W])
    pg_ref[...] = _dot(hb, w_ref[:, PA_W + PB_W:PA_W + PB_W + PG_W])
    c0 = PA_W + PB_W + PG_W
    cos = cos_ref[...]
    sin = sin_ref[...]
    q = _dot(hb, w_ref[:, c0:c0 + MIX_W])
    qc_ref[...] = (_rope(q, cos, sin) * (64.0 ** -0.5 * math.log2(math.e))).astype(BF16)
    k = _dot(hb, w_ref[:, c0 + MIX_W:c0 + 2 * MIX_W])
    kc_ref[...] = _rope(k, cos, sin).astype(BF16)
    vc_ref[...] = _dot(hb, w_ref[:, c0 + 2 * MIX_W:c0 + 3 * MIX_W]).astype(BF16)


def _norm_proj_fused(moe, mod, g, w, cos, sin, lbc, tm, ctx_tiles):
    slot, xp, modp, route, y_disp = moe
    bsz, tt, dm = xp.shape
    nt = tt // tm
    n = bsz * nt
    row = lambda i: (i // nt, i % nt, 0)
    mod_row = lambda i: (i // nt, jnp.where(i % nt >= ctx_tiles, 1, 0), 0, 0)
    tab = lambda i: (i % nt, 0)
    outs = [(dm, F32), (PA_W, F32), (PB_W, F32), (PG_W, F32), (MIX_W, BF16), (MIX_W, BF16), (MIX_W, BF16)]
    return pl.pallas_call(
        _norm_proj_fused_kernel,
        grid=(n,),
        in_specs=[pl.BlockSpec((None, 1, 2 * tm), lambda i: (i, 0, 0), memory_space=pltpu.SMEM),
                  pl.BlockSpec((None, 1, 2 * tm), lambda i: (jnp.minimum(i + 1, n - 1), 0, 0),
                               memory_space=pltpu.SMEM),
                  pl.BlockSpec((None, tm, dm), row),
                  pl.BlockSpec((None, None, 6, dm), mod_row),
                  pl.BlockSpec((None, tm, LANES), row),
                  pl.BlockSpec(memory_space=pl.ANY),
                  pl.BlockSpec((None, None, 6, dm), mod_row),
                  _const_spec((1, dm)),
                  _const_spec((dm, PROJ_W)),
                  pl.BlockSpec((tm, MIX_W), tab),
                  pl.BlockSpec((tm, MIX_W), tab),
                  _const_spec(lbc.shape)],
        out_specs=[pl.BlockSpec((None, tm, wd), row) for wd, _ in outs],
        out_shape=[jax.ShapeDtypeStruct((bsz, tt, wd), dt) for wd, dt in outs],
        scratch_shapes=[pltpu.VMEM((2, 2, tm, dm), F32), pltpu.SemaphoreType.DMA((2,))],
        compiler_params=_params(("arbitrary",)),
        name="combine_norm_proj",
    )(slot, slot, xp, modp, route, y_disp, mod, g, w, cos, sin, lbc)


def _norm_proj(x, mod, g, w, cos, sin, lbc, tm, ctx_tiles):
    bsz, tt, dm = x.shape
    nt = tt // tm
    row = lambda b, t: (b, t, 0)
    outs = [(PA_W, F32), (PB_W, F32), (PG_W, F32), (MIX_W, BF16), (MIX_W, BF16), (MIX_W, BF16)]
    return pl.pallas_call(
        _norm_proj_kernel,
        grid=(bsz, nt),
        in_specs=[pl.BlockSpec((None, tm, dm), row),
                  pl.BlockSpec((None, None, 6, dm), lambda b, t: (b, jnp.where(t >= ctx_tiles, 1, 0), 0, 0)),
                  _const_spec((1, dm)),
                  _const_spec((dm, PROJ_W)),
                  pl.BlockSpec((tm, MIX_W), lambda b, t: (t, 0)),
                  pl.BlockSpec((tm, MIX_W), lambda b, t: (t, 0)),
                  _const_spec(lbc.shape)],
        out_specs=[pl.BlockSpec((None, tm, wd), row) for wd, _ in outs],
        out_shape=[jax.ShapeDtypeStruct((bsz, tt, wd), dt) for wd, dt in outs],
        compiler_params=_params(("parallel", "parallel")),
        name="norm_proj",
    )(x, mod, g, w, cos, sin, lbc)


def _conv_kernel(x_ref, w_ref, b_ref, o_ref, *, ctx, q_blocks):
    x = x_ref[...]
    tt = x.shape[0]
    row = lax.broadcasted_iota(jnp.int32, (tt, 1), 0)
    prev = jnp.where((row == 0) | (row == ctx), 0.0, pltpu.roll(x, 1, 0))
    nxt = jnp.where((row == ctx - 1) | (row == tt - 1), 0.0, pltpu.roll(x, tt - 1, 0))
    w = w_ref[...]
    y = b_ref[...] + prev * w[0:1] + x * w[1:2] + nxt * w[2:3]
    y = y * _sigmoid(y)
    scale = jnp.where(pl.program_id(1) < q_blocks, HEAD_W ** -0.5, 1.0)
    o_ref[...] = y * scale


def _conv_silu(pb, conv_w, conv_b, ctx, cb):
    bsz, tt, _ = pb.shape
    width = 2 * MIX_W
    return pl.pallas_call(
        functools.partial(_conv_kernel, ctx=ctx, q_blocks=MIX_W // cb),
        grid=(bsz, width // cb),
        in_specs=[pl.BlockSpec((None, tt, cb), lambda b, j: (b, 0, j)),
                  pl.BlockSpec((B_CONV, cb), lambda b, j: (0, j)),
                  pl.BlockSpec((1, cb), lambda b, j: (0, j))],
        out_specs=pl.BlockSpec((None, tt, cb), lambda b, j: (b, 0, j)),
        out_shape=jax.ShapeDtypeStruct((bsz, tt, width), F32),
        compiler_params=_params(("parallel", "parallel")),
        name="mlstm_conv",
    )(pb, conv_w, conv_b.reshape(1, width))


def _chunk_consts(L):
    idx = np.arange(L)
    t, u = idx[:, None], idx[None, :]
    mats = [(u <= t)]
    for h in range(1, N_MATMUL_LEVELS + 1):
        mid = (t // (2 * h)) * (2 * h) + h - 1
        mats.append(((u > mid) & (u <= t)) | ((u > t) & (u <= mid)))
    cum = np.concatenate([m.astype(np.float32) for m in mats], axis=0)
    cum_b = np.concatenate([m[::-1, ::-1].astype(np.float32) for m in mats], axis=0)
    nlev = int(round(math.log2(L)))
    lvl = np.full((L, L), -1, np.int32)
    lvl[idx, idx] = nlev
    for i in range(nlev):
        h = L >> (i + 1)
        same = (t // (2 * h)) == (u // (2 * h))
        lvl[same & (t % (2 * h) >= h) & (u % (2 * h) < h)] = i
    sel = jnp.asarray(np.stack([cum, cum_b]), BF16)
    lvls = jnp.asarray(np.stack([lvl, lvl[::-1, ::-1]]))
    return sel, lvls, nlev


N_MATMUL_LEVELS = 0


def _neg_abs(x):
    return -jnp.abs(x)


def _midpoint_rows(b, h, d):
    L, w = b.shape
    two = 2 * h
    pos = h - 1 if d == 0 else h
    if two % 8 == 0:
        r = b.reshape(L // two, two, w)[:, pos:pos + 1, :]
        return jnp.broadcast_to(r, (L // two, two, w)).reshape(L, w)
    phase = lax.broadcasted_iota(jnp.int32, (L, 1), 0) % two
    ref = b
    for off in range(pos - two + 1, pos + 1):
        if off != 0:
            ref = jnp.where(phase == pos - off, pltpu.roll(b, (L - off) % L, 0), ref)
    return ref


def _scan_blocks(j, nc_ctx, nc):
    jb = jnp.where(j < nc_ctx, nc_ctx - 1 - j, nc - 1 + nc_ctx - j)
    return j, jb


def _hgrn_kernel(qf_ref, vf_ref, ff_ref, qb_ref, vb_ref, fb_ref, sel_ref, lvl_ref,
                 of_ref, ob_ref, st_ref, *, L, nlev):
    @pl.when(pl.program_id(1) == 0)
    def _():
        st_ref[...] = jnp.zeros_like(st_ref)

    row = lax.broadcasted_iota(jnp.int32, (L, 1), 0)
    dirs = ((qf_ref, vf_ref, ff_ref, of_ref), (qb_ref, vb_ref, fb_ref, ob_ref))
    for d, (q_ref, v_ref, f_ref, o_ref) in enumerate(dirs):
        lf = f_ref[...]
        kk = 1.0 - jnp.exp(lf)
        q_all = q_ref[...]
        ex = _dot_sel(sel_ref[d], lf) * math.log2(math.e)
        b_in = ex[0:L]
        lvl = lvl_ref[d]
        last = L - 1 if d == 0 else 0
        b_end = b_in[last:last + 1]
        heads = [slice(hd * HEAD_W, (hd + 1) * HEAD_W) for hd in range(HEADS)]
        qb = q_all.astype(BF16)
        kb = kk.astype(BF16)
        on_diag = lvl == nlev
        scores = [jnp.where(on_diag, _dot_nt(qb[:, cs], kb[:, cs]), 0.0) for cs in heads]
        for i in range(nlev):
            h = L >> (i + 1)
            if h <= N_MATMUL_LEVELS:
                expo = ex[h * L:(h + 1) * L]
            else:
                expo = _neg_abs(b_in - _midpoint_rows(b_in, h, d))
            ei = jnp.exp2(expo)
            is_query = (row % (2 * h) >= h) if d == 0 else (row % (2 * h) < h)
            xe = (jnp.where(is_query, q_all, kk) * ei).astype(BF16)
            at_level = lvl == i
            for hd, cs in enumerate(heads):
                scores[hd] = scores[hd] + jnp.where(at_level, _dot_nt(xe[:, cs], xe[:, cs]), 0.0)
        q_in = (q_all * jnp.exp2(b_in)).astype(BF16)
        k_out = (kk * jnp.exp2(b_end - b_in)).astype(BF16)
        decay = jnp.exp2(b_end)
        for hd, cs in enumerate(heads):
            vb = v_ref[:, cs].astype(BF16)
            st = st_ref[d, hd]
            o_ref[:, cs] = _dot(scores[hd].astype(BF16), vb) + _dot_nt(q_in[:, cs], st.astype(BF16))
            st_ref[d, hd] = st * decay[:, cs] + _dot_tn(vb, k_out[:, cs])


def _hgrn(pa, L, nc_ctx):
    bsz, tt, _ = pa.shape
    nc = tt // L
    sel, lvl, nlev = _chunk_consts(L)

    def spec(col, which):
        return pl.BlockSpec((None, L, MIX_W), lambda b, j: (b, _scan_blocks(j, nc_ctx, nc)[which], col))

    return pl.pallas_call(
        functools.partial(_hgrn_kernel, L=L, nlev=nlev),
        grid=(bsz, nc),
        in_specs=[spec(0, 0), spec(1, 0), spec(3, 0), spec(0, 1), spec(1, 1), spec(4, 1),
                  _const_spec(sel.shape), _const_spec(lvl.shape)],
        out_specs=[spec(0, 0), spec(0, 1)],
        out_shape=[jax.ShapeDtypeStruct((bsz, tt, MIX_W), F32)] * 2,
        scratch_shapes=[pltpu.VMEM((2, HEADS, HEAD_W, HEAD_W), F32)],
        compiler_params=_params(("parallel", "arbitrary")),
        name="hgrn2_scan",
    )(pa, pa, pa, pa, pa, pa, sel, lvl)


def _mlstm_kernel(qf_ref, kf_ref, vf_ref, gf_ref, qb_ref, kb_ref, vb_ref, gb_ref, gbias_ref,
                  sel_ref, lvl_ref, of_ref, ob_ref, c_ref, m_ref, *, L, nlev):
    @pl.when(pl.program_id(1) == 0)
    def _():
        c_ref[...] = jnp.zeros_like(c_ref)
        m_ref[...] = jnp.zeros_like(m_ref)

    lane = lax.broadcasted_iota(jnp.int32, (1, LANES), 1)
    ones_col = jnp.where(lane == 0, 1.0, 0.0).astype(BF16)
    dirs = ((qf_ref, kf_ref, vf_ref, gf_ref, of_ref), (qb_ref, kb_ref, vb_ref, gb_ref, ob_ref))
    for d, (q_ref, k_ref, v_ref, g_ref, o_ref) in enumerate(dirs):
        g = g_ref[...] + gbias_ref[...]
        lf = jnp.where((lane >= 2 * HEADS) & (lane < 4 * HEADS), _log_sigmoid(g), 0.0)
        bcum = _dot_sel(sel_ref[d, 0:L], lf)
        causal = lvl_ref[d] >= 0
        row_terms = (g - pltpu.roll(bcum, LANES - 2 * HEADS, 1)).T
        last = L - 1 if d == 0 else 0
        for hd in range(HEADS):
            cs = slice(hd * HEAD_W, (hd + 1) * HEAD_W)
            li_lane = d * HEADS + hd
            lf_lane = 2 * HEADS + d * HEADS + hd
            b = bcum[:, lf_lane:lf_lane + 1]
            row_term = row_terms[li_lane:li_lane + 1, :]
            b_end = b[last:last + 1]
            r = d * HEADS + hd
            m_prev = m_ref[r:r + 1, 0:1]
            q = q_ref[:, cs].astype(BF16)
            k = k_ref[:, cs]
            v_aug = jnp.concatenate([v_ref[:, cs].astype(BF16), jnp.broadcast_to(ones_col, (L, LANES))], axis=1)
            c_prev = c_ref[r]
            w_end = b_end + row_term
            m_new = jnp.maximum(b_end + m_prev, jnp.max(w_end, axis=-1, keepdims=True))
            e_end = jnp.exp(w_end - m_new)
            keep = jnp.exp(b_end + m_prev - m_new)
            c_ref[r] = keep * c_prev + _dot((k.T * e_end).astype(BF16), v_aug)
            m_ref[r:r + 1, :] = jnp.broadcast_to(m_new, (1, LANES))
            logw = jnp.where(causal, b + row_term, NEG_BIG)
            w_state = b + m_prev
            m_t = jnp.maximum(jnp.max(logw, axis=-1, keepdims=True), w_state)
            s = _dot_nt(q, k.astype(BF16)) * jnp.exp(logw - m_t)
            a_state = jnp.exp(w_state - m_t)
            num = _dot(s.astype(BF16), v_aug) + a_state * _dot(q, c_prev.astype(BF16))
            den = num[:, HEAD_W:HEAD_W + 1]
            o_ref[:, cs] = num[:, 0:HEAD_W] / jnp.maximum(jnp.abs(den), jnp.exp(-m_t))


def _mlstm(qk, pb, pg, gbias, L, nc_ctx):
    bsz, tt, _ = pb.shape
    nc = tt // L
    sel, lvl, nlev = _chunk_consts(L)

    def spec(col, which, width=MIX_W):
        return pl.BlockSpec((None, L, width), lambda b, j: (b, _scan_blocks(j, nc_ctx, nc)[which], col))

    return pl.pallas_call(
        functools.partial(_mlstm_kernel, L=L, nlev=nlev),
        grid=(bsz, nc),
        in_specs=[spec(0, 0), spec(1, 0), spec(2, 0), spec(0, 0, PG_W),
                  spec(0, 1), spec(1, 1), spec(2, 1), spec(0, 1, PG_W),
                  _const_spec((1, PG_W)), _const_spec(sel.shape), _const_spec(lvl.shape)],
        out_specs=[spec(0, 0), spec(0, 1)],
        out_shape=[jax.ShapeDtypeStruct((bsz, tt, MIX_W), F32)] * 2,
        scratch_shapes=[pltpu.VMEM((2 * HEADS, HEAD_W, 2 * HEAD_W), F32),
                        pltpu.VMEM((2 * HEADS, LANES), F32)],
        compiler_params=_params(("parallel", "arbitrary")),
        name="mlstm_scan",
    )(qk, qk, pb, pg, qk, qk, pb, pg, gbias, sel, lvl)


def _diff_lambda(lam_ref, lam_init):
    lv = lam_ref[...]
    return (jnp.exp(jnp.sum(lv[0:1] * lv[1:2], axis=-1, keepdims=True))
            - jnp.exp(jnp.sum(lv[2:3] * lv[3:4], axis=-1, keepdims=True)) + lam_init)


def _map_scores(q, k):
    lane = lax.broadcasted_iota(jnp.int32, (1, HEAD_W), 1)
    zero = jnp.zeros_like(q)
    return (_dot_nt(jnp.where(lane < HEAD_W // 2, q, zero), k),
            _dot_nt(jnp.where(lane >= HEAD_W // 2, q, zero), k))


def _row_max(s):
    return jnp.max(s, axis=-1, keepdims=True)


def _diff_softmax_v(s1, m1, s2, m2, lam, v):
    p1 = jnp.exp2(s1 - m1)
    p2 = jnp.exp2(s2 - m2)
    l1 = jnp.sum(p1, axis=-1, keepdims=True)
    l2 = jnp.sum(p2, axis=-1, keepdims=True)
    return _dot((p1 - (lam * l1 / l2) * p2).astype(BF16), v) / l1


def _attn_ctx_kernel(lam_ref, q_ref, k_ref, v_ref, o_ref, *, lam_init):
    s1, s2 = _map_scores(q_ref[...], k_ref[...])
    o_ref[...] = _diff_softmax_v(s1, _row_max(s1), s2, _row_max(s2), _diff_lambda(lam_ref, lam_init), v_ref[...])


def _attn_ctx(lam_vec, qc, kc, vc, lam_init, tq, t_ctx):
    bsz = qc.shape[0]
    kv = pl.BlockSpec((None, t_ctx, HEAD_W), lambda b, h, i: (b, 0, h))
    return pl.pallas_call(
        functools.partial(_attn_ctx_kernel, lam_init=lam_init),
        grid=(bsz, HEADS, t_ctx // tq),
        in_specs=[_const_spec(lam_vec.shape),
                  pl.BlockSpec((None, tq, HEAD_W), lambda b, h, i: (b, i, h)), kv, kv],
        out_specs=pl.BlockSpec((None, tq, HEAD_W), lambda b, h, i: (b, i, h)),
        out_shape=jax.ShapeDtypeStruct((bsz, t_ctx, MIX_W), F32),
        compiler_params=_params(("parallel", "parallel", "arbitrary")),
        name="diff_attn_ctx",
    )(lam_vec, qc, kc, vc)


def _attn_lat_kernel(lam_ref, q_ref, k_ref, v_ref, o_ref, s_even, m_even, s_odd, m_odd, *, lam_init, n_tiles):
    j = pl.program_id(2)
    lam = _diff_lambda(lam_ref, lam_init)
    even, odd = (s_even, m_even), (s_odd, m_odd)

    def scores_into(bufs):
        s_buf, m_buf = bufs
        for i, s in enumerate(_map_scores(q_ref[...], k_ref[...])):
            s_buf[i] = s
            m_buf[i] = jnp.broadcast_to(_row_max(s), m_buf.shape[1:])

    def finish_from(bufs):
        s_buf, m_buf = bufs
        o_ref[...] = _diff_softmax_v(s_buf[0], m_buf[0][:, 0:1], s_buf[1], m_buf[1][:, 0:1], lam, v_ref[...])

    @pl.when(j == 0)
    def _():
        scores_into(even)

    middle = (j > 0) & (j < n_tiles)

    @pl.when(middle & (j % 2 == 1))
    def _():
        scores_into(odd)
        finish_from(even)

    @pl.when(middle & (j % 2 == 0))
    def _():
        scores_into(even)
        finish_from(odd)

    @pl.when(j == n_tiles)
    def _():
        finish_from(odd if n_tiles % 2 == 0 else even)


def _attn_lat(lam_vec, qc, kc, vc, lam_init, tq, t_ctx):
    bsz, tt, _ = qc.shape
    n_tiles = (tt - t_ctx) // tq
    q0 = t_ctx // tq
    kv = pl.BlockSpec((None, tt, HEAD_W), lambda b, h, j: (b, 0, h))
    return pl.pallas_call(
        functools.partial(_attn_lat_kernel, lam_init=lam_init, n_tiles=n_tiles),
        grid=(bsz, HEADS, n_tiles + 1),
        in_specs=[_const_spec(lam_vec.shape),
                  pl.BlockSpec((None, tq, HEAD_W), lambda b, h, j: (b, q0 + jnp.minimum(j, n_tiles - 1), h)),
                  kv, kv],
        out_specs=pl.BlockSpec((None, tq, HEAD_W), lambda b, h, j: (b, jnp.maximum(j - 1, 0), h)),
        out_shape=jax.ShapeDtypeStruct((bsz, tt - t_ctx, MIX_W), F32),
        scratch_shapes=[pltpu.VMEM((2, tq, tt), F32), pltpu.VMEM((2, tq, LANES), F32),
                        pltpu.VMEM((2, tq, tt), F32), pltpu.VMEM((2, tq, LANES), F32)],
        compiler_params=_params(("parallel", "parallel", "arbitrary")),
        name="diff_attn_lat",
    )(lam_vec, qc, kc, vc)


def _head_norm(x, g):
    parts = []
    for hd in range(HEADS):
        xs = x[:, hd * HEAD_W:(hd + 1) * HEAD_W]
        parts.append(xs * lax.rsqrt(jnp.mean(xs * xs, axis=-1, keepdims=True) + EPS))
    return jnp.concatenate(parts, axis=1) * g


ROUTE_ROWS = 40


def _route(lg):
    n = lg.shape[1]
    row = lax.broadcasted_iota(jnp.int32, lg.shape, 0)
    neg = -jnp.inf
    big = ROUTE_ROWS
    gl = jnp.where(row < N_GROUPS, lg, neg)
    gmax = jnp.max(gl, axis=0, keepdims=True)
    gidx = jnp.min(jnp.where(gl == gmax, row, big), axis=0, keepdims=True)
    p_group = 1.0 / jnp.sum(jnp.exp(gl - gmax), axis=0, keepdims=True)
    lo = N_GROUPS + EXPERTS_PER_GROUP * gidx
    in_grp = (row >= lo) & (row < lo + EXPERTS_PER_GROUP)
    el = jnp.where(in_grp, lg, neg)
    pe = jnp.exp(el - jnp.max(el, axis=0, keepdims=True))
    pe = pe / jnp.sum(pe, axis=0, keepdims=True)
    pe = jnp.where(in_grp, pe, -1.0)
    v1 = jnp.max(pe, axis=0, keepdims=True)
    i1 = jnp.min(jnp.where(pe == v1, row, big), axis=0, keepdims=True)
    pe2 = jnp.where(row == i1, -1.0, pe)
    v2 = jnp.max(pe2, axis=0, keepdims=True)
    i2 = jnp.min(jnp.where(pe2 == v2, row, big), axis=0, keepdims=True)
    scale = p_group / (v1 + v2)
    erow = lax.broadcasted_iota(jnp.int32, (N_EXPERTS, n), 0) + N_GROUPS
    oh1 = erow == i1
    oh2 = erow == i2
    oh1f = jnp.where(oh1, 1.0, 0.0)
    oh2f = jnp.where(oh2, 1.0, 0.0)
    earlier = (lax.broadcasted_iota(jnp.int32, (n, n), 0) < lax.broadcasted_iota(jnp.int32, (n, n), 1))
    earlier = jnp.where(earlier, 1.0, 0.0).astype(BF16)
    tot1 = jnp.sum(oh1f, axis=1, keepdims=True)
    tot2 = jnp.sum(oh2f, axis=1, keepdims=True)
    rank1 = jnp.sum(jnp.where(oh1, _dot(oh1f.astype(BF16), earlier), 0.0), axis=0, keepdims=True)
    rank2 = jnp.sum(jnp.where(oh2, _dot(oh2f.astype(BF16), earlier) + tot1, 0.0), axis=0, keepdims=True)
    r8 = lax.broadcasted_iota(jnp.int32, (8, n), 0)
    rows = jnp.zeros((8, n), F32)
    for i, val in enumerate(((i1 - N_GROUPS).astype(F32), (i2 - N_GROUPS).astype(F32), v1 * scale, v2 * scale,
                             rank1, rank2)):
        rows = jnp.where(r8 == i, val, rows)
    return rows, tot1 + tot2


def _merge_kernel(x_ref, mod_ref, n1_ref, n2_ref, af_ref, ab_ref, ga_ref, bf_ref, bb_ref, gb_ref,
                  occ_ref, ocl_ref, hg_ref, wg_ref, bg_ref, wbr_ref, wo_ref, rw_ref, rb_ref,
                  xo_ref, h2_ref, rt_ref, rtt_ref, cnt_ref, *, lam_init, ctx_tiles, row0_tiles):
    x = x_ref[...]
    mod = mod_ref[...]
    dm = x.shape[-1]
    hb = (_rms(x, n1_ref[...]) * (1.0 + mod[1:2]) + mod[0:1]).astype(BF16)
    hg = hg_ref[...]
    ga = ga_ref[...]
    ya = _head_norm(af_ref[...] + ab_ref[...], hg[0:1]) * (ga * _sigmoid(ga))
    yb = _head_norm(bf_ref[...] + bb_ref[...], hg[1:2]) * _sigmoid(gb_ref[...])
    oc = jnp.where(row0_tiles + pl.program_id(1) < ctx_tiles, occ_ref[...], ocl_ref[...])
    yc = _head_norm(oc, hg[2:3]) * (1.0 - lam_init)
    y = jnp.zeros_like(x)
    for i, yi in enumerate((ya, yb, yc)):
        gate = _sigmoid(_dot(hb, wg_ref[:, i * dm:(i + 1) * dm]) + bg_ref[:, i * dm:(i + 1) * dm])
        y = y + gate * _dot(yi.astype(BF16), wbr_ref[i])
    xn = x + mod[2:3] * _dot(y.astype(BF16), wo_ref[...])
    xo_ref[...] = xn
    h2 = _rms(xn, n2_ref[...]) * (1.0 + mod[4:5]) + mod[3:4]
    h2_ref[...] = h2
    h_hi = h2.astype(BF16)
    h_mid = (h2 - h_hi.astype(F32)).astype(BF16)
    logits = (_dot_nt(rw_ref[0], h_hi) + _dot_nt(rw_ref[0], h_mid) + _dot_nt(rw_ref[1], h_hi)) + rb_ref[...]
    rows, counts = _route(logits[0:ROUTE_ROWS, :])
    rtt_ref[...] = rows
    cnt_ref[...] = jnp.broadcast_to(counts, cnt_ref.shape)
    tm = x.shape[0]
    rt_ref[...] = jnp.concatenate([rows, jnp.zeros((LANES - 8, tm), F32)], axis=0).T


def _merge(x, mod, n1, n2, oaf, oab, pa, obf, obb, pb, oc_ctx, oc_lat, hg, wg, bg, wbr, wo, rw, rb,
           lam_init, tm, ctx_tiles, row0_tiles):
    bsz, tt, dm = x.shape
    nt = tt // tm - row0_tiles
    rows = nt * tm
    src = lambda b, t: (b, row0_tiles + t, 0)

    def col(c):
        return pl.BlockSpec((None, tm, MIX_W), lambda b, t: (b, row0_tiles + t, c))

    dst = lambda b, t: (b, t, 0)
    outs = [(dm, F32), (dm, F32), (LANES, F32)]
    out_specs = [pl.BlockSpec((None, tm, wd), dst) for wd, _ in outs]
    out_shape = [jax.ShapeDtypeStruct((bsz, rows, wd), dt) for wd, dt in outs]
    out_specs.append(pl.BlockSpec((None, 8, tm), lambda b, t: (b * nt + t, 0, 0)))
    out_shape.append(jax.ShapeDtypeStruct((bsz * nt, 8, tm), F32))
    out_specs.append(pl.BlockSpec((None, N_EXPERTS, LANES), lambda b, t: (b * nt + t, 0, 0)))
    out_shape.append(jax.ShapeDtypeStruct((bsz * nt, N_EXPERTS, LANES), F32))
    return pl.pallas_call(
        functools.partial(_merge_kernel, lam_init=lam_init, ctx_tiles=ctx_tiles, row0_tiles=row0_tiles),
        grid=(bsz, nt),
        in_specs=[pl.BlockSpec((None, tm, dm), src),
                  pl.BlockSpec((None, None, 6, dm),
                               lambda b, t: (b, jnp.where(row0_tiles + t >= ctx_tiles, 1, 0), 0, 0)),
                  _const_spec((1, dm)), _const_spec((1, dm)),
                  col(0), col(0), col(2), col(0), col(0), col(3),
                  pl.BlockSpec((None, tm, MIX_W),
                               lambda b, t: (b, jnp.minimum(row0_tiles + t, max(ctx_tiles - 1, 0)), 0)),
                  pl.BlockSpec((None, tm, MIX_W),
                               lambda b, t: (b, jnp.maximum(row0_tiles + t - ctx_tiles, 0), 0)),
                  _const_spec((3, MIX_W)), _const_spec(wg.shape), _const_spec(bg.shape),
                  _const_spec(wbr.shape), _const_spec(wo.shape), _const_spec(rw.shape), _const_spec(rb.shape)],
        out_specs=out_specs,
        out_shape=out_shape,
        compiler_params=_params(("parallel", "parallel")),
        name="merge_route",
    )(x, mod, n1, n2, oaf, oab, pa, obf, obb, pb, oc_ctx, oc_lat, hg, wg, bg, wbr, wo, rw, rb)


def _slot_tables(route_rows, counts):
    tiles, _, tm = route_rows.shape
    cnt = counts[:, :, 0].astype(jnp.int32)
    tile_off = jnp.cumsum(cnt, axis=0) - cnt
    total = jnp.sum(cnt, axis=0)
    padded = (total + MOE_BLOCK - 1) // MOE_BLOCK * MOE_BLOCK
    p_end = jnp.cumsum(padded)
    base = (p_end - padded)[None, :] + tile_off
    e = route_rows[:, 0:2, :].astype(jnp.int32)
    rank = route_rows[:, 4:6, :].astype(jnp.int32)
    hit = e[:, :, :, None] == jnp.arange(N_EXPERTS, dtype=jnp.int32)
    slot = jnp.sum(jnp.where(hit, base[:, None, None, :], 0), axis=-1) + rank
    n_blocks = -(-(2 * tiles * tm) // MOE_BLOCK) + N_EXPERTS
    blk_start = jnp.arange(n_blocks, dtype=jnp.int32) * MOE_BLOCK
    block_e = jnp.minimum(jnp.sum(blk_start[:, None] >= p_end[None, :], axis=1), N_EXPERTS - 1).astype(jnp.int32)
    n_used = (p_end[-1] // MOE_BLOCK).astype(jnp.int32).reshape(1)
    filled = jnp.clip(((p_end - padded) + total)[block_e] - blk_start, 0, MOE_BLOCK)
    filled = jnp.where(blk_start < p_end[-1], filled, 0)
    partial = (filled < MOE_BLOCK).astype(jnp.int32)
    return slot.reshape(tiles, 1, 2 * tm), block_e, n_used, partial


def _row_copies(n_rows, make):
    for r in range(n_rows):
        for k in range(2):
            make(r, k).start()


def _scatter_kernel(partial_ref, slot_ref, h_ref, xd_ref, zeros, sem, zsem):
    tm = h_ref.shape[0]
    n_blocks = partial_ref.shape[0]

    @pl.when((pl.program_id(0) == 0) & (pl.program_id(1) == 0))
    def _():
        zeros[...] = jnp.zeros_like(zeros)

        def zero_block(blk):
            return pltpu.make_async_copy(zeros, xd_ref.at[pl.ds(blk * MOE_BLOCK, MOE_BLOCK), :], zsem)

        def start(blk, c):
            @pl.when(partial_ref[blk] != 0)
            def _():
                zero_block(blk).start()
            return c

        def wait(blk, c):
            @pl.when(partial_ref[blk] != 0)
            def _():
                zero_block(blk).wait()
            return c

        lax.fori_loop(0, n_blocks, start, 0)
        lax.fori_loop(0, n_blocks, wait, 0)

    _row_copies(tm, lambda r, k: pltpu.make_async_copy(
        h_ref.at[pl.ds(r, 1), :], xd_ref.at[pl.ds(slot_ref[0, k * tm + r], 1), :], sem))
    for _ in range(2):
        pltpu.make_async_copy(h_ref, xd_ref.at[pl.ds(0, tm), :], sem).wait()


def _scatter(partial, slot, h2, n_slots, tm):
    bsz, rows, dm = h2.shape
    nt = rows // tm
    grid_spec = pltpu.PrefetchScalarGridSpec(
        num_scalar_prefetch=1,
        grid=(bsz, nt),
        in_specs=[pl.BlockSpec((None, 1, 2 * tm), lambda b, t, p: (b * nt + t, 0, 0), memory_space=pltpu.SMEM),
                  pl.BlockSpec((tm, dm), lambda b, t, p: (b * nt + t, 0))],
        out_specs=pl.BlockSpec(memory_space=pl.ANY),
        scratch_shapes=[pltpu.VMEM((MOE_BLOCK, dm), F32), pltpu.SemaphoreType.DMA(()),
                        pltpu.SemaphoreType.DMA(())],
    )
    return pl.pallas_call(
        _scatter_kernel,
        grid_spec=grid_spec,
        out_shape=jax.ShapeDtypeStruct((n_slots, dm), F32),
        compiler_params=_params(("arbitrary", "arbitrary"), has_side_effects=True),
        name="moe_scatter",
    )(partial, slot, h2.reshape(bsz * rows, dm))


def _expert_kernel(be_ref, nu_ref, x_ref, w1_ref, w3_ref, w2_ref, y_ref, w1b, w3b, w2b):
    i = pl.program_id(0)

    @pl.when(i < nu_ref[0])
    def _():
        @pl.when((i == 0) | (be_ref[i] != be_ref[jnp.maximum(i - 1, 0)]))
        def _():
            w1b[...] = w1_ref[...].astype(BF16)
            w3b[...] = w3_ref[...].astype(BF16)
            w2b[...] = w2_ref[...].astype(BF16)

        xb = x_ref[...].astype(BF16)
        u = _dot(xb, w1b[...])
        hmid = (u * _sigmoid(u)) * _dot(xb, w3b[...])
        y_ref[...] = _dot(hmid.astype(BF16), w2b[...])

    @pl.when(i >= nu_ref[0])
    def _():
        y_ref[...] = jnp.zeros_like(y_ref)


def _experts(block_e, n_used, x_disp, w1, w3, w2, layer):
    n_slots, dm = x_disp.shape
    n_blocks = n_slots // MOE_BLOCK
    wspec = lambda shape: pl.BlockSpec((None, None) + shape, lambda i, be, nu: (layer, be[i], 0, 0))
    grid_spec = pltpu.PrefetchScalarGridSpec(
        num_scalar_prefetch=2,
        grid=(n_blocks,),
        in_specs=[pl.BlockSpec((MOE_BLOCK, dm), lambda i, be, nu: (jnp.minimum(i, nu[0] - 1), 0)),
                  wspec((dm, D_EXPERT)), wspec((dm, D_EXPERT)), wspec((D_EXPERT, dm))],
        out_specs=pl.BlockSpec((MOE_BLOCK, dm), lambda i, be, nu: (i, 0)),
        scratch_shapes=[pltpu.VMEM((dm, D_EXPERT), BF16), pltpu.VMEM((dm, D_EXPERT), BF16),
                        pltpu.VMEM((D_EXPERT, dm), BF16)],
    )
    return pl.pallas_call(
        _expert_kernel,
        grid_spec=grid_spec,
        out_shape=jax.ShapeDtypeStruct((n_slots, dm), F32),
        compiler_params=_params(("arbitrary",)),
        name="moe_experts",
    )(block_e, n_used, x_disp, w1, w3, w2)


def _combine_kernel(slot_ref, slot_next_ref, x_ref, mod_ref, rt_ref, yd_ref, fg_ref, o_ref, ybuf, sems):
    def body(y0, y1):
        o_ref[...] = _rms(_moe_residual(x_ref, mod_ref, rt_ref, y0, y1), fg_ref[...])

    _with_gathered_rows(slot_ref, slot_next_ref, yd_ref, ybuf, sems, body)


def _combine_final(slot, x, mod, route, y_disp, fg, tm, ctx_tiles, row0_tiles):
    bsz, rows, dm = x.shape
    nt = rows // tm
    n = bsz * nt
    row = lambda i: (i // nt, i % nt, 0)
    return pl.pallas_call(
        _combine_kernel,
        grid=(n,),
        in_specs=[pl.BlockSpec((None, 1, 2 * tm), lambda i: (i, 0, 0), memory_space=pltpu.SMEM),
                  pl.BlockSpec((None, 1, 2 * tm), lambda i: (jnp.minimum(i + 1, n - 1), 0, 0),
                               memory_space=pltpu.SMEM),
                  pl.BlockSpec((None, tm, dm), row),
                  pl.BlockSpec((None, None, 6, dm),
                               lambda i: (i // nt, jnp.where(row0_tiles + i % nt >= ctx_tiles, 1, 0), 0, 0)),
                  pl.BlockSpec((None, tm, LANES), row),
                  pl.BlockSpec(memory_space=pl.ANY),
                  _const_spec((1, dm))],
        out_specs=pl.BlockSpec((None, tm, dm), row),
        out_shape=jax.ShapeDtypeStruct((bsz, rows, dm), F32),
        scratch_shapes=[pltpu.VMEM((2, 2, tm, dm), F32), pltpu.SemaphoreType.DMA((2,))],
        compiler_params=_params(("arbitrary",)),
        name="moe_combine",
    )(slot, slot, x, mod, route, y_disp, fg)


def _rope_tables(ctx, t_lat):
    rows = t_lat // GRID_W
    row = np.repeat(np.arange(rows, dtype=np.float64), GRID_W)
    col = np.tile(np.arange(GRID_W, dtype=np.float64), rows)
    inv = ROPE_BASE ** (-np.arange(ROPE_PAIRS, dtype=np.float64) / ROPE_PAIRS)
    ang_r = row[:, None] * inv
    ang_c = col[:, None] * inv
    cos64 = np.concatenate([np.cos(ang_r), np.cos(ang_r), np.cos(ang_c), np.cos(ang_c)], axis=1)
    sin64 = np.concatenate([-np.sin(ang_r), np.sin(ang_r), -np.sin(ang_c), np.sin(ang_c)], axis=1)
    cos = np.tile(cos64, (1, MIX_W // 64))
    sin = np.tile(sin64, (1, MIX_W // 64))
    cos = np.concatenate([np.ones((ctx, MIX_W)), cos], axis=0).astype(np.float32)
    sin = np.concatenate([np.zeros((ctx, MIX_W)), sin], axis=0).astype(np.float32)
    return jnp.asarray(cos), jnp.asarray(sin)


def _pack_w_in(w):
    dm = w.shape[0]
    a_end = PA_W
    b_end = a_end + PB_W
    g_end = b_end + 4 * HEADS
    pad = jnp.zeros((dm, PG_W - 4 * HEADS), w.dtype)
    return jnp.concatenate([w[:, :b_end], w[:, b_end:g_end], pad, w[:, g_end:]], axis=1).astype(BF16)


def kernel(x, c, ctx, c_ctx, ada_w, ada_b, norm1_g, norm2_g, w_in, mlstm_conv_w, mlstm_conv_b, mlstm_gate_b,
           hgrn_lb_raw, hgrn_norm_g, mlstm_norm_g, diff_norm_g, diff_lambda, w_branch, w_gate, b_gate, w_out,
           router_g_w, router_g_b, router_e_w, router_e_b, moe_w1, moe_w3, moe_w2, final_g):
    bsz, t_lat, dm = x.shape
    t_ctx = ctx.shape[1]
    depth = ada_w.shape[0]
    tt = t_ctx + t_lat
    tm = min(256, t_ctx)
    hgrn_chunk = min(128, t_ctx)
    mlstm_chunk = min(256, t_ctx)
    assert t_ctx % tm == 0 and t_lat % tm == 0 and t_lat % GRID_W == 0
    assert t_ctx % hgrn_chunk == 0 and t_ctx % mlstm_chunk == 0
    ctx_tiles = t_ctx // tm

    n_rows = -(-(bsz + 1) // 8) * 8
    cc = jnp.zeros((n_rows, dm), F32).at[:bsz].set(c).at[bsz].set(c_ctx)
    mods = _ada_mod(cc, ada_w, ada_b).reshape(depth, n_rows, 6, dm)

    cos, sin = _rope_tables(t_ctx, t_lat)
    lb_cum = jnp.cumsum(jax.nn.softmax(hgrn_lb_raw.astype(F32), axis=0), axis=0)
    lower = lb_cum - lb_cum[0]

    xc = jnp.concatenate([ctx, x], axis=1)
    moe = None
    for l in range(depth):
        with_ctx = l < depth - 1
        lam_init = 0.8 - 0.6 * math.exp(-0.3 * l)
        mod = jnp.stack([jnp.broadcast_to(mods[l, bsz], (bsz, 6, dm)), mods[l, :bsz]], axis=1)
        n1 = norm1_g[l].reshape(1, dm)
        n2 = norm2_g[l].reshape(1, dm)

        lb = lower[l]
        pad = jnp.zeros((2, 6, MIX_W), F32)
        lbc = jnp.concatenate([jnp.log(lb)[:, None], jnp.log1p(-lb)[:, None], pad], axis=1)
        if moe is None:
            pa, pb, pg, qc, kc, vc = _norm_proj(xc, mod, n1, _pack_w_in(w_in[l]), cos, sin, lbc, tm, ctx_tiles)
        else:
            xc, pa, pb, pg, qc, kc, vc = _norm_proj_fused(moe, mod, n1, _pack_w_in(w_in[l]), cos, sin, lbc,
                                                          tm, ctx_tiles)

        oaf, oab = _hgrn(pa, hgrn_chunk, t_ctx // hgrn_chunk)

        qk = _conv_silu(pb, mlstm_conv_w[l], mlstm_conv_b[l], t_ctx, 256)
        gbias = jnp.zeros((1, PG_W), F32).at[0, :4 * HEADS].set(mlstm_gate_b[l].reshape(-1))
        obf, obb = _mlstm(qk, pb, pg, gbias, mlstm_chunk, t_ctx // mlstm_chunk)

        lam_vec = diff_lambda[l].astype(F32)
        oc_lat = _attn_lat(lam_vec, qc, kc, vc, lam_init, tm, t_ctx)
        oc_ctx = _attn_ctx(lam_vec, qc, kc, vc, lam_init, tm, t_ctx) if with_ctx else oc_lat

        hg = jnp.stack([hgrn_norm_g[l], mlstm_norm_g[l], diff_norm_g[l]])
        rw = jnp.zeros((LANES, dm), F32).at[:N_GROUPS].set(router_g_w[l].T)
        rw = rw.at[N_GROUPS:N_GROUPS + N_EXPERTS].set(router_e_w[l].T)
        rw_hi = rw.astype(BF16)
        rw = jnp.stack([rw_hi, (rw - rw_hi.astype(F32)).astype(BF16)])
        rb = jnp.zeros((LANES, 1), F32).at[:N_GROUPS, 0].set(router_g_b[l])
        rb = rb.at[N_GROUPS:N_GROUPS + N_EXPERTS, 0].set(router_e_b[l])
        row0_tiles = 0 if with_ctx else ctx_tiles
        xn, h2, route, route_rows, counts = _merge(xc, mod, n1, n2, oaf, oab, pa, obf, obb, pb, oc_ctx, oc_lat, hg,
                                                   w_gate[l].astype(BF16), b_gate[l].reshape(1, 3 * dm),
                                                   w_branch[l].astype(BF16), w_out[l].astype(BF16), rw, rb,
                                                   lam_init, tm, ctx_tiles, row0_tiles)

        slot, block_e, n_used, partial = _slot_tables(route_rows, counts)
        x_disp = _scatter(partial, slot, h2, partial.shape[0] * MOE_BLOCK, tm)
        y_disp = _experts(block_e, n_used, x_disp, moe_w1, moe_w3, moe_w2, l)
        moe = (slot, xn, mod, route, y_disp)
    slot, xn, mod, route, y_disp = moe
    return _combine_final(slot, xn, mod, route, y_disp, final_g.reshape(1, dm), tm, ctx_tiles, ctx_tiles)
```

```python
import functools
import math

import numpy as np
import jax
import jax.numpy as jnp
from jax import lax
from jax.experimental import pallas as pl
from jax.experimental.pallas import tpu as pltpu

F32 = jnp.float32
BF16 = jnp.bfloat16

EPS = 1e-6
NEG_BIG = -1e30
HEADS = 4
HEAD_W = 128
MIX_W = HEADS * HEAD_W
GRID_W = 64
ROPE_BASE = 10000.0
ROPE_PAIRS = 16
B_CONV = 3
N_GROUPS = 4
EXPERTS_PER_GROUP = 8
N_EXPERTS = N_GROUPS * EXPERTS_PER_GROUP
D_EXPERT = 512
MOE_BLOCK = 256
LANES = 128
VMEM_LIMIT = 50 * 1024 * 1024

PA_W = 5 * MIX_W
PB_W = 4 * MIX_W
PG_W = LANES
PC_W = 3 * MIX_W
PROJ_W = PA_W + PB_W + PG_W + PC_W


def _params(sem, **kw):
    return pltpu.CompilerParams(dimension_semantics=sem, vmem_limit_bytes=VMEM_LIMIT, **kw)


def _const_spec(shape):
    nd = len(shape)
    return pl.BlockSpec(shape, lambda *_: (0,) * nd, pipeline_mode=pl.Buffered(1))


def _dot(a, b):
    return jnp.dot(a, b, preferred_element_type=F32)


def _dot_nt(a, b):
    return lax.dot_general(a, b, (((1,), (1,)), ((), ())), preferred_element_type=F32)


def _dot_tn(a, b):
    return lax.dot_general(a, b, (((0,), (0,)), ((), ())), preferred_element_type=F32)


def _dot_sel(m_bf16, x):
    hi = x.astype(BF16)
    r1 = x - hi.astype(F32)
    mid = r1.astype(BF16)
    lo = (r1 - mid.astype(F32)).astype(BF16)
    return _dot(m_bf16, hi) + _dot(m_bf16, mid) + _dot(m_bf16, lo)


def _sigmoid(x):
    return 1.0 / (1.0 + jnp.exp(-x))


def _log_sigmoid(x):
    return jnp.minimum(x, 0.0) - jnp.log1p(jnp.exp(-jnp.abs(x)))


def _rms(x, g):
    return x * lax.rsqrt(jnp.mean(x * x, axis=-1, keepdims=True) + EPS) * g


def _ada_kernel(c_ref, w_ref, b_ref, o_ref):
    c = c_ref[...]
    s = c * _sigmoid(c)
    o_ref[...] = jnp.dot(s, w_ref[...], preferred_element_type=F32,
                         precision=lax.Precision.HIGHEST) + b_ref[...]


def _ada_mod(cc, ada_w, ada_b):
    depth, dm, six = ada_w.shape
    rows = cc.shape[0]
    tn = dm
    return pl.pallas_call(
        _ada_kernel,
        grid=(depth, six // tn),
        in_specs=[pl.BlockSpec((rows, dm), lambda l, n: (0, 0)),
                  pl.BlockSpec((None, dm, tn), lambda l, n: (l, 0, n)),
                  pl.BlockSpec((None, 1, tn), lambda l, n: (l, 0, n))],
        out_specs=pl.BlockSpec((None, rows, tn), lambda l, n: (l, 0, n)),
        out_shape=jax.ShapeDtypeStruct((depth, rows, six), F32),
        compiler_params=_params(("parallel", "parallel")),
        name="ada_mod",
    )(cc, ada_w, ada_b.reshape(depth, 1, six))


def _rope(x, cos, sin):
    n = x.shape[-1]
    lane = lax.broadcasted_iota(jnp.int32, (1, n), 1)
    first = (lane // ROPE_PAIRS) % 2 == 0
    partner = jnp.where(first, pltpu.roll(x, n - ROPE_PAIRS, 1), pltpu.roll(x, ROPE_PAIRS, 1))
    return x * cos + partner * sin


def _hgrn_log_forget(z, lbc):
    lsig = jnp.minimum(z, 0.0) - jnp.log(1.0 + jnp.exp(-jnp.abs(z)))
    a = lbc[0:1]
    bb = lbc[1:2] + lsig
    return jnp.maximum(a, bb) + jnp.log(1.0 + jnp.exp(-jnp.abs(a - bb)))


def _with_gathered_rows(slot_ref, slot_next_ref, yd_ref, ybuf, sems, body):
    i = pl.program_id(0)
    n = pl.num_programs(0)
    tm = ybuf.shape[2]

    def start(table_ref, b):
        _row_copies(tm, lambda r, k: pltpu.make_async_copy(
            yd_ref.at[pl.ds(table_ref[0, k * tm + r], 1), :], ybuf.at[b, k, pl.ds(r, 1), :], sems.at[b]))

    def wait(b):
        for k in range(2):
            pltpu.make_async_copy(yd_ref.at[pl.ds(0, tm), :], ybuf.at[b, k], sems.at[b]).wait()

    @pl.when(i == 0)
    def _():
        start(slot_ref, 0)

    for cur in range(2):
        @pl.when(i % 2 == cur)
        def _():
            wait(cur)
            start(slot_next_ref, 1 - cur)
            body(ybuf[cur, 0], ybuf[cur, 1])

            @pl.when(i == n - 1)
            def _():
                wait(1 - cur)


def _moe_residual(x_ref, mod_ref, rt_ref, y0, y1):
    rt = rt_ref[...]
    return x_ref[...] + mod_ref[5:6, :] * (rt[:, 2:3] * y0 + rt[:, 3:4] * y1)


def _norm_proj_fused_kernel(slot_ref, slot_next_ref, xp_ref, modp_ref, rt_ref, yd_ref,
                            mod_ref, g_ref, w_ref, cos_ref, sin_ref, lbc_ref,
                            xo_ref, pa_ref, pb_ref, pg_ref, qc_ref, kc_ref, vc_ref, ybuf, sems):
    def body(y0, y1):
        x = _moe_residual(xp_ref, modp_ref, rt_ref, y0, y1)
        xo_ref[...] = x
        _project(x, mod_ref, g_ref, w_ref, cos_ref, sin_ref, lbc_ref,
                 pa_ref, pb_ref, pg_ref, qc_ref, kc_ref, vc_ref)

    _with_gathered_rows(slot_ref, slot_next_ref, yd_ref, ybuf, sems, body)


def _norm_proj_kernel(xc_ref, xl_ref, mod_ref, g_ref, w_ref, cos_ref, sin_ref, lbc_ref,
                      pa_ref, pb_ref, pg_ref, qc_ref, kc_ref, vc_ref, *, ctx_tiles):
    x = jnp.where(pl.program_id(1) < ctx_tiles, xc_ref[...], xl_ref[...])
    _project(x, mod_ref, g_ref, w_ref, cos_ref, sin_ref, lbc_ref,
             pa_ref, pb_ref, pg_ref, qc_ref, kc_ref, vc_ref)


def _project(x, mod_ref, g_ref, w_ref, cos_ref, sin_ref, lbc_ref, pa_ref, pb_ref, pg_ref, qc_ref, kc_ref, vc_ref):
    mod = mod_ref[...]
    h = _rms(x, g_ref[...]) * (1.0 + mod[1:2]) + mod[0:1]
    hb = h.astype(BF16)
    pa_ref[:, 0:3 * MIX_W] = _dot(hb, w_ref[:, 0:3 * MIX_W])
    for d in range(2):
        cs = slice((3 + d) * MIX_W, (4 + d) * MIX_W)
        pa_ref[:, cs] = _hgrn_log_forget(_dot(hb, w_ref[:, cs]), lbc_ref[d])
    pb_ref[...] = _dot(hb, w_ref[:, PA_W:PA_W + PB_W])
    pg_ref[...] = _dot(hb, w_ref[:, PA_W + PB_W:PA_W + PB_W + PG_W])
    c0 = PA_W + PB_W + PG_W
    cos = cos_ref[...]
    sin = sin_ref[...]
    q = _dot(hb, w_ref[:, c0:c0 + MIX_W])
    qc_ref[...] = (_rope(q, cos, sin) * (64.0 ** -0.5 * math.log2(math.e))).astype(BF16)
    k = _dot(hb, w_ref[:, c0 + MIX_W:c0 + 2 * MIX_W])
    kc_ref[...] = _rope(k, cos, sin).astype(BF16)
    vc_ref[...] = _dot(hb, w_ref[:, c0 + 2 * MIX_W:c0 + 3 * MIX_W]).astype(BF16)


def _norm_proj_fused(moe, mod, g, w, cos, sin, lbc, tm, ctx_tiles):
    slot, xp, modp, route, y_disp = moe
    bsz, tt, dm = xp.shape
    nt = tt // tm
    n = bsz * nt
    row = lambda i: (i // nt, i % nt, 0)
    mod_row = lambda i: (i // nt, jnp.where(i % nt >= ctx_tiles, 1, 0), 0, 0)
    tab = lambda i: (i % nt, 0)
    outs = [(dm, F32), (PA_W, F32), (PB_W, F32), (PG_W, F32), (MIX_W, BF16), (MIX_W, BF16), (MIX_W, BF16)]
    return pl.pallas_call(
        _norm_proj_fused_kernel,
        grid=(n,),
        in_specs=[pl.BlockSpec((None, 1, 2 * tm), lambda i: (i, 0, 0), memory_space=pltpu.SMEM),
                  pl.BlockSpec((None, 1, 2 * tm), lambda i: (jnp.minimum(i + 1, n - 1), 0, 0),
                               memory_space=pltpu.SMEM),
                  pl.BlockSpec((None, tm, dm), row),
                  pl.BlockSpec((None, None, 6, dm), mod_row),
                  pl.BlockSpec((None, tm, LANES), row),
                  pl.BlockSpec(memory_space=pl.ANY),
                  pl.BlockSpec((None, None, 6, dm), mod_row),
                  _const_spec((1, dm)),
                  _const_spec((dm, PROJ_W)),
                  pl.BlockSpec((tm, MIX_W), tab),
                  pl.BlockSpec((tm, MIX_W), tab),
                  _const_spec(lbc.shape)],
        out_specs=[pl.BlockSpec((None, tm, wd), row) for wd, _ in outs],
        out_shape=[jax.ShapeDtypeStruct((bsz, tt, wd), dt) for wd, dt in outs],
        scratch_shapes=[pltpu.VMEM((2, 2, tm, dm), F32), pltpu.SemaphoreType.DMA((2,))],
        compiler_params=_params(("arbitrary",)),
        name="combine_norm_proj",
    )(slot, slot, xp, modp, route, y_disp, mod, g, w, cos, sin, lbc)


def _norm_proj(x_ctx, x_lat, mod, g, w, cos, sin, lbc, tm, ctx_tiles):
    bsz, t_ctx, dm = x_ctx.shape
    tt = t_ctx + x_lat.shape[1]
    nt = tt // tm
    row = lambda b, t: (b, t, 0)
    outs = [(PA_W, F32), (PB_W, F32), (PG_W, F32), (MIX_W, BF16), (MIX_W, BF16), (MIX_W, BF16)]
    return pl.pallas_call(
        functools.partial(_norm_proj_kernel, ctx_tiles=ctx_tiles),
        grid=(bsz, nt),
        in_specs=[pl.BlockSpec((None, tm, dm), lambda b, t: (b, jnp.minimum(t, ctx_tiles - 1), 0)),
                  pl.BlockSpec((None, tm, dm), lambda b, t: (b, jnp.maximum(t - ctx_tiles, 0), 0)),
                  pl.BlockSpec((None, None, 6, dm), lambda b, t: (b, jnp.where(t >= ctx_tiles, 1, 0), 0, 0)),
                  _const_spec((1, dm)),
                  _const_spec((dm, PROJ_W)),
                  pl.BlockSpec((tm, MIX_W), lambda b, t: (t, 0)),
                  pl.BlockSpec((tm, MIX_W), lambda b, t: (t, 0)),
                  _const_spec(lbc.shape)],
        out_specs=[pl.BlockSpec((None, tm, wd), row) for wd, _ in outs],
        out_shape=[jax.ShapeDtypeStruct((bsz, tt, wd), dt) for wd, dt in outs],
        compiler_params=_params(("parallel", "parallel")),
        name="norm_proj",
    )(x_ctx, x_lat, mod, g, w, cos, sin, lbc)


def _conv_kernel(x_ref, w_ref, b_ref, o_ref, *, ctx, q_blocks):
    x = x_ref[...]
    tt = x.shape[0]
    row = lax.broadcasted_iota(jnp.int32, (tt, 1), 0)
    prev = jnp.where((row == 0) | (row == ctx), 0.0, pltpu.roll(x, 1, 0))
    nxt = jnp.where((row == ctx - 1) | (row == tt - 1), 0.0, pltpu.roll(x, tt - 1, 0))
    w = w_ref[...]
    y = b_ref[...] + prev * w[0:1] + x * w[1:2] + nxt * w[2:3]
    y = y * _sigmoid(y)
    scale = jnp.where(pl.program_id(1) < q_blocks, HEAD_W ** -0.5, 1.0)
    o_ref[...] = y * scale


def _conv_silu(pb, conv_w, conv_b, ctx, cb):
    bsz, tt, _ = pb.shape
    width = 2 * MIX_W
    return pl.pallas_call(
        functools.partial(_conv_kernel, ctx=ctx, q_blocks=MIX_W // cb),
        grid=(bsz, width // cb),
        in_specs=[pl.BlockSpec((None, tt, cb), lambda b, j: (b, 0, j)),
                  pl.BlockSpec((B_CONV, cb), lambda b, j: (0, j)),
                  pl.BlockSpec((1, cb), lambda b, j: (0, j))],
        out_specs=pl.BlockSpec((None, tt, cb), lambda b, j: (b, 0, j)),
        out_shape=jax.ShapeDtypeStruct((bsz, tt, width), F32),
        compiler_params=_params(("parallel", "parallel")),
        name="mlstm_conv",
    )(pb, conv_w, conv_b.reshape(1, width))


def _chunk_consts(L):
    idx = np.arange(L)
    t, u = idx[:, None], idx[None, :]
    mats = [(u <= t)]
    for h in range(1, N_MATMUL_LEVELS + 1):
        mid = (t // (2 * h)) * (2 * h) + h - 1
        mats.append(((u > mid) & (u <= t)) | ((u > t) & (u <= mid)))
    cum = np.concatenate([m.astype(np.float32) for m in mats], axis=0)
    cum_b = np.concatenate([m[::-1, ::-1].astype(np.float32) for m in mats], axis=0)
    nlev = int(round(math.log2(L)))
    lvl = np.full((L, L), -1, np.int32)
    lvl[idx, idx] = nlev
    for i in range(nlev):
        h = L >> (i + 1)
        same = (t // (2 * h)) == (u // (2 * h))
        lvl[same & (t % (2 * h) >= h) & (u % (2 * h) < h)] = i
    sel = jnp.asarray(np.stack([cum, cum_b]), BF16)
    lvls = jnp.asarray(np.stack([lvl, lvl[::-1, ::-1]]))
    return sel, lvls, nlev


N_MATMUL_LEVELS = 0


def _neg_abs(x):
    return -jnp.abs(x)


def _midpoint_rows(b, h, d):
    L, w = b.shape
    two = 2 * h
    pos = h - 1 if d == 0 else h
    if two % 8 == 0:
        r = b.reshape(L // two, two, w)[:, pos:pos + 1, :]
        return jnp.broadcast_to(r, (L // two, two, w)).reshape(L, w)
    phase = lax.broadcasted_iota(jnp.int32, (L, 1), 0) % two
    ref = b
    for off in range(pos - two + 1, pos + 1):
        if off != 0:
            ref = jnp.where(phase == pos - off, pltpu.roll(b, (L - off) % L, 0), ref)
    return ref


def _scan_blocks(j, nc_ctx, nc):
    jb = jnp.where(j < nc_ctx, nc_ctx - 1 - j, nc - 1 + nc_ctx - j)
    return j, jb


def _hgrn_kernel(qf_ref, vf_ref, ff_ref, qb_ref, vb_ref, fb_ref, sel_ref, lvl_ref,
                 of_ref, ob_ref, st_ref, *, L, nlev):
    @pl.when(pl.program_id(1) == 0)
    def _():
        st_ref[...] = jnp.zeros_like(st_ref)

    row = lax.broadcasted_iota(jnp.int32, (L, 1), 0)
    dirs = ((qf_ref, vf_ref, ff_ref, of_ref), (qb_ref, vb_ref, fb_ref, ob_ref))
    for d, (q_ref, v_ref, f_ref, o_ref) in enumerate(dirs):
        lf = f_ref[...]
        kk = 1.0 - jnp.exp(lf)
        q_all = q_ref[...]
        ex = _dot_sel(sel_ref[d], lf) * math.log2(math.e)
        b_in = ex[0:L]
        lvl = lvl_ref[d]
        last = L - 1 if d == 0 else 0
        b_end = b_in[last:last + 1]
        heads = [slice(hd * HEAD_W, (hd + 1) * HEAD_W) for hd in range(HEADS)]
        qb = q_all.astype(BF16)
        kb = kk.astype(BF16)
        on_diag = lvl == nlev
        scores = [jnp.where(on_diag, _dot_nt(qb[:, cs], kb[:, cs]), 0.0) for cs in heads]
        for i in range(nlev):
            h = L >> (i + 1)
            if h <= N_MATMUL_LEVELS:
                expo = ex[h * L:(h + 1) * L]
            else:
                expo = _neg_abs(b_in - _midpoint_rows(b_in, h, d))
            ei = jnp.exp2(expo)
            is_query = (row % (2 * h) >= h) if d == 0 else (row % (2 * h) < h)
            xe = (jnp.where(is_query, q_all, kk) * ei).astype(BF16)
            at_level = lvl == i
            for hd, cs in enumerate(heads):
                scores[hd] = scores[hd] + jnp.where(at_level, _dot_nt(xe[:, cs], xe[:, cs]), 0.0)
        q_in = (q_all * jnp.exp2(b_in)).astype(BF16)
        k_out = (kk * jnp.exp2(b_end - b_in)).astype(BF16)
        decay = jnp.exp2(b_end)
        for hd, cs in enumerate(heads):
            vb = v_ref[:, cs].astype(BF16)
            st = st_ref[d, hd]
            o_ref[:, cs] = _dot(scores[hd].astype(BF16), vb) + _dot_nt(q_in[:, cs], st.astype(BF16))
            st_ref[d, hd] = st * decay[:, cs] + _dot_tn(vb, k_out[:, cs])


def _hgrn(pa, L, nc_ctx):
    bsz, tt, _ = pa.shape
    nc = tt // L
    sel, lvl, nlev = _chunk_consts(L)

    def spec(col, which):
        return pl.BlockSpec((None, L, MIX_W), lambda b, j: (b, _scan_blocks(j, nc_ctx, nc)[which], col))

    return pl.pallas_call(
        functools.partial(_hgrn_kernel, L=L, nlev=nlev),
        grid=(bsz, nc),
        in_specs=[spec(0, 0), spec(1, 0), spec(3, 0), spec(0, 1), spec(1, 1), spec(4, 1),
                  _const_spec(sel.shape), _const_spec(lvl.shape)],
        out_specs=[spec(0, 0), spec(0, 1)],
        out_shape=[jax.ShapeDtypeStruct((bsz, tt, MIX_W), F32)] * 2,
        scratch_shapes=[pltpu.VMEM((2, HEADS, HEAD_W, HEAD_W), F32)],
        compiler_params=_params(("parallel", "arbitrary")),
        name="hgrn2_scan",
    )(pa, pa, pa, pa, pa, pa, sel, lvl)


def _mlstm_kernel(qf_ref, kf_ref, vf_ref, gf_ref, qb_ref, kb_ref, vb_ref, gb_ref, gbias_ref,
                  sel_ref, lvl_ref, of_ref, ob_ref, c_ref, m_ref, *, L, nlev):
    @pl.when(pl.program_id(1) == 0)
    def _():
        c_ref[...] = jnp.zeros_like(c_ref)
        m_ref[...] = jnp.zeros_like(m_ref)

    lane = lax.broadcasted_iota(jnp.int32, (1, LANES), 1)
    tlane = lax.broadcasted_iota(jnp.int32, (HEADS, L), 1)
    ones_col = jnp.where(lane == 0, 1.0, 0.0).astype(BF16)
    dirs = ((qf_ref, kf_ref, vf_ref, gf_ref, of_ref), (qb_ref, kb_ref, vb_ref, gb_ref, ob_ref))
    for d, (q_ref, k_ref, v_ref, g_ref, o_ref) in enumerate(dirs):
        g = g_ref[...] + gbias_ref[...]
        lf = jnp.where((lane >= 2 * HEADS) & (lane < 4 * HEADS), _log_sigmoid(g), 0.0)
        bcum = _dot_sel(sel_ref[d, 0:L], lf)
        causal = lvl_ref[d] >= 0
        rows = jnp.where(lane < 2 * HEADS, g - pltpu.roll(bcum, LANES - 2 * HEADS, 1), bcum).T
        r4 = rows[d * HEADS:(d + 1) * HEADS]
        b4 = rows[(2 + d) * HEADS:(3 + d) * HEADS]
        m4 = m_ref[d * HEADS:(d + 1) * HEADS, 0:1]
        run = r4
        sh = 1
        while sh < L:
            if d == 0:
                run = jnp.maximum(run, jnp.where(tlane >= sh, pltpu.roll(run, sh, 1), -jnp.inf))
            else:
                run = jnp.maximum(run, jnp.where(tlane < L - sh, pltpu.roll(run, L - sh, 1), -jnp.inf))
            sh *= 2
        c4 = jnp.maximum(run, m4)
        cols = jnp.concatenate([c4, jnp.exp(m4 - c4), jnp.exp(-b4 - c4),
                                jnp.zeros((LANES - 3 * HEADS, L), F32)], axis=0).T
        last = L - 1 if d == 0 else 0
        for hd in range(HEADS):
            cs = slice(hd * HEAD_W, (hd + 1) * HEAD_W)
            row_term = r4[hd:hd + 1, :]
            b_end = b4[hd:hd + 1, last:last + 1]
            r = d * HEADS + hd
            m_prev = m4[hd:hd + 1]
            q = q_ref[:, cs].astype(BF16)
            k = k_ref[:, cs]
            v_aug = jnp.concatenate([v_ref[:, cs].astype(BF16), jnp.broadcast_to(ones_col, (L, LANES))], axis=1)
            c_prev = c_ref[r]
            w_end = b_end + row_term
            m_new = jnp.maximum(b_end + m_prev, jnp.max(w_end, axis=-1, keepdims=True))
            e_end = jnp.exp(w_end - m_new)
            keep = jnp.exp(b_end + m_prev - m_new)
            c_ref[r] = keep * c_prev + _dot((k.T * e_end).astype(BF16), v_aug)
            m_ref[r:r + 1, :] = jnp.broadcast_to(m_new, (1, LANES))
            c_t = cols[:, hd:hd + 1]
            a_state = cols[:, HEADS + hd:HEADS + hd + 1]
            floor = cols[:, 2 * HEADS + hd:2 * HEADS + hd + 1]
            s = _dot_nt(q, k.astype(BF16)) * jnp.exp(jnp.where(causal, row_term - c_t, NEG_BIG))
            num = _dot(s.astype(BF16), v_aug) + a_state * _dot(q, c_prev.astype(BF16))
            den = num[:, HEAD_W:HEAD_W + 1]
            o_ref[:, cs] = num[:, 0:HEAD_W] / jnp.maximum(jnp.abs(den), floor)


def _mlstm(qk, pb, pg, gbias, L, nc_ctx):
    bsz, tt, _ = pb.shape
    nc = tt // L
    sel, lvl, nlev = _chunk_consts(L)

    def spec(col, which, width=MIX_W):
        return pl.BlockSpec((None, L, width), lambda b, j: (b, _scan_blocks(j, nc_ctx, nc)[which], col))

    return pl.pallas_call(
        functools.partial(_mlstm_kernel, L=L, nlev=nlev),
        grid=(bsz, nc),
        in_specs=[spec(0, 0), spec(1, 0), spec(2, 0), spec(0, 0, PG_W),
                  spec(0, 1), spec(1, 1), spec(2, 1), spec(0, 1, PG_W),
                  _const_spec((1, PG_W)), _const_spec(sel.shape), _const_spec(lvl.shape)],
        out_specs=[spec(0, 0), spec(0, 1)],
        out_shape=[jax.ShapeDtypeStruct((bsz, tt, MIX_W), F32)] * 2,
        scratch_shapes=[pltpu.VMEM((2 * HEADS, HEAD_W, 2 * HEAD_W), F32),
                        pltpu.VMEM((2 * HEADS, LANES), F32)],
        compiler_params=_params(("parallel", "arbitrary")),
        name="mlstm_scan",
    )(qk, qk, pb, pg, qk, qk, pb, pg, gbias, sel, lvl)


def _diff_lambda(lam_ref, lam_init):
    lv = lam_ref[...]
    return (jnp.exp(jnp.sum(lv[0:1] * lv[1:2], axis=-1, keepdims=True))
            - jnp.exp(jnp.sum(lv[2:3] * lv[3:4], axis=-1, keepdims=True)) + lam_init)


def _map_scores(q, k):
    lane = lax.broadcasted_iota(jnp.int32, (1, HEAD_W), 1)
    zero = jnp.zeros_like(q)
    return (_dot_nt(jnp.where(lane < HEAD_W // 2, q, zero), k),
            _dot_nt(jnp.where(lane >= HEAD_W // 2, q, zero), k))


def _row_max(s):
    return jnp.max(s, axis=-1, keepdims=True)


def _diff_softmax_v(s1, m1, s2, m2, lam, v):
    p1 = jnp.exp2(s1 - m1)
    p2 = jnp.exp2(s2 - m2)
    l1 = jnp.sum(p1, axis=-1, keepdims=True)
    l2 = jnp.sum(p2, axis=-1, keepdims=True)
    return _dot((p1 - (lam * l1 / l2) * p2).astype(BF16), v) / l1


def _attn_ctx_kernel(lam_ref, q_ref, k_ref, v_ref, o_ref, *, lam_init):
    s1, s2 = _map_scores(q_ref[...], k_ref[...])
    o_ref[...] = _diff_softmax_v(s1, _row_max(s1), s2, _row_max(s2), _diff_lambda(lam_ref, lam_init), v_ref[...])


def _attn_ctx(lam_vec, qc, kc, vc, lam_init, tq, t_ctx):
    bsz = qc.shape[0]
    kv = pl.BlockSpec((None, t_ctx, HEAD_W), lambda b, h, i: (b, 0, h))
    return pl.pallas_call(
        functools.partial(_attn_ctx_kernel, lam_init=lam_init),
        grid=(bsz, HEADS, t_ctx // tq),
        in_specs=[_const_spec(lam_vec.shape),
                  pl.BlockSpec((None, tq, HEAD_W), lambda b, h, i: (b, i, h)), kv, kv],
        out_specs=pl.BlockSpec((None, tq, HEAD_W), lambda b, h, i: (b, i, h)),
        out_shape=jax.ShapeDtypeStruct((bsz, t_ctx, MIX_W), F32),
        compiler_params=_params(("parallel", "parallel", "arbitrary")),
        name="diff_attn_ctx",
    )(lam_vec, qc, kc, vc)


def _attn_lat_kernel(lam_ref, q_ref, k_ref, v_ref, o_ref, s_even, m_even, s_odd, m_odd, *, lam_init, n_tiles):
    j = pl.program_id(2)
    lam = _diff_lambda(lam_ref, lam_init)
    even, odd = (s_even, m_even), (s_odd, m_odd)

    def scores_into(bufs):
        s_buf, m_buf = bufs
        for i, s in enumerate(_map_scores(q_ref[...], k_ref[...])):
            s_buf[i] = s
            m_buf[i] = jnp.broadcast_to(_row_max(s), m_buf.shape[1:])

    def finish_from(bufs):
        s_buf, m_buf = bufs
        o_ref[...] = _diff_softmax_v(s_buf[0], m_buf[0][:, 0:1], s_buf[1], m_buf[1][:, 0:1], lam, v_ref[...])

    @pl.when(j == 0)
    def _():
        scores_into(even)

    middle = (j > 0) & (j < n_tiles)

    @pl.when(middle & (j % 2 == 1))
    def _():
        scores_into(odd)
        finish_from(even)

    @pl.when(middle & (j % 2 == 0))
    def _():
        scores_into(even)
        finish_from(odd)

    @pl.when(j == n_tiles)
    def _():
        finish_from(odd if n_tiles % 2 == 0 else even)


def _attn_lat(lam_vec, qc, kc, vc, lam_init, tq, t_ctx):
    bsz, tt, _ = qc.shape
    n_tiles = (tt - t_ctx) // tq
    q0 = t_ctx // tq
    kv = pl.BlockSpec((None, tt, HEAD_W), lambda b, h, j: (b, 0, h))
    return pl.pallas_call(
        functools.partial(_attn_lat_kernel, lam_init=lam_init, n_tiles=n_tiles),
        grid=(bsz, HEADS, n_tiles + 1),
        in_specs=[_const_spec(lam_vec.shape),
                  pl.BlockSpec((None, tq, HEAD_W), lambda b, h, j: (b, q0 + jnp.minimum(j, n_tiles - 1), h)),
                  kv, kv],
        out_specs=pl.BlockSpec((None, tq, HEAD_W), lambda b, h, j: (b, jnp.maximum(j - 1, 0), h)),
        out_shape=jax.ShapeDtypeStruct((bsz, tt - t_ctx, MIX_W), F32),
        scratch_shapes=[pltpu.VMEM((2, tq, tt), F32), pltpu.VMEM((2, tq, LANES), F32),
                        pltpu.VMEM((2, tq, tt), F32), pltpu.VMEM((2, tq, LANES), F32)],
        compiler_params=_params(("parallel", "parallel", "arbitrary")),
        name="diff_attn_lat",
    )(lam_vec, qc, kc, vc)


def _head_norm(x, g):
    parts = []
    for hd in range(HEADS):
        xs = x[:, hd * HEAD_W:(hd + 1) * HEAD_W]
        parts.append(xs * lax.rsqrt(jnp.mean(xs * xs, axis=-1, keepdims=True) + EPS))
    return jnp.concatenate(parts, axis=1) * g


ROUTE_ROWS = 40


def _route(lg):
    n = lg.shape[1]
    row = lax.broadcasted_iota(jnp.int32, lg.shape, 0)
    neg = -jnp.inf
    big = ROUTE_ROWS
    gl = jnp.where(row < N_GROUPS, lg, neg)
    gmax = jnp.max(gl, axis=0, keepdims=True)
    gidx = jnp.min(jnp.where(gl == gmax, row, big), axis=0, keepdims=True)
    p_group = 1.0 / jnp.sum(jnp.exp(gl - gmax), axis=0, keepdims=True)
    lo = N_GROUPS + EXPERTS_PER_GROUP * gidx
    in_grp = (row >= lo) & (row < lo + EXPERTS_PER_GROUP)
    el = jnp.where(in_grp, lg, neg)
    pe = jnp.exp(el - jnp.max(el, axis=0, keepdims=True))
    pe = pe / jnp.sum(pe, axis=0, keepdims=True)
    pe = jnp.where(in_grp, pe, -1.0)
    v1 = jnp.max(pe, axis=0, keepdims=True)
    i1 = jnp.min(jnp.where(pe == v1, row, big), axis=0, keepdims=True)
    pe2 = jnp.where(row == i1, -1.0, pe)
    v2 = jnp.max(pe2, axis=0, keepdims=True)
    i2 = jnp.min(jnp.where(pe2 == v2, row, big), axis=0, keepdims=True)
    scale = p_group / (v1 + v2)
    erow = lax.broadcasted_iota(jnp.int32, (N_EXPERTS, n), 0) + N_GROUPS
    oh1 = erow == i1
    oh2 = erow == i2
    oh1f = jnp.where(oh1, 1.0, 0.0)
    oh2f = jnp.where(oh2, 1.0, 0.0)
    earlier = (lax.broadcasted_iota(jnp.int32, (n, n), 0) < lax.broadcasted_iota(jnp.int32, (n, n), 1))
    earlier = jnp.where(earlier, 1.0, 0.0).astype(BF16)
    tot1 = jnp.sum(oh1f, axis=1, keepdims=True)
    tot2 = jnp.sum(oh2f, axis=1, keepdims=True)
    rank1 = jnp.sum(jnp.where(oh1, _dot(oh1f.astype(BF16), earlier), 0.0), axis=0, keepdims=True)
    rank2 = jnp.sum(jnp.where(oh2, _dot(oh2f.astype(BF16), earlier) + tot1, 0.0), axis=0, keepdims=True)
    r8 = lax.broadcasted_iota(jnp.int32, (8, n), 0)
    rows = jnp.zeros((8, n), F32)
    for i, val in enumerate(((i1 - N_GROUPS).astype(F32), (i2 - N_GROUPS).astype(F32), v1 * scale, v2 * scale,
                             rank1, rank2)):
        rows = jnp.where(r8 == i, val, rows)
    return rows, tot1 + tot2


def _merge_kernel(xc_ref, xl_ref, mod_ref, n1_ref, n2_ref, af_ref, ab_ref, ga_ref, bf_ref, bb_ref, gb_ref,
                  occ_ref, ocl_ref, hg_ref, wg_ref, bg_ref, wbr_ref, wo_ref, rw_ref, rb_ref,
                  xo_ref, h2_ref, rt_ref, rtt_ref, cnt_ref, *, lam_init, ctx_tiles, row0_tiles):
    is_ctx_tile = row0_tiles + pl.program_id(1) < ctx_tiles
    x = jnp.where(is_ctx_tile, xc_ref[...], xl_ref[...])
    mod = mod_ref[...]
    dm = x.shape[-1]
    hb = (_rms(x, n1_ref[...]) * (1.0 + mod[1:2]) + mod[0:1]).astype(BF16)
    hg = hg_ref[...]
    ga = ga_ref[...]
    ya = _head_norm(af_ref[...] + ab_ref[...], hg[0:1]) * (ga * _sigmoid(ga))
    yb = _head_norm(bf_ref[...] + bb_ref[...], hg[1:2]) * _sigmoid(gb_ref[...])
    oc = jnp.where(is_ctx_tile, occ_ref[...], ocl_ref[...])
    yc = _head_norm(oc, hg[2:3]) * (1.0 - lam_init)
    y = jnp.zeros_like(x)
    for i, yi in enumerate((ya, yb, yc)):
        gate = _sigmoid(_dot(hb, wg_ref[:, i * dm:(i + 1) * dm]) + bg_ref[:, i * dm:(i + 1) * dm])
        y = y + gate * _dot(yi.astype(BF16), wbr_ref[i])
    xn = x + mod[2:3] * _dot(y.astype(BF16), wo_ref[...])
    xo_ref[...] = xn
    h2 = _rms(xn, n2_ref[...]) * (1.0 + mod[4:5]) + mod[3:4]
    h2_ref[...] = h2
    h_hi = h2.astype(BF16)
    h_mid = (h2 - h_hi.astype(F32)).astype(BF16)
    logits = (_dot_nt(rw_ref[0], h_hi) + _dot_nt(rw_ref[0], h_mid) + _dot_nt(rw_ref[1], h_hi)) + rb_ref[...]
    rows, counts = _route(logits[0:ROUTE_ROWS, :])
    rtt_ref[...] = rows
    cnt_ref[...] = jnp.broadcast_to(counts, cnt_ref.shape)
    tm = x.shape[0]
    rt_ref[...] = jnp.concatenate([rows, jnp.zeros((LANES - 8, tm), F32)], axis=0).T


def _merge(x_ctx, x_lat, lat_row0, mod, n1, n2, oaf, oab, pa, obf, obb, pb, oc_ctx, oc_lat, hg, wg, bg, wbr, wo,
           rw, rb, lam_init, tm, ctx_tiles, row0_tiles):
    bsz, tt, _ = pa.shape
    dm = x_lat.shape[-1]
    nt = tt // tm - row0_tiles
    rows = nt * tm

    def col(c):
        return pl.BlockSpec((None, tm, MIX_W), lambda b, t: (b, row0_tiles + t, c))

    dst = lambda b, t: (b, t, 0)
    outs = [(dm, F32), (dm, F32), (LANES, F32)]
    out_specs = [pl.BlockSpec((None, tm, wd), dst) for wd, _ in outs]
    out_shape = [jax.ShapeDtypeStruct((bsz, rows, wd), dt) for wd, dt in outs]
    out_specs.append(pl.BlockSpec((None, 8, tm), lambda b, t: (b * nt + t, 0, 0)))
    out_shape.append(jax.ShapeDtypeStruct((bsz * nt, 8, tm), F32))
    out_specs.append(pl.BlockSpec((None, N_EXPERTS, LANES), lambda b, t: (b * nt + t, 0, 0)))
    out_shape.append(jax.ShapeDtypeStruct((bsz * nt, N_EXPERTS, LANES), F32))
    return pl.pallas_call(
        functools.partial(_merge_kernel, lam_init=lam_init, ctx_tiles=ctx_tiles, row0_tiles=row0_tiles),
        grid=(bsz, nt),
        in_specs=[pl.BlockSpec((None, tm, dm),
                               lambda b, t: (b, jnp.minimum(row0_tiles + t, max(ctx_tiles - 1, 0)), 0)),
                  pl.BlockSpec((None, tm, dm),
                               lambda b, t: (b, lat_row0 + jnp.maximum(row0_tiles + t - ctx_tiles, 0), 0)),
                  pl.BlockSpec((None, None, 6, dm),
                               lambda b, t: (b, jnp.where(row0_tiles + t >= ctx_tiles, 1, 0), 0, 0)),
                  _const_spec((1, dm)), _const_spec((1, dm)),
                  col(0), col(0), col(2), col(0), col(0), col(3),
                  pl.BlockSpec((None, tm, MIX_W),
                               lambda b, t: (b, jnp.minimum(row0_tiles + t, max(ctx_tiles - 1, 0)), 0)),
                  pl.BlockSpec((None, tm, MIX_W),
                               lambda b, t: (b, jnp.maximum(row0_tiles + t - ctx_tiles, 0), 0)),
                  _const_spec((3, MIX_W)), _const_spec(wg.shape), _const_spec(bg.shape),
                  _const_spec(wbr.shape), _const_spec(wo.shape), _const_spec(rw.shape), _const_spec(rb.shape)],
        out_specs=out_specs,
        out_shape=out_shape,
        compiler_params=_params(("parallel", "parallel")),
        name="merge_route",
    )(x_ctx, x_lat, mod, n1, n2, oaf, oab, pa, obf, obb, pb, oc_ctx, oc_lat, hg, wg, bg, wbr, wo, rw, rb)


def _slot_tables(route_rows, counts):
    tiles, _, tm = route_rows.shape
    cnt = counts[:, :, 0].astype(jnp.int32)
    tile_off = jnp.cumsum(cnt, axis=0) - cnt
    total = jnp.sum(cnt, axis=0)
    padded = (total + MOE_BLOCK - 1) // MOE_BLOCK * MOE_BLOCK
    p_end = jnp.cumsum(padded)
    base = (p_end - padded)[None, :] + tile_off
    e = route_rows[:, 0:2, :].astype(jnp.int32)
    rank = route_rows[:, 4:6, :].astype(jnp.int32)
    hit = e[:, :, :, None] == jnp.arange(N_EXPERTS, dtype=jnp.int32)
    slot = jnp.sum(jnp.where(hit, base[:, None, None, :], 0), axis=-1) + rank
    n_blocks = -(-(2 * tiles * tm) // MOE_BLOCK) + N_EXPERTS
    blk_start = jnp.arange(n_blocks, dtype=jnp.int32) * MOE_BLOCK
    block_e = jnp.minimum(jnp.sum(blk_start[:, None] >= p_end[None, :], axis=1), N_EXPERTS - 1).astype(jnp.int32)
    n_used = (p_end[-1] // MOE_BLOCK).astype(jnp.int32).reshape(1)
    filled = jnp.clip(((p_end - padded) + total)[block_e] - blk_start, 0, MOE_BLOCK)
    filled = jnp.where(blk_start < p_end[-1], filled, 0)
    partial = (filled < MOE_BLOCK).astype(jnp.int32)
    return slot.reshape(tiles, 1, 2 * tm), block_e, n_used, partial


def _row_copies(n_rows, make):
    for r in range(n_rows):
        for k in range(2):
            make(r, k).start()


def _scatter_kernel(partial_ref, slot_ref, h_ref, xd_ref, zeros, sem, zsem):
    tm = h_ref.shape[0]
    n_blocks = partial_ref.shape[0]

    @pl.when((pl.program_id(0) == 0) & (pl.program_id(1) == 0))
    def _():
        zeros[...] = jnp.zeros_like(zeros)

        def zero_block(blk):
            return pltpu.make_async_copy(zeros, xd_ref.at[pl.ds(blk * MOE_BLOCK, MOE_BLOCK), :], zsem)

        def start(blk, c):
            @pl.when(partial_ref[blk] != 0)
            def _():
                zero_block(blk).start()
            return c

        def wait(blk, c):
            @pl.when(partial_ref[blk] != 0)
            def _():
                zero_block(blk).wait()
            return c

        lax.fori_loop(0, n_blocks, start, 0)
        lax.fori_loop(0, n_blocks, wait, 0)

    _row_copies(tm, lambda r, k: pltpu.make_async_copy(
        h_ref.at[pl.ds(r, 1), :], xd_ref.at[pl.ds(slot_ref[0, k * tm + r], 1), :], sem))
    for _ in range(2):
        pltpu.make_async_copy(h_ref, xd_ref.at[pl.ds(0, tm), :], sem).wait()


def _scatter(partial, slot, h2, n_slots, tm):
    bsz, rows, dm = h2.shape
    nt = rows // tm
    grid_spec = pltpu.PrefetchScalarGridSpec(
        num_scalar_prefetch=1,
        grid=(bsz, nt),
        in_specs=[pl.BlockSpec((None, 1, 2 * tm), lambda b, t, p: (b * nt + t, 0, 0), memory_space=pltpu.SMEM),
                  pl.BlockSpec((tm, dm), lambda b, t, p: (b * nt + t, 0))],
        out_specs=pl.BlockSpec(memory_space=pl.ANY),
        scratch_shapes=[pltpu.VMEM((MOE_BLOCK, dm), F32), pltpu.SemaphoreType.DMA(()),
                        pltpu.SemaphoreType.DMA(())],
    )
    return pl.pallas_call(
        _scatter_kernel,
        grid_spec=grid_spec,
        out_shape=jax.ShapeDtypeStruct((n_slots, dm), F32),
        compiler_params=_params(("arbitrary", "arbitrary"), has_side_effects=True),
        name="moe_scatter",
    )(partial, slot, h2.reshape(bsz * rows, dm))


def _expert_kernel(be_ref, nu_ref, x_ref, w1_ref, w3_ref, w2_ref, y_ref, w1b, w3b, w2b):
    i = pl.program_id(0)

    @pl.when(i < nu_ref[0])
    def _():
        @pl.when((i == 0) | (be_ref[i] != be_ref[jnp.maximum(i - 1, 0)]))
        def _():
            w1b[...] = w1_ref[...].astype(BF16)
            w3b[...] = w3_ref[...].astype(BF16)
            w2b[...] = w2_ref[...].astype(BF16)

        xb = x_ref[...].astype(BF16)
        u = _dot(xb, w1b[...])
        hmid = (u * _sigmoid(u)) * _dot(xb, w3b[...])
        y_ref[...] = _dot(hmid.astype(BF16), w2b[...])

    @pl.when(i >= nu_ref[0])
    def _():
        y_ref[...] = jnp.zeros_like(y_ref)


def _experts(block_e, n_used, x_disp, w1, w3, w2, layer):
    n_slots, dm = x_disp.shape
    n_blocks = n_slots // MOE_BLOCK
    wspec = lambda shape: pl.BlockSpec((None, None) + shape, lambda i, be, nu: (layer, be[i], 0, 0))
    grid_spec = pltpu.PrefetchScalarGridSpec(
        num_scalar_prefetch=2,
        grid=(n_blocks,),
        in_specs=[pl.BlockSpec((MOE_BLOCK, dm), lambda i, be, nu: (jnp.minimum(i, nu[0] - 1), 0)),
                  wspec((dm, D_EXPERT)), wspec((dm, D_EXPERT)), wspec((D_EXPERT, dm))],
        out_specs=pl.BlockSpec((MOE_BLOCK, dm), lambda i, be, nu: (i, 0)),
        scratch_shapes=[pltpu.VMEM((dm, D_EXPERT), BF16), pltpu.VMEM((dm, D_EXPERT), BF16),
                        pltpu.VMEM((D_EXPERT, dm), BF16)],
    )
    return pl.pallas_call(
        _expert_kernel,
        grid_spec=grid_spec,
        out_shape=jax.ShapeDtypeStruct((n_slots, dm), F32),
        compiler_params=_params(("arbitrary",)),
        name="moe_experts",
    )(block_e, n_used, x_disp, w1, w3, w2)


def _combine_kernel(slot_ref, slot_next_ref, x_ref, mod_ref, rt_ref, yd_ref, fg_ref, o_ref, ybuf, sems):
    def body(y0, y1):
        o_ref[...] = _rms(_moe_residual(x_ref, mod_ref, rt_ref, y0, y1), fg_ref[...])

    _with_gathered_rows(slot_ref, slot_next_ref, yd_ref, ybuf, sems, body)


def _combine_final(slot, x, mod, route, y_disp, fg, tm, ctx_tiles, row0_tiles):
    bsz, rows, dm = x.shape
    nt = rows // tm
    n = bsz * nt
    row = lambda i: (i // nt, i % nt, 0)
    return pl.pallas_call(
        _combine_kernel,
        grid=(n,),
        in_specs=[pl.BlockSpec((None, 1, 2 * tm), lambda i: (i, 0, 0), memory_space=pltpu.SMEM),
                  pl.BlockSpec((None, 1, 2 * tm), lambda i: (jnp.minimum(i + 1, n - 1), 0, 0),
                               memory_space=pltpu.SMEM),
                  pl.BlockSpec((None, tm, dm), row),
                  pl.BlockSpec((None, None, 6, dm),
                               lambda i: (i // nt, jnp.where(row0_tiles + i % nt >= ctx_tiles, 1, 0), 0, 0)),
                  pl.BlockSpec((None, tm, LANES), row),
                  pl.BlockSpec(memory_space=pl.ANY),
                  _const_spec((1, dm))],
        out_specs=pl.BlockSpec((None, tm, dm), row),
        out_shape=jax.ShapeDtypeStruct((bsz, rows, dm), F32),
        scratch_shapes=[pltpu.VMEM((2, 2, tm, dm), F32), pltpu.SemaphoreType.DMA((2,))],
        compiler_params=_params(("arbitrary",)),
        name="moe_combine",
    )(slot, slot, x, mod, route, y_disp, fg)


def _rope_tables(ctx, t_lat):
    rows = t_lat // GRID_W
    row = np.repeat(np.arange(rows, dtype=np.float64), GRID_W)
    col = np.tile(np.arange(GRID_W, dtype=np.float64), rows)
    inv = ROPE_BASE ** (-np.arange(ROPE_PAIRS, dtype=np.float64) / ROPE_PAIRS)
    ang_r = row[:, None] * inv
    ang_c = col[:, None] * inv
    cos64 = np.concatenate([np.cos(ang_r), np.cos(ang_r), np.cos(ang_c), np.cos(ang_c)], axis=1)
    sin64 = np.concatenate([-np.sin(ang_r), np.sin(ang_r), -np.sin(ang_c), np.sin(ang_c)], axis=1)
    cos = np.tile(cos64, (1, MIX_W // 64))
    sin = np.tile(sin64, (1, MIX_W // 64))
    cos = np.concatenate([np.ones((ctx, MIX_W)), cos], axis=0).astype(np.float32)
    sin = np.concatenate([np.zeros((ctx, MIX_W)), sin], axis=0).astype(np.float32)
    return jnp.asarray(cos), jnp.asarray(sin)


def _pack_w_in(w):
    dm = w.shape[0]
    a_end = PA_W
    b_end = a_end + PB_W
    g_end = b_end + 4 * HEADS
    pad = jnp.zeros((dm, PG_W - 4 * HEADS), w.dtype)
    return jnp.concatenate([w[:, :b_end], w[:, b_end:g_end], pad, w[:, g_end:]], axis=1).astype(BF16)


def kernel(x, c, ctx, c_ctx, ada_w, ada_b, norm1_g, norm2_g, w_in, mlstm_conv_w, mlstm_conv_b, mlstm_gate_b,
           hgrn_lb_raw, hgrn_norm_g, mlstm_norm_g, diff_norm_g, diff_lambda, w_branch, w_gate, b_gate, w_out,
           router_g_w, router_g_b, router_e_w, router_e_b, moe_w1, moe_w3, moe_w2, final_g):
    bsz, t_lat, dm = x.shape
    t_ctx = ctx.shape[1]
    depth = ada_w.shape[0]
    tt = t_ctx + t_lat
    tm = min(256, t_ctx)
    hgrn_chunk = min(128, t_ctx)
    mlstm_chunk = min(256, t_ctx)
    assert t_ctx % tm == 0 and t_lat % tm == 0 and t_lat % GRID_W == 0
    assert t_ctx % hgrn_chunk == 0 and t_ctx % mlstm_chunk == 0
    ctx_tiles = t_ctx // tm

    n_rows = -(-(bsz + 1) // 8) * 8
    cc = jnp.zeros((n_rows, dm), F32).at[:bsz].set(c).at[bsz].set(c_ctx)
    mods = _ada_mod(cc, ada_w, ada_b).reshape(depth, n_rows, 6, dm)

    cos, sin = _rope_tables(t_ctx, t_lat)
    lb_cum = jnp.cumsum(jax.nn.softmax(hgrn_lb_raw.astype(F32), axis=0), axis=0)
    lower = lb_cum - lb_cum[0]

    x_ctx, x_lat, lat_row0 = ctx, x, 0
    moe = None
    for l in range(depth):
        with_ctx = l < depth - 1
        lam_init = 0.8 - 0.6 * math.exp(-0.3 * l)
        mod = jnp.stack([jnp.broadcast_to(mods[l, bsz], (bsz, 6, dm)), mods[l, :bsz]], axis=1)
        n1 = norm1_g[l].reshape(1, dm)
        n2 = norm2_g[l].reshape(1, dm)

        lb = lower[l]
        pad = jnp.zeros((2, 6, MIX_W), F32)
        lbc = jnp.concatenate([jnp.log(lb)[:, None], jnp.log1p(-lb)[:, None], pad], axis=1)
        if moe is None:
            pa, pb, pg, qc, kc, vc = _norm_proj(x_ctx, x_lat, mod, n1, _pack_w_in(w_in[l]), cos, sin, lbc,
                                                tm, ctx_tiles)
        else:
            xc, pa, pb, pg, qc, kc, vc = _norm_proj_fused(moe, mod, n1, _pack_w_in(w_in[l]), cos, sin, lbc,
                                                          tm, ctx_tiles)
            x_ctx, x_lat, lat_row0 = xc, xc, ctx_tiles

        oaf, oab = _hgrn(pa, hgrn_chunk, t_ctx // hgrn_chunk)

        qk = _conv_silu(pb, mlstm_conv_w[l], mlstm_conv_b[l], t_ctx, 256)
        gbias = jnp.zeros((1, PG_W), F32).at[0, :4 * HEADS].set(mlstm_gate_b[l].reshape(-1))
        obf, obb = _mlstm(qk, pb, pg, gbias, mlstm_chunk, t_ctx // mlstm_chunk)

        lam_vec = diff_lambda[l].astype(F32)
        oc_lat = _attn_lat(lam_vec, qc, kc, vc, lam_init, tm, t_ctx)
        oc_ctx = _attn_ctx(lam_vec, qc, kc, vc, lam_init, tm, t_ctx) if with_ctx else oc_lat

        hg = jnp.stack([hgrn_norm_g[l], mlstm_norm_g[l], diff_norm_g[l]])
        rw = jnp.zeros((LANES, dm), F32).at[:N_GROUPS].set(router_g_w[l].T)
        rw = rw.at[N_GROUPS:N_GROUPS + N_EXPERTS].set(router_e_w[l].T)
        rw_hi = rw.astype(BF16)
        rw = jnp.stack([rw_hi, (rw - rw_hi.astype(F32)).astype(BF16)])
        rb = jnp.zeros((LANES, 1), F32).at[:N_GROUPS, 0].set(router_g_b[l])
        rb = rb.at[N_GROUPS:N_GROUPS + N_EXPERTS, 0].set(router_e_b[l])
        row0_tiles = 0 if with_ctx else ctx_tiles
        xn, h2, route, route_rows, counts = _merge(x_ctx, x_lat, lat_row0, mod, n1, n2, oaf, oab, pa, obf, obb, pb,
                                                   oc_ctx, oc_lat, hg,
                                                   w_gate[l].astype(BF16), b_gate[l].reshape(1, 3 * dm),
                                                   w_branch[l].astype(BF16), w_out[l].astype(BF16), rw, rb,
                                                   lam_init, tm, ctx_tiles, row0_tiles)

        slot, block_e, n_used, partial = _slot_tables(route_rows, counts)
        x_disp = _scatter(partial, slot, h2, partial.shape[0] * MOE_BLOCK, tm)
        y_disp = _experts(block_e, n_used, x_disp, moe_w1, moe_w3, moe_w2, l)
        moe = (slot, xn, mod, route, y_disp)
    slot, xn, mod, route, y_disp = moe
    return _combine_final(slot, xn, mod, route, y_disp, final_g.reshape(1, dm), tm, ctx_tiles, ctx_tiles)
```

```python
import functools
import math

import numpy as np
import jax
import jax.numpy as jnp
from jax import lax
from jax.experimental import pallas as pl
from jax.experimental.pallas import tpu as pltpu

F32 = jnp.float32
BF16 = jnp.bfloat16

EPS = 1e-6
NEG_BIG = -1e30
HEADS = 4
HEAD_W = 128
MIX_W = HEADS * HEAD_W
GRID_W = 64
ROPE_BASE = 10000.0
ROPE_PAIRS = 16
B_CONV = 3
N_GROUPS = 4
EXPERTS_PER_GROUP = 8
N_EXPERTS = N_GROUPS * EXPERTS_PER_GROUP
D_EXPERT = 512
MOE_BLOCK = 256
LANES = 128
VMEM_LIMIT = 50 * 1024 * 1024

PA_W = 5 * MIX_W
PB_W = 4 * MIX_W
PG_W = LANES
PC_W = 3 * MIX_W
PROJ_W = PA_W + PB_W + PG_W + PC_W


def _params(sem, **kw):
    return pltpu.CompilerParams(dimension_semantics=sem, vmem_limit_bytes=VMEM_LIMIT, **kw)


def _const_spec(shape):
    nd = len(shape)
    return pl.BlockSpec(shape, lambda *_: (0,) * nd, pipeline_mode=pl.Buffered(1))


def _dot(a, b):
    return jnp.dot(a, b, preferred_element_type=F32)


def _dot_nt(a, b):
    return lax.dot_general(a, b, (((1,), (1,)), ((), ())), preferred_element_type=F32)


def _dot_tn(a, b):
    return lax.dot_general(a, b, (((0,), (0,)), ((), ())), preferred_element_type=F32)


def _dot_sel(m_bf16, x):
    hi = x.astype(BF16)
    r1 = x - hi.astype(F32)
    mid = r1.astype(BF16)
    lo = (r1 - mid.astype(F32)).astype(BF16)
    return _dot(m_bf16, hi) + _dot(m_bf16, mid) + _dot(m_bf16, lo)


def _sigmoid(x):
    return 1.0 / (1.0 + jnp.exp(-x))


def _log_sigmoid(x):
    return jnp.minimum(x, 0.0) - jnp.log1p(jnp.exp(-jnp.abs(x)))


def _rms(x, g):
    return x * lax.rsqrt(jnp.mean(x * x, axis=-1, keepdims=True) + EPS) * g


def _ada_kernel(c_ref, w_ref, b_ref, o_ref):
    c = c_ref[...]
    s = c * _sigmoid(c)
    o_ref[...] = jnp.dot(s, w_ref[...], preferred_element_type=F32,
                         precision=lax.Precision.HIGHEST) + b_ref[...]


def _ada_mod(cc, ada_w, ada_b):
    depth, dm, six = ada_w.shape
    rows = cc.shape[0]
    tn = dm
    return pl.pallas_call(
        _ada_kernel,
        grid=(depth, six // tn),
        in_specs=[pl.BlockSpec((rows, dm), lambda l, n: (0, 0)),
                  pl.BlockSpec((None, dm, tn), lambda l, n: (l, 0, n)),
                  pl.BlockSpec((None, 1, tn), lambda l, n: (l, 0, n))],
        out_specs=pl.BlockSpec((None, rows, tn), lambda l, n: (l, 0, n)),
        out_shape=jax.ShapeDtypeStruct((depth, rows, six), F32),
        compiler_params=_params(("parallel", "parallel")),
        name="ada_mod",
    )(cc, ada_w, ada_b.reshape(depth, 1, six))


def _rope(x, cos, sin):
    n = x.shape[-1]
    lane = lax.broadcasted_iota(jnp.int32, (1, n), 1)
    first = (lane // ROPE_PAIRS) % 2 == 0
    partner = jnp.where(first, pltpu.roll(x, n - ROPE_PAIRS, 1), pltpu.roll(x, ROPE_PAIRS, 1))
    return x * cos + partner * sin


def _hgrn_log_forget(z, lbc):
    lsig = jnp.minimum(z, 0.0) - jnp.log(1.0 + jnp.exp(-jnp.abs(z)))
    a = lbc[0:1]
    bb = lbc[1:2] + lsig
    return jnp.maximum(a, bb) + jnp.log(1.0 + jnp.exp(-jnp.abs(a - bb)))


def _with_gathered_rows(slot_ref, slot_next_ref, yd_ref, ybuf, sems, body):
    i = pl.program_id(0)
    n = pl.num_programs(0)
    tm = ybuf.shape[2]

    def start(table_ref, b):
        _row_copies(tm, lambda r, k: pltpu.make_async_copy(
            yd_ref.at[pl.ds(table_ref[0, k * tm + r], 1), :], ybuf.at[b, k, pl.ds(r, 1), :], sems.at[b]))

    def wait(b):
        for k in range(2):
            pltpu.make_async_copy(yd_ref.at[pl.ds(0, tm), :], ybuf.at[b, k], sems.at[b]).wait()

    @pl.when(i == 0)
    def _():
        start(slot_ref, 0)

    for cur in range(2):
        @pl.when(i % 2 == cur)
        def _():
            wait(cur)
            start(slot_next_ref, 1 - cur)
            body(ybuf[cur, 0], ybuf[cur, 1])

            @pl.when(i == n - 1)
            def _():
                wait(1 - cur)


def _moe_residual(x_ref, mod_ref, rt_ref, y0, y1):
    rt = rt_ref[...]
    return x_ref[...] + mod_ref[5:6, :] * (rt[:, 2:3] * y0 + rt[:, 3:4] * y1)


def _norm_proj_fused_kernel(slot_ref, slot_next_ref, xp_ref, modp_ref, rt_ref, yd_ref,
                            mod_ref, g_ref, w_ref, cos_ref, sin_ref, lbc_ref,
                            xo_ref, pa_ref, pb_ref, pg_ref, qc_ref, kc_ref, vc_ref, ybuf, sems):
    def body(y0, y1):
        x = _moe_residual(xp_ref, modp_ref, rt_ref, y0, y1)
        xo_ref[...] = x
        _project(x, mod_ref, g_ref, w_ref, cos_ref, sin_ref, lbc_ref,
                 pa_ref, pb_ref, pg_ref, qc_ref, kc_ref, vc_ref)

    _with_gathered_rows(slot_ref, slot_next_ref, yd_ref, ybuf, sems, body)


def _norm_proj_kernel(xc_ref, xl_ref, mod_ref, g_ref, w_ref, cos_ref, sin_ref, lbc_ref,
                      pa_ref, pb_ref, pg_ref, qc_ref, kc_ref, vc_ref, *, ctx_tiles):
    x = jnp.where(pl.program_id(1) < ctx_tiles, xc_ref[...], xl_ref[...])
    _project(x, mod_ref, g_ref, w_ref, cos_ref, sin_ref, lbc_ref,
             pa_ref, pb_ref, pg_ref, qc_ref, kc_ref, vc_ref)


def _project(x, mod_ref, g_ref, w_ref, cos_ref, sin_ref, lbc_ref, pa_ref, pb_ref, pg_ref, qc_ref, kc_ref, vc_ref):
    mod = mod_ref[...]
    h = _rms(x, g_ref[...]) * (1.0 + mod[1:2]) + mod[0:1]
    hb = h.astype(BF16)
    pa_ref[:, 0:3 * MIX_W] = _dot(hb, w_ref[:, 0:3 * MIX_W])
    for d in range(2):
        cs = slice((3 + d) * MIX_W, (4 + d) * MIX_W)
        pa_ref[:, cs] = _hgrn_log_forget(_dot(hb, w_ref[:, cs]), lbc_ref[d])
    pb_ref[...] = _dot(hb, w_ref[:, PA_W:PA_W + PB_W])
    pg_ref[...] = _dot(hb, w_ref[:, PA_W + PB_W:PA_W + PB_W + PG_W])
    c0 = PA_W + PB_W + PG_W
    cos = cos_ref[...]
    sin = sin_ref[...]
    q = _dot(hb, w_ref[:, c0:c0 + MIX_W])
    qc_ref[...] = (_rope(q, cos, sin) * (64.0 ** -0.5 * math.log2(math.e))).astype(BF16)
    k = _dot(hb, w_ref[:, c0 + MIX_W:c0 + 2 * MIX_W])
    kc_ref[...] = _rope(k, cos, sin).astype(BF16)
    vc_ref[...] = _dot(hb, w_ref[:, c0 + 2 * MIX_W:c0 + 3 * MIX_W]).astype(BF16)


def _norm_proj_fused(moe, mod, g, w, cos, sin, lbc, tm, ctx_tiles):
    slot, xp, modp, route, y_disp = moe
    bsz, tt, dm = xp.shape
    nt = tt // tm
    n = bsz * nt
    row = lambda i: (i // nt, i % nt, 0)
    mod_row = lambda i: (i // nt, jnp.where(i % nt >= ctx_tiles, 1, 0), 0, 0)
    tab = lambda i: (i % nt, 0)
    outs = [(dm, F32), (PA_W, F32), (PB_W, F32), (PG_W, F32), (MIX_W, BF16), (MIX_W, BF16), (MIX_W, BF16)]
    return pl.pallas_call(
        _norm_proj_fused_kernel,
        grid=(n,),
        in_specs=[pl.BlockSpec((None, 1, 2 * tm), lambda i: (i, 0, 0), memory_space=pltpu.SMEM),
                  pl.BlockSpec((None, 1, 2 * tm), lambda i: (jnp.minimum(i + 1, n - 1), 0, 0),
                               memory_space=pltpu.SMEM),
                  pl.BlockSpec((None, tm, dm), row),
                  pl.BlockSpec((None, None, 6, dm), mod_row),
                  pl.BlockSpec((None, tm, LANES), row),
                  pl.BlockSpec(memory_space=pl.ANY),
                  pl.BlockSpec((None, None, 6, dm), mod_row),
                  _const_spec((1, dm)),
                  _const_spec((dm, PROJ_W)),
                  pl.BlockSpec((tm, MIX_W), tab),
                  pl.BlockSpec((tm, MIX_W), tab),
                  _const_spec(lbc.shape)],
        out_specs=[pl.BlockSpec((None, tm, wd), row) for wd, _ in outs],
        out_shape=[jax.ShapeDtypeStruct((bsz, tt, wd), dt) for wd, dt in outs],
        scratch_shapes=[pltpu.VMEM((2, 2, tm, dm), F32), pltpu.SemaphoreType.DMA((2,))],
        compiler_params=_params(("arbitrary",)),
        name="combine_norm_proj",
    )(slot, slot, xp, modp, route, y_disp, mod, g, w, cos, sin, lbc)


def _norm_proj(x_ctx, x_lat, mod, g, w, cos, sin, lbc, tm, ctx_tiles):
    bsz, t_ctx, dm = x_ctx.shape
    tt = t_ctx + x_lat.shape[1]
    nt = tt // tm
    row = lambda b, t: (b, t, 0)
    outs = [(PA_W, F32), (PB_W, F32), (PG_W, F32), (MIX_W, BF16), (MIX_W, BF16), (MIX_W, BF16)]
    return pl.pallas_call(
        functools.partial(_norm_proj_kernel, ctx_tiles=ctx_tiles),
        grid=(bsz, nt),
        in_specs=[pl.BlockSpec((None, tm, dm), lambda b, t: (b, jnp.minimum(t, ctx_tiles - 1), 0)),
                  pl.BlockSpec((None, tm, dm), lambda b, t: (b, jnp.maximum(t - ctx_tiles, 0), 0)),
                  pl.BlockSpec((None, None, 6, dm), lambda b, t: (b, jnp.where(t >= ctx_tiles, 1, 0), 0, 0)),
                  _const_spec((1, dm)),
                  _const_spec((dm, PROJ_W)),
                  pl.BlockSpec((tm, MIX_W), lambda b, t: (t, 0)),
                  pl.BlockSpec((tm, MIX_W), lambda b, t: (t, 0)),
                  _const_spec(lbc.shape)],
        out_specs=[pl.BlockSpec((None, tm, wd), row) for wd, _ in outs],
        out_shape=[jax.ShapeDtypeStruct((bsz, tt, wd), dt) for wd, dt in outs],
        compiler_params=_params(("parallel", "parallel")),
        name="norm_proj",
    )(x_ctx, x_lat, mod, g, w, cos, sin, lbc)


def _conv_kernel(x_ref, w_ref, b_ref, o_ref, *, ctx, q_blocks):
    x = x_ref[...]
    tt = x.shape[0]
    row = lax.broadcasted_iota(jnp.int32, (tt, 1), 0)
    prev = jnp.where((row == 0) | (row == ctx), 0.0, pltpu.roll(x, 1, 0))
    nxt = jnp.where((row == ctx - 1) | (row == tt - 1), 0.0, pltpu.roll(x, tt - 1, 0))
    w = w_ref[...]
    y = b_ref[...] + prev * w[0:1] + x * w[1:2] + nxt * w[2:3]
    y = y * _sigmoid(y)
    scale = jnp.where(pl.program_id(1) < q_blocks, HEAD_W ** -0.5, 1.0)
    o_ref[...] = y * scale


def _conv_silu(pb, conv_w, conv_b, ctx, cb):
    bsz, tt, _ = pb.shape
    width = 2 * MIX_W
    return pl.pallas_call(
        functools.partial(_conv_kernel, ctx=ctx, q_blocks=MIX_W // cb),
        grid=(bsz, width // cb),
        in_specs=[pl.BlockSpec((None, tt, cb), lambda b, j: (b, 0, j)),
                  pl.BlockSpec((B_CONV, cb), lambda b, j: (0, j)),
                  pl.BlockSpec((1, cb), lambda b, j: (0, j))],
        out_specs=pl.BlockSpec((None, tt, cb), lambda b, j: (b, 0, j)),
        out_shape=jax.ShapeDtypeStruct((bsz, tt, width), F32),
        compiler_params=_params(("parallel", "parallel")),
        name="mlstm_conv",
    )(pb, conv_w, conv_b.reshape(1, width))


def _chunk_consts(L):
    idx = np.arange(L)
    t, u = idx[:, None], idx[None, :]
    mats = [(u <= t)]
    for h in range(1, N_MATMUL_LEVELS + 1):
        mid = (t // (2 * h)) * (2 * h) + h - 1
        mats.append(((u > mid) & (u <= t)) | ((u > t) & (u <= mid)))
    cum = np.concatenate([m.astype(np.float32) for m in mats], axis=0)
    cum_b = np.concatenate([m[::-1, ::-1].astype(np.float32) for m in mats], axis=0)
    nlev = int(round(math.log2(L)))
    lvl = np.full((L, L), -1, np.int32)
    lvl[idx, idx] = nlev
    for i in range(nlev):
        h = L >> (i + 1)
        same = (t // (2 * h)) == (u // (2 * h))
        lvl[same & (t % (2 * h) >= h) & (u % (2 * h) < h)] = i
    sel = jnp.asarray(np.stack([cum, cum_b]), BF16)
    lvls = jnp.asarray(np.stack([lvl, lvl[::-1, ::-1]]))
    return sel, lvls, nlev


N_MATMUL_LEVELS = 0


def _neg_abs(x):
    return -jnp.abs(x)


def _midpoint_rows(b, h, d):
    L, w = b.shape
    two = 2 * h
    pos = h - 1 if d == 0 else h
    if two % 8 == 0:
        r = b.reshape(L // two, two, w)[:, pos:pos + 1, :]
        return jnp.broadcast_to(r, (L // two, two, w)).reshape(L, w)
    phase = lax.broadcasted_iota(jnp.int32, (L, 1), 0) % two
    ref = b
    for off in range(pos - two + 1, pos + 1):
        if off != 0:
            ref = jnp.where(phase == pos - off, pltpu.roll(b, (L - off) % L, 0), ref)
    return ref


def _scan_blocks(j, nc_ctx, nc):
    jb = jnp.where(j < nc_ctx, nc_ctx - 1 - j, nc - 1 + nc_ctx - j)
    return j, jb


def _hgrn_kernel(qf_ref, vf_ref, ff_ref, qb_ref, vb_ref, fb_ref, sel_ref, lvl_ref,
                 of_ref, ob_ref, st_ref, *, L, nlev):
    @pl.when(pl.program_id(1) == 0)
    def _():
        st_ref[...] = jnp.zeros_like(st_ref)

    row = lax.broadcasted_iota(jnp.int32, (L, 1), 0)
    dirs = ((qf_ref, vf_ref, ff_ref, of_ref), (qb_ref, vb_ref, fb_ref, ob_ref))

    def one(bb, d, q_ref, v_ref, f_ref, o_ref):
        lf = f_ref[bb]
        kk = 1.0 - jnp.exp(lf)
        q_all = q_ref[bb]
        ex = _dot_sel(sel_ref[d], lf) * math.log2(math.e)
        b_in = ex[0:L]
        lvl = lvl_ref[d]
        last = L - 1 if d == 0 else 0
        b_end = b_in[last:last + 1]
        heads = [slice(hd * HEAD_W, (hd + 1) * HEAD_W) for hd in range(HEADS)]
        qb = q_all.astype(BF16)
        kb = kk.astype(BF16)
        on_diag = lvl == nlev
        scores = [jnp.where(on_diag, _dot_nt(qb[:, cs], kb[:, cs]), 0.0) for cs in heads]
        for i in range(nlev):
            h = L >> (i + 1)
            if h <= N_MATMUL_LEVELS:
                expo = ex[h * L:(h + 1) * L]
            else:
                expo = _neg_abs(b_in - _midpoint_rows(b_in, h, d))
            ei = jnp.exp2(expo)
            is_query = (row % (2 * h) >= h) if d == 0 else (row % (2 * h) < h)
            xe = (jnp.where(is_query, q_all, kk) * ei).astype(BF16)
            at_level = lvl == i
            for hd, cs in enumerate(heads):
                scores[hd] = scores[hd] + jnp.where(at_level, _dot_nt(xe[:, cs], xe[:, cs]), 0.0)
        q_in = (q_all * jnp.exp2(b_in)).astype(BF16)
        k_out = (kk * jnp.exp2(b_end - b_in)).astype(BF16)
        decay = jnp.exp2(b_end)
        for hd, cs in enumerate(heads):
            vb = v_ref[bb, :, cs].astype(BF16)
            st = st_ref[bb, d, hd]
            o_ref[bb, :, cs] = _dot(scores[hd].astype(BF16), vb) + _dot_nt(q_in[:, cs], st.astype(BF16))
            st_ref[bb, d, hd] = st * decay[:, cs] + _dot_tn(vb, k_out[:, cs])

    for bb in range(qf_ref.shape[0]):
        for d, refs in enumerate(dirs):
            one(bb, d, *refs)


def _hgrn(pa, L, nc_ctx):
    bsz, tt, _ = pa.shape
    nc = tt // L
    nb = 2 if bsz % 2 == 0 else 1
    sel, lvl, nlev = _chunk_consts(L)

    def spec(col, which):
        return pl.BlockSpec((nb, L, MIX_W), lambda b, j: (b, _scan_blocks(j, nc_ctx, nc)[which], col))

    return pl.pallas_call(
        functools.partial(_hgrn_kernel, L=L, nlev=nlev),
        grid=(bsz // nb, nc),
        in_specs=[spec(0, 0), spec(1, 0), spec(3, 0), spec(0, 1), spec(1, 1), spec(4, 1),
                  _const_spec(sel.shape), _const_spec(lvl.shape)],
        out_specs=[spec(0, 0), spec(0, 1)],
        out_shape=[jax.ShapeDtypeStruct((bsz, tt, MIX_W), F32)] * 2,
        scratch_shapes=[pltpu.VMEM((nb, 2, HEADS, HEAD_W, HEAD_W), F32)],
        compiler_params=_params(("parallel", "arbitrary")),
        name="hgrn2_scan",
    )(pa, pa, pa, pa, pa, pa, sel, lvl)


def _mlstm_kernel(qf_ref, kf_ref, vf_ref, gf_ref, qb_ref, kb_ref, vb_ref, gb_ref, gbias_ref,
                  sel_ref, lvl_ref, of_ref, ob_ref, c_ref, m_ref, *, L, nlev):
    @pl.when(pl.program_id(1) == 0)
    def _():
        c_ref[...] = jnp.zeros_like(c_ref)
        m_ref[...] = jnp.zeros_like(m_ref)

    lane = lax.broadcasted_iota(jnp.int32, (1, LANES), 1)
    tlane = lax.broadcasted_iota(jnp.int32, (HEADS, L), 1)
    ones_col = jnp.where(lane == 0, 1.0, 0.0).astype(BF16)
    dirs = ((qf_ref, kf_ref, vf_ref, gf_ref, of_ref), (qb_ref, kb_ref, vb_ref, gb_ref, ob_ref))
    for d, (q_ref, k_ref, v_ref, g_ref, o_ref) in enumerate(dirs):
        g = g_ref[...] + gbias_ref[...]
        lf = jnp.where((lane >= 2 * HEADS) & (lane < 4 * HEADS), _log_sigmoid(g), 0.0)
        bcum = _dot_sel(sel_ref[d, 0:L], lf)
        causal = lvl_ref[d] >= 0
        rows = jnp.where(lane < 2 * HEADS, g - pltpu.roll(bcum, LANES - 2 * HEADS, 1), bcum).T
        r4 = rows[d * HEADS:(d + 1) * HEADS]
        b4 = rows[(2 + d) * HEADS:(3 + d) * HEADS]
        m4 = m_ref[d * HEADS:(d + 1) * HEADS, 0:1]
        run = r4
        sh = 1
        while sh < L:
            if d == 0:
                run = jnp.maximum(run, jnp.where(tlane >= sh, pltpu.roll(run, sh, 1), -jnp.inf))
            else:
                run = jnp.maximum(run, jnp.where(tlane < L - sh, pltpu.roll(run, L - sh, 1), -jnp.inf))
            sh *= 2
        c4 = jnp.maximum(run, m4)
        cols = jnp.concatenate([c4, jnp.exp(m4 - c4), jnp.exp(-b4 - c4),
                                jnp.zeros((LANES - 3 * HEADS, L), F32)], axis=0).T
        last = L - 1 if d == 0 else 0
        for hd in range(HEADS):
            cs = slice(hd * HEAD_W, (hd + 1) * HEAD_W)
            row_term = r4[hd:hd + 1, :]
            b_end = b4[hd:hd + 1, last:last + 1]
            r = d * HEADS + hd
            m_prev = m4[hd:hd + 1]
            q = q_ref[:, cs].astype(BF16)
            k = k_ref[:, cs]
            v_aug = jnp.concatenate([v_ref[:, cs].astype(BF16), jnp.broadcast_to(ones_col, (L, LANES))], axis=1)
            c_prev = c_ref[r]
            w_end = b_end + row_term
            m_new = jnp.maximum(b_end + m_prev, jnp.max(w_end, axis=-1, keepdims=True))
            e_end = jnp.exp(w_end - m_new)
            keep = jnp.exp(b_end + m_prev - m_new)
            c_ref[r] = keep * c_prev + _dot((k.T * e_end).astype(BF16), v_aug)
            m_ref[r:r + 1, :] = jnp.broadcast_to(m_new, (1, LANES))
            c_t = cols[:, hd:hd + 1]
            a_state = cols[:, HEADS + hd:HEADS + hd + 1]
            floor = cols[:, 2 * HEADS + hd:2 * HEADS + hd + 1]
            s = _dot_nt(q, k.astype(BF16)) * jnp.exp(jnp.where(causal, row_term - c_t, NEG_BIG))
            num = _dot(s.astype(BF16), v_aug) + a_state * _dot(q, c_prev.astype(BF16))
            den = num[:, HEAD_W:HEAD_W + 1]
            o_ref[:, cs] = num[:, 0:HEAD_W] / jnp.maximum(jnp.abs(den), floor)


def _mlstm(qk, pb, pg, gbias, L, nc_ctx):
    bsz, tt, _ = pb.shape
    nc = tt // L
    sel, lvl, nlev = _chunk_consts(L)

    def spec(col, which, width=MIX_W):
        return pl.BlockSpec((None, L, width), lambda b, j: (b, _scan_blocks(j, nc_ctx, nc)[which], col))

    return pl.pallas_call(
        functools.partial(_mlstm_kernel, L=L, nlev=nlev),
        grid=(bsz, nc),
        in_specs=[spec(0, 0), spec(1, 0), spec(2, 0), spec(0, 0, PG_W),
                  spec(0, 1), spec(1, 1), spec(2, 1), spec(0, 1, PG_W),
                  _const_spec((1, PG_W)), _const_spec(sel.shape), _const_spec(lvl.shape)],
        out_specs=[spec(0, 0), spec(0, 1)],
        out_shape=[jax.ShapeDtypeStruct((bsz, tt, MIX_W), F32)] * 2,
        scratch_shapes=[pltpu.VMEM((2 * HEADS, HEAD_W, 2 * HEAD_W), F32),
                        pltpu.VMEM((2 * HEADS, LANES), F32)],
        compiler_params=_params(("parallel", "arbitrary")),
        name="mlstm_scan",
    )(qk, qk, pb, pg, qk, qk, pb, pg, gbias, sel, lvl)


def _diff_lambda(lam_ref, lam_init):
    lv = lam_ref[...]
    return (jnp.exp(jnp.sum(lv[0:1] * lv[1:2], axis=-1, keepdims=True))
            - jnp.exp(jnp.sum(lv[2:3] * lv[3:4], axis=-1, keepdims=True)) + lam_init)


def _map_scores(q, k):
    lane = lax.broadcasted_iota(jnp.int32, (1, HEAD_W), 1)
    zero = jnp.zeros_like(q)
    return (_dot_nt(jnp.where(lane < HEAD_W // 2, q, zero), k),
            _dot_nt(jnp.where(lane >= HEAD_W // 2, q, zero), k))


def _row_max(s):
    return jnp.max(s, axis=-1, keepdims=True)


def _diff_softmax_v(s1, m1, s2, m2, lam, v):
    p1 = jnp.exp2(s1 - m1)
    p2 = jnp.exp2(s2 - m2)
    l1 = jnp.sum(p1, axis=-1, keepdims=True)
    l2 = jnp.sum(p2, axis=-1, keepdims=True)
    return _dot((p1 - (lam * l1 / l2) * p2).astype(BF16), v) / l1


def _attn_ctx_kernel(lam_ref, q_ref, k_ref, v_ref, o_ref, *, lam_init):
    s1, s2 = _map_scores(q_ref[...], k_ref[...])
    o_ref[...] = _diff_softmax_v(s1, _row_max(s1), s2, _row_max(s2), _diff_lambda(lam_ref, lam_init), v_ref[...])


def _attn_ctx(lam_vec, qc, kc, vc, lam_init, tq, t_ctx):
    bsz = qc.shape[0]
    kv = pl.BlockSpec((None, t_ctx, HEAD_W), lambda b, h, i: (b, 0, h))
    return pl.pallas_call(
        functools.partial(_attn_ctx_kernel, lam_init=lam_init),
        grid=(bsz, HEADS, t_ctx // tq),
        in_specs=[_const_spec(lam_vec.shape),
                  pl.BlockSpec((None, tq, HEAD_W), lambda b, h, i: (b, i, h)), kv, kv],
        out_specs=pl.BlockSpec((None, tq, HEAD_W), lambda b, h, i: (b, i, h)),
        out_shape=jax.ShapeDtypeStruct((bsz, t_ctx, MIX_W), F32),
        compiler_params=_params(("parallel", "parallel", "arbitrary")),
        name="diff_attn_ctx",
    )(lam_vec, qc, kc, vc)


def _attn_lat_kernel(lam_ref, qa_ref, qb_ref, k_ref, v_ref, o_ref, s_even, m_even, s_odd, m_odd,
                     *, lam_init, n_tiles):
    j = pl.program_id(2)
    lam = _diff_lambda(lam_ref, lam_init)
    even, odd = (s_even, m_even), (s_odd, m_odd)
    half = qa_ref.shape[0]

    def scores_into(bufs):
        s_buf, m_buf = bufs
        for part, q_ref in enumerate((qa_ref, qb_ref)):
            rows = slice(part * half, (part + 1) * half)
            for i, s in enumerate(_map_scores(q_ref[...], k_ref[...])):
                s_buf[i, rows, :] = s
                m_buf[i, rows, :] = jnp.broadcast_to(_row_max(s), (half, LANES))

    def finish_from(bufs):
        s_buf, m_buf = bufs
        for part in range(2):
            rows = slice(part * half, (part + 1) * half)
            o_ref[rows, :] = _diff_softmax_v(s_buf[0, rows, :], m_buf[0, rows, 0:1],
                                             s_buf[1, rows, :], m_buf[1, rows, 0:1], lam, v_ref[...])

    @pl.when(j == 0)
    def _():
        scores_into(even)

    middle = (j > 0) & (j < n_tiles)

    @pl.when(middle & (j % 2 == 1))
    def _():
        scores_into(odd)
        finish_from(even)

    @pl.when(middle & (j % 2 == 0))
    def _():
        scores_into(even)
        finish_from(odd)

    @pl.when(j == n_tiles)
    def _():
        finish_from(odd if n_tiles % 2 == 0 else even)


def _attn_lat(lam_vec, qc, kc, vc, lam_init, tq, t_ctx):
    bsz, tt, _ = qc.shape
    n_tiles = (tt - t_ctx) // (2 * tq)
    q0 = t_ctx // tq

    def q_spec(part):
        return pl.BlockSpec((None, tq, HEAD_W),
                            lambda b, h, j: (b, q0 + 2 * jnp.minimum(j, n_tiles - 1) + part, h))

    kv = pl.BlockSpec((None, tt, HEAD_W), lambda b, h, j: (b, 0, h))
    return pl.pallas_call(
        functools.partial(_attn_lat_kernel, lam_init=lam_init, n_tiles=n_tiles),
        grid=(bsz, HEADS, n_tiles + 1),
        in_specs=[_const_spec(lam_vec.shape), q_spec(0), q_spec(1), kv, kv],
        out_specs=pl.BlockSpec((None, 2 * tq, HEAD_W), lambda b, h, j: (b, jnp.maximum(j - 1, 0), h)),
        out_shape=jax.ShapeDtypeStruct((bsz, tt - t_ctx, MIX_W), F32),
        scratch_shapes=[pltpu.VMEM((2, 2 * tq, tt), F32), pltpu.VMEM((2, 2 * tq, LANES), F32),
                        pltpu.VMEM((2, 2 * tq, tt), F32), pltpu.VMEM((2, 2 * tq, LANES), F32)],
        compiler_params=_params(("parallel", "parallel", "arbitrary")),
        name="diff_attn_lat",
    )(lam_vec, qc, qc, kc, vc)


def _head_norm(x, g):
    parts = []
    for hd in range(HEADS):
        xs = x[:, hd * HEAD_W:(hd + 1) * HEAD_W]
        parts.append(xs * lax.rsqrt(jnp.mean(xs * xs, axis=-1, keepdims=True) + EPS))
    return jnp.concatenate(parts, axis=1) * g


ROUTE_ROWS = 40


def _route(lg):
    n = lg.shape[1]
    row = lax.broadcasted_iota(jnp.int32, lg.shape, 0)
    neg = -jnp.inf
    big = ROUTE_ROWS
    gl = jnp.where(row < N_GROUPS, lg, neg)
    gmax = jnp.max(gl, axis=0, keepdims=True)
    gidx = jnp.min(jnp.where(gl == gmax, row, big), axis=0, keepdims=True)
    p_group = 1.0 / jnp.sum(jnp.exp(gl - gmax), axis=0, keepdims=True)
    lo = N_GROUPS + EXPERTS_PER_GROUP * gidx
    in_grp = (row >= lo) & (row < lo + EXPERTS_PER_GROUP)
    el = jnp.where(in_grp, lg, neg)
    pe = jnp.exp(el - jnp.max(el, axis=0, keepdims=True))
    pe = pe / jnp.sum(pe, axis=0, keepdims=True)
    pe = jnp.where(in_grp, pe, -1.0)
    v1 = jnp.max(pe, axis=0, keepdims=True)
    i1 = jnp.min(jnp.where(pe == v1, row, big), axis=0, keepdims=True)
    pe2 = jnp.where(row == i1, -1.0, pe)
    v2 = jnp.max(pe2, axis=0, keepdims=True)
    i2 = jnp.min(jnp.where(pe2 == v2, row, big), axis=0, keepdims=True)
    scale = p_group / (v1 + v2)
    erow = lax.broadcasted_iota(jnp.int32, (N_EXPERTS, n), 0) + N_GROUPS
    oh1 = erow == i1
    oh2 = erow == i2
    oh1f = jnp.where(oh1, 1.0, 0.0)
    oh2f = jnp.where(oh2, 1.0, 0.0)
    earlier = (lax.broadcasted_iota(jnp.int32, (n, n), 0) < lax.broadcasted_iota(jnp.int32, (n, n), 1))
    earlier = jnp.where(earlier, 1.0, 0.0).astype(BF16)
    tot1 = jnp.sum(oh1f, axis=1, keepdims=True)
    tot2 = jnp.sum(oh2f, axis=1, keepdims=True)
    rank1 = jnp.sum(jnp.where(oh1, _dot(oh1f.astype(BF16), earlier), 0.0), axis=0, keepdims=True)
    rank2 = jnp.sum(jnp.where(oh2, _dot(oh2f.astype(BF16), earlier) + tot1, 0.0), axis=0, keepdims=True)
    r8 = lax.broadcasted_iota(jnp.int32, (8, n), 0)
    rows = jnp.zeros((8, n), F32)
    for i, val in enumerate(((i1 - N_GROUPS).astype(F32), (i2 - N_GROUPS).astype(F32), v1 * scale, v2 * scale,
                             rank1, rank2)):
        rows = jnp.where(r8 == i, val, rows)
    return rows, tot1 + tot2


def _merge_kernel(xc_ref, xl_ref, mod_ref, n1_ref, n2_ref, af_ref, ab_ref, ga_ref, bf_ref, bb_ref, gb_ref,
                  occ_ref, ocl_ref, hg_ref, wg_ref, bg_ref, wbr_ref, wo_ref, rw_ref, rb_ref,
                  xo_ref, h2_ref, rt_ref, rtt_ref, cnt_ref, *, lam_init, ctx_tiles, row0_tiles):
    is_ctx_tile = row0_tiles + pl.program_id(1) < ctx_tiles
    x = jnp.where(is_ctx_tile, xc_ref[...], xl_ref[...])
    mod = mod_ref[...]
    dm = x.shape[-1]
    hb = (_rms(x, n1_ref[...]) * (1.0 + mod[1:2]) + mod[0:1]).astype(BF16)
    hg = hg_ref[...]
    ga = ga_ref[...]
    ya = _head_norm(af_ref[...] + ab_ref[...], hg[0:1]) * (ga * _sigmoid(ga))
    yb = _head_norm(bf_ref[...] + bb_ref[...], hg[1:2]) * _sigmoid(gb_ref[...])
    oc = jnp.where(is_ctx_tile, occ_ref[...], ocl_ref[...])
    yc = _head_norm(oc, hg[2:3]) * (1.0 - lam_init)
    y = jnp.zeros_like(x)
    for i, yi in enumerate((ya, yb, yc)):
        gate = _sigmoid(_dot(hb, wg_ref[:, i * dm:(i + 1) * dm]) + bg_ref[:, i * dm:(i + 1) * dm])
        y = y + gate * _dot(yi.astype(BF16), wbr_ref[i])
    xn = x + mod[2:3] * _dot(y.astype(BF16), wo_ref[...])
    xo_ref[...] = xn
    h2 = _rms(xn, n2_ref[...]) * (1.0 + mod[4:5]) + mod[3:4]
    h2_ref[...] = h2
    h_hi = h2.astype(BF16)
    h_mid = (h2 - h_hi.astype(F32)).astype(BF16)
    logits = (_dot_nt(rw_ref[0], h_hi) + _dot_nt(rw_ref[0], h_mid) + _dot_nt(rw_ref[1], h_hi)) + rb_ref[...]
    rows, counts = _route(logits[0:ROUTE_ROWS, :])
    rtt_ref[...] = rows
    cnt_ref[...] = jnp.broadcast_to(counts, cnt_ref.shape)
    tm = x.shape[0]
    rt_ref[...] = jnp.concatenate([rows, jnp.zeros((LANES - 8, tm), F32)], axis=0).T


def _merge(x_ctx, x_lat, lat_row0, mod, n1, n2, oaf, oab, pa, obf, obb, pb, oc_ctx, oc_lat, hg, wg, bg, wbr, wo,
           rw, rb, lam_init, tm, ctx_tiles, row0_tiles):
    bsz, tt, _ = pa.shape
    dm = x_lat.shape[-1]
    nt = tt // tm - row0_tiles
    rows = nt * tm

    def col(c):
        return pl.BlockSpec((None, tm, MIX_W), lambda b, t: (b, row0_tiles + t, c))

    dst = lambda b, t: (b, t, 0)
    outs = [(dm, F32), (dm, F32), (LANES, F32)]
    out_specs = [pl.BlockSpec((None, tm, wd), dst) for wd, _ in outs]
    out_shape = [jax.ShapeDtypeStruct((bsz, rows, wd), dt) for wd, dt in outs]
    out_specs.append(pl.BlockSpec((None, 8, tm), lambda b, t: (b * nt + t, 0, 0)))
    out_shape.append(jax.ShapeDtypeStruct((bsz * nt, 8, tm), F32))
    out_specs.append(pl.BlockSpec((None, N_EXPERTS, LANES), lambda b, t: (b * nt + t, 0, 0)))
    out_shape.append(jax.ShapeDtypeStruct((bsz * nt, N_EXPERTS, LANES), F32))
    return pl.pallas_call(
        functools.partial(_merge_kernel, lam_init=lam_init, ctx_tiles=ctx_tiles, row0_tiles=row0_tiles),
        grid=(bsz, nt),
        in_specs=[pl.BlockSpec((None, tm, dm),
                               lambda b, t: (b, jnp.minimum(row0_tiles + t, max(ctx_tiles - 1, 0)), 0)),
                  pl.BlockSpec((None, tm, dm),
                               lambda b, t: (b, lat_row0 + jnp.maximum(row0_tiles + t - ctx_tiles, 0), 0)),
                  pl.BlockSpec((None, None, 6, dm),
                               lambda b, t: (b, jnp.where(row0_tiles + t >= ctx_tiles, 1, 0), 0, 0)),
                  _const_spec((1, dm)), _const_spec((1, dm)),
                  col(0), col(0), col(2), col(0), col(0), col(3),
                  pl.BlockSpec((None, tm, MIX_W),
                               lambda b, t: (b, jnp.minimum(row0_tiles + t, max(ctx_tiles - 1, 0)), 0)),
                  pl.BlockSpec((None, tm, MIX_W),
                               lambda b, t: (b, jnp.maximum(row0_tiles + t - ctx_tiles, 0), 0)),
                  _const_spec((3, MIX_W)), _const_spec(wg.shape), _const_spec(bg.shape),
                  _const_spec(wbr.shape), _const_spec(wo.shape), _const_spec(rw.shape), _const_spec(rb.shape)],
        out_specs=out_specs,
        out_shape=out_shape,
        compiler_params=_params(("parallel", "parallel")),
        name="merge_route",
    )(x_ctx, x_lat, mod, n1, n2, oaf, oab, pa, obf, obb, pb, oc_ctx, oc_lat, hg, wg, bg, wbr, wo, rw, rb)


def _slot_tables(route_rows, counts):
    tiles, _, tm = route_rows.shape
    cnt = counts[:, :, 0].astype(jnp.int32)
    tile_off = jnp.cumsum(cnt, axis=0) - cnt
    total = jnp.sum(cnt, axis=0)
    padded = (total + MOE_BLOCK - 1) // MOE_BLOCK * MOE_BLOCK
    p_end = jnp.cumsum(padded)
    base = (p_end - padded)[None, :] + tile_off
    e = route_rows[:, 0:2, :].astype(jnp.int32)
    rank = route_rows[:, 4:6, :].astype(jnp.int32)
    hit = e[:, :, :, None] == jnp.arange(N_EXPERTS, dtype=jnp.int32)
    slot = jnp.sum(jnp.where(hit, base[:, None, None, :], 0), axis=-1) + rank
    n_blocks = -(-(2 * tiles * tm) // MOE_BLOCK) + N_EXPERTS
    blk_start = jnp.arange(n_blocks, dtype=jnp.int32) * MOE_BLOCK
    block_e = jnp.minimum(jnp.sum(blk_start[:, None] >= p_end[None, :], axis=1), N_EXPERTS - 1).astype(jnp.int32)
    n_used = (p_end[-1] // MOE_BLOCK).astype(jnp.int32).reshape(1)
    filled = jnp.clip(((p_end - padded) + total)[block_e] - blk_start, 0, MOE_BLOCK)
    filled = jnp.where(blk_start < p_end[-1], filled, 0)
    partial = (filled < MOE_BLOCK).astype(jnp.int32)
    return slot.reshape(tiles, 1, 2 * tm), block_e, n_used, partial


def _row_copies(n_rows, make):
    for r in range(n_rows):
        for k in range(2):
            make(r, k).start()


def _scatter_kernel(partial_ref, slot_ref, h_ref, xd_ref, zeros, sem, zsem):
    tm = h_ref.shape[0]
    n_blocks = partial_ref.shape[0]

    @pl.when((pl.program_id(0) == 0) & (pl.program_id(1) == 0))
    def _():
        zeros[...] = jnp.zeros_like(zeros)

        def zero_block(blk):
            return pltpu.make_async_copy(zeros, xd_ref.at[pl.ds(blk * MOE_BLOCK, MOE_BLOCK), :], zsem)

        def start(blk, c):
            @pl.when(partial_ref[blk] != 0)
            def _():
                zero_block(blk).start()
            return c

        def wait(blk, c):
            @pl.when(partial_ref[blk] != 0)
            def _():
                zero_block(blk).wait()
            return c

        lax.fori_loop(0, n_blocks, start, 0)
        lax.fori_loop(0, n_blocks, wait, 0)

    _row_copies(tm, lambda r, k: pltpu.make_async_copy(
        h_ref.at[pl.ds(r, 1), :], xd_ref.at[pl.ds(slot_ref[0, k * tm + r], 1), :], sem))
    for _ in range(2):
        pltpu.make_async_copy(h_ref, xd_ref.at[pl.ds(0, tm), :], sem).wait()


def _scatter(partial, slot, h2, n_slots, tm):
    bsz, rows, dm = h2.shape
    nt = rows // tm
    grid_spec = pltpu.PrefetchScalarGridSpec(
        num_scalar_prefetch=1,
        grid=(bsz, nt),
        in_specs=[pl.BlockSpec((None, 1, 2 * tm), lambda b, t, p: (b * nt + t, 0, 0), memory_space=pltpu.SMEM),
                  pl.BlockSpec((tm, dm), lambda b, t, p: (b * nt + t, 0))],
        out_specs=pl.BlockSpec(memory_space=pl.ANY),
        scratch_shapes=[pltpu.VMEM((MOE_BLOCK, dm), F32), pltpu.SemaphoreType.DMA(()),
                        pltpu.SemaphoreType.DMA(())],
    )
    return pl.pallas_call(
        _scatter_kernel,
        grid_spec=grid_spec,
        out_shape=jax.ShapeDtypeStruct((n_slots, dm), F32),
        compiler_params=_params(("arbitrary", "arbitrary"), has_side_effects=True),
        name="moe_scatter",
    )(partial, slot, h2.reshape(bsz * rows, dm))


def _expert_kernel(be_ref, nu_ref, x_ref, w1_ref, w3_ref, w2_ref, y_ref, w1b, w3b, w2b):
    i = pl.program_id(0)

    @pl.when(i < nu_ref[0])
    def _():
        @pl.when((i == 0) | (be_ref[i] != be_ref[jnp.maximum(i - 1, 0)]))
        def _():
            w1b[...] = w1_ref[...].astype(BF16)
            w3b[...] = w3_ref[...].astype(BF16)
            w2b[...] = w2_ref[...].astype(BF16)

        xb = x_ref[...].astype(BF16)
        u = _dot(xb, w1b[...])
        hmid = (u * _sigmoid(u)) * _dot(xb, w3b[...])
        y_ref[...] = _dot(hmid.astype(BF16), w2b[...])

    @pl.when(i >= nu_ref[0])
    def _():
        y_ref[...] = jnp.zeros_like(y_ref)


def _experts(block_e, n_used, x_disp, w1, w3, w2, layer):
    n_slots, dm = x_disp.shape
    n_blocks = n_slots // MOE_BLOCK
    wspec = lambda shape: pl.BlockSpec((None, None) + shape, lambda i, be, nu: (layer, be[i], 0, 0))
    grid_spec = pltpu.PrefetchScalarGridSpec(
        num_scalar_prefetch=2,
        grid=(n_blocks,),
        in_specs=[pl.BlockSpec((MOE_BLOCK, dm), lambda i, be, nu: (jnp.minimum(i, nu[0] - 1), 0)),
                  wspec((dm, D_EXPERT)), wspec((dm, D_EXPERT)), wspec((D_EXPERT, dm))],
        out_specs=pl.BlockSpec((MOE_BLOCK, dm), lambda i, be, nu: (i, 0)),
        scratch_shapes=[pltpu.VMEM((dm, D_EXPERT), BF16), pltpu.VMEM((dm, D_EXPERT), BF16),
                        pltpu.VMEM((D_EXPERT, dm), BF16)],
    )
    return pl.pallas_call(
        _expert_kernel,
        grid_spec=grid_spec,
        out_shape=jax.ShapeDtypeStruct((n_slots, dm), F32),
        compiler_params=_params(("arbitrary",)),
        name="moe_experts",
    )(block_e, n_used, x_disp, w1, w3, w2)


def _combine_kernel(slot_ref, slot_next_ref, x_ref, mod_ref, rt_ref, yd_ref, fg_ref, o_ref, ybuf, sems):
    def body(y0, y1):
        o_ref[...] = _rms(_moe_residual(x_ref, mod_ref, rt_ref, y0, y1), fg_ref[...])

    _with_gathered_rows(slot_ref, slot_next_ref, yd_ref, ybuf, sems, body)


def _combine_final(slot, x, mod, route, y_disp, fg, tm, ctx_tiles, row0_tiles):
    bsz, rows, dm = x.shape
    nt = rows // tm
    n = bsz * nt
    row = lambda i: (i // nt, i % nt, 0)
    return pl.pallas_call(
        _combine_kernel,
        grid=(n,),
        in_specs=[pl.BlockSpec((None, 1, 2 * tm), lambda i: (i, 0, 0), memory_space=pltpu.SMEM),
                  pl.BlockSpec((None, 1, 2 * tm), lambda i: (jnp.minimum(i + 1, n - 1), 0, 0),
                               memory_space=pltpu.SMEM),
                  pl.BlockSpec((None, tm, dm), row),
                  pl.BlockSpec((None, None, 6, dm),
                               lambda i: (i // nt, jnp.where(row0_tiles + i % nt >= ctx_tiles, 1, 0), 0, 0)),
                  pl.BlockSpec((None, tm, LANES), row),
                  pl.BlockSpec(memory_space=pl.ANY),
                  _const_spec((1, dm))],
        out_specs=pl.BlockSpec((None, tm, dm), row),
        out_shape=jax.ShapeDtypeStruct((bsz, rows, dm), F32),
        scratch_shapes=[pltpu.VMEM((2, 2, tm, dm), F32), pltpu.SemaphoreType.DMA((2,))],
        compiler_params=_params(("arbitrary",)),
        name="moe_combine",
    )(slot, slot, x, mod, route, y_disp, fg)


def _rope_tables(ctx, t_lat):
    rows = t_lat // GRID_W
    row = np.repeat(np.arange(rows, dtype=np.float64), GRID_W)
    col = np.tile(np.arange(GRID_W, dtype=np.float64), rows)
    inv = ROPE_BASE ** (-np.arange(ROPE_PAIRS, dtype=np.float64) / ROPE_PAIRS)
    ang_r = row[:, None] * inv
    ang_c = col[:, None] * inv
    cos64 = np.concatenate([np.cos(ang_r), np.cos(ang_r), np.cos(ang_c), np.cos(ang_c)], axis=1)
    sin64 = np.concatenate([-np.sin(ang_r), np.sin(ang_r), -np.sin(ang_c), np.sin(ang_c)], axis=1)
    cos = np.tile(cos64, (1, MIX_W // 64))
    sin = np.tile(sin64, (1, MIX_W // 64))
    cos = np.concatenate([np.ones((ctx, MIX_W)), cos], axis=0).astype(np.float32)
    sin = np.concatenate([np.zeros((ctx, MIX_W)), sin], axis=0).astype(np.float32)
    return jnp.asarray(cos), jnp.asarray(sin)


def _pack_w_in(w):
    dm = w.shape[0]
    a_end = PA_W
    b_end = a_end + PB_W
    g_end = b_end + 4 * HEADS
    pad = jnp.zeros((dm, PG_W - 4 * HEADS), w.dtype)
    return jnp.concatenate([w[:, :b_end], w[:, b_end:g_end], pad, w[:, g_end:]], axis=1).astype(BF16)


def kernel(x, c, ctx, c_ctx, ada_w, ada_b, norm1_g, norm2_g, w_in, mlstm_conv_w, mlstm_conv_b, mlstm_gate_b,
           hgrn_lb_raw, hgrn_norm_g, mlstm_norm_g, diff_norm_g, diff_lambda, w_branch, w_gate, b_gate, w_out,
           router_g_w, router_g_b, router_e_w, router_e_b, moe_w1, moe_w3, moe_w2, final_g):
    bsz, t_lat, dm = x.shape
    t_ctx = ctx.shape[1]
    depth = ada_w.shape[0]
    tt = t_ctx + t_lat
    tm = min(256, t_ctx)
    hgrn_chunk = min(128, t_ctx)
    mlstm_chunk = min(256, t_ctx)
    assert t_ctx % tm == 0 and t_lat % (2 * tm) == 0 and t_lat % GRID_W == 0
    assert t_ctx % hgrn_chunk == 0 and t_ctx % mlstm_chunk == 0
    ctx_tiles = t_ctx // tm

    n_rows = -(-(bsz + 1) // 8) * 8
    cc = jnp.zeros((n_rows, dm), F32).at[:bsz].set(c).at[bsz].set(c_ctx)
    mods = _ada_mod(cc, ada_w, ada_b).reshape(depth, n_rows, 6, dm)

    cos, sin = _rope_tables(t_ctx, t_lat)
    lb_cum = jnp.cumsum(jax.nn.softmax(hgrn_lb_raw.astype(F32), axis=0), axis=0)
    lower = lb_cum - lb_cum[0]

    x_ctx, x_lat, lat_row0 = ctx, x, 0
    moe = None
    for l in range(depth):
        with_ctx = l < depth - 1
        lam_init = 0.8 - 0.6 * math.exp(-0.3 * l)
        mod = jnp.stack([jnp.broadcast_to(mods[l, bsz], (bsz, 6, dm)), mods[l, :bsz]], axis=1)
        n1 = norm1_g[l].reshape(1, dm)
        n2 = norm2_g[l].reshape(1, dm)

        lb = lower[l]
        pad = jnp.zeros((2, 6, MIX_W), F32)
        lbc = jnp.concatenate([jnp.log(lb)[:, None], jnp.log1p(-lb)[:, None], pad], axis=1)
        if moe is None:
            pa, pb, pg, qc, kc, vc = _norm_proj(x_ctx, x_lat, mod, n1, _pack_w_in(w_in[l]), cos, sin, lbc,
                                                tm, ctx_tiles)
        else:
            xc, pa, pb, pg, qc, kc, vc = _norm_proj_fused(moe, mod, n1, _pack_w_in(w_in[l]), cos, sin, lbc,
                                                          tm, ctx_tiles)
            x_ctx, x_lat, lat_row0 = xc, xc, ctx_tiles

        oaf, oab = _hgrn(pa, hgrn_chunk, t_ctx // hgrn_chunk)

        qk = _conv_silu(pb, mlstm_conv_w[l], mlstm_conv_b[l], t_ctx, 256)
        gbias = jnp.zeros((1, PG_W), F32).at[0, :4 * HEADS].set(mlstm_gate_b[l].reshape(-1))
        obf, obb = _mlstm(qk, pb, pg, gbias, mlstm_chunk, t_ctx // mlstm_chunk)

        lam_vec = diff_lambda[l].astype(F32)
        oc_lat = _attn_lat(lam_vec, qc, kc, vc, lam_init, tm, t_ctx)
        oc_ctx = _attn_ctx(lam_vec, qc, kc, vc, lam_init, tm, t_ctx) if with_ctx else oc_lat

        hg = jnp.stack([hgrn_norm_g[l], mlstm_norm_g[l], diff_norm_g[l]])
        rw = jnp.zeros((LANES, dm), F32).at[:N_GROUPS].set(router_g_w[l].T)
        rw = rw.at[N_GROUPS:N_GROUPS + N_EXPERTS].set(router_e_w[l].T)
        rw_hi = rw.astype(BF16)
        rw = jnp.stack([rw_hi, (rw - rw_hi.astype(F32)).astype(BF16)])
        rb = jnp.zeros((LANES, 1), F32).at[:N_GROUPS, 0].set(router_g_b[l])
        rb = rb.at[N_GROUPS:N_GROUPS + N_EXPERTS, 0].set(router_e_b[l])
        row0_tiles = 0 if with_ctx else ctx_tiles
        xn, h2, route, route_rows, counts = _merge(x_ctx, x_lat, lat_row0, mod, n1, n2, oaf, oab, pa, obf, obb, pb,
                                                   oc_ctx, oc_lat, hg,
                                                   w_gate[l].astype(BF16), b_gate[l].reshape(1, 3 * dm),
                                                   w_branch[l].astype(BF16), w_out[l].astype(BF16), rw, rb,
                                                   lam_init, tm, ctx_tiles, row0_tiles)

        slot, block_e, n_used, partial = _slot_tables(route_rows, counts)
        x_disp = _scatter(partial, slot, h2, partial.shape[0] * MOE_BLOCK, tm)
        y_disp = _experts(block_e, n_used, x_disp, moe_w1, moe_w3, moe_w2, l)
        moe = (slot, xn, mod, route, y_disp)
    slot, xn, mod, route, y_disp = moe
    return _combine_final(slot, xn, mod, route, y_disp, final_g.reshape(1, dm), tm, ctx_tiles, ctx_tiles)
```

```python
import functools
import math

import numpy as np
import jax
import jax.numpy as jnp
from jax import lax
from jax.experimental import pallas as pl
from jax.experimental.pallas import tpu as pltpu

F32 = jnp.float32
BF16 = jnp.bfloat16

EPS = 1e-6
NEG_BIG = -1e30
HEADS = 4
HEAD_W = 128
MIX_W = HEADS * HEAD_W
GRID_W = 64
ROPE_BASE = 10000.0
ROPE_PAIRS = 16
B_CONV = 3
N_GROUPS = 4
EXPERTS_PER_GROUP = 8
N_EXPERTS = N_GROUPS * EXPERTS_PER_GROUP
D_EXPERT = 512
MOE_BLOCK = 256
LANES = 128
VMEM_LIMIT = 50 * 1024 * 1024

PA_W = 5 * MIX_W
PB_W = 4 * MIX_W
PG_W = LANES
PC_W = 3 * MIX_W
PROJ_W = PA_W + PB_W + PG_W + PC_W


def _params(sem, **kw):
    return pltpu.CompilerParams(dimension_semantics=sem, vmem_limit_bytes=VMEM_LIMIT, **kw)


def _const_spec(shape):
    nd = len(shape)
    return pl.BlockSpec(shape, lambda *_: (0,) * nd, pipeline_mode=pl.Buffered(1))


def _dot(a, b):
    return jnp.dot(a, b, preferred_element_type=F32)


def _dot_nt(a, b):
    return lax.dot_general(a, b, (((1,), (1,)), ((), ())), preferred_element_type=F32)


def _dot_tn(a, b):
    return lax.dot_general(a, b, (((0,), (0,)), ((), ())), preferred_element_type=F32)


def _dot_sel(m_bf16, x):
    hi = x.astype(BF16)
    r1 = x - hi.astype(F32)
    mid = r1.astype(BF16)
    lo = (r1 - mid.astype(F32)).astype(BF16)
    return _dot(m_bf16, hi) + _dot(m_bf16, mid) + _dot(m_bf16, lo)


def _sigmoid(x):
    return 1.0 / (1.0 + jnp.exp(-x))


def _log_sigmoid(x):
    return jnp.minimum(x, 0.0) - jnp.log1p(jnp.exp(-jnp.abs(x)))


def _rms(x, g):
    return x * lax.rsqrt(jnp.mean(x * x, axis=-1, keepdims=True) + EPS) * g


def _ada_kernel(c_ref, w_ref, b_ref, o_ref):
    c = c_ref[...]
    s = c * _sigmoid(c)
    o_ref[...] = jnp.dot(s, w_ref[...], preferred_element_type=F32,
                         precision=lax.Precision.HIGHEST) + b_ref[...]


def _ada_mod(cc, ada_w, ada_b):
    depth, dm, six = ada_w.shape
    rows = cc.shape[0]
    tn = dm
    return pl.pallas_call(
        _ada_kernel,
        grid=(depth, six // tn),
        in_specs=[pl.BlockSpec((rows, dm), lambda l, n: (0, 0)),
                  pl.BlockSpec((None, dm, tn), lambda l, n: (l, 0, n)),
                  pl.BlockSpec((None, 1, tn), lambda l, n: (l, 0, n))],
        out_specs=pl.BlockSpec((None, rows, tn), lambda l, n: (l, 0, n)),
        out_shape=jax.ShapeDtypeStruct((depth, rows, six), F32),
        compiler_params=_params(("parallel", "parallel")),
        name="ada_mod",
    )(cc, ada_w, ada_b.reshape(depth, 1, six))


def _rope(x, cos, sin):
    n = x.shape[-1]
    lane = lax.broadcasted_iota(jnp.int32, (1, n), 1)
    first = (lane // ROPE_PAIRS) % 2 == 0
    partner = jnp.where(first, pltpu.roll(x, n - ROPE_PAIRS, 1), pltpu.roll(x, ROPE_PAIRS, 1))
    return x * cos + partner * sin


def _hgrn_log_forget(z, lbc):
    lsig = jnp.minimum(z, 0.0) - jnp.log(1.0 + jnp.exp(-jnp.abs(z)))
    a = lbc[0:1]
    bb = lbc[1:2] + lsig
    return jnp.maximum(a, bb) + jnp.log(1.0 + jnp.exp(-jnp.abs(a - bb)))


def _with_gathered_rows(slot_ref, slot_next_ref, yd_ref, ybuf, sems, body):
    i = pl.program_id(0)
    n = pl.num_programs(0)
    tm = ybuf.shape[2]

    def start(table_ref, b):
        _row_copies(tm, lambda r, k: pltpu.make_async_copy(
            yd_ref.at[pl.ds(table_ref[0, k * tm + r], 1), :], ybuf.at[b, k, pl.ds(r, 1), :], sems.at[b]))

    def wait(b):
        for k in range(2):
            pltpu.make_async_copy(yd_ref.at[pl.ds(0, tm), :], ybuf.at[b, k], sems.at[b]).wait()

    @pl.when(i == 0)
    def _():
        start(slot_ref, 0)

    for cur in range(2):
        @pl.when(i % 2 == cur)
        def _():
            wait(cur)
            start(slot_next_ref, 1 - cur)
            body(ybuf[cur, 0], ybuf[cur, 1])

            @pl.when(i == n - 1)
            def _():
                wait(1 - cur)


def _moe_residual(x_ref, mod_ref, rt_ref, y0, y1):
    rt = rt_ref[...]
    return x_ref[...] + mod_ref[5:6, :] * (rt[:, 2:3] * y0 + rt[:, 3:4] * y1)


def _norm_proj_fused_kernel(slot_ref, slot_next_ref, xp_ref, modp_ref, rt_ref, yd_ref,
                            mod_ref, g_ref, w_ref, cos_ref, sin_ref, lbc_ref,
                            xo_ref, pa_ref, pb_ref, pg_ref, qc_ref, kc_ref, vc_ref, ybuf, sems):
    def body(y0, y1):
        x = _moe_residual(xp_ref, modp_ref, rt_ref, y0, y1)
        xo_ref[...] = x
        _project(x, mod_ref, g_ref, w_ref, cos_ref, sin_ref, lbc_ref,
                 pa_ref, pb_ref, pg_ref, qc_ref, kc_ref, vc_ref)

    _with_gathered_rows(slot_ref, slot_next_ref, yd_ref, ybuf, sems, body)


def _norm_proj_kernel(xc_ref, xl_ref, mod_ref, g_ref, w_ref, cos_ref, sin_ref, lbc_ref,
                      pa_ref, pb_ref, pg_ref, qc_ref, kc_ref, vc_ref, *, ctx_tiles):
    x = jnp.where(pl.program_id(1) < ctx_tiles, xc_ref[...], xl_ref[...])
    _project(x, mod_ref, g_ref, w_ref, cos_ref, sin_ref, lbc_ref,
             pa_ref, pb_ref, pg_ref, qc_ref, kc_ref, vc_ref)


def _project(x, mod_ref, g_ref, w_ref, cos_ref, sin_ref, lbc_ref, pa_ref, pb_ref, pg_ref, qc_ref, kc_ref, vc_ref):
    mod = mod_ref[...]
    h = _rms(x, g_ref[...]) * (1.0 + mod[1:2]) + mod[0:1]
    hb = h.astype(BF16)
    pa_ref[:, 0:3 * MIX_W] = _dot(hb, w_ref[:, 0:3 * MIX_W])
    for d in range(2):
        cs = slice((3 + d) * MIX_W, (4 + d) * MIX_W)
        pa_ref[:, cs] = _hgrn_log_forget(_dot(hb, w_ref[:, cs]), lbc_ref[d])
    pb_ref[...] = _dot(hb, w_ref[:, PA_W:PA_W + PB_W])
    pg_ref[...] = _dot(hb, w_ref[:, PA_W + PB_W:PA_W + PB_W + PG_W])
    c0 = PA_W + PB_W + PG_W
    cos = cos_ref[...]
    sin = sin_ref[...]
    q = _dot(hb, w_ref[:, c0:c0 + MIX_W])
    qc_ref[...] = (_rope(q, cos, sin) * (64.0 ** -0.5 * math.log2(math.e))).astype(BF16)
    k = _dot(hb, w_ref[:, c0 + MIX_W:c0 + 2 * MIX_W])
    kc_ref[...] = _rope(k, cos, sin).astype(BF16)
    vc_ref[...] = _dot(hb, w_ref[:, c0 + 2 * MIX_W:c0 + 3 * MIX_W]).astype(BF16)


def _norm_proj_fused(moe, mod, g, w, cos, sin, lbc, tm, ctx_tiles):
    slot, xp, modp, route, y_disp = moe
    bsz, tt, dm = xp.shape
    nt = tt // tm
    n = bsz * nt
    row = lambda i: (i // nt, i % nt, 0)
    mod_row = lambda i: (i // nt, jnp.where(i % nt >= ctx_tiles, 1, 0), 0, 0)
    tab = lambda i: (i % nt, 0)
    outs = [(dm, F32), (PA_W, F32), (PB_W, F32), (PG_W, F32), (MIX_W, BF16), (MIX_W, BF16), (MIX_W, BF16)]
    return pl.pallas_call(
        _norm_proj_fused_kernel,
        grid=(n,),
        in_specs=[pl.BlockSpec((None, 1, 2 * tm), lambda i: (i, 0, 0), memory_space=pltpu.SMEM),
                  pl.BlockSpec((None, 1, 2 * tm), lambda i: (jnp.minimum(i + 1, n - 1), 0, 0),
                               memory_space=pltpu.SMEM),
                  pl.BlockSpec((None, tm, dm), row),
                  pl.BlockSpec((None, None, 6, dm), mod_row),
                  pl.BlockSpec((None, tm, LANES), row),
                  pl.BlockSpec(memory_space=pl.ANY),
                  pl.BlockSpec((None, None, 6, dm), mod_row),
                  _const_spec((1, dm)),
                  _const_spec((dm, PROJ_W)),
                  pl.BlockSpec((tm, MIX_W), tab),
                  pl.BlockSpec((tm, MIX_W), tab),
                  _const_spec(lbc.shape)],
        out_specs=[pl.BlockSpec((None, tm, wd), row) for wd, _ in outs],
        out_shape=[jax.ShapeDtypeStruct((bsz, tt, wd), dt) for wd, dt in outs],
        scratch_shapes=[pltpu.VMEM((2, 2, tm, dm), F32), pltpu.SemaphoreType.DMA((2,))],
        compiler_params=_params(("arbitrary",)),
        name="combine_norm_proj",
    )(slot, slot, xp, modp, route, y_disp, mod, g, w, cos, sin, lbc)


def _norm_proj(x_ctx, x_lat, mod, g, w, cos, sin, lbc, tm, ctx_tiles):
    bsz, t_ctx, dm = x_ctx.shape
    tt = t_ctx + x_lat.shape[1]
    nt = tt // tm
    row = lambda b, t: (b, t, 0)
    outs = [(PA_W, F32), (PB_W, F32), (PG_W, F32), (MIX_W, BF16), (MIX_W, BF16), (MIX_W, BF16)]
    return pl.pallas_call(
        functools.partial(_norm_proj_kernel, ctx_tiles=ctx_tiles),
        grid=(bsz, nt),
        in_specs=[pl.BlockSpec((None, tm, dm), lambda b, t: (b, jnp.minimum(t, ctx_tiles - 1), 0)),
                  pl.BlockSpec((None, tm, dm), lambda b, t: (b, jnp.maximum(t - ctx_tiles, 0), 0)),
                  pl.BlockSpec((None, None, 6, dm), lambda b, t: (b, jnp.where(t >= ctx_tiles, 1, 0), 0, 0)),
                  _const_spec((1, dm)),
                  _const_spec((dm, PROJ_W)),
                  pl.BlockSpec((tm, MIX_W), lambda b, t: (t, 0)),
                  pl.BlockSpec((tm, MIX_W), lambda b, t: (t, 0)),
                  _const_spec(lbc.shape)],
        out_specs=[pl.BlockSpec((None, tm, wd), row) for wd, _ in outs],
        out_shape=[jax.ShapeDtypeStruct((bsz, tt, wd), dt) for wd, dt in outs],
        compiler_params=_params(("parallel", "parallel")),
        name="norm_proj",
    )(x_ctx, x_lat, mod, g, w, cos, sin, lbc)


def _conv_kernel(x_ref, w_ref, b_ref, o_ref, *, ctx, q_blocks):
    x = x_ref[...]
    tt = x.shape[0]
    row = lax.broadcasted_iota(jnp.int32, (tt, 1), 0)
    prev = jnp.where((row == 0) | (row == ctx), 0.0, pltpu.roll(x, 1, 0))
    nxt = jnp.where((row == ctx - 1) | (row == tt - 1), 0.0, pltpu.roll(x, tt - 1, 0))
    w = w_ref[...]
    y = b_ref[...] + prev * w[0:1] + x * w[1:2] + nxt * w[2:3]
    y = y * _sigmoid(y)
    scale = jnp.where(pl.program_id(1) < q_blocks, HEAD_W ** -0.5, 1.0)
    o_ref[...] = y * scale


def _conv_silu(pb, conv_w, conv_b, ctx, cb):
    bsz, tt, _ = pb.shape
    width = 2 * MIX_W
    return pl.pallas_call(
        functools.partial(_conv_kernel, ctx=ctx, q_blocks=MIX_W // cb),
        grid=(bsz, width // cb),
        in_specs=[pl.BlockSpec((None, tt, cb), lambda b, j: (b, 0, j)),
                  pl.BlockSpec((B_CONV, cb), lambda b, j: (0, j)),
                  pl.BlockSpec((1, cb), lambda b, j: (0, j))],
        out_specs=pl.BlockSpec((None, tt, cb), lambda b, j: (b, 0, j)),
        out_shape=jax.ShapeDtypeStruct((bsz, tt, width), F32),
        compiler_params=_params(("parallel", "parallel")),
        name="mlstm_conv",
    )(pb, conv_w, conv_b.reshape(1, width))


def _chunk_consts(L):
    idx = np.arange(L)
    t, u = idx[:, None], idx[None, :]
    mats = [(u <= t)]
    for h in range(1, N_MATMUL_LEVELS + 1):
        mid = (t // (2 * h)) * (2 * h) + h - 1
        mats.append(((u > mid) & (u <= t)) | ((u > t) & (u <= mid)))
    cum = np.concatenate([m.astype(np.float32) for m in mats], axis=0)
    cum_b = np.concatenate([m[::-1, ::-1].astype(np.float32) for m in mats], axis=0)
    nlev = int(round(math.log2(L)))
    lvl = np.full((L, L), -1, np.int32)
    lvl[idx, idx] = nlev
    for i in range(nlev):
        h = L >> (i + 1)
        same = (t // (2 * h)) == (u // (2 * h))
        lvl[same & (t % (2 * h) >= h) & (u % (2 * h) < h)] = i
    sel = jnp.asarray(np.stack([cum, cum_b]), BF16)
    lvls = jnp.asarray(np.stack([lvl, lvl[::-1, ::-1]]))
    return sel, lvls, nlev


N_MATMUL_LEVELS = 0


def _neg_abs(x):
    return -jnp.abs(x)


def _midpoint_rows(b, h, d):
    L, w = b.shape
    two = 2 * h
    pos = h - 1 if d == 0 else h
    if two % 8 == 0:
        r = b.reshape(L // two, two, w)[:, pos:pos + 1, :]
        return jnp.broadcast_to(r, (L // two, two, w)).reshape(L, w)
    phase = lax.broadcasted_iota(jnp.int32, (L, 1), 0) % two
    ref = b
    for off in range(pos - two + 1, pos + 1):
        if off != 0:
            ref = jnp.where(phase == pos - off, pltpu.roll(b, (L - off) % L, 0), ref)
    return ref


def _scan_blocks(j, nc_ctx, nc):
    jb = jnp.where(j < nc_ctx, nc_ctx - 1 - j, nc - 1 + nc_ctx - j)
    return j, jb


def _hgrn_kernel(qf_ref, vf_ref, ff_ref, qb_ref, vb_ref, fb_ref, sel_ref, lvl_ref,
                 of_ref, ob_ref, st_ref, *, L, nlev):
    @pl.when(pl.program_id(1) == 0)
    def _():
        st_ref[...] = jnp.zeros_like(st_ref)

    row = lax.broadcasted_iota(jnp.int32, (L, 1), 0)
    dirs = ((qf_ref, vf_ref, ff_ref, of_ref), (qb_ref, vb_ref, fb_ref, ob_ref))

    def one(bb, d, q_ref, v_ref, f_ref, o_ref):
        lf = f_ref[bb]
        kk = 1.0 - jnp.exp(lf)
        q_all = q_ref[bb]
        ex = _dot_sel(sel_ref[d], lf) * math.log2(math.e)
        b_in = ex[0:L]
        lvl = lvl_ref[d]
        last = L - 1 if d == 0 else 0
        b_end = b_in[last:last + 1]
        heads = [slice(hd * HEAD_W, (hd + 1) * HEAD_W) for hd in range(HEADS)]
        qb = q_all.astype(BF16)
        kb = kk.astype(BF16)
        on_diag = lvl == nlev
        scores = [jnp.where(on_diag, _dot_nt(qb[:, cs], kb[:, cs]), 0.0) for cs in heads]
        for i in range(nlev):
            h = L >> (i + 1)
            if h <= N_MATMUL_LEVELS:
                expo = ex[h * L:(h + 1) * L]
            else:
                expo = _neg_abs(b_in - _midpoint_rows(b_in, h, d))
            ei = jnp.exp2(expo)
            is_query = (row % (2 * h) >= h) if d == 0 else (row % (2 * h) < h)
            xe = (jnp.where(is_query, q_all, kk) * ei).astype(BF16)
            at_level = lvl == i
            for hd, cs in enumerate(heads):
                scores[hd] = scores[hd] + jnp.where(at_level, _dot_nt(xe[:, cs], xe[:, cs]), 0.0)
        q_in = (q_all * jnp.exp2(b_in)).astype(BF16)
        k_out = (kk * jnp.exp2(b_end - b_in)).astype(BF16)
        decay = jnp.exp2(b_end)
        for hd, cs in enumerate(heads):
            vb = v_ref[bb, :, cs].astype(BF16)
            st = st_ref[bb, d, hd]
            o_ref[bb, :, cs] = _dot(scores[hd].astype(BF16), vb) + _dot_nt(q_in[:, cs], st.astype(BF16))
            st_ref[bb, d, hd] = st * decay[:, cs] + _dot_tn(vb, k_out[:, cs])

    for bb in range(qf_ref.shape[0]):
        for d, refs in enumerate(dirs):
            one(bb, d, *refs)


def _hgrn(pa, L, nc_ctx):
    bsz, tt, _ = pa.shape
    nc = tt // L
    nb = 2 if bsz % 2 == 0 else 1
    sel, lvl, nlev = _chunk_consts(L)

    def spec(col, which):
        return pl.BlockSpec((nb, L, MIX_W), lambda b, j: (b, _scan_blocks(j, nc_ctx, nc)[which], col))

    return pl.pallas_call(
        functools.partial(_hgrn_kernel, L=L, nlev=nlev),
        grid=(bsz // nb, nc),
        in_specs=[spec(0, 0), spec(1, 0), spec(3, 0), spec(0, 1), spec(1, 1), spec(4, 1),
                  _const_spec(sel.shape), _const_spec(lvl.shape)],
        out_specs=[spec(0, 0), spec(0, 1)],
        out_shape=[jax.ShapeDtypeStruct((bsz, tt, MIX_W), F32)] * 2,
        scratch_shapes=[pltpu.VMEM((nb, 2, HEADS, HEAD_W, HEAD_W), F32)],
        compiler_params=_params(("parallel", "arbitrary")),
        name="hgrn2_scan",
    )(pa, pa, pa, pa, pa, pa, sel, lvl)


def _mlstm_kernel(qf_ref, kf_ref, vf_ref, gf_ref, qb_ref, kb_ref, vb_ref, gb_ref, gbias_ref,
                  sel_ref, lvl_ref, of_ref, ob_ref, c_ref, m_ref, *, L, nlev):
    @pl.when(pl.program_id(1) == 0)
    def _():
        c_ref[...] = jnp.zeros_like(c_ref)
        m_ref[...] = jnp.zeros_like(m_ref)

    lane = lax.broadcasted_iota(jnp.int32, (1, LANES), 1)
    tlane = lax.broadcasted_iota(jnp.int32, (HEADS, L), 1)
    ones_col = jnp.where(lane == 0, 1.0, 0.0).astype(BF16)
    dirs = ((qf_ref, kf_ref, vf_ref, gf_ref, of_ref), (qb_ref, kb_ref, vb_ref, gb_ref, ob_ref))
    for d, (q_ref, k_ref, v_ref, g_ref, o_ref) in enumerate(dirs):
        g = g_ref[...] + gbias_ref[...]
        lf = jnp.where((lane >= 2 * HEADS) & (lane < 4 * HEADS), _log_sigmoid(g), 0.0)
        bcum = _dot_sel(sel_ref[d, 0:L], lf)
        causal = lvl_ref[d] >= 0
        rows = jnp.where(lane < 2 * HEADS, g - pltpu.roll(bcum, LANES - 2 * HEADS, 1), bcum).T
        r4 = rows[d * HEADS:(d + 1) * HEADS]
        b4 = rows[(2 + d) * HEADS:(3 + d) * HEADS]
        m4 = m_ref[d * HEADS:(d + 1) * HEADS, 0:1]
        run = r4
        sh = 1
        while sh < L:
            if d == 0:
                run = jnp.maximum(run, jnp.where(tlane >= sh, pltpu.roll(run, sh, 1), -jnp.inf))
            else:
                run = jnp.maximum(run, jnp.where(tlane < L - sh, pltpu.roll(run, L - sh, 1), -jnp.inf))
            sh *= 2
        c4 = jnp.maximum(run, m4)
        cols = jnp.concatenate([c4, jnp.exp(m4 - c4), jnp.exp(-b4 - c4),
                                jnp.zeros((LANES - 3 * HEADS, L), F32)], axis=0).T
        last = L - 1 if d == 0 else 0
        for hd in range(HEADS):
            cs = slice(hd * HEAD_W, (hd + 1) * HEAD_W)
            row_term = r4[hd:hd + 1, :]
            b_end = b4[hd:hd + 1, last:last + 1]
            r = d * HEADS + hd
            m_prev = m4[hd:hd + 1]
            q = q_ref[:, cs].astype(BF16)
            k = k_ref[:, cs]
            v_aug = jnp.concatenate([v_ref[:, cs].astype(BF16), jnp.broadcast_to(ones_col, (L, LANES))], axis=1)
            c_prev = c_ref[r]
            w_end = b_end + row_term
            m_new = jnp.maximum(b_end + m_prev, jnp.max(w_end, axis=-1, keepdims=True))
            e_end = jnp.exp(w_end - m_new)
            keep = jnp.exp(b_end + m_prev - m_new)
            c_ref[r] = keep * c_prev + _dot((k.T * e_end).astype(BF16), v_aug)
            m_ref[r:r + 1, :] = jnp.broadcast_to(m_new, (1, LANES))
            c_t = cols[:, hd:hd + 1]
            a_state = cols[:, HEADS + hd:HEADS + hd + 1]
            floor = cols[:, 2 * HEADS + hd:2 * HEADS + hd + 1]
            s = _dot_nt(q, k.astype(BF16)) * jnp.exp(jnp.where(causal, row_term - c_t, NEG_BIG))
            num = _dot(s.astype(BF16), v_aug) + a_state * _dot(q, c_prev.astype(BF16))
            den = num[:, HEAD_W:HEAD_W + 1]
            o_ref[:, cs] = num[:, 0:HEAD_W] / jnp.maximum(jnp.abs(den), floor)


def _mlstm(qk, pb, pg, gbias, L, nc_ctx):
    bsz, tt, _ = pb.shape
    nc = tt // L
    sel, lvl, nlev = _chunk_consts(L)

    def spec(col, which, width=MIX_W):
        return pl.BlockSpec((None, L, width), lambda b, j: (b, _scan_blocks(j, nc_ctx, nc)[which], col))

    return pl.pallas_call(
        functools.partial(_mlstm_kernel, L=L, nlev=nlev),
        grid=(bsz, nc),
        in_specs=[spec(0, 0), spec(1, 0), spec(2, 0), spec(0, 0, PG_W),
                  spec(0, 1), spec(1, 1), spec(2, 1), spec(0, 1, PG_W),
                  _const_spec((1, PG_W)), _const_spec(sel.shape), _const_spec(lvl.shape)],
        out_specs=[spec(0, 0), spec(0, 1)],
        out_shape=[jax.ShapeDtypeStruct((bsz, tt, MIX_W), F32)] * 2,
        scratch_shapes=[pltpu.VMEM((2 * HEADS, HEAD_W, 2 * HEAD_W), F32),
                        pltpu.VMEM((2 * HEADS, LANES), F32)],
        compiler_params=_params(("parallel", "arbitrary")),
        name="mlstm_scan",
    )(qk, qk, pb, pg, qk, qk, pb, pg, gbias, sel, lvl)


def _diff_lambda(lam_ref, lam_init):
    lv = lam_ref[...]
    return (jnp.exp(jnp.sum(lv[0:1] * lv[1:2], axis=-1, keepdims=True))
            - jnp.exp(jnp.sum(lv[2:3] * lv[3:4], axis=-1, keepdims=True)) + lam_init)


def _map_scores(q, k):
    lane = lax.broadcasted_iota(jnp.int32, (1, HEAD_W), 1)
    zero = jnp.zeros_like(q)
    return (_dot_nt(jnp.where(lane < HEAD_W // 2, q, zero), k),
            _dot_nt(jnp.where(lane >= HEAD_W // 2, q, zero), k))


def _row_max(s):
    return jnp.max(s, axis=-1, keepdims=True)


def _diff_softmax_v(s1, m1, s2, m2, lam, v):
    p1 = jnp.exp2(s1 - m1)
    p2 = jnp.exp2(s2 - m2)
    l1 = jnp.sum(p1, axis=-1, keepdims=True)
    l2 = jnp.sum(p2, axis=-1, keepdims=True)
    return _dot((p1 - (lam * l1 / l2) * p2).astype(BF16), v) / l1


def _attn_ctx_kernel(lam_ref, q_ref, k_ref, v_ref, o_ref, *, lam_init):
    s1, s2 = _map_scores(q_ref[...], k_ref[...])
    o_ref[...] = _diff_softmax_v(s1, _row_max(s1), s2, _row_max(s2), _diff_lambda(lam_ref, lam_init), v_ref[...])


def _attn_ctx(lam_vec, qc, kc, vc, lam_init, tq, t_ctx):
    bsz = qc.shape[0]
    kv = pl.BlockSpec((None, t_ctx, HEAD_W), lambda b, h, i: (b, 0, h))
    return pl.pallas_call(
        functools.partial(_attn_ctx_kernel, lam_init=lam_init),
        grid=(bsz, HEADS, t_ctx // tq),
        in_specs=[_const_spec(lam_vec.shape),
                  pl.BlockSpec((None, tq, HEAD_W), lambda b, h, i: (b, i, h)), kv, kv],
        out_specs=pl.BlockSpec((None, tq, HEAD_W), lambda b, h, i: (b, i, h)),
        out_shape=jax.ShapeDtypeStruct((bsz, t_ctx, MIX_W), F32),
        compiler_params=_params(("parallel", "parallel", "arbitrary")),
        name="diff_attn_ctx",
    )(lam_vec, qc, kc, vc)


def _attn_lat_kernel(lam_ref, qa_ref, qb_ref, k_ref, v_ref, o_ref, s_even, m_even, s_odd, m_odd,
                     *, lam_init, n_tiles):
    j = pl.program_id(2)
    lam = _diff_lambda(lam_ref, lam_init)
    even, odd = (s_even, m_even), (s_odd, m_odd)
    half = qa_ref.shape[0]

    def scores_into(bufs):
        s_buf, m_buf = bufs
        for part, q_ref in enumerate((qa_ref, qb_ref)):
            rows = slice(part * half, (part + 1) * half)
            for i, s in enumerate(_map_scores(q_ref[...], k_ref[...])):
                s_buf[i, rows, :] = s
                m_buf[i, rows, :] = jnp.broadcast_to(_row_max(s), (half, LANES))

    def finish_from(bufs):
        s_buf, m_buf = bufs
        for part in range(2):
            rows = slice(part * half, (part + 1) * half)
            o_ref[rows, :] = _diff_softmax_v(s_buf[0, rows, :], m_buf[0, rows, 0:1],
                                             s_buf[1, rows, :], m_buf[1, rows, 0:1], lam, v_ref[...])

    @pl.when(j == 0)
    def _():
        scores_into(even)

    middle = (j > 0) & (j < n_tiles)

    @pl.when(middle & (j % 2 == 1))
    def _():
        scores_into(odd)
        finish_from(even)

    @pl.when(middle & (j % 2 == 0))
    def _():
        scores_into(even)
        finish_from(odd)

    @pl.when(j == n_tiles)
    def _():
        finish_from(odd if n_tiles % 2 == 0 else even)


def _attn_lat(lam_vec, qc, kc, vc, lam_init, tq, t_ctx):
    bsz, tt, _ = qc.shape
    n_tiles = (tt - t_ctx) // (2 * tq)
    q0 = t_ctx // tq

    def q_spec(part):
        return pl.BlockSpec((None, tq, HEAD_W),
                            lambda b, h, j: (b, q0 + 2 * jnp.minimum(j, n_tiles - 1) + part, h))

    kv = pl.BlockSpec((None, tt, HEAD_W), lambda b, h, j: (b, 0, h))
    return pl.pallas_call(
        functools.partial(_attn_lat_kernel, lam_init=lam_init, n_tiles=n_tiles),
        grid=(bsz, HEADS, n_tiles + 1),
        in_specs=[_const_spec(lam_vec.shape), q_spec(0), q_spec(1), kv, kv],
        out_specs=pl.BlockSpec((None, 2 * tq, HEAD_W), lambda b, h, j: (b, jnp.maximum(j - 1, 0), h)),
        out_shape=jax.ShapeDtypeStruct((bsz, tt - t_ctx, MIX_W), F32),
        scratch_shapes=[pltpu.VMEM((2, 2 * tq, tt), F32), pltpu.VMEM((2, 2 * tq, LANES), F32),
                        pltpu.VMEM((2, 2 * tq, tt), F32), pltpu.VMEM((2, 2 * tq, LANES), F32)],
        compiler_params=_params(("parallel", "parallel", "arbitrary")),
        name="diff_attn_lat",
    )(lam_vec, qc, qc, kc, vc)


def _head_norm(x, g):
    parts = []
    for hd in range(HEADS):
        xs = x[:, hd * HEAD_W:(hd + 1) * HEAD_W]
        parts.append(xs * lax.rsqrt(jnp.mean(xs * xs, axis=-1, keepdims=True) + EPS))
    return jnp.concatenate(parts, axis=1) * g


ROUTE_ROWS = 40


def _route(lg):
    n = lg.shape[1]
    row = lax.broadcasted_iota(jnp.int32, lg.shape, 0)
    neg = -jnp.inf
    big = ROUTE_ROWS
    gl = jnp.where(row < N_GROUPS, lg, neg)
    gmax = jnp.max(gl, axis=0, keepdims=True)
    gidx = jnp.min(jnp.where(gl == gmax, row, big), axis=0, keepdims=True)
    p_group = 1.0 / jnp.sum(jnp.exp(gl - gmax), axis=0, keepdims=True)
    lo = N_GROUPS + EXPERTS_PER_GROUP * gidx
    in_grp = (row >= lo) & (row < lo + EXPERTS_PER_GROUP)
    el = jnp.where(in_grp, lg, neg)
    pe = jnp.exp(el - jnp.max(el, axis=0, keepdims=True))
    pe = pe / jnp.sum(pe, axis=0, keepdims=True)
    pe = jnp.where(in_grp, pe, -1.0)
    v1 = jnp.max(pe, axis=0, keepdims=True)
    i1 = jnp.min(jnp.where(pe == v1, row, big), axis=0, keepdims=True)
    pe2 = jnp.where(row == i1, -1.0, pe)
    v2 = jnp.max(pe2, axis=0, keepdims=True)
    i2 = jnp.min(jnp.where(pe2 == v2, row, big), axis=0, keepdims=True)
    scale = p_group / (v1 + v2)
    erow = lax.broadcasted_iota(jnp.int32, (N_EXPERTS, n), 0) + N_GROUPS
    oh1 = erow == i1
    oh2 = erow == i2
    oh1f = jnp.where(oh1, 1.0, 0.0)
    oh2f = jnp.where(oh2, 1.0, 0.0)
    earlier = (lax.broadcasted_iota(jnp.int32, (n, n), 0) < lax.broadcasted_iota(jnp.int32, (n, n), 1))
    earlier = jnp.where(earlier, 1.0, 0.0).astype(BF16)
    tot1 = jnp.sum(oh1f, axis=1, keepdims=True)
    tot2 = jnp.sum(oh2f, axis=1, keepdims=True)
    rank1 = jnp.sum(jnp.where(oh1, _dot(oh1f.astype(BF16), earlier), 0.0), axis=0, keepdims=True)
    rank2 = jnp.sum(jnp.where(oh2, _dot(oh2f.astype(BF16), earlier) + tot1, 0.0), axis=0, keepdims=True)
    r8 = lax.broadcasted_iota(jnp.int32, (8, n), 0)
    rows = jnp.zeros((8, n), F32)
    for i, val in enumerate(((i1 - N_GROUPS).astype(F32), (i2 - N_GROUPS).astype(F32), v1 * scale, v2 * scale,
                             rank1, rank2)):
        rows = jnp.where(r8 == i, val, rows)
    return rows, tot1 + tot2


def _merge_kernel(xc_ref, xl_ref, mod_ref, modb_ref, n1_ref, n2_ref, af_ref, ab_ref, ga_ref, bf_ref, bb_ref, gb_ref,
                  occ_ref, ocl_ref, hg_ref, wg_ref, bg_ref, wbr_ref, wo_ref, rw_ref, rb_ref,
                  xo_ref, h2_ref, rt_ref, rtt_ref, cnt_ref, x_even, x_odd,
                  *, lam_init, ctx_tiles, row0_tiles, nt, n):
    i = pl.program_id(0)
    is_ctx_tile = row0_tiles + jnp.minimum(i, n - 1) % nt < ctx_tiles

    def front(stash):
        xn = _merge_front(is_ctx_tile, xc_ref, xl_ref, mod_ref, n1_ref, af_ref, ab_ref, ga_ref, bf_ref, bb_ref,
                          gb_ref, occ_ref, ocl_ref, hg_ref, wg_ref, bg_ref, wbr_ref, wo_ref, lam_init)
        xo_ref[...] = xn
        stash[...] = xn

    def back(stash):
        _merge_back(stash[...], modb_ref, n2_ref, rw_ref, rb_ref, h2_ref, rt_ref, rtt_ref, cnt_ref)

    @pl.when(i == 0)
    def _():
        front(x_even)

    middle = (i > 0) & (i < n)

    @pl.when(middle & (i % 2 == 1))
    def _():
        front(x_odd)
        back(x_even)

    @pl.when(middle & (i % 2 == 0))
    def _():
        front(x_even)
        back(x_odd)

    @pl.when(i == n)
    def _():
        back(x_odd if n % 2 == 0 else x_even)


def _merge_front(is_ctx_tile, xc_ref, xl_ref, mod_ref, n1_ref, af_ref, ab_ref, ga_ref, bf_ref, bb_ref, gb_ref,
                 occ_ref, ocl_ref, hg_ref, wg_ref, bg_ref, wbr_ref, wo_ref, lam_init):
    x = jnp.where(is_ctx_tile, xc_ref[...], xl_ref[...])
    mod = mod_ref[...]
    dm = x.shape[-1]
    hb = (_rms(x, n1_ref[...]) * (1.0 + mod[1:2]) + mod[0:1]).astype(BF16)
    hg = hg_ref[...]
    ga = ga_ref[...]
    ya = _head_norm(af_ref[...] + ab_ref[...], hg[0:1]) * (ga * _sigmoid(ga))
    yb = _head_norm(bf_ref[...] + bb_ref[...], hg[1:2]) * _sigmoid(gb_ref[...])
    oc = jnp.where(is_ctx_tile, occ_ref[...], ocl_ref[...])
    yc = _head_norm(oc, hg[2:3]) * (1.0 - lam_init)
    y = jnp.zeros_like(x)
    for i, yi in enumerate((ya, yb, yc)):
        gate = _sigmoid(_dot(hb, wg_ref[:, i * dm:(i + 1) * dm]) + bg_ref[:, i * dm:(i + 1) * dm])
        y = y + gate * _dot(yi.astype(BF16), wbr_ref[i])
    return x + mod[2:3] * _dot(y.astype(BF16), wo_ref[...])


def _merge_back(xn, mod_ref, n2_ref, rw_ref, rb_ref, h2_ref, rt_ref, rtt_ref, cnt_ref):
    mod = mod_ref[...]
    h2 = _rms(xn, n2_ref[...]) * (1.0 + mod[4:5]) + mod[3:4]
    h2_ref[...] = h2
    h_hi = h2.astype(BF16)
    h_mid = (h2 - h_hi.astype(F32)).astype(BF16)
    logits = (_dot_nt(rw_ref[0], h_hi) + _dot_nt(rw_ref[0], h_mid) + _dot_nt(rw_ref[1], h_hi)) + rb_ref[...]
    rows, counts = _route(logits[0:ROUTE_ROWS, :])
    rtt_ref[...] = rows
    cnt_ref[...] = jnp.broadcast_to(counts, cnt_ref.shape)
    tm = xn.shape[0]
    rt_ref[...] = jnp.concatenate([rows, jnp.zeros((LANES - 8, tm), F32)], axis=0).T


def _merge(x_ctx, x_lat, lat_row0, mod, n1, n2, oaf, oab, pa, obf, obb, pb, oc_ctx, oc_lat, hg, wg, bg, wbr, wo,
           rw, rb, lam_init, tm, ctx_tiles, row0_tiles):
    bsz, tt, _ = pa.shape
    dm = x_lat.shape[-1]
    nt = tt // tm - row0_tiles
    rows = nt * tm
    n = bsz * nt

    def front(fn):
        return lambda i: fn(jnp.minimum(i, n - 1) // nt, jnp.minimum(i, n - 1) % nt)

    def back(fn):
        return lambda i: fn(jnp.maximum(i - 1, 0) // nt, jnp.maximum(i - 1, 0) % nt)

    def col(c):
        return pl.BlockSpec((None, tm, MIX_W), front(lambda b, t: (b, row0_tiles + t, c)))

    mod_of = lambda b, t: (b, jnp.where(row0_tiles + t >= ctx_tiles, 1, 0), 0, 0)
    out_specs = [pl.BlockSpec((None, tm, dm), front(lambda b, t: (b, t, 0))),
                 pl.BlockSpec((None, tm, dm), back(lambda b, t: (b, t, 0))),
                 pl.BlockSpec((None, tm, LANES), back(lambda b, t: (b, t, 0))),
                 pl.BlockSpec((None, 8, tm), back(lambda b, t: (b * nt + t, 0, 0))),
                 pl.BlockSpec((None, N_EXPERTS, LANES), back(lambda b, t: (b * nt + t, 0, 0)))]
    out_shape = [jax.ShapeDtypeStruct((bsz, rows, dm), F32), jax.ShapeDtypeStruct((bsz, rows, dm), F32),
                 jax.ShapeDtypeStruct((bsz, rows, LANES), F32), jax.ShapeDtypeStruct((n, 8, tm), F32),
                 jax.ShapeDtypeStruct((n, N_EXPERTS, LANES), F32)]
    return pl.pallas_call(
        functools.partial(_merge_kernel, lam_init=lam_init, ctx_tiles=ctx_tiles, row0_tiles=row0_tiles,
                          nt=nt, n=n),
        grid=(n + 1,),
        in_specs=[pl.BlockSpec((None, tm, dm),
                               front(lambda b, t: (b, jnp.minimum(row0_tiles + t, max(ctx_tiles - 1, 0)), 0))),
                  pl.BlockSpec((None, tm, dm),
                               front(lambda b, t: (b, lat_row0 + jnp.maximum(row0_tiles + t - ctx_tiles, 0), 0))),
                  pl.BlockSpec((None, None, 6, dm), front(mod_of)),
                  pl.BlockSpec((None, None, 6, dm), back(mod_of)),
                  _const_spec((1, dm)), _const_spec((1, dm)),
                  col(0), col(0), col(2), col(0), col(0), col(3),
                  pl.BlockSpec((None, tm, MIX_W),
                               front(lambda b, t: (b, jnp.minimum(row0_tiles + t, max(ctx_tiles - 1, 0)), 0))),
                  pl.BlockSpec((None, tm, MIX_W),
                               front(lambda b, t: (b, jnp.maximum(row0_tiles + t - ctx_tiles, 0), 0))),
                  _const_spec((3, MIX_W)), _const_spec(wg.shape), _const_spec(bg.shape),
                  _const_spec(wbr.shape), _const_spec(wo.shape), _const_spec(rw.shape), _const_spec(rb.shape)],
        out_specs=out_specs,
        out_shape=out_shape,
        scratch_shapes=[pltpu.VMEM((tm, dm), F32), pltpu.VMEM((tm, dm), F32)],
        compiler_params=_params(("arbitrary",)),
        name="merge_route",
    )(x_ctx, x_lat, mod, mod, n1, n2, oaf, oab, pa, obf, obb, pb, oc_ctx, oc_lat, hg, wg, bg, wbr, wo, rw, rb)


def _slot_tables(route_rows, counts):
    tiles, _, tm = route_rows.shape
    cnt = counts[:, :, 0].astype(jnp.int32)
    tile_off = jnp.cumsum(cnt, axis=0) - cnt
    total = jnp.sum(cnt, axis=0)
    padded = (total + MOE_BLOCK - 1) // MOE_BLOCK * MOE_BLOCK
    p_end = jnp.cumsum(padded)
    base = (p_end - padded)[None, :] + tile_off
    e = route_rows[:, 0:2, :].astype(jnp.int32)
    rank = route_rows[:, 4:6, :].astype(jnp.int32)
    hit = e[:, :, :, None] == jnp.arange(N_EXPERTS, dtype=jnp.int32)
    slot = jnp.sum(jnp.where(hit, base[:, None, None, :], 0), axis=-1) + rank
    n_blocks = -(-(2 * tiles * tm) // MOE_BLOCK) + N_EXPERTS
    blk_start = jnp.arange(n_blocks, dtype=jnp.int32) * MOE_BLOCK
    block_e = jnp.minimum(jnp.sum(blk_start[:, None] >= p_end[None, :], axis=1), N_EXPERTS - 1).astype(jnp.int32)
    n_used = (p_end[-1] // MOE_BLOCK).astype(jnp.int32).reshape(1)
    filled = jnp.clip(((p_end - padded) + total)[block_e] - blk_start, 0, MOE_BLOCK)
    filled = jnp.where(blk_start < p_end[-1], filled, 0)
    partial = (filled < MOE_BLOCK).astype(jnp.int32)
    return slot.reshape(tiles, 1, 2 * tm), block_e, n_used, partial


def _row_copies(n_rows, make):
    for r in range(n_rows):
        for k in range(2):
            make(r, k).start()


def _scatter_kernel(partial_ref, slot_ref, h_ref, xd_ref, zeros, sem, zsem):
    tm = h_ref.shape[0]
    n_blocks = partial_ref.shape[0]

    @pl.when((pl.program_id(0) == 0) & (pl.program_id(1) == 0))
    def _():
        zeros[...] = jnp.zeros_like(zeros)

        def zero_block(blk):
            return pltpu.make_async_copy(zeros, xd_ref.at[pl.ds(blk * MOE_BLOCK, MOE_BLOCK), :], zsem)

        def start(blk, c):
            @pl.when(partial_ref[blk] != 0)
            def _():
                zero_block(blk).start()
            return c

        def wait(blk, c):
            @pl.when(partial_ref[blk] != 0)
            def _():
                zero_block(blk).wait()
            return c

        lax.fori_loop(0, n_blocks, start, 0)
        lax.fori_loop(0, n_blocks, wait, 0)

    _row_copies(tm, lambda r, k: pltpu.make_async_copy(
        h_ref.at[pl.ds(r, 1), :], xd_ref.at[pl.ds(slot_ref[0, k * tm + r], 1), :], sem))
    for _ in range(2):
        pltpu.make_async_copy(h_ref, xd_ref.at[pl.ds(0, tm), :], sem).wait()


def _scatter(partial, slot, h2, n_slots, tm):
    bsz, rows, dm = h2.shape
    nt = rows // tm
    grid_spec = pltpu.PrefetchScalarGridSpec(
        num_scalar_prefetch=1,
        grid=(bsz, nt),
        in_specs=[pl.BlockSpec((None, 1, 2 * tm), lambda b, t, p: (b * nt + t, 0, 0), memory_space=pltpu.SMEM),
                  pl.BlockSpec((tm, dm), lambda b, t, p: (b * nt + t, 0))],
        out_specs=pl.BlockSpec(memory_space=pl.ANY),
        scratch_shapes=[pltpu.VMEM((MOE_BLOCK, dm), F32), pltpu.SemaphoreType.DMA(()),
                        pltpu.SemaphoreType.DMA(())],
    )
    return pl.pallas_call(
        _scatter_kernel,
        grid_spec=grid_spec,
        out_shape=jax.ShapeDtypeStruct((n_slots, dm), F32),
        compiler_params=_params(("arbitrary", "arbitrary"), has_side_effects=True),
        name="moe_scatter",
    )(partial, slot, h2.reshape(bsz * rows, dm))


def _expert_kernel(be_ref, nu_ref, x_ref, w1_ref, w3_ref, w2_ref, y_ref, w1b, w3b, w2b):
    i = pl.program_id(0)

    @pl.when(i < nu_ref[0])
    def _():
        @pl.when((i == 0) | (be_ref[i] != be_ref[jnp.maximum(i - 1, 0)]))
        def _():
            w1b[...] = w1_ref[...].astype(BF16)
            w3b[...] = w3_ref[...].astype(BF16)
            w2b[...] = w2_ref[...].astype(BF16)

        xb = x_ref[...].astype(BF16)
        u = _dot(xb, w1b[...])
        hmid = (u * _sigmoid(u)) * _dot(xb, w3b[...])
        y_ref[...] = _dot(hmid.astype(BF16), w2b[...])

    @pl.when(i >= nu_ref[0])
    def _():
        y_ref[...] = jnp.zeros_like(y_ref)


def _experts(block_e, n_used, x_disp, w1, w3, w2, layer):
    n_slots, dm = x_disp.shape
    n_blocks = n_slots // MOE_BLOCK
    wspec = lambda shape: pl.BlockSpec((None, None) + shape, lambda i, be, nu: (layer, be[i], 0, 0))
    grid_spec = pltpu.PrefetchScalarGridSpec(
        num_scalar_prefetch=2,
        grid=(n_blocks,),
        in_specs=[pl.BlockSpec((MOE_BLOCK, dm), lambda i, be, nu: (jnp.minimum(i, nu[0] - 1), 0)),
                  wspec((dm, D_EXPERT)), wspec((dm, D_EXPERT)), wspec((D_EXPERT, dm))],
        out_specs=pl.BlockSpec((MOE_BLOCK, dm), lambda i, be, nu: (i, 0)),
        scratch_shapes=[pltpu.VMEM((dm, D_EXPERT), BF16), pltpu.VMEM((dm, D_EXPERT), BF16),
                        pltpu.VMEM((D_EXPERT, dm), BF16)],
    )
    return pl.pallas_call(
        _expert_kernel,
        grid_spec=grid_spec,
        out_shape=jax.ShapeDtypeStruct((n_slots, dm), F32),
        compiler_params=_params(("arbitrary",)),
        name="moe_experts",
    )(block_e, n_used, x_disp, w1, w3, w2)


def _combine_kernel(slot_ref, slot_next_ref, x_ref, mod_ref, rt_ref, yd_ref, fg_ref, o_ref, ybuf, sems):
    def body(y0, y1):
        o_ref[...] = _rms(_moe_residual(x_ref, mod_ref, rt_ref, y0, y1), fg_ref[...])

    _with_gathered_rows(slot_ref, slot_next_ref, yd_ref, ybuf, sems, body)


def _combine_final(slot, x, mod, route, y_disp, fg, tm, ctx_tiles, row0_tiles):
    bsz, rows, dm = x.shape
    nt = rows // tm
    n = bsz * nt
    row = lambda i: (i // nt, i % nt, 0)
    return pl.pallas_call(
        _combine_kernel,
        grid=(n,),
        in_specs=[pl.BlockSpec((None, 1, 2 * tm), lambda i: (i, 0, 0), memory_space=pltpu.SMEM),
                  pl.BlockSpec((None, 1, 2 * tm), lambda i: (jnp.minimum(i + 1, n - 1), 0, 0),
                               memory_space=pltpu.SMEM),
                  pl.BlockSpec((None, tm, dm), row),
                  pl.BlockSpec((None, None, 6, dm),
                               lambda i: (i // nt, jnp.where(row0_tiles + i % nt >= ctx_tiles, 1, 0), 0, 0)),
                  pl.BlockSpec((None, tm, LANES), row),
                  pl.BlockSpec(memory_space=pl.ANY),
                  _const_spec((1, dm))],
        out_specs=pl.BlockSpec((None, tm, dm), row),
        out_shape=jax.ShapeDtypeStruct((bsz, rows, dm), F32),
        scratch_shapes=[pltpu.VMEM((2, 2, tm, dm), F32), pltpu.SemaphoreType.DMA((2,))],
        compiler_params=_params(("arbitrary",)),
        name="moe_combine",
    )(slot, slot, x, mod, route, y_disp, fg)


def _rope_tables(ctx, t_lat):
    rows = t_lat // GRID_W
    row = np.repeat(np.arange(rows, dtype=np.float64), GRID_W)
    col = np.tile(np.arange(GRID_W, dtype=np.float64), rows)
    inv = ROPE_BASE ** (-np.arange(ROPE_PAIRS, dtype=np.float64) / ROPE_PAIRS)
    ang_r = row[:, None] * inv
    ang_c = col[:, None] * inv
    cos64 = np.concatenate([np.cos(ang_r), np.cos(ang_r), np.cos(ang_c), np.cos(ang_c)], axis=1)
    sin64 = np.concatenate([-np.sin(ang_r), np.sin(ang_r), -np.sin(ang_c), np.sin(ang_c)], axis=1)
    cos = np.tile(cos64, (1, MIX_W // 64))
    sin = np.tile(sin64, (1, MIX_W // 64))
    cos = np.concatenate([np.ones((ctx, MIX_W)), cos], axis=0).astype(np.float32)
    sin = np.concatenate([np.zeros((ctx, MIX_W)), sin], axis=0).astype(np.float32)
    return jnp.asarray(cos), jnp.asarray(sin)


def _pack_w_in(w):
    dm = w.shape[0]
    a_end = PA_W
    b_end = a_end + PB_W
    g_end = b_end + 4 * HEADS
    pad = jnp.zeros((dm, PG_W - 4 * HEADS), w.dtype)
    return jnp.concatenate([w[:, :b_end], w[:, b_end:g_end], pad, w[:, g_end:]], axis=1).astype(BF16)


def kernel(x, c, ctx, c_ctx, ada_w, ada_b, norm1_g, norm2_g, w_in, mlstm_conv_w, mlstm_conv_b, mlstm_gate_b,
           hgrn_lb_raw, hgrn_norm_g, mlstm_norm_g, diff_norm_g, diff_lambda, w_branch, w_gate, b_gate, w_out,
           router_g_w, router_g_b, router_e_w, router_e_b, moe_w1, moe_w3, moe_w2, final_g):
    bsz, t_lat, dm = x.shape
    t_ctx = ctx.shape[1]
    depth = ada_w.shape[0]
    tt = t_ctx + t_lat
    tm = min(256, t_ctx)
    hgrn_chunk = min(128, t_ctx)
    mlstm_chunk = min(256, t_ctx)
    assert t_ctx % tm == 0 and t_lat % (2 * tm) == 0 and t_lat % GRID_W == 0
    assert t_ctx % hgrn_chunk == 0 and t_ctx % mlstm_chunk == 0
    ctx_tiles = t_ctx // tm

    n_rows = -(-(bsz + 1) // 8) * 8
    cc = jnp.zeros((n_rows, dm), F32).at[:bsz].set(c).at[bsz].set(c_ctx)
    mods = _ada_mod(cc, ada_w, ada_b).reshape(depth, n_rows, 6, dm)

    cos, sin = _rope_tables(t_ctx, t_lat)
    lb_cum = jnp.cumsum(jax.nn.softmax(hgrn_lb_raw.astype(F32), axis=0), axis=0)
    lower = lb_cum - lb_cum[0]

    x_ctx, x_lat, lat_row0 = ctx, x, 0
    moe = None
    for l in range(depth):
        with_ctx = l < depth - 1
        lam_init = 0.8 - 0.6 * math.exp(-0.3 * l)
        mod = jnp.stack([jnp.broadcast_to(mods[l, bsz], (bsz, 6, dm)), mods[l, :bsz]], axis=1)
        n1 = norm1_g[l].reshape(1, dm)
        n2 = norm2_g[l].reshape(1, dm)

        lb = lower[l]
        pad = jnp.zeros((2, 6, MIX_W), F32)
        lbc = jnp.concatenate([jnp.log(lb)[:, None], jnp.log1p(-lb)[:, None], pad], axis=1)
        if moe is None:
            pa, pb, pg, qc, kc, vc = _norm_proj(x_ctx, x_lat, mod, n1, _pack_w_in(w_in[l]), cos, sin, lbc,
                                                tm, ctx_tiles)
        else:
            xc, pa, pb, pg, qc, kc, vc = _norm_proj_fused(moe, mod, n1, _pack_w_in(w_in[l]), cos, sin, lbc,
                                                          tm, ctx_tiles)
            x_ctx, x_lat, lat_row0 = xc, xc, ctx_tiles

        oaf, oab = _hgrn(pa, hgrn_chunk, t_ctx // hgrn_chunk)

        qk = _conv_silu(pb, mlstm_conv_w[l], mlstm_conv_b[l], t_ctx, 256)
        gbias = jnp.zeros((1, PG_W), F32).at[0, :4 * HEADS].set(mlstm_gate_b[l].reshape(-1))
        obf, obb = _mlstm(qk, pb, pg, gbias, mlstm_chunk, t_ctx // mlstm_chunk)

        lam_vec = diff_lambda[l].astype(F32)
        oc_lat = _attn_lat(lam_vec, qc, kc, vc, lam_init, tm, t_ctx)
        oc_ctx = _attn_ctx(lam_vec, qc, kc, vc, lam_init, tm, t_ctx) if with_ctx else oc_lat

        hg = jnp.stack([hgrn_norm_g[l], mlstm_norm_g[l], diff_norm_g[l]])
        rw = jnp.zeros((LANES, dm), F32).at[:N_GROUPS].set(router_g_w[l].T)
        rw = rw.at[N_GROUPS:N_GROUPS + N_EXPERTS].set(router_e_w[l].T)
        rw_hi = rw.astype(BF16)
        rw = jnp.stack([rw_hi, (rw - rw_hi.astype(F32)).astype(BF16)])
        rb = jnp.zeros((LANES, 1), F32).at[:N_GROUPS, 0].set(router_g_b[l])
        rb = rb.at[N_GROUPS:N_GROUPS + N_EXPERTS, 0].set(router_e_b[l])
        row0_tiles = 0 if with_ctx else ctx_tiles
        xn, h2, route, route_rows, counts = _merge(x_ctx, x_lat, lat_row0, mod, n1, n2, oaf, oab, pa, obf, obb, pb,
                                                   oc_ctx, oc_lat, hg,
                                                   w_gate[l].astype(BF16), b_gate[l].reshape(1, 3 * dm),
                                                   w_branch[l].astype(BF16), w_out[l].astype(BF16), rw, rb,
                                                   lam_init, tm, ctx_tiles, row0_tiles)

        slot, block_e, n_used, partial = _slot_tables(route_rows, counts)
        x_disp = _scatter(partial, slot, h2, partial.shape[0] * MOE_BLOCK, tm)
        y_disp = _experts(block_e, n_used, x_disp, moe_w1, moe_w3, moe_w2, l)
        moe = (slot, xn, mod, route, y_disp)
    slot, xn, mod, route, y_disp = moe
    return _combine_final(slot, xn, mod, route, y_disp, final_g.reshape(1, dm), tm, ctx_tiles, ctx_tiles)
```

```python
import functools
import math

import numpy as np
import jax
import jax.numpy as jnp
from jax import lax
from jax.experimental import pallas as pl
from jax.experimental.pallas import tpu as pltpu

F32 = jnp.float32
BF16 = jnp.bfloat16

EPS = 1e-6
NEG_BIG = -1e30
HEADS = 4
HEAD_W = 128
MIX_W = HEADS * HEAD_W
GRID_W = 64
ROPE_BASE = 10000.0
ROPE_PAIRS = 16
B_CONV = 3
N_GROUPS = 4
EXPERTS_PER_GROUP = 8
N_EXPERTS = N_GROUPS * EXPERTS_PER_GROUP
D_EXPERT = 512
MOE_BLOCK = 256
LANES = 128
VMEM_LIMIT = 50 * 1024 * 1024

PA_W = 5 * MIX_W
PB_W = 4 * MIX_W
PG_W = LANES
PC_W = 3 * MIX_W
PROJ_W = PA_W + PB_W + PG_W + PC_W


def _params(sem, **kw):
    return pltpu.CompilerParams(dimension_semantics=sem, vmem_limit_bytes=VMEM_LIMIT, **kw)


def _const_spec(shape):
    nd = len(shape)
    return pl.BlockSpec(shape, lambda *_: (0,) * nd, pipeline_mode=pl.Buffered(1))


def _dot(a, b):
    return jnp.dot(a, b, preferred_element_type=F32)


def _dot_nt(a, b):
    return lax.dot_general(a, b, (((1,), (1,)), ((), ())), preferred_element_type=F32)


def _dot_tn(a, b):
    return lax.dot_general(a, b, (((0,), (0,)), ((), ())), preferred_element_type=F32)


def _dot_sel(m_bf16, x):
    hi = x.astype(BF16)
    r1 = x - hi.astype(F32)
    mid = r1.astype(BF16)
    lo = (r1 - mid.astype(F32)).astype(BF16)
    return _dot(m_bf16, hi) + _dot(m_bf16, mid) + _dot(m_bf16, lo)


def _sigmoid(x):
    return 1.0 / (1.0 + jnp.exp(-x))


def _log_sigmoid(x):
    return jnp.minimum(x, 0.0) - jnp.log1p(jnp.exp(-jnp.abs(x)))


def _rms(x, g):
    return x * lax.rsqrt(jnp.mean(x * x, axis=-1, keepdims=True) + EPS) * g


def _ada_kernel(c_ref, w_ref, b_ref, o_ref):
    c = c_ref[...]
    s = c * _sigmoid(c)
    o_ref[...] = jnp.dot(s, w_ref[...], preferred_element_type=F32,
                         precision=lax.Precision.HIGHEST) + b_ref[...]


def _ada_mod(cc, ada_w, ada_b):
    depth, dm, six = ada_w.shape
    rows = cc.shape[0]
    tn = dm
    return pl.pallas_call(
        _ada_kernel,
        grid=(depth, six // tn),
        in_specs=[pl.BlockSpec((rows, dm), lambda l, n: (0, 0)),
                  pl.BlockSpec((None, dm, tn), lambda l, n: (l, 0, n)),
                  pl.BlockSpec((None, 1, tn), lambda l, n: (l, 0, n))],
        out_specs=pl.BlockSpec((None, rows, tn), lambda l, n: (l, 0, n)),
        out_shape=jax.ShapeDtypeStruct((depth, rows, six), F32),
        compiler_params=_params(("parallel", "parallel")),
        name="ada_mod",
    )(cc, ada_w, ada_b.reshape(depth, 1, six))


def _rope(x, cos, sin):
    n = x.shape[-1]
    lane = lax.broadcasted_iota(jnp.int32, (1, n), 1)
    first = (lane // ROPE_PAIRS) % 2 == 0
    partner = jnp.where(first, pltpu.roll(x, n - ROPE_PAIRS, 1), pltpu.roll(x, ROPE_PAIRS, 1))
    return x * cos + partner * sin


def _hgrn_log_forget(z, lbc):
    lsig = jnp.minimum(z, 0.0) - jnp.log(1.0 + jnp.exp(-jnp.abs(z)))
    a = lbc[0:1]
    bb = lbc[1:2] + lsig
    return jnp.maximum(a, bb) + jnp.log(1.0 + jnp.exp(-jnp.abs(a - bb)))


def _with_gathered_rows(slot_ref, slot_next_ref, yd_ref, ybuf, sems, body):
    i = pl.program_id(0)
    n = pl.num_programs(0)
    tm = ybuf.shape[2]

    def start(table_ref, b):
        _row_copies(tm, lambda r, k: pltpu.make_async_copy(
            yd_ref.at[pl.ds(table_ref[0, k * tm + r], 1), :], ybuf.at[b, k, pl.ds(r, 1), :], sems.at[b]))

    def wait(b):
        for k in range(2):
            pltpu.make_async_copy(yd_ref.at[pl.ds(0, tm), :], ybuf.at[b, k], sems.at[b]).wait()

    @pl.when(i == 0)
    def _():
        start(slot_ref, 0)

    for cur in range(2):
        @pl.when(i % 2 == cur)
        def _():
            wait(cur)
            start(slot_next_ref, 1 - cur)
            body(ybuf[cur, 0], ybuf[cur, 1])

            @pl.when(i == n - 1)
            def _():
                wait(1 - cur)


def _moe_residual(x_ref, mod_ref, rt_ref, y0, y1):
    rt = rt_ref[...]
    return x_ref[...] + mod_ref[5:6, :] * (rt[:, 2:3] * y0 + rt[:, 3:4] * y1)


def _norm_proj_fused_kernel(slot_ref, slot_next_ref, xp_ref, modp_ref, rt_ref, yd_ref,
                            mod_ref, g_ref, w_ref, cos_ref, sin_ref, lbc_ref,
                            xo_ref, pa_ref, pb_ref, pg_ref, qc_ref, kc_ref, vc_ref, ybuf, sems):
    def body(y0, y1):
        x = _moe_residual(xp_ref, modp_ref, rt_ref, y0, y1)
        xo_ref[...] = x
        _project(x, mod_ref, g_ref, w_ref, cos_ref, sin_ref, lbc_ref,
                 pa_ref, pb_ref, pg_ref, qc_ref, kc_ref, vc_ref)

    _with_gathered_rows(slot_ref, slot_next_ref, yd_ref, ybuf, sems, body)


def _norm_proj_kernel(xc_ref, xl_ref, mod_ref, g_ref, w_ref, cos_ref, sin_ref, lbc_ref,
                      pa_ref, pb_ref, pg_ref, qc_ref, kc_ref, vc_ref, *, ctx_tiles):
    x = jnp.where(pl.program_id(1) < ctx_tiles, xc_ref[...], xl_ref[...])
    _project(x, mod_ref, g_ref, w_ref, cos_ref, sin_ref, lbc_ref,
             pa_ref, pb_ref, pg_ref, qc_ref, kc_ref, vc_ref)


def _project(x, mod_ref, g_ref, w_ref, cos_ref, sin_ref, lbc_ref, pa_ref, pb_ref, pg_ref, qc_ref, kc_ref, vc_ref):
    mod = mod_ref[...]
    h = _rms(x, g_ref[...]) * (1.0 + mod[1:2]) + mod[0:1]
    hb = h.astype(BF16)
    pa_ref[:, 0:3 * MIX_W] = _dot(hb, w_ref[:, 0:3 * MIX_W])
    for d in range(2):
        cs = slice((3 + d) * MIX_W, (4 + d) * MIX_W)
        pa_ref[:, cs] = _hgrn_log_forget(_dot(hb, w_ref[:, cs]), lbc_ref[d])
    pb_ref[...] = _dot(hb, w_ref[:, PA_W:PA_W + PB_W])
    pg_ref[...] = _dot(hb, w_ref[:, PA_W + PB_W:PA_W + PB_W + PG_W])
    c0 = PA_W + PB_W + PG_W
    cos = cos_ref[...]
    sin = sin_ref[...]
    q = _dot(hb, w_ref[:, c0:c0 + MIX_W])
    qc_ref[...] = (_rope(q, cos, sin) * (64.0 ** -0.5 * math.log2(math.e))).astype(BF16)
    k = _dot(hb, w_ref[:, c0 + MIX_W:c0 + 2 * MIX_W])
    kc_ref[...] = _rope(k, cos, sin).astype(BF16)
    vc_ref[...] = _dot(hb, w_ref[:, c0 + 2 * MIX_W:c0 + 3 * MIX_W]).astype(BF16)


def _norm_proj_fused(moe, mod, g, w, cos, sin, lbc, tm, ctx_tiles):
    slot, xp, modp, route, y_disp = moe
    bsz, tt, dm = xp.shape
    nt = tt // tm
    n = bsz * nt
    row = lambda i: (i // nt, i % nt, 0)
    mod_row = lambda i: (i // nt, jnp.where(i % nt >= ctx_tiles, 1, 0), 0, 0)
    tab = lambda i: (i % nt, 0)
    outs = [(dm, F32), (PA_W, F32), (PB_W, F32), (PG_W, F32), (MIX_W, BF16), (MIX_W, BF16), (MIX_W, BF16)]
    return pl.pallas_call(
        _norm_proj_fused_kernel,
        grid=(n,),
        in_specs=[pl.BlockSpec((None, 1, 2 * tm), lambda i: (i, 0, 0), memory_space=pltpu.SMEM),
                  pl.BlockSpec((None, 1, 2 * tm), lambda i: (jnp.minimum(i + 1, n - 1), 0, 0),
                               memory_space=pltpu.SMEM),
                  pl.BlockSpec((None, tm, dm), row),
                  pl.BlockSpec((None, None, 6, dm), mod_row),
                  pl.BlockSpec((None, tm, LANES), row),
                  pl.BlockSpec(memory_space=pl.ANY),
                  pl.BlockSpec((None, None, 6, dm), mod_row),
                  _const_spec((1, dm)),
                  _const_spec((dm, PROJ_W)),
                  pl.BlockSpec((tm, MIX_W), tab),
                  pl.BlockSpec((tm, MIX_W), tab),
                  _const_spec(lbc.shape)],
        out_specs=[pl.BlockSpec((None, tm, wd), row) for wd, _ in outs],
        out_shape=[jax.ShapeDtypeStruct((bsz, tt, wd), dt) for wd, dt in outs],
        scratch_shapes=[pltpu.VMEM((2, 2, tm, dm), F32), pltpu.SemaphoreType.DMA((2,))],
        compiler_params=_params(("arbitrary",)),
        name="combine_norm_proj",
    )(slot, slot, xp, modp, route, y_disp, mod, g, w, cos, sin, lbc)


def _norm_proj(x_ctx, x_lat, mod, g, w, cos, sin, lbc, tm, ctx_tiles):
    bsz, t_ctx, dm = x_ctx.shape
    tt = t_ctx + x_lat.shape[1]
    nt = tt // tm
    row = lambda b, t: (b, t, 0)
    outs = [(PA_W, F32), (PB_W, F32), (PG_W, F32), (MIX_W, BF16), (MIX_W, BF16), (MIX_W, BF16)]
    return pl.pallas_call(
        functools.partial(_norm_proj_kernel, ctx_tiles=ctx_tiles),
        grid=(bsz, nt),
        in_specs=[pl.BlockSpec((None, tm, dm), lambda b, t: (b, jnp.minimum(t, ctx_tiles - 1), 0)),
                  pl.BlockSpec((None, tm, dm), lambda b, t: (b, jnp.maximum(t - ctx_tiles, 0), 0)),
                  pl.BlockSpec((None, None, 6, dm), lambda b, t: (b, jnp.where(t >= ctx_tiles, 1, 0), 0, 0)),
                  _const_spec((1, dm)),
                  _const_spec((dm, PROJ_W)),
                  pl.BlockSpec((tm, MIX_W), lambda b, t: (t, 0)),
                  pl.BlockSpec((tm, MIX_W), lambda b, t: (t, 0)),
                  _const_spec(lbc.shape)],
        out_specs=[pl.BlockSpec((None, tm, wd), row) for wd, _ in outs],
        out_shape=[jax.ShapeDtypeStruct((bsz, tt, wd), dt) for wd, dt in outs],
        compiler_params=_params(("parallel", "parallel")),
        name="norm_proj",
    )(x_ctx, x_lat, mod, g, w, cos, sin, lbc)


def _conv_kernel(x_ref, w_ref, b_ref, o_ref, *, ctx, q_blocks):
    x = x_ref[...]
    tt = x.shape[0]
    row = lax.broadcasted_iota(jnp.int32, (tt, 1), 0)
    prev = jnp.where((row == 0) | (row == ctx), 0.0, pltpu.roll(x, 1, 0))
    nxt = jnp.where((row == ctx - 1) | (row == tt - 1), 0.0, pltpu.roll(x, tt - 1, 0))
    w = w_ref[...]
    y = b_ref[...] + prev * w[0:1] + x * w[1:2] + nxt * w[2:3]
    y = y * _sigmoid(y)
    scale = jnp.where(pl.program_id(1) < q_blocks, HEAD_W ** -0.5, 1.0)
    o_ref[...] = y * scale


def _conv_silu(pb, conv_w, conv_b, ctx, cb):
    bsz, tt, _ = pb.shape
    width = 2 * MIX_W
    return pl.pallas_call(
        functools.partial(_conv_kernel, ctx=ctx, q_blocks=MIX_W // cb),
        grid=(bsz, width // cb),
        in_specs=[pl.BlockSpec((None, tt, cb), lambda b, j: (b, 0, j)),
                  pl.BlockSpec((B_CONV, cb), lambda b, j: (0, j)),
                  pl.BlockSpec((1, cb), lambda b, j: (0, j))],
        out_specs=pl.BlockSpec((None, tt, cb), lambda b, j: (b, 0, j)),
        out_shape=jax.ShapeDtypeStruct((bsz, tt, width), F32),
        compiler_params=_params(("parallel", "parallel")),
        name="mlstm_conv",
    )(pb, conv_w, conv_b.reshape(1, width))


def _chunk_consts(L):
    idx = np.arange(L)
    t, u = idx[:, None], idx[None, :]
    mats = [(u <= t)]
    for h in range(1, N_MATMUL_LEVELS + 1):
        mid = (t // (2 * h)) * (2 * h) + h - 1
        mats.append(((u > mid) & (u <= t)) | ((u > t) & (u <= mid)))
    cum = np.concatenate([m.astype(np.float32) for m in mats], axis=0)
    cum_b = np.concatenate([m[::-1, ::-1].astype(np.float32) for m in mats], axis=0)
    nlev = int(round(math.log2(L)))
    lvl = np.full((L, L), -1, np.int32)
    lvl[idx, idx] = nlev
    for i in range(nlev):
        h = L >> (i + 1)
        same = (t // (2 * h)) == (u // (2 * h))
        lvl[same & (t % (2 * h) >= h) & (u % (2 * h) < h)] = i
    sel = jnp.asarray(np.stack([cum, cum_b]), BF16)
    lvls = jnp.asarray(np.stack([lvl, lvl[::-1, ::-1]]))
    return sel, lvls, nlev


N_MATMUL_LEVELS = 0


def _neg_abs(x):
    return -jnp.abs(x)


def _midpoint_rows(b, h, d):
    L, w = b.shape
    two = 2 * h
    pos = h - 1 if d == 0 else h
    if two % 8 == 0:
        r = b.reshape(L // two, two, w)[:, pos:pos + 1, :]
        return jnp.broadcast_to(r, (L // two, two, w)).reshape(L, w)
    phase = lax.broadcasted_iota(jnp.int32, (L, 1), 0) % two
    ref = b
    for off in range(pos - two + 1, pos + 1):
        if off != 0:
            ref = jnp.where(phase == pos - off, pltpu.roll(b, (L - off) % L, 0), ref)
    return ref


def _scan_blocks(j, nc_ctx, nc):
    jb = jnp.where(j < nc_ctx, nc_ctx - 1 - j, nc - 1 + nc_ctx - j)
    return j, jb


def _hgrn_kernel(qf_ref, vf_ref, ff_ref, qb_ref, vb_ref, fb_ref, sel_ref, lvl_ref,
                 of_ref, ob_ref, st_ref, *, L, nlev):
    @pl.when(pl.program_id(1) == 0)
    def _():
        st_ref[...] = jnp.zeros_like(st_ref)

    row = lax.broadcasted_iota(jnp.int32, (L, 1), 0)
    dirs = ((qf_ref, vf_ref, ff_ref, of_ref), (qb_ref, vb_ref, fb_ref, ob_ref))

    def one(bb, d, q_ref, v_ref, f_ref, o_ref):
        lf = f_ref[bb]
        kk = 1.0 - jnp.exp(lf)
        q_all = q_ref[bb]
        ex = _dot_sel(sel_ref[d], lf) * math.log2(math.e)
        b_in = ex[0:L]
        lvl = lvl_ref[d]
        last = L - 1 if d == 0 else 0
        b_end = b_in[last:last + 1]
        heads = [slice(hd * HEAD_W, (hd + 1) * HEAD_W) for hd in range(HEADS)]
        qb = q_all.astype(BF16)
        kb = kk.astype(BF16)
        on_diag = lvl == nlev
        scores = [jnp.where(on_diag, _dot_nt(qb[:, cs], kb[:, cs]), 0.0) for cs in heads]
        for i in range(nlev):
            h = L >> (i + 1)
            if h <= N_MATMUL_LEVELS:
                expo = ex[h * L:(h + 1) * L]
            else:
                expo = _neg_abs(b_in - _midpoint_rows(b_in, h, d))
            ei = jnp.exp2(expo)
            is_query = (row % (2 * h) >= h) if d == 0 else (row % (2 * h) < h)
            xe = (jnp.where(is_query, q_all, kk) * ei).astype(BF16)
            at_level = lvl == i
            for hd, cs in enumerate(heads):
                scores[hd] = scores[hd] + jnp.where(at_level, _dot_nt(xe[:, cs], xe[:, cs]), 0.0)
        q_in = (q_all * jnp.exp2(b_in)).astype(BF16)
        k_out = (kk * jnp.exp2(b_end - b_in)).astype(BF16)
        decay = jnp.exp2(b_end)
        for hd, cs in enumerate(heads):
            vb = v_ref[bb, :, cs].astype(BF16)
            st = st_ref[bb, d, hd]
            o_ref[bb, :, cs] = _dot(scores[hd].astype(BF16), vb) + _dot_nt(q_in[:, cs], st.astype(BF16))
            st_ref[bb, d, hd] = st * decay[:, cs] + _dot_tn(vb, k_out[:, cs])

    for bb in range(qf_ref.shape[0]):
        for d, refs in enumerate(dirs):
            one(bb, d, *refs)


def _hgrn(pa, L, nc_ctx):
    bsz, tt, _ = pa.shape
    nc = tt // L
    nb = 4 if bsz % 4 == 0 else (2 if bsz % 2 == 0 else 1)
    sel, lvl, nlev = _chunk_consts(L)

    def spec(col, which):
        return pl.BlockSpec((nb, L, MIX_W), lambda b, j: (b, _scan_blocks(j, nc_ctx, nc)[which], col))

    return pl.pallas_call(
        functools.partial(_hgrn_kernel, L=L, nlev=nlev),
        grid=(bsz // nb, nc),
        in_specs=[spec(0, 0), spec(1, 0), spec(3, 0), spec(0, 1), spec(1, 1), spec(4, 1),
                  _const_spec(sel.shape), _const_spec(lvl.shape)],
        out_specs=[spec(0, 0), spec(0, 1)],
        out_shape=[jax.ShapeDtypeStruct((bsz, tt, MIX_W), F32)] * 2,
        scratch_shapes=[pltpu.VMEM((nb, 2, HEADS, HEAD_W, HEAD_W), F32)],
        compiler_params=_params(("parallel", "arbitrary")),
        name="hgrn2_scan",
    )(pa, pa, pa, pa, pa, pa, sel, lvl)


def _mlstm_kernel(qf_ref, kf_ref, vf_ref, gf_ref, qb_ref, kb_ref, vb_ref, gb_ref, gbias_ref,
                  sel_ref, lvl_ref, of_ref, ob_ref, c_ref, m_ref, *, L, nlev):
    @pl.when(pl.program_id(1) == 0)
    def _():
        c_ref[...] = jnp.zeros_like(c_ref)
        m_ref[...] = jnp.zeros_like(m_ref)

    lane = lax.broadcasted_iota(jnp.int32, (1, LANES), 1)
    tlane = lax.broadcasted_iota(jnp.int32, (HEADS, L), 1)
    ones_col = jnp.where(lane == 0, 1.0, 0.0).astype(BF16)
    dirs = ((qf_ref, kf_ref, vf_ref, gf_ref, of_ref), (qb_ref, kb_ref, vb_ref, gb_ref, ob_ref))
    for d, (q_ref, k_ref, v_ref, g_ref, o_ref) in enumerate(dirs):
        g = g_ref[...] + gbias_ref[...]
        lf = jnp.where((lane >= 2 * HEADS) & (lane < 4 * HEADS), _log_sigmoid(g), 0.0)
        bcum = _dot_sel(sel_ref[d, 0:L], lf)
        causal = lvl_ref[d] >= 0
        rows = jnp.where(lane < 2 * HEADS, g - pltpu.roll(bcum, LANES - 2 * HEADS, 1), bcum).T
        r4 = rows[d * HEADS:(d + 1) * HEADS]
        b4 = rows[(2 + d) * HEADS:(3 + d) * HEADS]
        m4 = m_ref[d * HEADS:(d + 1) * HEADS, 0:1]
        run = r4
        sh = 1
        while sh < L:
            if d == 0:
                run = jnp.maximum(run, jnp.where(tlane >= sh, pltpu.roll(run, sh, 1), -jnp.inf))
            else:
                run = jnp.maximum(run, jnp.where(tlane < L - sh, pltpu.roll(run, L - sh, 1), -jnp.inf))
            sh *= 2
        c4 = jnp.maximum(run, m4)
        cols = jnp.concatenate([c4, jnp.exp(m4 - c4), jnp.exp(-b4 - c4),
                                jnp.zeros((LANES - 3 * HEADS, L), F32)], axis=0).T
        last = L - 1 if d == 0 else 0
        for hd in range(HEADS):
            cs = slice(hd * HEAD_W, (hd + 1) * HEAD_W)
            row_term = r4[hd:hd + 1, :]
            b_end = b4[hd:hd + 1, last:last + 1]
            r = d * HEADS + hd
            m_prev = m4[hd:hd + 1]
            q = q_ref[:, cs].astype(BF16)
            k = k_ref[:, cs]
            v_aug = jnp.concatenate([v_ref[:, cs].astype(BF16), jnp.broadcast_to(ones_col, (L, LANES))], axis=1)
            c_prev = c_ref[r]
            w_end = b_end + row_term
            m_new = jnp.maximum(b_end + m_prev, jnp.max(w_end, axis=-1, keepdims=True))
            e_end = jnp.exp(w_end - m_new)
            keep = jnp.exp(b_end + m_prev - m_new)
            c_ref[r] = keep * c_prev + _dot((k.T * e_end).astype(BF16), v_aug)
            m_ref[r:r + 1, :] = jnp.broadcast_to(m_new, (1, LANES))
            c_t = cols[:, hd:hd + 1]
            a_state = cols[:, HEADS + hd:HEADS + hd + 1]
            floor = cols[:, 2 * HEADS + hd:2 * HEADS + hd + 1]
            s = _dot_nt(q, k.astype(BF16)) * jnp.exp(jnp.where(causal, row_term - c_t, NEG_BIG))
            num = _dot(s.astype(BF16), v_aug) + a_state * _dot(q, c_prev.astype(BF16))
            den = num[:, HEAD_W:HEAD_W + 1]
            o_ref[:, cs] = num[:, 0:HEAD_W] / jnp.maximum(jnp.abs(den), floor)


def _mlstm(qk, pb, pg, gbias, L, nc_ctx):
    bsz, tt, _ = pb.shape
    nc = tt // L
    sel, lvl, nlev = _chunk_consts(L)

    def spec(col, which, width=MIX_W):
        return pl.BlockSpec((None, L, width), lambda b, j: (b, _scan_blocks(j, nc_ctx, nc)[which], col))

    return pl.pallas_call(
        functools.partial(_mlstm_kernel, L=L, nlev=nlev),
        grid=(bsz, nc),
        in_specs=[spec(0, 0), spec(1, 0), spec(2, 0), spec(0, 0, PG_W),
                  spec(0, 1), spec(1, 1), spec(2, 1), spec(0, 1, PG_W),
                  _const_spec((1, PG_W)), _const_spec(sel.shape), _const_spec(lvl.shape)],
        out_specs=[spec(0, 0), spec(0, 1)],
        out_shape=[jax.ShapeDtypeStruct((bsz, tt, MIX_W), F32)] * 2,
        scratch_shapes=[pltpu.VMEM((2 * HEADS, HEAD_W, 2 * HEAD_W), F32),
                        pltpu.VMEM((2 * HEADS, LANES), F32)],
        compiler_params=_params(("parallel", "arbitrary")),
        name="mlstm_scan",
    )(qk, qk, pb, pg, qk, qk, pb, pg, gbias, sel, lvl)


def _diff_lambda(lam_ref, lam_init):
    lv = lam_ref[...]
    return (jnp.exp(jnp.sum(lv[0:1] * lv[1:2], axis=-1, keepdims=True))
            - jnp.exp(jnp.sum(lv[2:3] * lv[3:4], axis=-1, keepdims=True)) + lam_init)


def _map_scores(q, k):
    lane = lax.broadcasted_iota(jnp.int32, (1, HEAD_W), 1)
    zero = jnp.zeros_like(q)
    return (_dot_nt(jnp.where(lane < HEAD_W // 2, q, zero), k),
            _dot_nt(jnp.where(lane >= HEAD_W // 2, q, zero), k))


def _row_max(s):
    return jnp.max(s, axis=-1, keepdims=True)


def _diff_softmax_v(s1, m1, s2, m2, lam, v):
    p1 = jnp.exp2(s1 - m1)
    p2 = jnp.exp2(s2 - m2)
    l1 = jnp.sum(p1, axis=-1, keepdims=True)
    l2 = jnp.sum(p2, axis=-1, keepdims=True)
    return _dot((p1 - (lam * l1 / l2) * p2).astype(BF16), v) / l1


def _attn_ctx_kernel(lam_ref, q_ref, k_ref, v_ref, o_ref, *, lam_init):
    s1, s2 = _map_scores(q_ref[...], k_ref[...])
    o_ref[...] = _diff_softmax_v(s1, _row_max(s1), s2, _row_max(s2), _diff_lambda(lam_ref, lam_init), v_ref[...])


def _attn_ctx(lam_vec, qc, kc, vc, lam_init, tq, t_ctx):
    bsz = qc.shape[0]
    kv = pl.BlockSpec((None, t_ctx, HEAD_W), lambda b, h, i: (b, 0, h))
    return pl.pallas_call(
        functools.partial(_attn_ctx_kernel, lam_init=lam_init),
        grid=(bsz, HEADS, t_ctx // tq),
        in_specs=[_const_spec(lam_vec.shape),
                  pl.BlockSpec((None, tq, HEAD_W), lambda b, h, i: (b, i, h)), kv, kv],
        out_specs=pl.BlockSpec((None, tq, HEAD_W), lambda b, h, i: (b, i, h)),
        out_shape=jax.ShapeDtypeStruct((bsz, t_ctx, MIX_W), F32),
        compiler_params=_params(("parallel", "parallel", "arbitrary")),
        name="diff_attn_ctx",
    )(lam_vec, qc, kc, vc)


def _attn_lat_kernel(lam_ref, qa_ref, qb_ref, k_ref, v_ref, o_ref, s_even, m_even, s_odd, m_odd,
                     *, lam_init, n_tiles):
    j = pl.program_id(2)
    lam = _diff_lambda(lam_ref, lam_init)
    even, odd = (s_even, m_even), (s_odd, m_odd)
    half = qa_ref.shape[0]

    def scores_into(bufs):
        s_buf, m_buf = bufs
        for part, q_ref in enumerate((qa_ref, qb_ref)):
            rows = slice(part * half, (part + 1) * half)
            for i, s in enumerate(_map_scores(q_ref[...], k_ref[...])):
                s_buf[i, rows, :] = s
                m_buf[i, rows, :] = jnp.broadcast_to(_row_max(s), (half, LANES))

    def finish_from(bufs):
        s_buf, m_buf = bufs
        for part in range(2):
            rows = slice(part * half, (part + 1) * half)
            o_ref[rows, :] = _diff_softmax_v(s_buf[0, rows, :], m_buf[0, rows, 0:1],
                                             s_buf[1, rows, :], m_buf[1, rows, 0:1], lam, v_ref[...])

    @pl.when(j == 0)
    def _():
        scores_into(even)

    middle = (j > 0) & (j < n_tiles)

    @pl.when(middle & (j % 2 == 1))
    def _():
        scores_into(odd)
        finish_from(even)

    @pl.when(middle & (j % 2 == 0))
    def _():
        scores_into(even)
        finish_from(odd)

    @pl.when(j == n_tiles)
    def _():
        finish_from(odd if n_tiles % 2 == 0 else even)


def _attn_lat(lam_vec, qc, kc, vc, lam_init, tq, t_ctx):
    bsz, tt, _ = qc.shape
    n_tiles = (tt - t_ctx) // (2 * tq)
    q0 = t_ctx // tq

    def q_spec(part):
        return pl.BlockSpec((None, tq, HEAD_W),
                            lambda b, h, j: (b, q0 + 2 * jnp.minimum(j, n_tiles - 1) + part, h))

    kv = pl.BlockSpec((None, tt, HEAD_W), lambda b, h, j: (b, 0, h))
    return pl.pallas_call(
        functools.partial(_attn_lat_kernel, lam_init=lam_init, n_tiles=n_tiles),
        grid=(bsz, HEADS, n_tiles + 1),
        in_specs=[_const_spec(lam_vec.shape), q_spec(0), q_spec(1), kv, kv],
        out_specs=pl.BlockSpec((None, 2 * tq, HEAD_W), lambda b, h, j: (b, jnp.maximum(j - 1, 0), h)),
        out_shape=jax.ShapeDtypeStruct((bsz, tt - t_ctx, MIX_W), F32),
        scratch_shapes=[pltpu.VMEM((2, 2 * tq, tt), F32), pltpu.VMEM((2, 2 * tq, LANES), F32),
                        pltpu.VMEM((2, 2 * tq, tt), F32), pltpu.VMEM((2, 2 * tq, LANES), F32)],
        compiler_params=_params(("parallel", "parallel", "arbitrary")),
        name="diff_attn_lat",
    )(lam_vec, qc, qc, kc, vc)


def _head_norm(x, g):
    parts = []
    for hd in range(HEADS):
        xs = x[:, hd * HEAD_W:(hd + 1) * HEAD_W]
        parts.append(xs * lax.rsqrt(jnp.mean(xs * xs, axis=-1, keepdims=True) + EPS))
    return jnp.concatenate(parts, axis=1) * g


ROUTE_ROWS = 40


def _route(lg):
    n = lg.shape[1]
    row = lax.broadcasted_iota(jnp.int32, lg.shape, 0)
    neg = -jnp.inf
    big = ROUTE_ROWS
    gl = jnp.where(row < N_GROUPS, lg, neg)
    gmax = jnp.max(gl, axis=0, keepdims=True)
    gidx = jnp.min(jnp.where(gl == gmax, row, big), axis=0, keepdims=True)
    p_group = 1.0 / jnp.sum(jnp.exp(gl - gmax), axis=0, keepdims=True)
    lo = N_GROUPS + EXPERTS_PER_GROUP * gidx
    in_grp = (row >= lo) & (row < lo + EXPERTS_PER_GROUP)
    el = jnp.where(in_grp, lg, neg)
    pe = jnp.exp(el - jnp.max(el, axis=0, keepdims=True))
    pe = pe / jnp.sum(pe, axis=0, keepdims=True)
    pe = jnp.where(in_grp, pe, -1.0)
    v1 = jnp.max(pe, axis=0, keepdims=True)
    i1 = jnp.min(jnp.where(pe == v1, row, big), axis=0, keepdims=True)
    pe2 = jnp.where(row == i1, -1.0, pe)
    v2 = jnp.max(pe2, axis=0, keepdims=True)
    i2 = jnp.min(jnp.where(pe2 == v2, row, big), axis=0, keepdims=True)
    scale = p_group / (v1 + v2)
    erow = lax.broadcasted_iota(jnp.int32, (N_EXPERTS, n), 0) + N_GROUPS
    oh1 = erow == i1
    oh2 = erow == i2
    oh1f = jnp.where(oh1, 1.0, 0.0)
    oh2f = jnp.where(oh2, 1.0, 0.0)
    earlier = (lax.broadcasted_iota(jnp.int32, (n, n), 0) < lax.broadcasted_iota(jnp.int32, (n, n), 1))
    earlier = jnp.where(earlier, 1.0, 0.0).astype(BF16)
    tot1 = jnp.sum(oh1f, axis=1, keepdims=True)
    tot2 = jnp.sum(oh2f, axis=1, keepdims=True)
    rank1 = jnp.sum(jnp.where(oh1, _dot(oh1f.astype(BF16), earlier), 0.0), axis=0, keepdims=True)
    rank2 = jnp.sum(jnp.where(oh2, _dot(oh2f.astype(BF16), earlier) + tot1, 0.0), axis=0, keepdims=True)
    r8 = lax.broadcasted_iota(jnp.int32, (8, n), 0)
    rows = jnp.zeros((8, n), F32)
    for i, val in enumerate(((i1 - N_GROUPS).astype(F32), (i2 - N_GROUPS).astype(F32), v1 * scale, v2 * scale,
                             rank1, rank2)):
        rows = jnp.where(r8 == i, val, rows)
    return rows, tot1 + tot2


def _merge_kernel(xc_ref, xl_ref, mod_ref, modb_ref, n1_ref, n2_ref, af_ref, ab_ref, ga_ref, bf_ref, bb_ref, gb_ref,
                  occ_ref, ocl_ref, hg_ref, wg_ref, bg_ref, wbr_ref, wo_ref, rw_ref, rb_ref,
                  xo_ref, h2_ref, rt_ref, rtt_ref, cnt_ref, x_even, x_odd,
                  *, lam_init, ctx_tiles, row0_tiles, nt, n):
    i = pl.program_id(0)
    is_ctx_tile = row0_tiles + jnp.minimum(i, n - 1) % nt < ctx_tiles

    def front(stash):
        xn = _merge_front(is_ctx_tile, xc_ref, xl_ref, mod_ref, n1_ref, af_ref, ab_ref, ga_ref, bf_ref, bb_ref,
                          gb_ref, occ_ref, ocl_ref, hg_ref, wg_ref, bg_ref, wbr_ref, wo_ref, lam_init)
        xo_ref[...] = xn
        stash[...] = xn

    def back(stash):
        _merge_back(stash[...], modb_ref, n2_ref, rw_ref, rb_ref, h2_ref, rt_ref, rtt_ref, cnt_ref)

    @pl.when(i == 0)
    def _():
        front(x_even)

    middle = (i > 0) & (i < n)

    @pl.when(middle & (i % 2 == 1))
    def _():
        front(x_odd)
        back(x_even)

    @pl.when(middle & (i % 2 == 0))
    def _():
        front(x_even)
        back(x_odd)

    @pl.when(i == n)
    def _():
        back(x_odd if n % 2 == 0 else x_even)


def _merge_front(is_ctx_tile, xc_ref, xl_ref, mod_ref, n1_ref, af_ref, ab_ref, ga_ref, bf_ref, bb_ref, gb_ref,
                 occ_ref, ocl_ref, hg_ref, wg_ref, bg_ref, wbr_ref, wo_ref, lam_init):
    x = jnp.where(is_ctx_tile, xc_ref[...], xl_ref[...])
    mod = mod_ref[...]
    dm = x.shape[-1]
    hb = (_rms(x, n1_ref[...]) * (1.0 + mod[1:2]) + mod[0:1]).astype(BF16)
    hg = hg_ref[...]
    ga = ga_ref[...]
    ya = _head_norm(af_ref[...] + ab_ref[...], hg[0:1]) * (ga * _sigmoid(ga))
    yb = _head_norm(bf_ref[...] + bb_ref[...], hg[1:2]) * _sigmoid(gb_ref[...])
    oc = jnp.where(is_ctx_tile, occ_ref[...], ocl_ref[...])
    yc = _head_norm(oc, hg[2:3]) * (1.0 - lam_init)
    y = jnp.zeros_like(x)
    for i, yi in enumerate((ya, yb, yc)):
        gate = _sigmoid(_dot(hb, wg_ref[:, i * dm:(i + 1) * dm]) + bg_ref[:, i * dm:(i + 1) * dm])
        y = y + gate * _dot(yi.astype(BF16), wbr_ref[i])
    return x + mod[2:3] * _dot(y.astype(BF16), wo_ref[...])


def _merge_back(xn, mod_ref, n2_ref, rw_ref, rb_ref, h2_ref, rt_ref, rtt_ref, cnt_ref):
    mod = mod_ref[...]
    h2 = _rms(xn, n2_ref[...]) * (1.0 + mod[4:5]) + mod[3:4]
    h2_ref[...] = h2
    h_hi = h2.astype(BF16)
    h_mid = (h2 - h_hi.astype(F32)).astype(BF16)
    logits = (_dot_nt(rw_ref[0], h_hi) + _dot_nt(rw_ref[0], h_mid) + _dot_nt(rw_ref[1], h_hi)) + rb_ref[...]
    rows, counts = _route(logits[0:ROUTE_ROWS, :])
    rtt_ref[...] = rows
    cnt_ref[...] = jnp.broadcast_to(counts, cnt_ref.shape)
    tm = xn.shape[0]
    rt_ref[...] = jnp.concatenate([rows, jnp.zeros((LANES - 8, tm), F32)], axis=0).T


def _merge(x_ctx, x_lat, lat_row0, mod, n1, n2, oaf, oab, pa, obf, obb, pb, oc_ctx, oc_lat, hg, wg, bg, wbr, wo,
           rw, rb, lam_init, tm, ctx_tiles, row0_tiles):
    bsz, tt, _ = pa.shape
    dm = x_lat.shape[-1]
    nt = tt // tm - row0_tiles
    rows = nt * tm
    n = bsz * nt

    def front(fn):
        return lambda i: fn(jnp.minimum(i, n - 1) // nt, jnp.minimum(i, n - 1) % nt)

    def back(fn):
        return lambda i: fn(jnp.maximum(i - 1, 0) // nt, jnp.maximum(i - 1, 0) % nt)

    def col(c):
        return pl.BlockSpec((None, tm, MIX_W), front(lambda b, t: (b, row0_tiles + t, c)))

    mod_of = lambda b, t: (b, jnp.where(row0_tiles + t >= ctx_tiles, 1, 0), 0, 0)
    out_specs = [pl.BlockSpec((None, tm, dm), front(lambda b, t: (b, t, 0))),
                 pl.BlockSpec((None, tm, dm), back(lambda b, t: (b, t, 0))),
                 pl.BlockSpec((None, tm, LANES), back(lambda b, t: (b, t, 0))),
                 pl.BlockSpec((None, 8, tm), back(lambda b, t: (b * nt + t, 0, 0))),
                 pl.BlockSpec((None, N_EXPERTS, LANES), back(lambda b, t: (b * nt + t, 0, 0)))]
    out_shape = [jax.ShapeDtypeStruct((bsz, rows, dm), F32), jax.ShapeDtypeStruct((bsz, rows, dm), F32),
                 jax.ShapeDtypeStruct((bsz, rows, LANES), F32), jax.ShapeDtypeStruct((n, 8, tm), F32),
                 jax.ShapeDtypeStruct((n, N_EXPERTS, LANES), F32)]
    return pl.pallas_call(
        functools.partial(_merge_kernel, lam_init=lam_init, ctx_tiles=ctx_tiles, row0_tiles=row0_tiles,
                          nt=nt, n=n),
        grid=(n + 1,),
        in_specs=[pl.BlockSpec((None, tm, dm),
                               front(lambda b, t: (b, jnp.minimum(row0_tiles + t, max(ctx_tiles - 1, 0)), 0))),
                  pl.BlockSpec((None, tm, dm),
                               front(lambda b, t: (b, lat_row0 + jnp.maximum(row0_tiles + t - ctx_tiles, 0), 0))),
                  pl.BlockSpec((None, None, 6, dm), front(mod_of)),
                  pl.BlockSpec((None, None, 6, dm), back(mod_of)),
                  _const_spec((1, dm)), _const_spec((1, dm)),
                  col(0), col(0), col(2), col(0), col(0), col(3),
                  pl.BlockSpec((None, tm, MIX_W),
                               front(lambda b, t: (b, jnp.minimum(row0_tiles + t, max(ctx_tiles - 1, 0)), 0))),
                  pl.BlockSpec((None, tm, MIX_W),
                               front(lambda b, t: (b, jnp.maximum(row0_tiles + t - ctx_tiles, 0), 0))),
                  _const_spec((3, MIX_W)), _const_spec(wg.shape), _const_spec(bg.shape),
                  _const_spec(wbr.shape), _const_spec(wo.shape), _const_spec(rw.shape), _const_spec(rb.shape)],
        out_specs=out_specs,
        out_shape=out_shape,
        scratch_shapes=[pltpu.VMEM((tm, dm), F32), pltpu.VMEM((tm, dm), F32)],
        compiler_params=_params(("arbitrary",)),
        name="merge_route",
    )(x_ctx, x_lat, mod, mod, n1, n2, oaf, oab, pa, obf, obb, pb, oc_ctx, oc_lat, hg, wg, bg, wbr, wo, rw, rb)


def _slot_tables(route_rows, counts):
    tiles, _, tm = route_rows.shape
    cnt = counts[:, :, 0].astype(jnp.int32)
    tile_off = jnp.cumsum(cnt, axis=0) - cnt
    total = jnp.sum(cnt, axis=0)
    padded = (total + MOE_BLOCK - 1) // MOE_BLOCK * MOE_BLOCK
    p_end = jnp.cumsum(padded)
    base = (p_end - padded)[None, :] + tile_off
    e = route_rows[:, 0:2, :].astype(jnp.int32)
    rank = route_rows[:, 4:6, :].astype(jnp.int32)
    hit = e[:, :, :, None] == jnp.arange(N_EXPERTS, dtype=jnp.int32)
    slot = jnp.sum(jnp.where(hit, base[:, None, None, :], 0), axis=-1) + rank
    n_blocks = -(-(2 * tiles * tm) // MOE_BLOCK) + N_EXPERTS
    blk_start = jnp.arange(n_blocks, dtype=jnp.int32) * MOE_BLOCK
    block_e = jnp.minimum(jnp.sum(blk_start[:, None] >= p_end[None, :], axis=1), N_EXPERTS - 1).astype(jnp.int32)
    n_used = (p_end[-1] // MOE_BLOCK).astype(jnp.int32).reshape(1)
    filled = jnp.clip(((p_end - padded) + total)[block_e] - blk_start, 0, MOE_BLOCK)
    filled = jnp.where(blk_start < p_end[-1], filled, 0)
    partial = (filled < MOE_BLOCK).astype(jnp.int32)
    return slot.reshape(tiles, 1, 2 * tm), block_e, n_used, partial


def _row_copies(n_rows, make):
    for r in range(n_rows):
        for k in range(2):
            make(r, k).start()


def _scatter_kernel(partial_ref, slot_ref, h_ref, xd_ref, zeros, sem, zsem):
    tm = h_ref.shape[0]
    n_blocks = partial_ref.shape[0]

    @pl.when((pl.program_id(0) == 0) & (pl.program_id(1) == 0))
    def _():
        zeros[...] = jnp.zeros_like(zeros)

        def zero_block(blk):
            return pltpu.make_async_copy(zeros, xd_ref.at[pl.ds(blk * MOE_BLOCK, MOE_BLOCK), :], zsem)

        def start(blk, c):
            @pl.when(partial_ref[blk] != 0)
            def _():
                zero_block(blk).start()
            return c

        def wait(blk, c):
            @pl.when(partial_ref[blk] != 0)
            def _():
                zero_block(blk).wait()
            return c

        lax.fori_loop(0, n_blocks, start, 0)
        lax.fori_loop(0, n_blocks, wait, 0)

    _row_copies(tm, lambda r, k: pltpu.make_async_copy(
        h_ref.at[pl.ds(r, 1), :], xd_ref.at[pl.ds(slot_ref[0, k * tm + r], 1), :], sem))
    for _ in range(2):
        pltpu.make_async_copy(h_ref, xd_ref.at[pl.ds(0, tm), :], sem).wait()


def _scatter(partial, slot, h2, n_slots, tm):
    bsz, rows, dm = h2.shape
    nt = rows // tm
    grid_spec = pltpu.PrefetchScalarGridSpec(
        num_scalar_prefetch=1,
        grid=(bsz, nt),
        in_specs=[pl.BlockSpec((None, 1, 2 * tm), lambda b, t, p: (b * nt + t, 0, 0), memory_space=pltpu.SMEM),
                  pl.BlockSpec((tm, dm), lambda b, t, p: (b * nt + t, 0))],
        out_specs=pl.BlockSpec(memory_space=pl.ANY),
        scratch_shapes=[pltpu.VMEM((MOE_BLOCK, dm), F32), pltpu.SemaphoreType.DMA(()),
                        pltpu.SemaphoreType.DMA(())],
    )
    return pl.pallas_call(
        _scatter_kernel,
        grid_spec=grid_spec,
        out_shape=jax.ShapeDtypeStruct((n_slots, dm), F32),
        compiler_params=_params(("arbitrary", "arbitrary"), has_side_effects=True),
        name="moe_scatter",
    )(partial, slot, h2.reshape(bsz * rows, dm))


def _expert_kernel(be_ref, nu_ref, x_ref, w1_ref, w3_ref, w2_ref, y_ref, w1b, w3b, w2b):
    i = pl.program_id(0)

    @pl.when(i < nu_ref[0])
    def _():
        @pl.when((i == 0) | (be_ref[i] != be_ref[jnp.maximum(i - 1, 0)]))
        def _():
            w1b[...] = w1_ref[...].astype(BF16)
            w3b[...] = w3_ref[...].astype(BF16)
            w2b[...] = w2_ref[...].astype(BF16)

        xb = x_ref[...].astype(BF16)
        u = _dot(xb, w1b[...])
        hmid = (u * _sigmoid(u)) * _dot(xb, w3b[...])
        y_ref[...] = _dot(hmid.astype(BF16), w2b[...])

    @pl.when(i >= nu_ref[0])
    def _():
        y_ref[...] = jnp.zeros_like(y_ref)


def _experts(block_e, n_used, x_disp, w1, w3, w2, layer):
    n_slots, dm = x_disp.shape
    n_blocks = n_slots // MOE_BLOCK
    wspec = lambda shape: pl.BlockSpec((None, None) + shape, lambda i, be, nu: (layer, be[i], 0, 0))
    grid_spec = pltpu.PrefetchScalarGridSpec(
        num_scalar_prefetch=2,
        grid=(n_blocks,),
        in_specs=[pl.BlockSpec((MOE_BLOCK, dm), lambda i, be, nu: (jnp.minimum(i, nu[0] - 1), 0)),
                  wspec((dm, D_EXPERT)), wspec((dm, D_EXPERT)), wspec((D_EXPERT, dm))],
        out_specs=pl.BlockSpec((MOE_BLOCK, dm), lambda i, be, nu: (i, 0)),
        scratch_shapes=[pltpu.VMEM((dm, D_EXPERT), BF16), pltpu.VMEM((dm, D_EXPERT), BF16),
                        pltpu.VMEM((D_EXPERT, dm), BF16)],
    )
    return pl.pallas_call(
        _expert_kernel,
        grid_spec=grid_spec,
        out_shape=jax.ShapeDtypeStruct((n_slots, dm), F32),
        compiler_params=_params(("arbitrary",)),
        name="moe_experts",
    )(block_e, n_used, x_disp, w1, w3, w2)


def _combine_kernel(slot_ref, slot_next_ref, x_ref, mod_ref, rt_ref, yd_ref, fg_ref, o_ref, ybuf, sems):
    def body(y0, y1):
        o_ref[...] = _rms(_moe_residual(x_ref, mod_ref, rt_ref, y0, y1), fg_ref[...])

    _with_gathered_rows(slot_ref, slot_next_ref, yd_ref, ybuf, sems, body)


def _combine_final(slot, x, mod, route, y_disp, fg, tm, ctx_tiles, row0_tiles):
    bsz, rows, dm = x.shape
    nt = rows // tm
    n = bsz * nt
    row = lambda i: (i // nt, i % nt, 0)
    return pl.pallas_call(
        _combine_kernel,
        grid=(n,),
        in_specs=[pl.BlockSpec((None, 1, 2 * tm), lambda i: (i, 0, 0), memory_space=pltpu.SMEM),
                  pl.BlockSpec((None, 1, 2 * tm), lambda i: (jnp.minimum(i + 1, n - 1), 0, 0),
                               memory_space=pltpu.SMEM),
                  pl.BlockSpec((None, tm, dm), row),
                  pl.BlockSpec((None, None, 6, dm),
                               lambda i: (i // nt, jnp.where(row0_tiles + i % nt >= ctx_tiles, 1, 0), 0, 0)),
                  pl.BlockSpec((None, tm, LANES), row),
                  pl.BlockSpec(memory_space=pl.ANY),
                  _const_spec((1, dm))],
        out_specs=pl.BlockSpec((None, tm, dm), row),
        out_shape=jax.ShapeDtypeStruct((bsz, rows, dm), F32),
        scratch_shapes=[pltpu.VMEM((2, 2, tm, dm), F32), pltpu.SemaphoreType.DMA((2,))],
        compiler_params=_params(("arbitrary",)),
        name="moe_combine",
    )(slot, slot, x, mod, route, y_disp, fg)


def _rope_tables(ctx, t_lat):
    rows = t_lat // GRID_W
    row = np.repeat(np.arange(rows, dtype=np.float64), GRID_W)
    col = np.tile(np.arange(GRID_W, dtype=np.float64), rows)
    inv = ROPE_BASE ** (-np.arange(ROPE_PAIRS, dtype=np.float64) / ROPE_PAIRS)
    ang_r = row[:, None] * inv
    ang_c = col[:, None] * inv
    cos64 = np.concatenate([np.cos(ang_r), np.cos(ang_r), np.cos(ang_c), np.cos(ang_c)], axis=1)
    sin64 = np.concatenate([-np.sin(ang_r), np.sin(ang_r), -np.sin(ang_c), np.sin(ang_c)], axis=1)
    cos = np.tile(cos64, (1, MIX_W // 64))
    sin = np.tile(sin64, (1, MIX_W // 64))
    cos = np.concatenate([np.ones((ctx, MIX_W)), cos], axis=0).astype(np.float32)
    sin = np.concatenate([np.zeros((ctx, MIX_W)), sin], axis=0).astype(np.float32)
    return jnp.asarray(cos), jnp.asarray(sin)


def _pack_w_in(w):
    dm = w.shape[0]
    a_end = PA_W
    b_end = a_end + PB_W
    g_end = b_end + 4 * HEADS
    pad = jnp.zeros((dm, PG_W - 4 * HEADS), w.dtype)
    return jnp.concatenate([w[:, :b_end], w[:, b_end:g_end], pad, w[:, g_end:]], axis=1).astype(BF16)


def kernel(x, c, ctx, c_ctx, ada_w, ada_b, norm1_g, norm2_g, w_in, mlstm_conv_w, mlstm_conv_b, mlstm_gate_b,
           hgrn_lb_raw, hgrn_norm_g, mlstm_norm_g, diff_norm_g, diff_lambda, w_branch, w_gate, b_gate, w_out,
           router_g_w, router_g_b, router_e_w, router_e_b, moe_w1, moe_w3, moe_w2, final_g):
    bsz, t_lat, dm = x.shape
    t_ctx = ctx.shape[1]
    depth = ada_w.shape[0]
    tt = t_ctx + t_lat
    tm = min(256, t_ctx)
    hgrn_chunk = min(128, t_ctx)
    mlstm_chunk = min(256, t_ctx)
    assert t_ctx % tm == 0 and t_lat % (2 * tm) == 0 and t_lat % GRID_W == 0
    assert t_ctx % hgrn_chunk == 0 and t_ctx % mlstm_chunk == 0
    ctx_tiles = t_ctx // tm

    n_rows = -(-(bsz + 1) // 8) * 8
    cc = jnp.zeros((n_rows, dm), F32).at[:bsz].set(c).at[bsz].set(c_ctx)
    mods = _ada_mod(cc, ada_w, ada_b).reshape(depth, n_rows, 6, dm)

    cos, sin = _rope_tables(t_ctx, t_lat)
    lb_cum = jnp.cumsum(jax.nn.softmax(hgrn_lb_raw.astype(F32), axis=0), axis=0)
    lower = lb_cum - lb_cum[0]

    x_ctx, x_lat, lat_row0 = ctx, x, 0
    moe = None
    for l in range(depth):
        with_ctx = l < depth - 1
        lam_init = 0.8 - 0.6 * math.exp(-0.3 * l)
        mod = jnp.stack([jnp.broadcast_to(mods[l, bsz], (bsz, 6, dm)), mods[l, :bsz]], axis=1)
        n1 = norm1_g[l].reshape(1, dm)
        n2 = norm2_g[l].reshape(1, dm)

        lb = lower[l]
        pad = jnp.zeros((2, 6, MIX_W), F32)
        lbc = jnp.concatenate([jnp.log(lb)[:, None], jnp.log1p(-lb)[:, None], pad], axis=1)
        if moe is None:
            pa, pb, pg, qc, kc, vc = _norm_proj(x_ctx, x_lat, mod, n1, _pack_w_in(w_in[l]), cos, sin, lbc,
                                                tm, ctx_tiles)
        else:
            xc, pa, pb, pg, qc, kc, vc = _norm_proj_fused(moe, mod, n1, _pack_w_in(w_in[l]), cos, sin, lbc,
                                                          tm, ctx_tiles)
            x_ctx, x_lat, lat_row0 = xc, xc, ctx_tiles

        oaf, oab = _hgrn(pa, hgrn_chunk, t_ctx // hgrn_chunk)

        qk = _conv_silu(pb, mlstm_conv_w[l], mlstm_conv_b[l], t_ctx, 256)
        gbias = jnp.zeros((1, PG_W), F32).at[0, :4 * HEADS].set(mlstm_gate_b[l].reshape(-1))
        obf, obb = _mlstm(qk, pb, pg, gbias, mlstm_chunk, t_ctx // mlstm_chunk)

        lam_vec = diff_lambda[l].astype(F32)
        oc_lat = _attn_lat(lam_vec, qc, kc, vc, lam_init, tm, t_ctx)
        oc_ctx = _attn_ctx(lam_vec, qc, kc, vc, lam_init, tm, t_ctx) if with_ctx else oc_lat

        hg = jnp.stack([hgrn_norm_g[l], mlstm_norm_g[l], diff_norm_g[l]])
        rw = jnp.zeros((LANES, dm), F32).at[:N_GROUPS].set(router_g_w[l].T)
        rw = rw.at[N_GROUPS:N_GROUPS + N_EXPERTS].set(router_e_w[l].T)
        rw_hi = rw.astype(BF16)
        rw = jnp.stack([rw_hi, (rw - rw_hi.astype(F32)).astype(BF16)])
        rb = jnp.zeros((LANES, 1), F32).at[:N_GROUPS, 0].set(router_g_b[l])
        rb = rb.at[N_GROUPS:N_GROUPS + N_EXPERTS, 0].set(router_e_b[l])
        row0_tiles = 0 if with_ctx else ctx_tiles
        xn, h2, route, route_rows, counts = _merge(x_ctx, x_lat, lat_row0, mod, n1, n2, oaf, oab, pa, obf, obb, pb,
                                                   oc_ctx, oc_lat, hg,
                                                   w_gate[l].astype(BF16), b_gate[l].reshape(1, 3 * dm),
                                                   w_branch[l].astype(BF16), w_out[l].astype(BF16), rw, rb,
                                                   lam_init, tm, ctx_tiles, row0_tiles)

        slot, block_e, n_used, partial = _slot_tables(route_rows, counts)
        x_disp = _scatter(partial, slot, h2, partial.shape[0] * MOE_BLOCK, tm)
        y_disp = _experts(block_e, n_used, x_disp, moe_w1, moe_w3, moe_w2, l)
        moe = (slot, xn, mod, route, y_disp)
    slot, xn, mod, route, y_disp = moe
    return _combine_final(slot, xn, mod, route, y_disp, final_g.reshape(1, dm), tm, ctx_tiles, ctx_tiles)
```

```python
import functools
import math

import numpy as np
import jax
import jax.numpy as jnp
from jax import lax
from jax.experimental import pallas as pl
from jax.experimental.pallas import tpu as pltpu

F32 = jnp.float32
BF16 = jnp.bfloat16

EPS = 1e-6
NEG_BIG = -1e30
HEADS = 4
HEAD_W = 128
MIX_W = HEADS * HEAD_W
GRID_W = 64
ROPE_BASE = 10000.0
ROPE_PAIRS = 16
B_CONV = 3
N_GROUPS = 4
EXPERTS_PER_GROUP = 8
N_EXPERTS = N_GROUPS * EXPERTS_PER_GROUP
D_EXPERT = 512
MOE_BLOCK = 256
LANES = 128
VMEM_LIMIT = 50 * 1024 * 1024

PA_W = 5 * MIX_W
PB_W = 4 * MIX_W
PG_W = LANES
PC_W = 3 * MIX_W
PROJ_W = PA_W + PB_W + PG_W + PC_W


def _params(sem, **kw):
    return pltpu.CompilerParams(dimension_semantics=sem, vmem_limit_bytes=VMEM_LIMIT, **kw)


def _const_spec(shape):
    nd = len(shape)
    return pl.BlockSpec(shape, lambda *_: (0,) * nd, pipeline_mode=pl.Buffered(1))


def _dot(a, b):
    return jnp.dot(a, b, preferred_element_type=F32)


def _dot_nt(a, b):
    return lax.dot_general(a, b, (((1,), (1,)), ((), ())), preferred_element_type=F32)


def _dot_tn(a, b):
    return lax.dot_general(a, b, (((0,), (0,)), ((), ())), preferred_element_type=F32)


def _dot_sel(m_bf16, x):
    hi = x.astype(BF16)
    r1 = x - hi.astype(F32)
    mid = r1.astype(BF16)
    lo = (r1 - mid.astype(F32)).astype(BF16)
    return _dot(m_bf16, hi) + _dot(m_bf16, mid) + _dot(m_bf16, lo)


def _sigmoid(x):
    return 1.0 / (1.0 + jnp.exp(-x))


def _log_sigmoid(x):
    return jnp.minimum(x, 0.0) - jnp.log1p(jnp.exp(-jnp.abs(x)))


def _rms(x, g):
    return x * lax.rsqrt(jnp.mean(x * x, axis=-1, keepdims=True) + EPS) * g


def _ada_kernel(c_ref, w_ref, b_ref, o_ref):
    c = c_ref[...]
    s = c * _sigmoid(c)
    o_ref[...] = jnp.dot(s, w_ref[...], preferred_element_type=F32,
                         precision=lax.Precision.HIGHEST) + b_ref[...]


def _ada_mod(cc, ada_w, ada_b):
    depth, dm, six = ada_w.shape
    rows = cc.shape[0]
    tn = dm
    return pl.pallas_call(
        _ada_kernel,
        grid=(depth, six // tn),
        in_specs=[pl.BlockSpec((rows, dm), lambda l, n: (0, 0)),
                  pl.BlockSpec((None, dm, tn), lambda l, n: (l, 0, n)),
                  pl.BlockSpec((None, 1, tn), lambda l, n: (l, 0, n))],
        out_specs=pl.BlockSpec((None, rows, tn), lambda l, n: (l, 0, n)),
        out_shape=jax.ShapeDtypeStruct((depth, rows, six), F32),
        compiler_params=_params(("parallel", "parallel")),
        name="ada_mod",
    )(cc, ada_w, ada_b.reshape(depth, 1, six))


def _rope(x, cos, sin):
    n = x.shape[-1]
    lane = lax.broadcasted_iota(jnp.int32, (1, n), 1)
    first = (lane // ROPE_PAIRS) % 2 == 0
    partner = jnp.where(first, pltpu.roll(x, n - ROPE_PAIRS, 1), pltpu.roll(x, ROPE_PAIRS, 1))
    return x * cos + partner * sin


def _hgrn_log_forget(z, lbc):
    lsig = jnp.minimum(z, 0.0) - jnp.log(1.0 + jnp.exp(-jnp.abs(z)))
    a = lbc[0:1]
    bb = lbc[1:2] + lsig
    return jnp.maximum(a, bb) + jnp.log(1.0 + jnp.exp(-jnp.abs(a - bb)))


def _with_gathered_rows(slot_ref, slot_next_ref, yd_ref, ybuf, sems, body):
    i = pl.program_id(0)
    n = pl.num_programs(0)
    tm = ybuf.shape[2]

    def start(table_ref, b):
        _row_copies(tm, lambda r, k: pltpu.make_async_copy(
            yd_ref.at[pl.ds(table_ref[0, k * tm + r], 1), :], ybuf.at[b, k, pl.ds(r, 1), :], sems.at[b]))

    def wait(b):
        for k in range(2):
            pltpu.make_async_copy(yd_ref.at[pl.ds(0, tm), :], ybuf.at[b, k], sems.at[b]).wait()

    @pl.when(i == 0)
    def _():
        start(slot_ref, 0)

    for cur in range(2):
        @pl.when(i % 2 == cur)
        def _():
            wait(cur)
            start(slot_next_ref, 1 - cur)
            body(ybuf[cur, 0], ybuf[cur, 1])

            @pl.when(i == n - 1)
            def _():
                wait(1 - cur)


def _moe_residual(x_ref, mod_ref, rt_ref, y0, y1):
    rt = rt_ref[...]
    return x_ref[...] + mod_ref[5:6, :] * (rt[:, 2:3] * y0 + rt[:, 3:4] * y1)


def _norm_proj_fused_kernel(slot_ref, slot_next_ref, xp_ref, modp_ref, rt_ref, yd_ref,
                            mod_ref, g_ref, w_ref, cos_ref, sin_ref, lbc_ref,
                            xo_ref, pa_ref, pb_ref, pg_ref, qc_ref, kc_ref, vc_ref, ybuf, sems):
    def body(y0, y1):
        x = _moe_residual(xp_ref, modp_ref, rt_ref, y0, y1)
        xo_ref[...] = x
        _project(x, mod_ref, g_ref, w_ref, cos_ref, sin_ref, lbc_ref,
                 pa_ref, pb_ref, pg_ref, qc_ref, kc_ref, vc_ref)

    _with_gathered_rows(slot_ref, slot_next_ref, yd_ref, ybuf, sems, body)


def _norm_proj_kernel(xc_ref, xl_ref, mod_ref, g_ref, w_ref, cos_ref, sin_ref, lbc_ref,
                      pa_ref, pb_ref, pg_ref, qc_ref, kc_ref, vc_ref, *, ctx_tiles):
    x = jnp.where(pl.program_id(1) < ctx_tiles, xc_ref[...], xl_ref[...])
    _project(x, mod_ref, g_ref, w_ref, cos_ref, sin_ref, lbc_ref,
             pa_ref, pb_ref, pg_ref, qc_ref, kc_ref, vc_ref)


def _project(x, mod_ref, g_ref, w_ref, cos_ref, sin_ref, lbc_ref, pa_ref, pb_ref, pg_ref, qc_ref, kc_ref, vc_ref):
    mod = mod_ref[...]
    h = _rms(x, g_ref[...]) * (1.0 + mod[1:2]) + mod[0:1]
    hb = h.astype(BF16)
    pa_ref[:, 0:3 * MIX_W] = _dot(hb, w_ref[:, 0:3 * MIX_W])
    for d in range(2):
        cs = slice((3 + d) * MIX_W, (4 + d) * MIX_W)
        pa_ref[:, cs] = _hgrn_log_forget(_dot(hb, w_ref[:, cs]), lbc_ref[d])
    pb_ref[...] = _dot(hb, w_ref[:, PA_W:PA_W + PB_W])
    pg_ref[...] = _dot(hb, w_ref[:, PA_W + PB_W:PA_W + PB_W + PG_W])
    c0 = PA_W + PB_W + PG_W
    cos = cos_ref[...]
    sin = sin_ref[...]
    q = _dot(hb, w_ref[:, c0:c0 + MIX_W])
    qc_ref[...] = (_rope(q, cos, sin) * (64.0 ** -0.5 * math.log2(math.e))).astype(BF16)
    k = _dot(hb, w_ref[:, c0 + MIX_W:c0 + 2 * MIX_W])
    kc_ref[...] = _rope(k, cos, sin).astype(BF16)
    vc_ref[...] = _dot(hb, w_ref[:, c0 + 2 * MIX_W:c0 + 3 * MIX_W]).astype(BF16)


def _norm_proj_fused(moe, mod, g, w, cos, sin, lbc, tm, ctx_tiles):
    slot, xp, modp, route, y_disp = moe
    bsz, tt, dm = xp.shape
    nt = tt // tm
    n = bsz * nt
    row = lambda i: (i // nt, i % nt, 0)
    mod_row = lambda i: (i // nt, jnp.where(i % nt >= ctx_tiles, 1, 0), 0, 0)
    tab = lambda i: (i % nt, 0)
    outs = [(dm, F32), (PA_W, F32), (PB_W, F32), (PG_W, F32), (MIX_W, BF16), (MIX_W, BF16), (MIX_W, BF16)]
    return pl.pallas_call(
        _norm_proj_fused_kernel,
        grid=(n,),
        in_specs=[pl.BlockSpec((None, 1, 2 * tm), lambda i: (i, 0, 0), memory_space=pltpu.SMEM),
                  pl.BlockSpec((None, 1, 2 * tm), lambda i: (jnp.minimum(i + 1, n - 1), 0, 0),
                               memory_space=pltpu.SMEM),
                  pl.BlockSpec((None, tm, dm), row),
                  pl.BlockSpec((None, None, 6, dm), mod_row),
                  pl.BlockSpec((None, tm, LANES), row),
                  pl.BlockSpec(memory_space=pl.ANY),
                  pl.BlockSpec((None, None, 6, dm), mod_row),
                  _const_spec((1, dm)),
                  _const_spec((dm, PROJ_W)),
                  pl.BlockSpec((tm, MIX_W), tab),
                  pl.BlockSpec((tm, MIX_W), tab),
                  _const_spec(lbc.shape)],
        out_specs=[pl.BlockSpec((None, tm, wd), row) for wd, _ in outs],
        out_shape=[jax.ShapeDtypeStruct((bsz, tt, wd), dt) for wd, dt in outs],
        scratch_shapes=[pltpu.VMEM((2, 2, tm, dm), F32), pltpu.SemaphoreType.DMA((2,))],
        compiler_params=_params(("arbitrary",)),
        name="combine_norm_proj",
    )(slot, slot, xp, modp, route, y_disp, mod, g, w, cos, sin, lbc)


def _norm_proj(x_ctx, x_lat, mod, g, w, cos, sin, lbc, tm, ctx_tiles):
    bsz, t_ctx, dm = x_ctx.shape
    tt = t_ctx + x_lat.shape[1]
    nt = tt // tm
    row = lambda b, t: (b, t, 0)
    outs = [(PA_W, F32), (PB_W, F32), (PG_W, F32), (MIX_W, BF16), (MIX_W, BF16), (MIX_W, BF16)]
    return pl.pallas_call(
        functools.partial(_norm_proj_kernel, ctx_tiles=ctx_tiles),
        grid=(bsz, nt),
        in_specs=[pl.BlockSpec((None, tm, dm), lambda b, t: (b, jnp.minimum(t, ctx_tiles - 1), 0)),
                  pl.BlockSpec((None, tm, dm), lambda b, t: (b, jnp.maximum(t - ctx_tiles, 0), 0)),
                  pl.BlockSpec((None, None, 6, dm), lambda b, t: (b, jnp.where(t >= ctx_tiles, 1, 0), 0, 0)),
                  _const_spec((1, dm)),
                  _const_spec((dm, PROJ_W)),
                  pl.BlockSpec((tm, MIX_W), lambda b, t: (t, 0)),
                  pl.BlockSpec((tm, MIX_W), lambda b, t: (t, 0)),
                  _const_spec(lbc.shape)],
        out_specs=[pl.BlockSpec((None, tm, wd), row) for wd, _ in outs],
        out_shape=[jax.ShapeDtypeStruct((bsz, tt, wd), dt) for wd, dt in outs],
        compiler_params=_params(("parallel", "parallel")),
        name="norm_proj",
    )(x_ctx, x_lat, mod, g, w, cos, sin, lbc)


def _conv_kernel(x_ref, w_ref, b_ref, o_ref, *, ctx, q_blocks):
    x = x_ref[...]
    tt = x.shape[0]
    row = lax.broadcasted_iota(jnp.int32, (tt, 1), 0)
    prev = jnp.where((row == 0) | (row == ctx), 0.0, pltpu.roll(x, 1, 0))
    nxt = jnp.where((row == ctx - 1) | (row == tt - 1), 0.0, pltpu.roll(x, tt - 1, 0))
    w = w_ref[...]
    y = b_ref[...] + prev * w[0:1] + x * w[1:2] + nxt * w[2:3]
    y = y * _sigmoid(y)
    scale = jnp.where(pl.program_id(1) < q_blocks, HEAD_W ** -0.5, 1.0)
    o_ref[...] = y * scale


def _conv_silu(pb, conv_w, conv_b, ctx, cb):
    bsz, tt, _ = pb.shape
    width = 2 * MIX_W
    return pl.pallas_call(
        functools.partial(_conv_kernel, ctx=ctx, q_blocks=MIX_W // cb),
        grid=(bsz, width // cb),
        in_specs=[pl.BlockSpec((None, tt, cb), lambda b, j: (b, 0, j)),
                  pl.BlockSpec((B_CONV, cb), lambda b, j: (0, j)),
                  pl.BlockSpec((1, cb), lambda b, j: (0, j))],
        out_specs=pl.BlockSpec((None, tt, cb), lambda b, j: (b, 0, j)),
        out_shape=jax.ShapeDtypeStruct((bsz, tt, width), F32),
        compiler_params=_params(("parallel", "parallel")),
        name="mlstm_conv",
    )(pb, conv_w, conv_b.reshape(1, width))


def _chunk_consts(L):
    idx = np.arange(L)
    t, u = idx[:, None], idx[None, :]
    cum = (u <= t).astype(np.float32)
    cum_b = cum[::-1, ::-1]
    nlev = int(round(math.log2(L)))
    lvl = np.full((L, L), -1, np.int32)
    lvl[idx, idx] = nlev
    for i in range(nlev):
        h = L >> (i + 1)
        same = (t // (2 * h)) == (u // (2 * h))
        lvl[same & (t % (2 * h) >= h) & (u % (2 * h) < h)] = i
    sel = jnp.asarray(np.stack([cum, cum_b]), BF16)
    lvls = jnp.asarray(np.stack([lvl, lvl[::-1, ::-1]]))
    return sel, lvls, nlev


def _midpoint_rows(b, h, d):
    L, w = b.shape
    two = 2 * h
    pos = h - 1 if d == 0 else h
    if two % 8 == 0:
        r = b.reshape(L // two, two, w)[:, pos:pos + 1, :]
        return jnp.broadcast_to(r, (L // two, two, w)).reshape(L, w)
    phase = lax.broadcasted_iota(jnp.int32, (L, 1), 0) % two
    ref = b
    for off in range(pos - two + 1, pos + 1):
        if off != 0:
            ref = jnp.where(phase == pos - off, pltpu.roll(b, (L - off) % L, 0), ref)
    return ref


def _scan_blocks(j, nc_ctx, nc):
    jb = jnp.where(j < nc_ctx, nc_ctx - 1 - j, nc - 1 + nc_ctx - j)
    return j, jb


def _hgrn_kernel(qf_ref, vf_ref, ff_ref, qb_ref, vb_ref, fb_ref, sel_ref, lvl_ref,
                 of_ref, ob_ref, st_ref, *, L, nlev):
    @pl.when(pl.program_id(1) == 0)
    def _():
        st_ref[...] = jnp.zeros_like(st_ref)

    row = lax.broadcasted_iota(jnp.int32, (L, 1), 0)
    dirs = ((qf_ref, vf_ref, ff_ref, of_ref), (qb_ref, vb_ref, fb_ref, ob_ref))

    def one(bb, d, q_ref, v_ref, f_ref, o_ref):
        lf = f_ref[bb]
        kk = 1.0 - jnp.exp(lf)
        q_all = q_ref[bb]
        b_in = _dot_sel(sel_ref[d], lf) * math.log2(math.e)
        lvl = lvl_ref[d]
        last = L - 1 if d == 0 else 0
        b_end = b_in[last:last + 1]
        heads = [slice(hd * HEAD_W, (hd + 1) * HEAD_W) for hd in range(HEADS)]
        qb = q_all.astype(BF16)
        kb = kk.astype(BF16)
        on_diag = lvl == nlev
        scores = [jnp.where(on_diag, _dot_nt(qb[:, cs], kb[:, cs]), 0.0) for cs in heads]
        for i in range(nlev):
            h = L >> (i + 1)
            ei = jnp.exp2(-jnp.abs(b_in - _midpoint_rows(b_in, h, d)))
            is_query = (row % (2 * h) >= h) if d == 0 else (row % (2 * h) < h)
            xe = (jnp.where(is_query, q_all, kk) * ei).astype(BF16)
            at_level = lvl == i
            for hd, cs in enumerate(heads):
                scores[hd] = scores[hd] + jnp.where(at_level, _dot_nt(xe[:, cs], xe[:, cs]), 0.0)
        q_in = (q_all * jnp.exp2(b_in)).astype(BF16)
        k_out = (kk * jnp.exp2(b_end - b_in)).astype(BF16)
        decay = jnp.exp2(b_end)
        for hd, cs in enumerate(heads):
            vb = v_ref[bb, :, cs].astype(BF16)
            st = st_ref[bb, d, hd]
            o_ref[bb, :, cs] = _dot(scores[hd].astype(BF16), vb) + _dot_nt(q_in[:, cs], st.astype(BF16))
            st_ref[bb, d, hd] = st * decay[:, cs] + _dot_tn(vb, k_out[:, cs])

    for bb in range(qf_ref.shape[0]):
        for d, refs in enumerate(dirs):
            one(bb, d, *refs)


def _hgrn(pa, L, nc_ctx):
    bsz, tt, _ = pa.shape
    nc = tt // L
    nb = 4 if bsz % 4 == 0 else (2 if bsz % 2 == 0 else 1)
    sel, lvl, nlev = _chunk_consts(L)

    def spec(col, which):
        return pl.BlockSpec((nb, L, MIX_W), lambda b, j: (b, _scan_blocks(j, nc_ctx, nc)[which], col))

    return pl.pallas_call(
        functools.partial(_hgrn_kernel, L=L, nlev=nlev),
        grid=(bsz // nb, nc),
        in_specs=[spec(0, 0), spec(1, 0), spec(3, 0), spec(0, 1), spec(1, 1), spec(4, 1),
                  _const_spec(sel.shape), _const_spec(lvl.shape)],
        out_specs=[spec(0, 0), spec(0, 1)],
        out_shape=[jax.ShapeDtypeStruct((bsz, tt, MIX_W), F32)] * 2,
        scratch_shapes=[pltpu.VMEM((nb, 2, HEADS, HEAD_W, HEAD_W), F32)],
        compiler_params=_params(("parallel", "arbitrary")),
        name="hgrn2_scan",
    )(pa, pa, pa, pa, pa, pa, sel, lvl)


def _mlstm_kernel(qf_ref, kf_ref, vf_ref, gf_ref, qb_ref, kb_ref, vb_ref, gb_ref, gbias_ref,
                  sel_ref, lvl_ref, of_ref, ob_ref, c_ref, m_ref, *, L):
    @pl.when(pl.program_id(1) == 0)
    def _():
        c_ref[...] = jnp.zeros_like(c_ref)
        m_ref[...] = jnp.zeros_like(m_ref)

    lane = lax.broadcasted_iota(jnp.int32, (1, LANES), 1)
    tlane = lax.broadcasted_iota(jnp.int32, (HEADS, L), 1)
    ones_col = jnp.where(lane == 0, 1.0, 0.0).astype(BF16)
    dirs = ((qf_ref, kf_ref, vf_ref, gf_ref, of_ref), (qb_ref, kb_ref, vb_ref, gb_ref, ob_ref))
    for d, (q_ref, k_ref, v_ref, g_ref, o_ref) in enumerate(dirs):
        g = g_ref[...] + gbias_ref[...]
        lf = jnp.where((lane >= 2 * HEADS) & (lane < 4 * HEADS), _log_sigmoid(g), 0.0)
        bcum = _dot_sel(sel_ref[d], lf)
        causal = lvl_ref[d] >= 0
        rows = jnp.where(lane < 2 * HEADS, g - pltpu.roll(bcum, LANES - 2 * HEADS, 1), bcum).T
        r4 = rows[d * HEADS:(d + 1) * HEADS]
        b4 = rows[(2 + d) * HEADS:(3 + d) * HEADS]
        m4 = m_ref[d * HEADS:(d + 1) * HEADS, 0:1]
        run = r4
        sh = 1
        while sh < L:
            if d == 0:
                run = jnp.maximum(run, jnp.where(tlane >= sh, pltpu.roll(run, sh, 1), -jnp.inf))
            else:
                run = jnp.maximum(run, jnp.where(tlane < L - sh, pltpu.roll(run, L - sh, 1), -jnp.inf))
            sh *= 2
        c4 = jnp.maximum(run, m4)
        cols = jnp.concatenate([c4, jnp.exp(m4 - c4), jnp.exp(-b4 - c4),
                                jnp.zeros((LANES - 3 * HEADS, L), F32)], axis=0).T
        last = L - 1 if d == 0 else 0
        for hd in range(HEADS):
            cs = slice(hd * HEAD_W, (hd + 1) * HEAD_W)
            row_term = r4[hd:hd + 1, :]
            b_end = b4[hd:hd + 1, last:last + 1]
            r = d * HEADS + hd
            m_prev = m4[hd:hd + 1]
            q = q_ref[:, cs].astype(BF16)
            k = k_ref[:, cs]
            v_aug = jnp.concatenate([v_ref[:, cs].astype(BF16), jnp.broadcast_to(ones_col, (L, LANES))], axis=1)
            c_prev = c_ref[r]
            w_end = b_end + row_term
            m_new = jnp.maximum(b_end + m_prev, jnp.max(w_end, axis=-1, keepdims=True))
            e_end = jnp.exp(w_end - m_new)
            keep = jnp.exp(b_end + m_prev - m_new)
            c_ref[r] = keep * c_prev + _dot((k.T * e_end).astype(BF16), v_aug)
            m_ref[r:r + 1, :] = jnp.broadcast_to(m_new, (1, LANES))
            c_t = cols[:, hd:hd + 1]
            a_state = cols[:, HEADS + hd:HEADS + hd + 1]
            floor = cols[:, 2 * HEADS + hd:2 * HEADS + hd + 1]
            s = _dot_nt(q, k.astype(BF16)) * jnp.exp(jnp.where(causal, row_term - c_t, NEG_BIG))
            num = _dot(s.astype(BF16), v_aug) + a_state * _dot(q, c_prev.astype(BF16))
            den = num[:, HEAD_W:HEAD_W + 1]
            o_ref[:, cs] = num[:, 0:HEAD_W] / jnp.maximum(jnp.abs(den), floor)


def _mlstm(qk, pb, pg, gbias, L, nc_ctx):
    bsz, tt, _ = pb.shape
    nc = tt // L
    sel, lvl, _ = _chunk_consts(L)

    def spec(col, which, width=MIX_W):
        return pl.BlockSpec((None, L, width), lambda b, j: (b, _scan_blocks(j, nc_ctx, nc)[which], col))

    return pl.pallas_call(
        functools.partial(_mlstm_kernel, L=L),
        grid=(bsz, nc),
        in_specs=[spec(0, 0), spec(1, 0), spec(2, 0), spec(0, 0, PG_W),
                  spec(0, 1), spec(1, 1), spec(2, 1), spec(0, 1, PG_W),
                  _const_spec((1, PG_W)), _const_spec(sel.shape), _const_spec(lvl.shape)],
        out_specs=[spec(0, 0), spec(0, 1)],
        out_shape=[jax.ShapeDtypeStruct((bsz, tt, MIX_W), F32)] * 2,
        scratch_shapes=[pltpu.VMEM((2 * HEADS, HEAD_W, 2 * HEAD_W), F32),
                        pltpu.VMEM((2 * HEADS, LANES), F32)],
        compiler_params=_params(("parallel", "arbitrary")),
        name="mlstm_scan",
    )(qk, qk, pb, pg, qk, qk, pb, pg, gbias, sel, lvl)


def _diff_lambda(lam_ref, lam_init):
    lv = lam_ref[...]
    return (jnp.exp(jnp.sum(lv[0:1] * lv[1:2], axis=-1, keepdims=True))
            - jnp.exp(jnp.sum(lv[2:3] * lv[3:4], axis=-1, keepdims=True)) + lam_init)


def _map_scores(q, k):
    lane = lax.broadcasted_iota(jnp.int32, (1, HEAD_W), 1)
    zero = jnp.zeros_like(q)
    return (_dot_nt(jnp.where(lane < HEAD_W // 2, q, zero), k),
            _dot_nt(jnp.where(lane >= HEAD_W // 2, q, zero), k))


def _row_max(s):
    return jnp.max(s, axis=-1, keepdims=True)


def _diff_softmax_v(s1, m1, s2, m2, lam, v):
    p1 = jnp.exp2(s1 - m1)
    p2 = jnp.exp2(s2 - m2)
    l1 = jnp.sum(p1, axis=-1, keepdims=True)
    l2 = jnp.sum(p2, axis=-1, keepdims=True)
    return _dot((p1 - (lam * l1 / l2) * p2).astype(BF16), v) / l1


def _attn_ctx_kernel(lam_ref, q_ref, k_ref, v_ref, o_ref, *, lam_init):
    s1, s2 = _map_scores(q_ref[...], k_ref[...])
    o_ref[...] = _diff_softmax_v(s1, _row_max(s1), s2, _row_max(s2), _diff_lambda(lam_ref, lam_init), v_ref[...])


def _attn_ctx(lam_vec, qc, kc, vc, lam_init, tq, t_ctx):
    bsz = qc.shape[0]
    kv = pl.BlockSpec((None, t_ctx, HEAD_W), lambda b, h, i: (b, 0, h))
    return pl.pallas_call(
        functools.partial(_attn_ctx_kernel, lam_init=lam_init),
        grid=(bsz, HEADS, t_ctx // tq),
        in_specs=[_const_spec(lam_vec.shape),
                  pl.BlockSpec((None, tq, HEAD_W), lambda b, h, i: (b, i, h)), kv, kv],
        out_specs=pl.BlockSpec((None, tq, HEAD_W), lambda b, h, i: (b, i, h)),
        out_shape=jax.ShapeDtypeStruct((bsz, t_ctx, MIX_W), F32),
        compiler_params=_params(("parallel", "parallel", "arbitrary")),
        name="diff_attn_ctx",
    )(lam_vec, qc, kc, vc)


def _attn_lat_kernel(lam_ref, qa_ref, qb_ref, k_ref, v_ref, o_ref, s_even, m_even, s_odd, m_odd,
                     *, lam_init, n_tiles):
    j = pl.program_id(2)
    lam = _diff_lambda(lam_ref, lam_init)
    even, odd = (s_even, m_even), (s_odd, m_odd)
    half = qa_ref.shape[0]

    def scores_into(bufs):
        s_buf, m_buf = bufs
        for part, q_ref in enumerate((qa_ref, qb_ref)):
            rows = slice(part * half, (part + 1) * half)
            for i, s in enumerate(_map_scores(q_ref[...], k_ref[...])):
                s_buf[i, rows, :] = s
                m_buf[i, rows, :] = jnp.broadcast_to(_row_max(s), (half, LANES))

    def finish_from(bufs):
        s_buf, m_buf = bufs
        for part in range(2):
            rows = slice(part * half, (part + 1) * half)
            o_ref[rows, :] = _diff_softmax_v(s_buf[0, rows, :], m_buf[0, rows, 0:1],
                                             s_buf[1, rows, :], m_buf[1, rows, 0:1], lam, v_ref[...])

    @pl.when(j == 0)
    def _():
        scores_into(even)

    middle = (j > 0) & (j < n_tiles)

    @pl.when(middle & (j % 2 == 1))
    def _():
        scores_into(odd)
        finish_from(even)

    @pl.when(middle & (j % 2 == 0))
    def _():
        scores_into(even)
        finish_from(odd)

    @pl.when(j == n_tiles)
    def _():
        finish_from(odd if n_tiles % 2 == 0 else even)


def _attn_lat(lam_vec, qc, kc, vc, lam_init, tq, t_ctx):
    bsz, tt, _ = qc.shape
    n_tiles = (tt - t_ctx) // (2 * tq)
    q0 = t_ctx // tq

    def q_spec(part):
        return pl.BlockSpec((None, tq, HEAD_W),
                            lambda b, h, j: (b, q0 + 2 * jnp.minimum(j, n_tiles - 1) + part, h))

    kv = pl.BlockSpec((None, tt, HEAD_W), lambda b, h, j: (b, 0, h))
    return pl.pallas_call(
        functools.partial(_attn_lat_kernel, lam_init=lam_init, n_tiles=n_tiles),
        grid=(bsz, HEADS, n_tiles + 1),
        in_specs=[_const_spec(lam_vec.shape), q_spec(0), q_spec(1), kv, kv],
        out_specs=pl.BlockSpec((None, 2 * tq, HEAD_W), lambda b, h, j: (b, jnp.maximum(j - 1, 0), h)),
        out_shape=jax.ShapeDtypeStruct((bsz, tt - t_ctx, MIX_W), F32),
        scratch_shapes=[pltpu.VMEM((2, 2 * tq, tt), F32), pltpu.VMEM((2, 2 * tq, LANES), F32),
                        pltpu.VMEM((2, 2 * tq, tt), F32), pltpu.VMEM((2, 2 * tq, LANES), F32)],
        compiler_params=_params(("parallel", "parallel", "arbitrary")),
        name="diff_attn_lat",
    )(lam_vec, qc, qc, kc, vc)


def _head_norm(x, g):
    parts = []
    for hd in range(HEADS):
        xs = x[:, hd * HEAD_W:(hd + 1) * HEAD_W]
        parts.append(xs * lax.rsqrt(jnp.mean(xs * xs, axis=-1, keepdims=True) + EPS))
    return jnp.concatenate(parts, axis=1) * g


ROUTE_ROWS = 40


def _route(lg):
    n = lg.shape[1]
    row = lax.broadcasted_iota(jnp.int32, lg.shape, 0)
    neg = -jnp.inf
    big = ROUTE_ROWS
    gl = jnp.where(row < N_GROUPS, lg, neg)
    gmax = jnp.max(gl, axis=0, keepdims=True)
    gidx = jnp.min(jnp.where(gl == gmax, row, big), axis=0, keepdims=True)
    p_group = 1.0 / jnp.sum(jnp.exp(gl - gmax), axis=0, keepdims=True)
    lo = N_GROUPS + EXPERTS_PER_GROUP * gidx
    in_grp = (row >= lo) & (row < lo + EXPERTS_PER_GROUP)
    el = jnp.where(in_grp, lg, neg)
    pe = jnp.exp(el - jnp.max(el, axis=0, keepdims=True))
    pe = pe / jnp.sum(pe, axis=0, keepdims=True)
    pe = jnp.where(in_grp, pe, -1.0)
    v1 = jnp.max(pe, axis=0, keepdims=True)
    i1 = jnp.min(jnp.where(pe == v1, row, big), axis=0, keepdims=True)
    pe2 = jnp.where(row == i1, -1.0, pe)
    v2 = jnp.max(pe2, axis=0, keepdims=True)
    i2 = jnp.min(jnp.where(pe2 == v2, row, big), axis=0, keepdims=True)
    scale = p_group / (v1 + v2)
    erow = lax.broadcasted_iota(jnp.int32, (N_EXPERTS, n), 0) + N_GROUPS
    oh1 = erow == i1
    oh2 = erow == i2
    oh1f = jnp.where(oh1, 1.0, 0.0)
    oh2f = jnp.where(oh2, 1.0, 0.0)
    earlier = (lax.broadcasted_iota(jnp.int32, (n, n), 0) < lax.broadcasted_iota(jnp.int32, (n, n), 1))
    earlier = jnp.where(earlier, 1.0, 0.0).astype(BF16)
    tot1 = jnp.sum(oh1f, axis=1, keepdims=True)
    tot2 = jnp.sum(oh2f, axis=1, keepdims=True)
    rank1 = jnp.sum(jnp.where(oh1, _dot(oh1f.astype(BF16), earlier), 0.0), axis=0, keepdims=True)
    rank2 = jnp.sum(jnp.where(oh2, _dot(oh2f.astype(BF16), earlier) + tot1, 0.0), axis=0, keepdims=True)
    r8 = lax.broadcasted_iota(jnp.int32, (8, n), 0)
    rows = jnp.zeros((8, n), F32)
    for i, val in enumerate(((i1 - N_GROUPS).astype(F32), (i2 - N_GROUPS).astype(F32), v1 * scale, v2 * scale,
                             rank1, rank2)):
        rows = jnp.where(r8 == i, val, rows)
    return rows, tot1 + tot2


def _merge_kernel(xc_ref, xl_ref, mod_ref, modb_ref, n1_ref, n2_ref, af_ref, ab_ref, ga_ref, bf_ref, bb_ref, gb_ref,
                  occ_ref, ocl_ref, hg_ref, wg_ref, bg_ref, wbr_ref, wo_ref, rw_ref, rb_ref,
                  xo_ref, h2_ref, rt_ref, rtt_ref, cnt_ref, x_even, x_odd,
                  *, lam_init, ctx_tiles, row0_tiles, nt, n):
    i = pl.program_id(0)
    is_ctx_tile = row0_tiles + jnp.minimum(i, n - 1) % nt < ctx_tiles

    def front(stash):
        xn = _merge_front(is_ctx_tile, xc_ref, xl_ref, mod_ref, n1_ref, af_ref, ab_ref, ga_ref, bf_ref, bb_ref,
                          gb_ref, occ_ref, ocl_ref, hg_ref, wg_ref, bg_ref, wbr_ref, wo_ref, lam_init)
        xo_ref[...] = xn
        stash[...] = xn

    def back(stash):
        _merge_back(stash[...], modb_ref, n2_ref, rw_ref, rb_ref, h2_ref, rt_ref, rtt_ref, cnt_ref)

    @pl.when(i == 0)
    def _():
        front(x_even)

    middle = (i > 0) & (i < n)

    @pl.when(middle & (i % 2 == 1))
    def _():
        front(x_odd)
        back(x_even)

    @pl.when(middle & (i % 2 == 0))
    def _():
        front(x_even)
        back(x_odd)

    @pl.when(i == n)
    def _():
        back(x_odd if n % 2 == 0 else x_even)


def _merge_front(is_ctx_tile, xc_ref, xl_ref, mod_ref, n1_ref, af_ref, ab_ref, ga_ref, bf_ref, bb_ref, gb_ref,
                 occ_ref, ocl_ref, hg_ref, wg_ref, bg_ref, wbr_ref, wo_ref, lam_init):
    x = jnp.where(is_ctx_tile, xc_ref[...], xl_ref[...])
    mod = mod_ref[...]
    dm = x.shape[-1]
    hb = (_rms(x, n1_ref[...]) * (1.0 + mod[1:2]) + mod[0:1]).astype(BF16)
    hg = hg_ref[...]
    ga = ga_ref[...]
    ya = _head_norm(af_ref[...] + ab_ref[...], hg[0:1]) * (ga * _sigmoid(ga))
    yb = _head_norm(bf_ref[...] + bb_ref[...], hg[1:2]) * _sigmoid(gb_ref[...])
    oc = jnp.where(is_ctx_tile, occ_ref[...], ocl_ref[...])
    yc = _head_norm(oc, hg[2:3]) * (1.0 - lam_init)
    y = jnp.zeros_like(x)
    for i, yi in enumerate((ya, yb, yc)):
        gate = _sigmoid(_dot(hb, wg_ref[:, i * dm:(i + 1) * dm]) + bg_ref[:, i * dm:(i + 1) * dm])
        y = y + gate * _dot(yi.astype(BF16), wbr_ref[i])
    return x + mod[2:3] * _dot(y.astype(BF16), wo_ref[...])


def _merge_back(xn, mod_ref, n2_ref, rw_ref, rb_ref, h2_ref, rt_ref, rtt_ref, cnt_ref):
    mod = mod_ref[...]
    h2 = _rms(xn, n2_ref[...]) * (1.0 + mod[4:5]) + mod[3:4]
    h2_ref[...] = h2
    h_hi = h2.astype(BF16)
    h_mid = (h2 - h_hi.astype(F32)).astype(BF16)
    logits = (_dot_nt(rw_ref[0], h_hi) + _dot_nt(rw_ref[0], h_mid) + _dot_nt(rw_ref[1], h_hi)) + rb_ref[...]
    rows, counts = _route(logits[0:ROUTE_ROWS, :])
    rtt_ref[...] = rows
    cnt_ref[...] = jnp.broadcast_to(counts, cnt_ref.shape)
    tm = xn.shape[0]
    rt_ref[...] = jnp.concatenate([rows, jnp.zeros((LANES - 8, tm), F32)], axis=0).T


def _merge(x_ctx, x_lat, lat_row0, mod, n1, n2, oaf, oab, pa, obf, obb, pb, oc_ctx, oc_lat, hg, wg, bg, wbr, wo,
           rw, rb, lam_init, tm, ctx_tiles, row0_tiles):
    bsz, tt, _ = pa.shape
    dm = x_lat.shape[-1]
    nt = tt // tm - row0_tiles
    rows = nt * tm
    n = bsz * nt

    def front(fn):
        return lambda i: fn(jnp.minimum(i, n - 1) // nt, jnp.minimum(i, n - 1) % nt)

    def back(fn):
        return lambda i: fn(jnp.maximum(i - 1, 0) // nt, jnp.maximum(i - 1, 0) % nt)

    def col(c):
        return pl.BlockSpec((None, tm, MIX_W), front(lambda b, t: (b, row0_tiles + t, c)))

    mod_of = lambda b, t: (b, jnp.where(row0_tiles + t >= ctx_tiles, 1, 0), 0, 0)
    out_specs = [pl.BlockSpec((None, tm, dm), front(lambda b, t: (b, t, 0))),
                 pl.BlockSpec((None, tm, dm), back(lambda b, t: (b, t, 0))),
                 pl.BlockSpec((None, tm, LANES), back(lambda b, t: (b, t, 0))),
                 pl.BlockSpec((None, 8, tm), back(lambda b, t: (b * nt + t, 0, 0))),
                 pl.BlockSpec((None, N_EXPERTS, LANES), back(lambda b, t: (b * nt + t, 0, 0)))]
    out_shape = [jax.ShapeDtypeStruct((bsz, rows, dm), F32), jax.ShapeDtypeStruct((bsz, rows, dm), F32),
                 jax.ShapeDtypeStruct((bsz, rows, LANES), F32), jax.ShapeDtypeStruct((n, 8, tm), F32),
                 jax.ShapeDtypeStruct((n, N_EXPERTS, LANES), F32)]
    return pl.pallas_call(
        functools.partial(_merge_kernel, lam_init=lam_init, ctx_tiles=ctx_tiles, row0_tiles=row0_tiles,
                          nt=nt, n=n),
        grid=(n + 1,),
        in_specs=[pl.BlockSpec((None, tm, dm),
                               front(lambda b, t: (b, jnp.minimum(row0_tiles + t, max(ctx_tiles - 1, 0)), 0))),
                  pl.BlockSpec((None, tm, dm),
                               front(lambda b, t: (b, lat_row0 + jnp.maximum(row0_tiles + t - ctx_tiles, 0), 0))),
                  pl.BlockSpec((None, None, 6, dm), front(mod_of)),
                  pl.BlockSpec((None, None, 6, dm), back(mod_of)),
                  _const_spec((1, dm)), _const_spec((1, dm)),
                  col(0), col(0), col(2), col(0), col(0), col(3),
                  pl.BlockSpec((None, tm, MIX_W),
                               front(lambda b, t: (b, jnp.minimum(row0_tiles + t, max(ctx_tiles - 1, 0)), 0))),
                  pl.BlockSpec((None, tm, MIX_W),
                               front(lambda b, t: (b, jnp.maximum(row0_tiles + t - ctx_tiles, 0), 0))),
                  _const_spec((3, MIX_W)), _const_spec(wg.shape), _const_spec(bg.shape),
                  _const_spec(wbr.shape), _const_spec(wo.shape), _const_spec(rw.shape), _const_spec(rb.shape)],
        out_specs=out_specs,
        out_shape=out_shape,
        scratch_shapes=[pltpu.VMEM((tm, dm), F32), pltpu.VMEM((tm, dm), F32)],
        compiler_params=_params(("arbitrary",)),
        name="merge_route",
    )(x_ctx, x_lat, mod, mod, n1, n2, oaf, oab, pa, obf, obb, pb, oc_ctx, oc_lat, hg, wg, bg, wbr, wo, rw, rb)


def _slot_tables(route_rows, counts):
    tiles, _, tm = route_rows.shape
    cnt = counts[:, :, 0].astype(jnp.int32)
    tile_off = jnp.cumsum(cnt, axis=0) - cnt
    total = jnp.sum(cnt, axis=0)
    padded = (total + MOE_BLOCK - 1) // MOE_BLOCK * MOE_BLOCK
    p_end = jnp.cumsum(padded)
    base = (p_end - padded)[None, :] + tile_off
    e = route_rows[:, 0:2, :].astype(jnp.int32)
    rank = route_rows[:, 4:6, :].astype(jnp.int32)
    hit = e[:, :, :, None] == jnp.arange(N_EXPERTS, dtype=jnp.int32)
    slot = jnp.sum(jnp.where(hit, base[:, None, None, :], 0), axis=-1) + rank
    n_blocks = -(-(2 * tiles * tm) // MOE_BLOCK) + N_EXPERTS
    blk_start = jnp.arange(n_blocks, dtype=jnp.int32) * MOE_BLOCK
    block_e = jnp.minimum(jnp.sum(blk_start[:, None] >= p_end[None, :], axis=1), N_EXPERTS - 1).astype(jnp.int32)
    n_used = (p_end[-1] // MOE_BLOCK).astype(jnp.int32).reshape(1)
    filled = jnp.clip(((p_end - padded) + total)[block_e] - blk_start, 0, MOE_BLOCK)
    filled = jnp.where(blk_start < p_end[-1], filled, 0)
    partial = (filled < MOE_BLOCK).astype(jnp.int32)
    return slot.reshape(tiles, 1, 2 * tm), block_e, n_used, partial


def _row_copies(n_rows, make):
    for r in range(n_rows):
        for k in range(2):
            make(r, k).start()


def _scatter_kernel(partial_ref, slot_ref, h_ref, xd_ref, zeros, sem, zsem):
    tm = h_ref.shape[0]
    n_blocks = partial_ref.shape[0]

    @pl.when((pl.program_id(0) == 0) & (pl.program_id(1) == 0))
    def _():
        zeros[...] = jnp.zeros_like(zeros)

        def zero_block(blk):
            return pltpu.make_async_copy(zeros, xd_ref.at[pl.ds(blk * MOE_BLOCK, MOE_BLOCK), :], zsem)

        def start(blk, c):
            @pl.when(partial_ref[blk] != 0)
            def _():
                zero_block(blk).start()
            return c

        def wait(blk, c):
            @pl.when(partial_ref[blk] != 0)
            def _():
                zero_block(blk).wait()
            return c

        lax.fori_loop(0, n_blocks, start, 0)
        lax.fori_loop(0, n_blocks, wait, 0)

    _row_copies(tm, lambda r, k: pltpu.make_async_copy(
        h_ref.at[pl.ds(r, 1), :], xd_ref.at[pl.ds(slot_ref[0, k * tm + r], 1), :], sem))
    for _ in range(2):
        pltpu.make_async_copy(h_ref, xd_ref.at[pl.ds(0, tm), :], sem).wait()


def _scatter(partial, slot, h2, n_slots, tm):
    bsz, rows, dm = h2.shape
    nt = rows // tm
    grid_spec = pltpu.PrefetchScalarGridSpec(
        num_scalar_prefetch=1,
        grid=(bsz, nt),
        in_specs=[pl.BlockSpec((None, 1, 2 * tm), lambda b, t, p: (b * nt + t, 0, 0), memory_space=pltpu.SMEM),
                  pl.BlockSpec((tm, dm), lambda b, t, p: (b * nt + t, 0))],
        out_specs=pl.BlockSpec(memory_space=pl.ANY),
        scratch_shapes=[pltpu.VMEM((MOE_BLOCK, dm), F32), pltpu.SemaphoreType.DMA(()),
                        pltpu.SemaphoreType.DMA(())],
    )
    return pl.pallas_call(
        _scatter_kernel,
        grid_spec=grid_spec,
        out_shape=jax.ShapeDtypeStruct((n_slots, dm), F32),
        compiler_params=_params(("arbitrary", "arbitrary"), has_side_effects=True),
        name="moe_scatter",
    )(partial, slot, h2.reshape(bsz * rows, dm))


def _expert_kernel(be_ref, nu_ref, x_ref, w1_ref, w3_ref, w2_ref, y_ref, w1b, w3b, w2b):
    i = pl.program_id(0)

    @pl.when(i < nu_ref[0])
    def _():
        @pl.when((i == 0) | (be_ref[i] != be_ref[jnp.maximum(i - 1, 0)]))
        def _():
            w1b[...] = w1_ref[...].astype(BF16)
            w3b[...] = w3_ref[...].astype(BF16)
            w2b[...] = w2_ref[...].astype(BF16)

        xb = x_ref[...].astype(BF16)
        u = _dot(xb, w1b[...])
        hmid = (u * _sigmoid(u)) * _dot(xb, w3b[...])
        y_ref[...] = _dot(hmid.astype(BF16), w2b[...])

    @pl.when(i >= nu_ref[0])
    def _():
        y_ref[...] = jnp.zeros_like(y_ref)


def _experts(block_e, n_used, x_disp, w1, w3, w2, layer):
    n_slots, dm = x_disp.shape
    n_blocks = n_slots // MOE_BLOCK
    wspec = lambda shape: pl.BlockSpec((None, None) + shape, lambda i, be, nu: (layer, be[i], 0, 0))
    grid_spec = pltpu.PrefetchScalarGridSpec(
        num_scalar_prefetch=2,
        grid=(n_blocks,),
        in_specs=[pl.BlockSpec((MOE_BLOCK, dm), lambda i, be, nu: (jnp.minimum(i, nu[0] - 1), 0)),
                  wspec((dm, D_EXPERT)), wspec((dm, D_EXPERT)), wspec((D_EXPERT, dm))],
        out_specs=pl.BlockSpec((MOE_BLOCK, dm), lambda i, be, nu: (i, 0)),
        scratch_shapes=[pltpu.VMEM((dm, D_EXPERT), BF16), pltpu.VMEM((dm, D_EXPERT), BF16),
                        pltpu.VMEM((D_EXPERT, dm), BF16)],
    )
    return pl.pallas_call(
        _expert_kernel,
        grid_spec=grid_spec,
        out_shape=jax.ShapeDtypeStruct((n_slots, dm), F32),
        compiler_params=_params(("arbitrary",)),
        name="moe_experts",
    )(block_e, n_used, x_disp, w1, w3, w2)


def _combine_kernel(slot_ref, slot_next_ref, x_ref, mod_ref, rt_ref, yd_ref, fg_ref, o_ref, ybuf, sems):
    def body(y0, y1):
        o_ref[...] = _rms(_moe_residual(x_ref, mod_ref, rt_ref, y0, y1), fg_ref[...])

    _with_gathered_rows(slot_ref, slot_next_ref, yd_ref, ybuf, sems, body)


def _combine_final(slot, x, mod, route, y_disp, fg, tm, ctx_tiles, row0_tiles):
    bsz, rows, dm = x.shape
    nt = rows // tm
    n = bsz * nt
    row = lambda i: (i // nt, i % nt, 0)
    return pl.pallas_call(
        _combine_kernel,
        grid=(n,),
        in_specs=[pl.BlockSpec((None, 1, 2 * tm), lambda i: (i, 0, 0), memory_space=pltpu.SMEM),
                  pl.BlockSpec((None, 1, 2 * tm), lambda i: (jnp.minimum(i + 1, n - 1), 0, 0),
                               memory_space=pltpu.SMEM),
                  pl.BlockSpec((None, tm, dm), row),
                  pl.BlockSpec((None, None, 6, dm),
                               lambda i: (i // nt, jnp.where(row0_tiles + i % nt >= ctx_tiles, 1, 0), 0, 0)),
                  pl.BlockSpec((None, tm, LANES), row),
                  pl.BlockSpec(memory_space=pl.ANY),
                  _const_spec((1, dm))],
        out_specs=pl.BlockSpec((None, tm, dm), row),
        out_shape=jax.ShapeDtypeStruct((bsz, rows, dm), F32),
        scratch_shapes=[pltpu.VMEM((2, 2, tm, dm), F32), pltpu.SemaphoreType.DMA((2,))],
        compiler_params=_params(("arbitrary",)),
        name="moe_combine",
    )(slot, slot, x, mod, route, y_disp, fg)


def _rope_tables(ctx, t_lat):
    rows = t_lat // GRID_W
    row = np.repeat(np.arange(rows, dtype=np.float64), GRID_W)
    col = np.tile(np.arange(GRID_W, dtype=np.float64), rows)
    inv = ROPE_BASE ** (-np.arange(ROPE_PAIRS, dtype=np.float64) / ROPE_PAIRS)
    ang_r = row[:, None] * inv
    ang_c = col[:, None] * inv
    cos64 = np.concatenate([np.cos(ang_r), np.cos(ang_r), np.cos(ang_c), np.cos(ang_c)], axis=1)
    sin64 = np.concatenate([-np.sin(ang_r), np.sin(ang_r), -np.sin(ang_c), np.sin(ang_c)], axis=1)
    cos = np.tile(cos64, (1, MIX_W // 64))
    sin = np.tile(sin64, (1, MIX_W // 64))
    cos = np.concatenate([np.ones((ctx, MIX_W)), cos], axis=0).astype(np.float32)
    sin = np.concatenate([np.zeros((ctx, MIX_W)), sin], axis=0).astype(np.float32)
    return jnp.asarray(cos), jnp.asarray(sin)


def _pack_w_in(w):
    dm = w.shape[0]
    a_end = PA_W
    b_end = a_end + PB_W
    g_end = b_end + 4 * HEADS
    pad = jnp.zeros((dm, PG_W - 4 * HEADS), w.dtype)
    return jnp.concatenate([w[:, :b_end], w[:, b_end:g_end], pad, w[:, g_end:]], axis=1).astype(BF16)


def kernel(x, c, ctx, c_ctx, ada_w, ada_b, norm1_g, norm2_g, w_in, mlstm_conv_w, mlstm_conv_b, mlstm_gate_b,
           hgrn_lb_raw, hgrn_norm_g, mlstm_norm_g, diff_norm_g, diff_lambda, w_branch, w_gate, b_gate, w_out,
           router_g_w, router_g_b, router_e_w, router_e_b, moe_w1, moe_w3, moe_w2, final_g):
    bsz, t_lat, dm = x.shape
    t_ctx = ctx.shape[1]
    depth = ada_w.shape[0]
    tt = t_ctx + t_lat
    tm = min(256, t_ctx)
    hgrn_chunk = min(128, t_ctx)
    mlstm_chunk = min(256, t_ctx)
    assert t_ctx % tm == 0 and t_lat % (2 * tm) == 0 and t_lat % GRID_W == 0
    assert t_ctx % hgrn_chunk == 0 and t_ctx % mlstm_chunk == 0
    ctx_tiles = t_ctx // tm

    n_rows = -(-(bsz + 1) // 8) * 8
    cc = jnp.zeros((n_rows, dm), F32).at[:bsz].set(c).at[bsz].set(c_ctx)
    mods = _ada_mod(cc, ada_w, ada_b).reshape(depth, n_rows, 6, dm)

    cos, sin = _rope_tables(t_ctx, t_lat)
    lb_cum = jnp.cumsum(jax.nn.softmax(hgrn_lb_raw.astype(F32), axis=0), axis=0)
    lower = lb_cum - lb_cum[0]

    x_ctx, x_lat, lat_row0 = ctx, x, 0
    moe = None
    for l in range(depth):
        with_ctx = l < depth - 1
        lam_init = 0.8 - 0.6 * math.exp(-0.3 * l)
        mod = jnp.stack([jnp.broadcast_to(mods[l, bsz], (bsz, 6, dm)), mods[l, :bsz]], axis=1)
        n1 = norm1_g[l].reshape(1, dm)
        n2 = norm2_g[l].reshape(1, dm)

        lb = lower[l]
        pad = jnp.zeros((2, 6, MIX_W), F32)
        lbc = jnp.concatenate([jnp.log(lb)[:, None], jnp.log1p(-lb)[:, None], pad], axis=1)
        if moe is None:
            pa, pb, pg, qc, kc, vc = _norm_proj(x_ctx, x_lat, mod, n1, _pack_w_in(w_in[l]), cos, sin, lbc,
                                                tm, ctx_tiles)
        else:
            xc, pa, pb, pg, qc, kc, vc = _norm_proj_fused(moe, mod, n1, _pack_w_in(w_in[l]), cos, sin, lbc,
                                                          tm, ctx_tiles)
            x_ctx, x_lat, lat_row0 = xc, xc, ctx_tiles

        oaf, oab = _hgrn(pa, hgrn_chunk, t_ctx // hgrn_chunk)

        qk = _conv_silu(pb, mlstm_conv_w[l], mlstm_conv_b[l], t_ctx, 256)
        gbias = jnp.zeros((1, PG_W), F32).at[0, :4 * HEADS].set(mlstm_gate_b[l].reshape(-1))
        obf, obb = _mlstm(qk, pb, pg, gbias, mlstm_chunk, t_ctx // mlstm_chunk)

        lam_vec = diff_lambda[l].astype(F32)
        oc_lat = _attn_lat(lam_vec, qc, kc, vc, lam_init, tm, t_ctx)
        oc_ctx = _attn_ctx(lam_vec, qc, kc, vc, lam_init, tm, t_ctx) if with_ctx else oc_lat

        hg = jnp.stack([hgrn_norm_g[l], mlstm_norm_g[l], diff_norm_g[l]])
        rw = jnp.zeros((LANES, dm), F32).at[:N_GROUPS].set(router_g_w[l].T)
        rw = rw.at[N_GROUPS:N_GROUPS + N_EXPERTS].set(router_e_w[l].T)
        rw_hi = rw.astype(BF16)
        rw = jnp.stack([rw_hi, (rw - rw_hi.astype(F32)).astype(BF16)])
        rb = jnp.zeros((LANES, 1), F32).at[:N_GROUPS, 0].set(router_g_b[l])
        rb = rb.at[N_GROUPS:N_GROUPS + N_EXPERTS, 0].set(router_e_b[l])
        row0_tiles = 0 if with_ctx else ctx_tiles
        xn, h2, route, route_rows, counts = _merge(x_ctx, x_lat, lat_row0, mod, n1, n2, oaf, oab, pa, obf, obb, pb,
                                                   oc_ctx, oc_lat, hg,
                                                   w_gate[l].astype(BF16), b_gate[l].reshape(1, 3 * dm),
                                                   w_branch[l].astype(BF16), w_out[l].astype(BF16), rw, rb,
                                                   lam_init, tm, ctx_tiles, row0_tiles)

        slot, block_e, n_used, partial = _slot_tables(route_rows, counts)
        x_disp = _scatter(partial, slot, h2, partial.shape[0] * MOE_BLOCK, tm)
        y_disp = _experts(block_e, n_used, x_disp, moe_w1, moe_w3, moe_w2, l)
        moe = (slot, xn, mod, route, y_disp)
    slot, xn, mod, route, y_disp = moe
    return _combine_final(slot, xn, mod, route, y_disp, final_g.reshape(1, dm), tm, ctx_tiles, ctx_tiles)
```
